```python
import jax
import jax.numpy as jnp
from jax import lax
import numpy as np

D_MODEL = 4096
BATCH = 4
SEQ = 2048
DEPTH = 2
DEC_BATCH = 8
DEC_SEQ = 8
PAST_LEN = 16384
PAGE_SIZE = 128

N_DN_LAYERS = (DEPTH + 1) // 2
N_NSA_LAYERS = DEPTH // 2
EPS = 1e-6

DN_HEADS = 32
DN_DK = 128
DN_DV = 128
DN_CONV = 4
DN_CHUNK = 64
DN_QK = DN_HEADS * DN_DK
DN_V = DN_HEADS * DN_DV
DN_CONV_DIM = 2 * DN_QK + DN_V
DN_IN = DN_CONV_DIM + DN_V + 2 * DN_HEADS

NSA_HEADS = 32
NSA_KV = 4
NSA_REP = NSA_HEADS // NSA_KV
NSA_DK = 192
NSA_DV = 128
CMP_BLOCK = 32
CMP_STRIDE = 16
CMP_RATIO = CMP_BLOCK // CMP_STRIDE
CMP_HIDDEN = 256
SLC_BLOCK = 64
SLC_TOPN = 16
WINDOW = 512
NSA_QBLOCK = 32
NSA_Q = NSA_HEADS * NSA_DK
NSA_KW = NSA_KV * NSA_DK
NSA_VW = NSA_KV * NSA_DV
NSA_O = NSA_HEADS * NSA_DV
NSA_SPLITS = (NSA_Q, NSA_KW, NSA_VW, NSA_KW, NSA_VW, NSA_KW, NSA_VW, 3 * NSA_HEADS, NSA_O)
NSA_IN = sum(NSA_SPLITS)

kernel_name = 'hybrid_gdn_nsa_decoder_step'


def rmsnorm(x, w):
    xf = x.astype(jnp.float32)
    y = xf * lax.rsqrt(jnp.mean(xf * xf, axis=-1, keepdims=True) + EPS)
    return (y * w.astype(jnp.float32)).astype(x.dtype)


def l2norm(x):
    return x * lax.rsqrt(jnp.sum(x * x, axis=-1, keepdims=True) + EPS)


def masked_softmax(s, mask):
    s = jnp.where(mask, s, -jnp.inf)
    m = jnp.max(s, axis=-1, keepdims=True)
    m = jnp.where(jnp.isfinite(m), m, 0.0)
    p = jnp.exp(s - m)
    return p / jnp.maximum(jnp.sum(p, axis=-1, keepdims=True), 1e-30)


def causal_conv(buf, x, w):
    L = x.shape[1]
    xp = jnp.concatenate([buf.astype(x.dtype), x], axis=1)
    y = sum(xp[:, j:j + L] * w[j] for j in range(DN_CONV))
    return jax.nn.silu(y), xp[:, L:]


def gated_delta_chunked(q, k, v, g, beta, s0):
    B, T, H, Dk = q.shape
    Dv = v.shape[-1]
    C = min(DN_CHUNK, T)
    pad = (-T) % C
    if pad:
        pw = ((0, 0), (0, pad), (0, 0), (0, 0))
        q, k, v = [jnp.pad(t, pw) for t in (q, k, v)]
        g, beta = [jnp.pad(t, pw[:3]) for t in (g, beta)]
    N = (T + pad) // C

    def chunks(t):
        t = t.reshape((B, N, C, H) + t.shape[3:])
        return jnp.moveaxis(t, (1, 3), (0, 2))

    qc, kc, vc, bc = chunks(q), chunks(k), chunks(v), chunks(beta)
    gc = jnp.cumsum(chunks(g), axis=-1)
    idx = jnp.arange(C)
    lower = idx[:, None] >= idx[None, :]
    strict = idx[:, None] > idx[None, :]
    decay = jnp.exp(jnp.where(lower, gc[..., :, None] - gc[..., None, :], -jnp.inf))
    kb = kc * bc[..., None]
    lmat = jnp.where(strict, jnp.einsum('nbhcd,nbhsd->nbhcs', kb, kc) * decay, 0.0)
    amat = lmat + jnp.eye(C, dtype=lmat.dtype)
    rhs = jnp.concatenate([vc * bc[..., None], kb * jnp.exp(gc)[..., None]], axis=-1)
    sol = lax.linalg.triangular_solve(amat, rhs, left_side=True, lower=True, unit_diagonal=True)
    u, w = sol[..., :Dv], sol[..., Dv:]
    attn = jnp.einsum('nbhcd,nbhsd->nbhcs', qc, kc) * decay

    def step(s, xs):
        q_n, k_n, u_n, w_n, g_n, a_n = xs
        v_new = u_n - jnp.einsum('bhcd,bhde->bhce', w_n, s)
        o_n = (jnp.einsum('bhcd,bhde->bhce', q_n * jnp.exp(g_n)[..., None], s)
               + jnp.einsum('bhcs,bhse->bhce', a_n, v_new))
        g_last = g_n[..., -1]
        s = (s * jnp.exp(g_last)[..., None, None]
             + jnp.einsum('bhcd,bhce->bhde', k_n * jnp.exp(g_last[..., None] - g_n)[..., None], v_new))
        return s, o_n

    s_fin, o = lax.scan(step, s0, (qc, kc, u, w, gc, attn))
    o = jnp.moveaxis(o, (0, 2), (1, 3)).reshape(B, N * C, H, Dv)[:, :T]
    return o, s_fin


def deltanet_mixer(h, conv_buf, s0, w_in, conv_w, a_log, dt_bias, out_norm, w_out):
    f32 = jnp.float32
    B, L, _ = h.shape
    proj = h @ w_in
    qkv, z, b, a = jnp.split(proj, [DN_CONV_DIM, DN_CONV_DIM + DN_V, DN_CONV_DIM + DN_V + DN_HEADS], axis=-1)
    qkv, new_buf = causal_conv(conv_buf, qkv, conv_w)
    q, k, v = jnp.split(qkv.astype(f32), [DN_QK, 2 * DN_QK], axis=-1)
    q = l2norm(q.reshape(B, L, DN_HEADS, DN_DK)) * (DN_DK ** -0.5)
    k = l2norm(k.reshape(B, L, DN_HEADS, DN_DK))
    v = v.reshape(B, L, DN_HEADS, DN_DV)
    beta = jax.nn.sigmoid(b.astype(f32))
    g = -jnp.exp(a_log.astype(f32)) * jax.nn.softplus(a.astype(f32) + dt_bias.astype(f32))
    o, s_new = gated_delta_chunked(q, k, v, g, beta, s0.astype(f32))
    o = rmsnorm(o, out_norm) * jax.nn.silu(z.reshape(B, L, DN_HEADS, DN_DV).astype(f32))
    y = o.reshape(B, L, DN_V).astype(h.dtype) @ w_out
    return y, (s_new, new_buf)


def alibi_slopes():
    hh = jnp.arange(1, NSA_HEADS + 1, dtype=jnp.float32)
    return jnp.exp2(-8.0 * hh / NSA_HEADS).reshape(NSA_KV, NSA_REP)


def compress(rows, pe, w1, w2):
    B, L = rows.shape[:2]
    pad = (-L) % CMP_STRIDE
    rows = jnp.pad(rows, ((0, 0), (0, pad), (0, 0), (0, 0)))
    nseg = (L + pad) // CMP_STRIDE
    seg = rows.reshape((B, nseg, CMP_STRIDE) + rows.shape[2:])
    nc = nseg - CMP_RATIO + 1
    blocks = jnp.concatenate([seg[:, r:r + nc] for r in range(CMP_RATIO)], axis=2)
    hid = jax.nn.silu(jnp.einsum('bnlgd,ldh->bngh', blocks + pe[:, None, :], w1))
    return jnp.einsum('bngh,hd->bngd', hid, w2)


def cmp_to_slc(nc, nsb):
    i = jnp.arange(nc)[:, None] * CMP_STRIDE
    j = jnp.arange(nsb)[None, :] * SLC_BLOCK
    ov = jnp.minimum(i + CMP_BLOCK, j + SLC_BLOCK) - jnp.maximum(i, j)
    return jnp.maximum(ov, 0).astype(jnp.float32) / CMP_STRIDE


def slc_blocks(rows):
    B, L = rows.shape[:2]
    pad = (-L) % SLC_BLOCK
    rows = jnp.pad(rows, ((0, 0), (0, pad), (0, 0), (0, 0)))
    return rows.reshape((B, (L + pad) // SLC_BLOCK, SLC_BLOCK) + rows.shape[2:]).transpose(0, 3, 1, 2, 4)


def nsa_core(q, gates, tq, kc, vc, ec, ks, vs, kw, vw, tw):
    f32 = jnp.float32
    slopes = alibi_slopes()
    qf = q.astype(f32) * (NSA_DK ** -0.5)
    s = jnp.einsum('bqgrd,bngd->bqgrn', qf, kc.astype(f32))
    dist = (tq[:, None] - ec[None, :]).astype(f32)
    s = s - slopes[None, None, :, :, None] * dist[None, :, None, None, :]
    p_c = masked_softmax(s, (ec[None, :] <= tq[:, None])[None, :, None, None, :])
    o_c = jnp.einsum('bqgrn,bngd->bqgrd', p_c, vc.astype(f32))
    nsb = ks.shape[2]
    imp = jnp.einsum('bqgrn,nj->bqgj', p_c, cmp_to_slc(kc.shape[1], nsb))
    jb = jnp.arange(nsb)[None, :]
    cur = (tq // SLC_BLOCK)[:, None]
    forced = (jb == 0) | (jb == cur) | (jb == cur - 1)
    valid = jb * SLC_BLOCK <= tq[:, None]
    imp = jnp.where(forced[None, :, None, :], jnp.inf, jnp.where(valid[None, :, None, :], imp, -jnp.inf))
    _, idx = lax.top_k(imp, min(SLC_TOPN, nsb))
    B, Q, G = idx.shape[:3]
    bi = jnp.arange(B)[:, None, None, None]
    gi = jnp.arange(G)[None, None, :, None]
    ksel = ks[bi, gi, idx].astype(f32)
    vsel = vs[bi, gi, idx].astype(f32)
    pos = idx[..., None] * SLC_BLOCK + jnp.arange(SLC_BLOCK)
    s = jnp.einsum('bqgrd,bqgnkd->bqgrnk', qf, ksel)
    dsel = (tq[None, :, None, None, None] - pos).astype(f32)[:, :, :, None]
    s = s - slopes[None, None, :, :, None, None] * dsel
    smask = (pos <= tq[None, :, None, None, None]).reshape(B, Q, G, 1, -1)
    p_s = masked_softmax(s.reshape(B, Q, G, NSA_REP, -1), smask)
    o_s = jnp.einsum('bqgrm,bqgmd->bqgrd', p_s, vsel.reshape(B, Q, G, -1, vsel.shape[-1]))
    s = jnp.einsum('bqgrd,bwgd->bqgrw', qf, kw.astype(f32))
    dw = tq[:, None] - tw[None, :]
    s = s - slopes[None, None, :, :, None] * dw.astype(f32)[None, :, None, None, :]
    wmask = (dw >= 0) & (dw < WINDOW) & (tw[None, :] >= 0)
    p_w = masked_softmax(s, wmask[None, :, None, None, :])
    o_w = jnp.einsum('bqgrw,bwgd->bqgrd', p_w, vw.astype(f32))
    gf = gates.astype(f32)
    return gf[..., 0:1] * o_c + gf[..., 1:2] * o_s + gf[..., 2:3] * o_w


def nsa_project(h, w_in, q_norm, kn_s, kn_w):
    B, L, _ = h.shape
    parts = jnp.split(h @ w_in, np.cumsum(NSA_SPLITS)[:-1].tolist(), axis=-1)
    q, kc, vc, ks, vs, kw, vw, gl, z = parts
    q = rmsnorm(q.reshape(B, L, NSA_KV, NSA_REP, NSA_DK), q_norm)
    kc = kc.reshape(B, L, NSA_KV, NSA_DK)
    vc = vc.reshape(B, L, NSA_KV, NSA_DV)
    ks = rmsnorm(ks.reshape(B, L, NSA_KV, NSA_DK), kn_s)
    vs = vs.reshape(B, L, NSA_KV, NSA_DV)
    kw = rmsnorm(kw.reshape(B, L, NSA_KV, NSA_DK), kn_w)
    vw = vw.reshape(B, L, NSA_KV, NSA_DV)
    gates = jax.nn.sigmoid(gl.astype(jnp.float32)).reshape(B, L, NSA_KV, NSA_REP, 3)
    return q, kc, vc, ks, vs, kw, vw, gates, z


def nsa_compressed(kc_rows, vc_rows, kn_c, pe_k, w1_k, w2_k, pe_v, w1_v, w2_v):
    kc = rmsnorm(compress(kc_rows, pe_k, w1_k, w2_k), kn_c)
    vc = compress(vc_rows, pe_v, w1_v, w2_v)
    ec = jnp.arange(kc.shape[1]) * CMP_STRIDE + CMP_BLOCK - 1
    return kc, vc, ec


def nsa_prompt(h, w_in, q_norm, kn_c, kn_s, kn_w, pe_k, w1_k, w2_k, pe_v, w1_v, w2_v, w_out):
    B, T, _ = h.shape
    q, kc_r, vc_r, ks_r, vs_r, kw_r, vw_r, gates, z = nsa_project(h, w_in, q_norm, kn_s, kn_w)
    kc, vc, ec = nsa_compressed(kc_r, vc_r, kn_c, pe_k, w1_k, w2_k, pe_v, w1_v, w2_v)
    ks, vs = slc_blocks(ks_r), slc_blocks(vs_r)
    pw = ((0, 0), (WINDOW, 0), (0, 0), (0, 0))
    kw_pad, vw_pad = jnp.pad(kw_r, pw), jnp.pad(vw_r, pw)
    qb = min(NSA_QBLOCK, T)
    nb = T // qb

    def block(xs):
        i, q_i, g_i = xs
        start = i * qb
        tq = start + jnp.arange(qb)
        kw_i = lax.dynamic_slice_in_dim(kw_pad, start, WINDOW + qb, axis=1)
        vw_i = lax.dynamic_slice_in_dim(vw_pad, start, WINDOW + qb, axis=1)
        tw = start - WINDOW + jnp.arange(WINDOW + qb)
        return nsa_core(q_i, g_i, tq, kc, vc, ec, ks, vs, kw_i, vw_i, tw)

    qs = q.reshape(B, nb, qb, NSA_KV, NSA_REP, NSA_DK).swapaxes(0, 1)
    gs = gates.reshape(B, nb, qb, NSA_KV, NSA_REP, 3).swapaxes(0, 1)
    o = lax.map(block, (jnp.arange(nb), qs, gs))
    o = o.swapaxes(0, 1).reshape(B, T, NSA_O)
    y = (o * jax.nn.silu(z.astype(jnp.float32))).astype(h.dtype) @ w_out
    wl = min(WINDOW, T)
    return y, (kc_r, vc_r, ks_r, vs_r, kw_r[:, T - wl:], vw_r[:, T - wl:])


def gather_past(pool, page_table):
    g = pool[page_table]
    return g.reshape((g.shape[0], g.shape[1] * g.shape[2]) + g.shape[3:])


def nsa_sample(h, ck, cv, sk, sv, wk_buf, wv_buf, page_table,
               w_in, q_norm, kn_c, kn_s, kn_w, pe_k, w1_k, w2_k, pe_v, w1_v, w2_v, w_out):
    B, S, _ = h.shape
    P = page_table.shape[1] * ck.shape[1]
    q, kc_r, vc_r, ks_r, vs_r, kw_r, vw_r, gates, z = nsa_project(h, w_in, q_norm, kn_s, kn_w)
    kc_all = jnp.concatenate([gather_past(ck, page_table), kc_r], axis=1)
    vc_all = jnp.concatenate([gather_past(cv, page_table), vc_r], axis=1)
    kc, vc, ec = nsa_compressed(kc_all, vc_all, kn_c, pe_k, w1_k, w2_k, pe_v, w1_v, w2_v)
    ks = slc_blocks(jnp.concatenate([gather_past(sk, page_table), ks_r], axis=1))
    vs = slc_blocks(jnp.concatenate([gather_past(sv, page_table), vs_r], axis=1))
    kw = jnp.concatenate([wk_buf, kw_r], axis=1)
    vw = jnp.concatenate([wv_buf, vw_r], axis=1)
    wl = wk_buf.shape[1]
    tw = P - wl + jnp.arange(wl + S)
    tq = P + jnp.arange(S)
    o = nsa_core(q, gates, tq, kc, vc, ec, ks, vs, kw, vw, tw).reshape(B, S, NSA_O)
    y = (o * jax.nn.silu(z.astype(jnp.float32))).astype(h.dtype) @ w_out
    return y, (kc_r, vc_r, ks_r, vs_r, kw[:, S:], vw[:, S:])


def stack_field(group, n):
    return jnp.stack([st[n] for st in group])


def setup_inputs(seed: int = 0) -> dict:
    key = jax.random.key(seed)
    keys = jax.random.split(key, 40)
    counter = iter(range(40))
    f32 = jnp.float32

    def nk():
        return keys[next(counter)]

    def nrm(shape, scale=1.0):
        return jax.random.normal(nk(), shape, f32) * scale

    def gain(shape):
        return 1.0 + 0.02 * jax.random.normal(nk(), shape, f32)

    n_pages = PAST_LEN // PAGE_SIZE
    n_used = DEC_BATCH * n_pages
    n_pool = n_used + max(1, n_used // 4)
    win_len = min(WINDOW, PAST_LEN)
    LA, LB = N_DN_LAYERS, N_NSA_LAYERS
    return {
        'x_prompt': nrm((BATCH, SEQ, D_MODEL)),
        'x_sample': nrm((DEC_BATCH, DEC_SEQ, D_MODEL)),
        'state_delta': nrm((LA, DEC_BATCH, DN_HEADS, DN_DK, DN_DV), 0.1),
        'state_conv': nrm((LA, DEC_BATCH, DN_CONV - 1, DN_CONV_DIM)),
        'cache_cmp_k': nrm((LB, n_pool, PAGE_SIZE, NSA_KV, NSA_DK)),
        'cache_cmp_v': nrm((LB, n_pool, PAGE_SIZE, NSA_KV, NSA_DV)),
        'cache_slc_k': nrm((LB, n_pool, PAGE_SIZE, NSA_KV, NSA_DK)),
        'cache_slc_v': nrm((LB, n_pool, PAGE_SIZE, NSA_KV, NSA_DV)),
        'cache_win_k': nrm((LB, DEC_BATCH, win_len, NSA_KV, NSA_DK)),
        'cache_win_v': nrm((LB, DEC_BATCH, win_len, NSA_KV, NSA_DV)),
        'page_table': jax.random.permutation(nk(), n_pool)[:n_used].reshape(DEC_BATCH, n_pages).astype(jnp.int32),
        'norm_dn': gain((LA, D_MODEL)),
        'w_in_dn': nrm((LA, D_MODEL, DN_IN), D_MODEL ** -0.5),
        'conv_w_dn': nrm((LA, DN_CONV, DN_CONV_DIM), DN_CONV ** -0.5),
        'a_log_dn': jnp.log(jax.random.uniform(nk(), (LA, DN_HEADS), f32, minval=1.0, maxval=16.0)),
        'dt_bias_dn': gain((LA, DN_HEADS)),
        'out_norm_dn': gain((LA, DN_DV)),
        'w_out_dn': nrm((LA, DN_V, D_MODEL), DN_V ** -0.5),
        'norm_nsa': gain((LB, D_MODEL)),
        'w_in_nsa': nrm((LB, D_MODEL, NSA_IN), D_MODEL ** -0.5),
        'q_norm_nsa': gain((LB, NSA_DK)),
        'k_norm_cmp': gain((LB, NSA_DK)),
        'k_norm_slc': gain((LB, NSA_DK)),
        'k_norm_win': gain((LB, NSA_DK)),
        'cmp_pe_k': nrm((LB, CMP_BLOCK, NSA_DK), 0.1),
        'cmp_w1_k': nrm((LB, CMP_BLOCK, NSA_DK, CMP_HIDDEN), (CMP_BLOCK * NSA_DK) ** -0.5),
        'cmp_w2_k': nrm((LB, CMP_HIDDEN, NSA_DK), CMP_HIDDEN ** -0.5),
        'cmp_pe_v': nrm((LB, CMP_BLOCK, NSA_DV), 0.1),
        'cmp_w1_v': nrm((LB, CMP_BLOCK, NSA_DV, CMP_HIDDEN), (CMP_BLOCK * NSA_DV) ** -0.5),
        'cmp_w2_v': nrm((LB, CMP_HIDDEN, NSA_DV), CMP_HIDDEN ** -0.5),
        'w_out_nsa': nrm((LB, NSA_O, D_MODEL), NSA_O ** -0.5),
    }


def reference(x_prompt, x_sample, state_delta, state_conv, cache_cmp_k, cache_cmp_v,
              cache_slc_k, cache_slc_v, cache_win_k, cache_win_v, page_table,
              norm_dn, w_in_dn, conv_w_dn, a_log_dn, dt_bias_dn, out_norm_dn, w_out_dn,
              norm_nsa, w_in_nsa, q_norm_nsa, k_norm_cmp, k_norm_slc, k_norm_win,
              cmp_pe_k, cmp_w1_k, cmp_w2_k, cmp_pe_v, cmp_w1_v, cmp_w2_v, w_out_nsa):
    xp, xs = x_prompt, x_sample
    B = xp.shape[0]
    p_dn, s_dn, p_nsa, s_nsa = [], [], [], []
    for i in range(DEPTH):
        j = i // 2
        if i % 2 == 0:
            dw = (w_in_dn[j], conv_w_dn[j], a_log_dn[j], dt_bias_dn[j], out_norm_dn[j], w_out_dn[j])
            buf0 = jnp.zeros((B, DN_CONV - 1, DN_CONV_DIM), xp.dtype)
            st0 = jnp.zeros((B, DN_HEADS, DN_DK, DN_DV), jnp.float32)
            yp, pst = deltanet_mixer(rmsnorm(xp, norm_dn[j]), buf0, st0, *dw)
            ys, sst = deltanet_mixer(rmsnorm(xs, norm_dn[j]), state_conv[j], state_delta[j], *dw)
            p_dn.append(pst)
            s_dn.append(sst)
        else:
            nw = (w_in_nsa[j], q_norm_nsa[j], k_norm_cmp[j], k_norm_slc[j], k_norm_win[j],
                  cmp_pe_k[j], cmp_w1_k[j], cmp_w2_k[j], cmp_pe_v[j], cmp_w1_v[j], cmp_w2_v[j], w_out_nsa[j])
            yp, pst = nsa_prompt(rmsnorm(xp, norm_nsa[j]), *nw)
            ys, sst = nsa_sample(rmsnorm(xs, norm_nsa[j]), cache_cmp_k[j], cache_cmp_v[j],
                                 cache_slc_k[j], cache_slc_v[j], cache_win_k[j], cache_win_v[j],
                                 page_table, *nw)
            p_nsa.append(pst)
            s_nsa.append(sst)
        xp = xp + yp.astype(xp.dtype)
        xs = xs + ys.astype(xs.dtype)
    return (xp, xs,
            stack_field(p_dn, 0), stack_field(p_dn, 1),
            stack_field(p_nsa, 0), stack_field(p_nsa, 1), stack_field(p_nsa, 2),
            stack_field(p_nsa, 3), stack_field(p_nsa, 4), stack_field(p_nsa, 5),
            stack_field(s_dn, 0), stack_field(s_dn, 1),
            stack_field(s_nsa, 0), stack_field(s_nsa, 1), stack_field(s_nsa, 2),
            stack_field(s_nsa, 3), stack_field(s_nsa, 4), stack_field(s_nsa, 5))
```

```python
import functools

import jax
import jax.numpy as jnp
import numpy as np
from jax import lax
from jax.experimental import pallas as pl
from jax.experimental.pallas import tpu as pltpu

D_MODEL = 4096
EPS = 1e-6

DN_HEADS = 32
DN_DK = 128
DN_DV = 128
DN_CONV = 4
DN_CHUNK = 64
DN_QK = DN_HEADS * DN_DK
DN_V = DN_HEADS * DN_DV
DN_CONV_DIM = 2 * DN_QK + DN_V

NSA_HEADS = 32
NSA_KV = 4
NSA_REP = NSA_HEADS // NSA_KV
NSA_DK = 192
NSA_DV = 128
CMP_BLOCK = 32
CMP_STRIDE = 16
CMP_RATIO = CMP_BLOCK // CMP_STRIDE
CMP_HIDDEN = 256
SLC_BLOCK = 64
SLC_TOPN = 16
WINDOW = 512
NSA_QBLOCK = 32
NSA_Q = NSA_HEADS * NSA_DK
NSA_KW = NSA_KV * NSA_DK
NSA_VW = NSA_KV * NSA_DV
NSA_O = NSA_HEADS * NSA_DV
NSA_SPLITS = (NSA_Q, NSA_KW, NSA_VW, NSA_KW, NSA_VW, NSA_KW, NSA_VW, 3 * NSA_HEADS, NSA_O)

VMEM_LIMIT_BYTES = 56 * 1024 * 1024
LANE = 128


def _cparams(*sem):
    return pltpu.CompilerParams(dimension_semantics=sem, vmem_limit_bytes=VMEM_LIMIT_BYTES)


def _rmsnorm_kernel(x_ref, w_ref, o_ref):
    x = x_ref[...]
    ms = jnp.mean(x * x, axis=-1, keepdims=True)
    o_ref[...] = (x * lax.rsqrt(ms + EPS) * w_ref[...]).astype(o_ref.dtype)


def rmsnorm_bf16(x, w, *, tm):
    m, d = x.shape
    return pl.pallas_call(
        _rmsnorm_kernel,
        grid=(m // tm,),
        in_specs=[pl.BlockSpec((tm, d), lambda i: (i, 0)), pl.BlockSpec((1, d), lambda i: (0, 0))],
        out_specs=pl.BlockSpec((tm, d), lambda i: (i, 0)),
        out_shape=jax.ShapeDtypeStruct((m, d), jnp.bfloat16),
        compiler_params=_cparams("parallel"),
        name="rmsnorm_bf16",
    )(x, w.reshape(1, d))


def _mm_kernel(a_ref, b_ref, o_ref):
    o_ref[...] = jnp.dot(a_ref[...], b_ref[...], preferred_element_type=jnp.float32).astype(o_ref.dtype)


def _mm_res_kernel(a_ref, b_ref, r_ref, o_ref):
    acc = jnp.dot(a_ref[...], b_ref[...], preferred_element_type=jnp.float32)
    o_ref[...] = r_ref[...] + acc


def matmul(a, b, *, tm, tn, residual=None, out_dtype=jnp.float32, name="matmul"):
    m, k = a.shape
    _, n = b.shape
    assert m % tm == 0 and n % tn == 0
    in_specs = [pl.BlockSpec((tm, k), lambda i, j: (i, 0)), pl.BlockSpec((k, tn), lambda i, j: (0, j))]
    args = [a, b]
    kern = _mm_kernel
    if residual is not None:
        in_specs.append(pl.BlockSpec((tm, tn), lambda i, j: (i, j)))
        args.append(residual)
        kern = _mm_res_kernel
    return pl.pallas_call(
        kern,
        grid=(m // tm, n // tn),
        in_specs=in_specs,
        out_specs=pl.BlockSpec((tm, tn), lambda i, j: (i, j)),
        out_shape=jax.ShapeDtypeStruct((m, n), out_dtype),
        compiler_params=_cparams("parallel", "parallel"),
        name=name,
    )(*args)


def _row_tile(m):
    return 1024 if m % 1024 == 0 else m


def _rmsnorm_j(x, w):
    xf = x.astype(jnp.float32)
    y = xf * lax.rsqrt(jnp.mean(xf * xf, axis=-1, keepdims=True) + EPS)
    return (y * w.astype(jnp.float32)).astype(x.dtype)


def _l2norm(x):
    return x * lax.rsqrt(jnp.sum(x * x, axis=-1, keepdims=True) + EPS)


def _masked_softmax(s, mask):
    s = jnp.where(mask, s, -jnp.inf)
    m = jnp.max(s, axis=-1, keepdims=True)
    m = jnp.where(jnp.isfinite(m), m, 0.0)
    p = jnp.exp(s - m)
    return p / jnp.maximum(jnp.sum(p, axis=-1, keepdims=True), 1e-30)


def _causal_conv(buf, x, w):
    L = x.shape[1]
    xp = jnp.concatenate([buf.astype(x.dtype), x], axis=1)
    y = sum(xp[:, j:j + L] * w[j] for j in range(DN_CONV))
    return jax.nn.silu(y), xp[:, L:]


def _gated_delta_chunked(q, k, v, g, beta, s0):
    B, T, H, Dk = q.shape
    Dv = v.shape[-1]
    C = min(DN_CHUNK, T)
    N = T // C

    def chunks(t):
        t = t.reshape((B, N, C, H) + t.shape[3:])
        return jnp.moveaxis(t, (1, 3), (0, 2))

    qc, kc, vc, bc = chunks(q), chunks(k), chunks(v), chunks(beta)
    gc = jnp.cumsum(chunks(g), axis=-1)
    idx = jnp.arange(C)
    lower = idx[:, None] >= idx[None, :]
    strict = idx[:, None] > idx[None, :]
    decay = jnp.exp(jnp.where(lower, gc[..., :, None] - gc[..., None, :], -jnp.inf))
    kb = kc * bc[..., None]
    lmat = jnp.where(strict, jnp.einsum('nbhcd,nbhsd->nbhcs', kb, kc) * decay, 0.0)
    amat = lmat + jnp.eye(C, dtype=lmat.dtype)
    rhs = jnp.concatenate([vc * bc[..., None], kb * jnp.exp(gc)[..., None]], axis=-1)
    sol = lax.linalg.triangular_solve(amat, rhs, left_side=True, lower=True, unit_diagonal=True)
    u, w = sol[..., :Dv], sol[..., Dv:]
    attn = jnp.einsum('nbhcd,nbhsd->nbhcs', qc, kc) * decay

    def step(s, xs):
        q_n, k_n, u_n, w_n, g_n, a_n = xs
        v_new = u_n - jnp.einsum('bhcd,bhde->bhce', w_n, s)
        o_n = (jnp.einsum('bhcd,bhde->bhce', q_n * jnp.exp(g_n)[..., None], s)
               + jnp.einsum('bhcs,bhse->bhce', a_n, v_new))
        g_last = g_n[..., -1]
        s = (s * jnp.exp(g_last)[..., None, None]
             + jnp.einsum('bhcd,bhce->bhde', k_n * jnp.exp(g_last[..., None] - g_n)[..., None], v_new))
        return s, o_n

    s_fin, o = lax.scan(step, s0, (qc, kc, u, w, gc, attn))
    o = jnp.moveaxis(o, (0, 2), (1, 3)).reshape(B, N * C, H, Dv)
    return o, s_fin


def _project(h2d, w_bf16, name):
    m = h2d.shape[0]
    n = w_bf16.shape[1]
    tn = 1024 if n % 1024 == 0 else (512 if n % 512 == 0 else n)
    return matmul(h2d, w_bf16, tm=_row_tile(m), tn=tn, name=name)


def _deltanet_mixer(x, conv_buf, s0, norm_w, w_in, conv_w, a_log, dt_bias, out_norm, w_out):
    f32 = jnp.float32
    B, L, D = x.shape
    x2 = x.reshape(B * L, D)
    h = rmsnorm_bf16(x2, norm_w, tm=min(256, B * L))
    w_main = w_in[:, :DN_CONV_DIM + DN_V].astype(jnp.bfloat16)
    w_ba = jnp.pad(w_in[:, DN_CONV_DIM + DN_V:], ((0, 0), (0, LANE - 2 * DN_HEADS))).astype(jnp.bfloat16)
    proj = _project(h, w_main, "dn_in_proj").reshape(B, L, -1)
    ba = matmul(h, w_ba, tm=_row_tile(B * L), tn=LANE, name="dn_ba_proj").reshape(B, L, LANE)
    qkv, z = proj[..., :DN_CONV_DIM], proj[..., DN_CONV_DIM:]
    b, a = ba[..., :DN_HEADS], ba[..., DN_HEADS:2 * DN_HEADS]
    qkv, new_buf = _causal_conv(conv_buf, qkv, conv_w)
    q, k, v = jnp.split(qkv.astype(f32), [DN_QK, 2 * DN_QK], axis=-1)
    q = _l2norm(q.reshape(B, L, DN_HEADS, DN_DK)) * (DN_DK ** -0.5)
    k = _l2norm(k.reshape(B, L, DN_HEADS, DN_DK))
    v = v.reshape(B, L, DN_HEADS, DN_DV)
    beta = jax.nn.sigmoid(b.astype(f32))
    g = -jnp.exp(a_log.astype(f32)) * jax.nn.softplus(a.astype(f32) + dt_bias.astype(f32))
    o, s_new = _gated_delta_chunked(q, k, v, g, beta, s0.astype(f32))
    o = _rmsnorm_j(o, out_norm) * jax.nn.silu(z.reshape(B, L, DN_HEADS, DN_DV).astype(f32))
    o2 = o.reshape(B * L, DN_V).astype(jnp.bfloat16)
    y = matmul(o2, w_out.astype(jnp.bfloat16), tm=_row_tile(B * L), tn=1024, residual=x2, name="dn_out_proj")
    return y.reshape(B, L, D), (s_new, new_buf)


def _alibi_slopes():
    hh = jnp.arange(1, NSA_HEADS + 1, dtype=jnp.float32)
    return jnp.exp2(-8.0 * hh / NSA_HEADS).reshape(NSA_KV, NSA_REP)


def _compress(rows, pe, w1, w2):
    B, L = rows.shape[:2]
    pad = (-L) % CMP_STRIDE
    rows = jnp.pad(rows, ((0, 0), (0, pad), (0, 0), (0, 0)))
    nseg = (L + pad) // CMP_STRIDE
    seg = rows.reshape((B, nseg, CMP_STRIDE) + rows.shape[2:])
    nc = nseg - CMP_RATIO + 1
    blocks = jnp.concatenate([seg[:, r:r + nc] for r in range(CMP_RATIO)], axis=2)
    hid = jax.nn.silu(jnp.einsum('bnlgd,ldh->bngh', blocks + pe[:, None, :], w1))
    return jnp.einsum('bngh,hd->bngd', hid, w2)


def _cmp_to_slc(nc, nsb):
    i = jnp.arange(nc)[:, None] * CMP_STRIDE
    j = jnp.arange(nsb)[None, :] * SLC_BLOCK
    ov = jnp.minimum(i + CMP_BLOCK, j + SLC_BLOCK) - jnp.maximum(i, j)
    return jnp.maximum(ov, 0).astype(jnp.float32) / CMP_STRIDE


def _slc_blocks(rows):
    B, L = rows.shape[:2]
    pad = (-L) % SLC_BLOCK
    rows = jnp.pad(rows, ((0, 0), (0, pad), (0, 0), (0, 0)))
    return rows.reshape((B, (L + pad) // SLC_BLOCK, SLC_BLOCK) + rows.shape[2:]).transpose(0, 3, 1, 2, 4)


def _nsa_core(q, gates, tq, kc, vc, ec, ks, vs, kw, vw, tw):
    f32 = jnp.float32
    slopes = _alibi_slopes()
    qf = q.astype(f32) * (NSA_DK ** -0.5)
    s = jnp.einsum('bqgrd,bngd->bqgrn', qf, kc.astype(f32))
    dist = (tq[:, None] - ec[None, :]).astype(f32)
    s = s - slopes[None, None, :, :, None] * dist[None, :, None, None, :]
    p_c = _masked_softmax(s, (ec[None, :] <= tq[:, None])[None, :, None, None, :])
    o_c = jnp.einsum('bqgrn,bngd->bqgrd', p_c, vc.astype(f32))
    nsb = ks.shape[2]
    imp = jnp.einsum('bqgrn,nj->bqgj', p_c, _cmp_to_slc(kc.shape[1], nsb))
    jb = jnp.arange(nsb)[None, :]
    cur = (tq // SLC_BLOCK)[:, None]
    forced = (jb == 0) | (jb == cur) | (jb == cur - 1)
    valid = jb * SLC_BLOCK <= tq[:, None]
    imp = jnp.where(forced[None, :, None, :], jnp.inf, jnp.where(valid[None, :, None, :], imp, -jnp.inf))
    _, idx = lax.top_k(imp, min(SLC_TOPN, nsb))
    B, Q, G = idx.shape[:3]
    bi = jnp.arange(B)[:, None, None, None]
    gi = jnp.arange(G)[None, None, :, None]
    ksel = ks[bi, gi, idx].astype(f32)
    vsel = vs[bi, gi, idx].astype(f32)
    pos = idx[..., None] * SLC_BLOCK + jnp.arange(SLC_BLOCK)
    s = jnp.einsum('bqgrd,bqgnkd->bqgrnk', qf, ksel)
    dsel = (tq[None, :, None, None, None] - pos).astype(f32)[:, :, :, None]
    s = s - slopes[None, None, :, :, None, None] * dsel
    smask = (pos <= tq[None, :, None, None, None]).reshape(B, Q, G, 1, -1)
    p_s = _masked_softmax(s.reshape(B, Q, G, NSA_REP, -1), smask)
    o_s = jnp.einsum('bqgrm,bqgmd->bqgrd', p_s, vsel.reshape(B, Q, G, -1, vsel.shape[-1]))
    s = jnp.einsum('bqgrd,bwgd->bqgrw', qf, kw.astype(f32))
    dw = tq[:, None] - tw[None, :]
    s = s - slopes[None, None, :, :, None] * dw.astype(f32)[None, :, None, None, :]
    wmask = (dw >= 0) & (dw < WINDOW) & (tw[None, :] >= 0)
    p_w = _masked_softmax(s, wmask[None, :, None, None, :])
    o_w = jnp.einsum('bqgrw,bwgd->bqgrd', p_w, vw.astype(f32))
    gf = gates.astype(f32)
    return gf[..., 0:1] * o_c + gf[..., 1:2] * o_s + gf[..., 2:3] * o_w


def _nsa_project(x, norm_w, w_in, q_norm, kn_s, kn_w):
    B, L, D = x.shape
    x2 = x.reshape(B * L, D)
    h = rmsnorm_bf16(x2, norm_w, tm=min(256, B * L))
    offs = np.cumsum((0,) + NSA_SPLITS)
    tm = _row_tile(B * L)

    def seg(i, tn, name):
        w = w_in[:, offs[i]:offs[i + 1]].astype(jnp.bfloat16)
        return matmul(h, w, tm=tm, tn=tn, name=name)

    q = seg(0, 1024, "nsa_q_proj")
    kc = seg(1, 768, "nsa_kc_proj")
    vc = seg(2, 512, "nsa_vc_proj")
    ks = seg(3, 768, "nsa_ks_proj")
    vs = seg(4, 512, "nsa_vs_proj")
    kw = seg(5, 768, "nsa_kw_proj")
    vw = seg(6, 512, "nsa_vw_proj")
    wg = jnp.pad(w_in[:, offs[7]:offs[8]], ((0, 0), (0, LANE - 3 * NSA_HEADS))).astype(jnp.bfloat16)
    gl = matmul(h, wg, tm=tm, tn=LANE, name="nsa_gate_proj")[:, :3 * NSA_HEADS]
    z = seg(8, 1024, "nsa_z_proj")
    q = _rmsnorm_j(q.reshape(B, L, NSA_KV, NSA_REP, NSA_DK), q_norm)
    kc = kc.reshape(B, L, NSA_KV, NSA_DK)
    vc = vc.reshape(B, L, NSA_KV, NSA_DV)
    ks = _rmsnorm_j(ks.reshape(B, L, NSA_KV, NSA_DK), kn_s)
    vs = vs.reshape(B, L, NSA_KV, NSA_DV)
    kw = _rmsnorm_j(kw.reshape(B, L, NSA_KV, NSA_DK), kn_w)
    vw = vw.reshape(B, L, NSA_KV, NSA_DV)
    gates = jax.nn.sigmoid(gl).reshape(B, L, NSA_KV, NSA_REP, 3)
    return x2, q, kc, vc, ks, vs, kw, vw, gates, z


def _nsa_compressed(kc_rows, vc_rows, kn_c, pe_k, w1_k, w2_k, pe_v, w1_v, w2_v):
    kc = _rmsnorm_j(_compress(kc_rows, pe_k, w1_k, w2_k), kn_c)
    vc = _compress(vc_rows, pe_v, w1_v, w2_v)
    ec = jnp.arange(kc.shape[1]) * CMP_STRIDE + CMP_BLOCK - 1
    return kc, vc, ec


def _nsa_out(x2, o, z, w_out, shape):
    og = (o * jax.nn.silu(z)).astype(jnp.bfloat16)
    y = matmul(og, w_out.astype(jnp.bfloat16), tm=_row_tile(og.shape[0]), tn=1024, residual=x2, name="nsa_out_proj")
    return y.reshape(shape)


def _nsa_prompt(x, norm_w, w_in, q_norm, kn_c, kn_s, kn_w, pe_k, w1_k, w2_k, pe_v, w1_v, w2_v, w_out):
    B, T, _ = x.shape
    x2, q, kc_r, vc_r, ks_r, vs_r, kw_r, vw_r, gates, z = _nsa_project(x, norm_w, w_in, q_norm, kn_s, kn_w)
    kc, vc, ec = _nsa_compressed(kc_r, vc_r, kn_c, pe_k, w1_k, w2_k, pe_v, w1_v, w2_v)
    ks, vs = _slc_blocks(ks_r), _slc_blocks(vs_r)
    pw = ((0, 0), (WINDOW, 0), (0, 0), (0, 0))
    kw_pad, vw_pad = jnp.pad(kw_r, pw), jnp.pad(vw_r, pw)
    qb = min(NSA_QBLOCK, T)
    nb = T // qb

    def block(xs):
        i, q_i, g_i = xs
        start = i * qb
        tq = start + jnp.arange(qb)
        kw_i = lax.dynamic_slice_in_dim(kw_pad, start, WINDOW + qb, axis=1)
        vw_i = lax.dynamic_slice_in_dim(vw_pad, start, WINDOW + qb, axis=1)
        tw = start - WINDOW + jnp.arange(WINDOW + qb)
        return _nsa_core(q_i, g_i, tq, kc, vc, ec, ks, vs, kw_i, vw_i, tw)

    qs = q.reshape(B, nb, qb, NSA_KV, NSA_REP, NSA_DK).swapaxes(0, 1)
    gs = gates.reshape(B, nb, qb, NSA_KV, NSA_REP, 3).swapaxes(0, 1)
    o = lax.map(block, (jnp.arange(nb), qs, gs))
    o = o.swapaxes(0, 1).reshape(B * T, NSA_O)
    y = _nsa_out(x2, o, z, w_out, x.shape)
    wl = min(WINDOW, T)
    return y, (kc_r, vc_r, ks_r, vs_r, kw_r[:, T - wl:], vw_r[:, T - wl:])


def _gather_past(pool, page_table):
    g = pool[page_table]
    return g.reshape((g.shape[0], g.shape[1] * g.shape[2]) + g.shape[3:])


def _nsa_sample(x, ck, cv, sk, sv, wk_buf, wv_buf, page_table, norm_w,
                w_in, q_norm, kn_c, kn_s, kn_w, pe_k, w1_k, w2_k, pe_v, w1_v, w2_v, w_out):
    B, S, _ = x.shape
    P = page_table.shape[1] * ck.shape[1]
    x2, q, kc_r, vc_r, ks_r, vs_r, kw_r, vw_r, gates, z = _nsa_project(x, norm_w, w_in, q_norm, kn_s, kn_w)
    kc_all = jnp.concatenate([_gather_past(ck, page_table), kc_r], axis=1)
    vc_all = jnp.concatenate([_gather_past(cv, page_table), vc_r], axis=1)
    kc, vc, ec = _nsa_compressed(kc_all, vc_all, kn_c, pe_k, w1_k, w2_k, pe_v, w1_v, w2_v)
    ks = _slc_blocks(jnp.concatenate([_gather_past(sk, page_table), ks_r], axis=1))
    vs = _slc_blocks(jnp.concatenate([_gather_past(sv, page_table), vs_r], axis=1))
    kw = jnp.concatenate([wk_buf, kw_r], axis=1)
    vw = jnp.concatenate([wv_buf, vw_r], axis=1)
    wl = wk_buf.shape[1]
    tw = P - wl + jnp.arange(wl + S)
    tq = P + jnp.arange(S)
    o = _nsa_core(q, gates, tq, kc, vc, ec, ks, vs, kw, vw, tw).reshape(B * S, NSA_O)
    y = _nsa_out(x2, o, z, w_out, x.shape)
    return y, (kc_r, vc_r, ks_r, vs_r, kw[:, S:], vw[:, S:])


def kernel(x_prompt, x_sample, state_delta, state_conv, cache_cmp_k, cache_cmp_v, cache_slc_k, cache_slc_v, cache_win_k, cache_win_v, page_table, norm_dn, w_in_dn, conv_w_dn, a_log_dn, dt_bias_dn, out_norm_dn, w_out_dn, norm_nsa, w_in_nsa, q_norm_nsa, k_norm_cmp, k_norm_slc, k_norm_win, cmp_pe_k, cmp_w1_k, cmp_w2_k, cmp_pe_v, cmp_w1_v, cmp_w2_v, w_out_nsa):
    xp, xs = x_prompt, x_sample
    B = xp.shape[0]
    dw = (norm_dn[0], w_in_dn[0], conv_w_dn[0], a_log_dn[0], dt_bias_dn[0], out_norm_dn[0], w_out_dn[0])
    buf0 = jnp.zeros((B, DN_CONV - 1, DN_CONV_DIM), xp.dtype)
    st0 = jnp.zeros((B, DN_HEADS, DN_DK, DN_DV), jnp.float32)
    xp, p_dn = _deltanet_mixer(xp, buf0, st0, *dw)
    xs, s_dn = _deltanet_mixer(xs, state_conv[0], state_delta[0], *dw)
    nw = (norm_nsa[0], w_in_nsa[0], q_norm_nsa[0], k_norm_cmp[0], k_norm_slc[0], k_norm_win[0],
          cmp_pe_k[0], cmp_w1_k[0], cmp_w2_k[0], cmp_pe_v[0], cmp_w1_v[0], cmp_w2_v[0], w_out_nsa[0])
    xp, p_nsa = _nsa_prompt(xp, *nw)
    xs, s_nsa = _nsa_sample(xs, cache_cmp_k[0], cache_cmp_v[0], cache_slc_k[0], cache_slc_v[0],
                            cache_win_k[0], cache_win_v[0], page_table, *nw)
    return ((xp, xs, p_dn[0][None], p_dn[1][None]) + tuple(t[None] for t in p_nsa)
            + (s_dn[0][None], s_dn[1][None]) + tuple(t[None] for t in s_nsa))
```

```python
import functools

import jax
import jax.numpy as jnp
import numpy as np
from jax import lax
from jax.experimental import pallas as pl
from jax.experimental.pallas import tpu as pltpu

D_MODEL = 4096
EPS = 1e-6

DN_HEADS = 32
DN_DK = 128
DN_DV = 128
DN_CONV = 4
DN_CHUNK = 64

NSA_HEADS = 32
NSA_KV = 4
NSA_REP = NSA_HEADS // NSA_KV
NSA_DK = 192
NSA_DV = 128
CMP_BLOCK = 32
CMP_STRIDE = 16
CMP_RATIO = CMP_BLOCK // CMP_STRIDE
CMP_HIDDEN = 256
SLC_BLOCK = 64
SLC_TOPN = 16
WINDOW = 512
NSA_Q = NSA_HEADS * NSA_DK
NSA_KW = NSA_KV * NSA_DK
NSA_VW = NSA_KV * NSA_DV
NSA_O = NSA_HEADS * NSA_DV
NSA_SPLITS = (NSA_Q, NSA_KW, NSA_VW, NSA_KW, NSA_VW, NSA_KW, NSA_VW, 3 * NSA_HEADS, NSA_O)

VMEM_LIMIT_BYTES = 56 * 1024 * 1024
LANE = 128


def _cparams(*sem):
    return pltpu.CompilerParams(dimension_semantics=sem, vmem_limit_bytes=VMEM_LIMIT_BYTES)


def _rmsnorm_kernel(x_ref, w_ref, o_ref):
    x = x_ref[...]
    ms = jnp.mean(x * x, axis=-1, keepdims=True)
    o_ref[...] = (x * lax.rsqrt(ms + EPS) * w_ref[...]).astype(o_ref.dtype)


def rmsnorm_bf16(x, w, *, tm):
    m, d = x.shape
    return pl.pallas_call(
        _rmsnorm_kernel,
        grid=(m // tm,),
        in_specs=[pl.BlockSpec((tm, d), lambda i: (i, 0)), pl.BlockSpec((1, d), lambda i: (0, 0))],
        out_specs=pl.BlockSpec((tm, d), lambda i: (i, 0)),
        out_shape=jax.ShapeDtypeStruct((m, d), jnp.bfloat16),
        compiler_params=_cparams("parallel"),
        name="rmsnorm_bf16",
    )(x, w.reshape(1, d))


def _mm_kernel(a_ref, b_ref, o_ref):
    o_ref[...] = jnp.dot(a_ref[...], b_ref[...], preferred_element_type=jnp.float32).astype(o_ref.dtype)


def _mm_res_kernel(a_ref, b_ref, r_ref, o_ref):
    acc = jnp.dot(a_ref[...], b_ref[...], preferred_element_type=jnp.float32)
    o_ref[...] = r_ref[...] + acc


def matmul(a, b, *, tm, tn, residual=None, out_dtype=jnp.float32, name="matmul"):
    m, k = a.shape
    _, n = b.shape
    assert m % tm == 0 and n % tn == 0
    in_specs = [pl.BlockSpec((tm, k), lambda i, j: (i, 0)), pl.BlockSpec((k, tn), lambda i, j: (0, j))]
    args = [a, b]
    kern = _mm_kernel
    if residual is not None:
        in_specs.append(pl.BlockSpec((tm, tn), lambda i, j: (i, j)))
        args.append(residual)
        kern = _mm_res_kernel
    return pl.pallas_call(
        kern,
        grid=(m // tm, n // tn),
        in_specs=in_specs,
        out_specs=pl.BlockSpec((tm, tn), lambda i, j: (i, j)),
        out_shape=jax.ShapeDtypeStruct((m, n), out_dtype),
        compiler_params=_cparams("parallel", "parallel"),
        name=name,
    )(*args)


def _row_tile(m):
    return 1024 if m % 1024 == 0 else m


def _dot(a, b):
    return jnp.dot(a.astype(jnp.bfloat16), b.astype(jnp.bfloat16), preferred_element_type=jnp.float32)


def _dot_nt(a, b):
    return lax.dot_general(a.astype(jnp.bfloat16), b.astype(jnp.bfloat16), (((1,), (1,)), ((), ())),
                           preferred_element_type=jnp.float32)


def _dot_tn(a, b):
    return lax.dot_general(a.astype(jnp.bfloat16), b.astype(jnp.bfloat16), (((0,), (0,)), ((), ())),
                           preferred_element_type=jnp.float32)


def _split_bf16(x):
    hi = x.astype(jnp.bfloat16)
    lo = (x - hi.astype(jnp.float32)).astype(jnp.bfloat16)
    return hi, lo


def _dn_gate_kernel(h_ref, wt_ref, alog_ref, dtb_ref, o_ref, *, heads):
    r = lax.dot_general(wt_ref[...], h_ref[...], (((1,), (1,)), ((), ())),
                        preferred_element_type=jnp.float32)
    b = r[:heads]
    a = r[heads:] + dtb_ref[...]
    softplus = jnp.maximum(a, 0.0) + jnp.log(1.0 + jnp.exp(-jnp.abs(a)))
    o_ref[0:heads, :] = 1.0 / (1.0 + jnp.exp(-b))
    o_ref[heads:2 * heads, :] = -jnp.exp(alog_ref[...]) * softplus


def dn_gates(h, w_ba_t, a_log, dt_bias, *, tm):
    m, d = h.shape
    heads = a_log.shape[0]
    return pl.pallas_call(
        functools.partial(_dn_gate_kernel, heads=heads),
        grid=(m // tm,),
        in_specs=[pl.BlockSpec((tm, d), lambda i: (i, 0)),
                  pl.BlockSpec((2 * heads, d), lambda i: (0, 0)),
                  pl.BlockSpec((heads, 1), lambda i: (0, 0)),
                  pl.BlockSpec((heads, 1), lambda i: (0, 0))],
        out_specs=pl.BlockSpec((2 * heads, tm), lambda i: (0, i)),
        out_shape=jax.ShapeDtypeStruct((2 * heads, m), jnp.float32),
        compiler_params=_cparams("parallel"),
        name="dn_gates",
    )(h, w_ba_t, a_log.reshape(heads, 1), dt_bias.reshape(heads, 1))


_DN_TAIL = 8
_DN_TB = 256
_DN_TMIN = 2 * DN_CHUNK
_DN_HB = 4


def _dn_chunk_prep(q, k, g_row, b_row):
    c = q.shape[0]
    ii = lax.broadcasted_iota(jnp.int32, (c, c), 0)
    jj = lax.broadcasted_iota(jnp.int32, (c, c), 1)
    lower, strict, eye = ii >= jj, ii > jj, ii == jj
    g_b = jnp.broadcast_to(g_row, (c, c))
    b_b = jnp.broadcast_to(b_row, (c, c))
    b_col = jnp.sum(jnp.where(eye, b_b, 0.0), axis=1, keepdims=True)
    g_col = jnp.sum(jnp.where(eye, g_b, 0.0), axis=1, keepdims=True)
    gc_col = jnp.sum(jnp.where(lower, g_b, 0.0), axis=1, keepdims=True)
    gc_row = jnp.sum(jnp.where(ii <= jj, g_col, 0.0), axis=0, keepdims=True)
    g_last = jnp.sum(g_row, axis=1, keepdims=True)
    decay = jnp.exp(jnp.where(lower, gc_col - gc_row, -jnp.inf))
    kk = _dot_nt(k, k)
    qk = _dot_nt(q, k)
    lmat = jnp.where(strict, kk * decay, 0.0) * b_col
    attn = qk * decay
    return dict(x=-lmat, p=eye.astype(jnp.float32) - lmat, attn=attn, b_row=b_row,
                e_row=jnp.exp(gc_row), e_col=jnp.exp(gc_col), c_col=jnp.exp(g_last - gc_col),
                e_last=jnp.exp(g_last))


def _dn_inverse_levels(items, c):
    m = 2
    while m < c:
        for it in items:
            it["x"] = _dot(it["x"], it["x"])
        for it in items:
            it["p"] = it["p"] + _dot(it["p"], it["x"])
        m *= 2


def _dn_kernel(q_ref, k_ref, v_ref, z_ref, beta_ref, g_ref, cw_ref, cs_ref, s0_ref, onw_ref,
               o_ref, s_out_ref, xpad, s_scr, *, tb, hb):
    n = pl.program_id(2)
    c = DN_CHUNK
    d = DN_DK
    nc = tb // c

    @pl.when(n == 0)
    def _init():
        xpad[:, 0:_DN_TAIL, :] = cs_ref[0]
        s_scr[...] = s0_ref[0]

    def conv(i, ref):
        xpad[i, _DN_TAIL:_DN_TAIL + tb, :] = ref[...]
        acc = None
        for j in range(DN_CONV):
            off = _DN_TAIL - (DN_CONV - 1) + j
            term = xpad[i, off:off + tb, :] * cw_ref[i, j:j + 1, :]
            acc = term if acc is None else acc + term
        xpad[i, 0:_DN_TAIL, :] = xpad[i, tb:tb + _DN_TAIL, :]
        return acc * (1.0 / (1.0 + jnp.exp(-acc)))

    qc, kc, vc = conv(0, q_ref), conv(1, k_ref), conv(2, v_ref)
    items = []
    for hh in range(hb):
        sl = slice(hh * d, (hh + 1) * d)
        q = qc[:, sl]
        k = kc[:, sl]
        q = q * (lax.rsqrt(jnp.sum(q * q, axis=-1, keepdims=True) + EPS) * (d ** -0.5))
        k = k * lax.rsqrt(jnp.sum(k * k, axis=-1, keepdims=True) + EPS)
        g_all = g_ref[hh]
        b_all = beta_ref[hh]
        for ci in range(nc):
            rs = slice(ci * c, (ci + 1) * c)
            it = _dn_chunk_prep(q[rs], k[rs], g_all[:, rs], b_all[:, rs])
            it.update(q=q[rs], k=k[rs], v=vc[rs, sl])
            items.append(it)
    _dn_inverse_levels(items, c)
    for it in items:
        inv_b = it["p"] * it["b_row"]
        it["u"] = _dot(inv_b, it["v"])
        it["w"] = _dot(inv_b * it["e_row"], it["k"])
    states = [s_scr[hh] for hh in range(hb)]
    outs = [[None] * nc for _ in range(hb)]
    for ci in range(nc):
        for hh in range(hb):
            it = items[hh * nc + ci]
            s = states[hh]
            v_new = it["u"] - _dot(it["w"], s)
            outs[hh][ci] = it["e_col"] * _dot(it["q"], s) + _dot(it["attn"], v_new)
            states[hh] = s * it["e_last"] + _dot_tn(it["k"] * it["c_col"], v_new)
    for hh in range(hb):
        sl = slice(hh * d, (hh + 1) * d)
        s_scr[hh] = states[hh]
        o = jnp.concatenate(outs[hh], axis=0) if nc > 1 else outs[hh][0]
        o = o * lax.rsqrt(jnp.mean(o * o, axis=-1, keepdims=True) + EPS) * onw_ref[...]
        z = z_ref[:, sl]
        o_ref[:, sl] = (o * (z * (1.0 / (1.0 + jnp.exp(-z))))).astype(o_ref.dtype)

    @pl.when(n == pl.num_programs(2) - 1)
    def _fin():
        s_out_ref[0] = s_scr[...]


def dn_delta(proj, gates, conv_w, conv_state, s0, out_norm, *, batch, tb, hb):
    rows, width = proj.shape
    d = DN_DK
    heads = width // (4 * d)
    t = rows // batch
    nt = t // tb
    hg = heads // hb
    wd = hb * d
    row_map = lambda off: (lambda b, h, n: (b * nt + n, off * hg + h))
    gate_map = lambda off: (lambda b, h, n: (off * hg + h, 0, b * nt + n))
    return pl.pallas_call(
        functools.partial(_dn_kernel, tb=tb, hb=hb),
        grid=(batch, hg, nt),
        in_specs=[pl.BlockSpec((tb, wd), row_map(0)),
                  pl.BlockSpec((tb, wd), row_map(1)),
                  pl.BlockSpec((tb, wd), row_map(2)),
                  pl.BlockSpec((tb, wd), row_map(3)),
                  pl.BlockSpec((hb, 1, tb), gate_map(0)),
                  pl.BlockSpec((hb, 1, tb), gate_map(1)),
                  pl.BlockSpec((3, DN_CONV, wd), lambda b, h, n: (0, 0, h)),
                  pl.BlockSpec((1, 3, _DN_TAIL, wd), lambda b, h, n: (b, 0, 0, h)),
                  pl.BlockSpec((1, hb, d, d), lambda b, h, n: (b, h, 0, 0)),
                  pl.BlockSpec((1, d), lambda b, h, n: (0, 0))],
        out_specs=[pl.BlockSpec((tb, wd), lambda b, h, n: (b * nt + n, h)),
                   pl.BlockSpec((1, hb, d, d), lambda b, h, n: (b, h, 0, 0))],
        out_shape=[jax.ShapeDtypeStruct((rows, heads * d), jnp.bfloat16),
                   jax.ShapeDtypeStruct(s0.shape, jnp.float32)],
        scratch_shapes=[pltpu.VMEM((3, tb + _DN_TAIL, wd), jnp.float32),
                        pltpu.VMEM((hb, d, d), jnp.float32)],
        compiler_params=_cparams("parallel", "parallel", "arbitrary"),
        name="dn_delta",
    )(proj, proj, proj, proj, gates, gates, conv_w, conv_state, s0, out_norm.reshape(1, d))


def _deltanet_mixer(x, conv_buf, s0, norm_w, w_in, conv_w, a_log, dt_bias, out_norm, w_out):
    B, L, D = x.shape
    H = a_log.shape[0]
    cdim = conv_w.shape[1]
    hd = cdim // 3
    x2 = x.reshape(B * L, D)
    h = rmsnorm_bf16(x2, norm_w, tm=min(256, B * L))
    w_main = w_in[:, :cdim + hd].astype(jnp.bfloat16)
    w_ba_t = w_in[:, cdim + hd:].T.astype(jnp.bfloat16)
    proj = matmul(h, w_main, tm=_row_tile(B * L), tn=1024, name="dn_in_proj")
    gates = dn_gates(h, w_ba_t, a_log, dt_bias, tm=_row_tile(B * L))
    new_buf = proj[:, :cdim].reshape(B, L, cdim)[:, L - (DN_CONV - 1):]
    T = -(-L // _DN_TMIN) * _DN_TMIN
    if T != L:
        proj = jnp.pad(proj.reshape(B, L, -1), ((0, 0), (0, T - L), (0, 0))).reshape(B * T, -1)
        gates = jnp.pad(gates.reshape(2 * H, B, L), ((0, 0), (0, 0), (0, T - L))).reshape(2 * H, B * T)
    cw = conv_w.reshape(DN_CONV, 3, hd).transpose(1, 0, 2)
    cs = conv_buf.reshape(B, DN_CONV - 1, 3, hd).transpose(0, 2, 1, 3)
    cs = jnp.pad(cs, ((0, 0), (0, 0), (_DN_TAIL - (DN_CONV - 1), 0), (0, 0)))
    tb = min(_DN_TB, T)
    o, s_new = dn_delta(proj, gates.reshape(2 * H, 1, B * T), cw, cs, s0.astype(jnp.float32), out_norm,
                        batch=B, tb=tb, hb=_DN_HB)
    if T != L:
        o = o.reshape(B, T, -1)[:, :L].reshape(B * L, -1)
    y = matmul(o, w_out.astype(jnp.bfloat16), tm=_row_tile(B * L), tn=1024, residual=x2, name="dn_out_proj")
    return y.reshape(B, L, D), (s_new, new_buf)


def _alibi_slopes():
    hh = jnp.arange(1, NSA_HEADS + 1, dtype=jnp.float32)
    return jnp.exp2(-8.0 * hh / NSA_HEADS).reshape(NSA_KV, NSA_REP)


PAGE_ROWS = 128
NSA_DKP = 256
_POS_LANE = NSA_DK
_SEL_LANE = NSA_DK + 4
_SEL_LANES = 32
_MASK_BIG = 2.0 ** 100
_NEG = -1e30
_NSA_TQ = 128
_NSA_TK = 256
_CMP_PAGES = 16


def _slope_table():
    s = _alibi_slopes()
    hi = s.astype(jnp.bfloat16).astype(jnp.float32)
    lo = (s - hi).astype(jnp.bfloat16).astype(jnp.float32)
    tab = jnp.stack([64.0 * hi, hi, 64.0 * lo, lo], axis=-1)
    return jnp.pad(tab, ((0, 0), (0, 0), (0, NSA_DKP - NSA_DK - 4)))


def _q_prep_kernel(q_ref, w_ref, tab_ref, o_ref):
    tq = q_ref.shape[0]
    for r in range(NSA_REP):
        x = q_ref[:, r * NSA_DK:(r + 1) * NSA_DK]
        y = x * lax.rsqrt(jnp.mean(x * x, axis=-1, keepdims=True) + EPS) * (w_ref[...] * (NSA_DK ** -0.5))
        o_ref[0, r, :, 0:NSA_DK] = y.astype(o_ref.dtype)
        o_ref[0, r, :, NSA_DK:NSA_DKP] = jnp.broadcast_to(tab_ref[0, r:r + 1, :], (tq, NSA_DKP - NSA_DK)).astype(o_ref.dtype)


def nsa_q_prep(q_raw, q_norm, *, tq):
    rows = q_raw.shape[0]
    return pl.pallas_call(
        _q_prep_kernel,
        grid=(rows // tq, NSA_KV),
        in_specs=[pl.BlockSpec((tq, NSA_REP * NSA_DK), lambda i, g: (i, g)),
                  pl.BlockSpec((1, NSA_DK), lambda i, g: (0, 0)),
                  pl.BlockSpec((1, NSA_REP, NSA_DKP - NSA_DK), lambda i, g: (g, 0, 0))],
        out_specs=pl.BlockSpec((1, NSA_REP, tq, NSA_DKP), lambda i, g: (g, 0, i, 0)),
        out_shape=jax.ShapeDtypeStruct((NSA_KV, NSA_REP, rows, NSA_DKP), jnp.bfloat16),
        compiler_params=_cparams("parallel", "parallel"),
        name="nsa_q_prep",
    )(q_raw, q_norm.reshape(1, NSA_DK), _slope_table())


def _k_prep_kernel(k_ref, w_ref, on_ref, oh_ref, *, seq, onehot):
    tk = k_ref.shape[0]
    pos = (pl.program_id(0) * tk) % seq + lax.broadcasted_iota(jnp.int32, (tk, NSA_DKP - NSA_DK), 0)
    lane = lax.broadcasted_iota(jnp.int32, (tk, NSA_DKP - NSA_DK), 1)
    blk, off = pos // SLC_BLOCK, pos % SLC_BLOCK
    extra = jnp.where((lane == 0) | (lane == 2), blk, jnp.where((lane == 1) | (lane == 3), off, 0))
    if onehot:
        extra = jnp.where((lane >= 4) & (lane - 4 == blk), 1, extra)
    extra = extra.astype(jnp.float32)
    for g in range(NSA_KV):
        x = k_ref[:, g * NSA_DK:(g + 1) * NSA_DK]
        x = x * lax.rsqrt(jnp.mean(x * x, axis=-1, keepdims=True) + EPS) * w_ref[...]
        on_ref[:, g * NSA_DK:(g + 1) * NSA_DK] = x
        oh_ref[g, :, 0:NSA_DK] = x.astype(oh_ref.dtype)
        oh_ref[g, :, NSA_DK:NSA_DKP] = extra.astype(oh_ref.dtype)


def nsa_k_prep(k_raw, k_norm, *, seq, tk, onehot):
    rows = k_raw.shape[0]
    return pl.pallas_call(
        functools.partial(_k_prep_kernel, seq=seq, onehot=onehot),
        grid=(rows // tk,),
        in_specs=[pl.BlockSpec((tk, NSA_KW), lambda i: (i, 0)), pl.BlockSpec((1, NSA_DK), lambda i: (0, 0))],
        out_specs=[pl.BlockSpec((tk, NSA_KW), lambda i: (i, 0)),
                   pl.BlockSpec((NSA_KV, tk, NSA_DKP), lambda i: (0, i, 0))],
        out_shape=[jax.ShapeDtypeStruct((rows, NSA_KW), jnp.float32),
                   jax.ShapeDtypeStruct((NSA_KV, rows, NSA_DKP), jnp.bfloat16)],
        compiler_params=_cparams("parallel"),
        name="nsa_k_prep",
    )(k_raw, k_norm.reshape(1, NSA_DK))


def _v_prep_kernel(v_ref, o_ref):
    for g in range(NSA_KV):
        o_ref[g] = v_ref[:, g * NSA_DV:(g + 1) * NSA_DV].astype(o_ref.dtype)


def nsa_v_prep(v_raw, *, tk):
    rows = v_raw.shape[0]
    return pl.pallas_call(
        _v_prep_kernel,
        grid=(rows // tk,),
        in_specs=[pl.BlockSpec((tk, NSA_VW), lambda i: (i, 0))],
        out_specs=pl.BlockSpec((NSA_KV, tk, NSA_DV), lambda i: (0, i, 0)),
        out_shape=jax.ShapeDtypeStruct((NSA_KV, rows, NSA_DV), jnp.bfloat16),
        compiler_params=_cparams("parallel"),
        name="nsa_v_prep",
    )(v_raw)


def _compress_partials_kernel(tab_ref, *refs, d, pages):
    x_refs, (pe_ref, w1_ref, a_ref, b_ref) = refs[:pages], refs[pages:]
    seg = x_refs[0].shape[0]
    acc_a = acc_b = None
    for l in range(CMP_STRIDE):
        x = jnp.concatenate([r[:, l, :] for r in x_refs], axis=0) if pages > 1 else x_refs[0][:, l, :]
        xg = jnp.concatenate([x[:, g * d:(g + 1) * d] for g in range(NSA_KV)], axis=0)
        ta = jnp.dot((xg + pe_ref[l:l + 1, :]).astype(jnp.bfloat16), w1_ref[l], preferred_element_type=jnp.float32)
        tb = jnp.dot((xg + pe_ref[CMP_STRIDE + l:CMP_STRIDE + l + 1, :]).astype(jnp.bfloat16),
                     w1_ref[CMP_STRIDE + l], preferred_element_type=jnp.float32)
        acc_a = ta if acc_a is None else acc_a + ta
        acc_b = tb if acc_b is None else acc_b + tb
    n = pages * seg
    for g in range(NSA_KV):
        a_ref[0, g] = acc_a[g * n:(g + 1) * n]
        b_ref[0, g] = acc_b[g * n:(g + 1) * n]


def nsa_compress_partials(pool3, table, pe, w1, *, pages):
    d = pe.shape[1]
    batch, npg = table.shape
    seg = PAGE_ROWS // CMP_STRIDE
    x_spec = lambda k: pl.BlockSpec((seg, CMP_STRIDE, NSA_KV * d), lambda b, i, tab: (tab[b, i * pages + k], 0, 0))
    out_spec = pl.BlockSpec((1, NSA_KV, pages * seg, CMP_HIDDEN), lambda b, i, tab: (b, 0, i, 0))
    out_sds = jax.ShapeDtypeStruct((batch, NSA_KV, npg * seg, CMP_HIDDEN), jnp.float32)
    return pl.pallas_call(
        functools.partial(_compress_partials_kernel, d=d, pages=pages),
        grid_spec=pltpu.PrefetchScalarGridSpec(
            num_scalar_prefetch=1,
            grid=(batch, npg // pages),
            in_specs=[x_spec(k) for k in range(pages)]
            + [pl.BlockSpec((CMP_BLOCK, d), lambda b, i, tab: (0, 0)),
               pl.BlockSpec((CMP_BLOCK, d, CMP_HIDDEN), lambda b, i, tab: (0, 0, 0))],
            out_specs=[out_spec, out_spec]),
        out_shape=[out_sds, out_sds],
        compiler_params=_cparams("parallel", "arbitrary"),
        name="nsa_compress_partials",
    )(table, *([pool3] * pages), pe, w1.astype(jnp.bfloat16))


def _cmp_extra_lanes(nrows, width):
    n = lax.broadcasted_iota(jnp.int32, (nrows, width), 0)
    lane = lax.broadcasted_iota(jnp.int32, (nrows, width), 1)
    ec = n * CMP_STRIDE + CMP_BLOCK - 1
    blk, off = ec // SLC_BLOCK, ec % SLC_BLOCK
    pl_ = lane - _POS_LANE
    return jnp.where((pl_ == 0) | (pl_ == 2), blk, jnp.where((pl_ == 1) | (pl_ == 3), off, 0)).astype(jnp.float32)


def _compress_finish_kernel(a_ref, b_ref, bn_ref, w2_ref, kn_ref, o_ref, *, is_key, has_new, alibi_lanes):
    ns = a_ref.shape[2]
    row = lax.broadcasted_iota(jnp.int32, (ns, 1), 0)
    for g in range(NSA_KV):
        nxt = pltpu.roll(b_ref[0, g], ns - 1, 0)
        if has_new:
            nxt = jnp.where(row == ns - 1, bn_ref[0, g, 0:1, :], nxt)
        hid_pre = a_ref[0, g] + nxt
        hid = hid_pre * (1.0 / (1.0 + jnp.exp(-hid_pre)))
        out = jnp.dot(hid.astype(jnp.bfloat16), w2_ref[...], preferred_element_type=jnp.float32)
        if is_key:
            ms = jnp.sum(out * out, axis=-1, keepdims=True) * (1.0 / NSA_DK)
            out = out * lax.rsqrt(ms + EPS) * kn_ref[...]
            if alibi_lanes:
                out = out + _cmp_extra_lanes(ns, out.shape[1])
        o_ref[0, g] = out.astype(o_ref.dtype)


def nsa_compress_finish(part_a, part_b, part_b_new, w2, kn, *, is_key, alibi_lanes):
    batch, _, ns, _ = part_a.shape
    d = w2.shape[1]
    dp = NSA_DKP if is_key else d
    has_new = part_b_new is not None
    if not has_new:
        part_b_new = jnp.zeros((batch, NSA_KV, 8, CMP_HIDDEN), jnp.float32)
    w2p = jnp.pad(w2, ((0, 0), (0, dp - d))).astype(jnp.bfloat16)
    knp = jnp.pad(kn, (0, dp - d)).reshape(1, dp)
    blk = lambda n: pl.BlockSpec((1, NSA_KV, n, CMP_HIDDEN), lambda b: (b, 0, 0, 0))
    return pl.pallas_call(
        functools.partial(_compress_finish_kernel, is_key=is_key, has_new=has_new, alibi_lanes=alibi_lanes),
        grid=(batch,),
        in_specs=[blk(ns), blk(ns), blk(part_b_new.shape[2]),
                  pl.BlockSpec((CMP_HIDDEN, dp), lambda b: (0, 0)),
                  pl.BlockSpec((1, dp), lambda b: (0, 0))],
        out_specs=pl.BlockSpec((1, NSA_KV, ns, dp), lambda b: (b, 0, 0, 0)),
        out_shape=jax.ShapeDtypeStruct((batch, NSA_KV, ns, dp), jnp.bfloat16),
        compiler_params=_cparams("parallel"),
        name="nsa_compress_finish_k" if is_key else "nsa_compress_finish_v",
    )(part_a, part_b, part_b_new, w2p, knp)


def _split_dot(x, m_bf16):
    hi, lo = _split_bf16(x)
    f = functools.partial(jnp.dot, preferred_element_type=jnp.float32)
    return f(hi, m_bf16) + f(lo, m_bf16)


def _topn_mask(imp, t_col, nsb, topn):
    jb = lax.broadcasted_iota(jnp.int32, imp.shape, 1)
    cur = t_col // SLC_BLOCK
    forced = (jb == 0) | (jb == cur) | (jb == cur - 1)
    valid = (jb * SLC_BLOCK <= t_col) & (jb < nsb)
    impm = jnp.where(forced, jnp.inf, jnp.where(valid, imp, -jnp.inf))
    rank = jnp.zeros(imp.shape, jnp.float32)
    for k in range(nsb):
        col = impm[:, k:k + 1]
        beats = (col > impm) | ((col == impm) & (k < jb))
        rank = rank + jnp.where(beats, 1.0, 0.0)
    return (rank < topn) & (jb < nsb)


def _flash_step(q, k, v, mask, carry):
    m, l, acc = carry
    s = _dot_nt(q, k)
    s = jnp.where(mask, s, _NEG)
    m_new = jnp.maximum(m, jnp.max(s, axis=1, keepdims=True))
    alpha = jnp.exp(m - m_new)
    p = jnp.exp(s - m_new)
    l = alpha * l + jnp.sum(p, axis=1, keepdims=True)
    acc = alpha * acc + jnp.dot(p.astype(jnp.bfloat16), v, preferred_element_type=jnp.float32)
    return m_new, l, acc


def _nsa_prompt_kernel(q_ref, kc_ref, vc_ref, ks_ref, vs_ref, kw_ref, vw_ref, gl_ref, z_ref, o_ref,
                       *, tq, tk, seq):
    qi = pl.program_id(2)
    t0 = qi * tq
    rep = NSA_REP
    m_rows = rep * tq
    nc = seq // CMP_STRIDE - CMP_RATIO + 1
    nseg = kc_ref.shape[2]
    nsb = seq // SLC_BLOCK
    q = q_ref[0].reshape(m_rows, NSA_DKP)
    t_tok = t0 + lax.broadcasted_iota(jnp.int32, (tq, 1), 0)
    t_col = t0 + lax.broadcasted_iota(jnp.int32, (rep, tq, 1), 1).reshape(m_rows, 1)

    n_row = lax.broadcasted_iota(jnp.int32, (1, nseg), 1)
    cvalid = (n_row * CMP_STRIDE + CMP_BLOCK - 1 <= t_col) & (n_row < nc)
    s = jnp.where(cvalid, _dot_nt(q, kc_ref[0, 0]), _NEG)
    p = jnp.where(cvalid, jnp.exp(s - jnp.max(s, axis=1, keepdims=True)), 0.0)
    p = p / jnp.maximum(jnp.sum(p, axis=1, keepdims=True), 1e-30)
    o_c = jnp.dot(p.astype(jnp.bfloat16), vc_ref[0, 0], preferred_element_type=jnp.float32)
    psum = jnp.sum(p.reshape(rep, tq, nseg), axis=0)
    ci = lax.broadcasted_iota(jnp.int32, (nseg, _SEL_LANES), 0) * CMP_STRIDE
    cj = lax.broadcasted_iota(jnp.int32, (nseg, _SEL_LANES), 1) * SLC_BLOCK
    ov = jnp.maximum(jnp.minimum(ci + CMP_BLOCK, cj + SLC_BLOCK) - jnp.maximum(ci, cj), 0)
    c2s = (ov.astype(jnp.float32) * (1.0 / CMP_STRIDE)).astype(jnp.bfloat16)
    imp = _split_dot(psum, c2s)

    sel = _topn_mask(imp, t_tok, nsb, min(SLC_TOPN, nsb))
    pen = jnp.where(sel, 0.0, -_MASK_BIG).astype(jnp.bfloat16)
    pi = lax.broadcasted_iota(jnp.int32, (_SEL_LANES, NSA_DKP), 0)
    pj = lax.broadcasted_iota(jnp.int32, (_SEL_LANES, NSA_DKP), 1)
    place = jnp.where(pj == pi + _SEL_LANE, 1.0, 0.0).astype(jnp.bfloat16)
    q_add = jnp.dot(pen, place, preferred_element_type=jnp.float32).astype(jnp.bfloat16)
    q_slc = (q.reshape(rep, tq, NSA_DKP) + q_add[None]).reshape(m_rows, NSA_DKP)

    kpos = lax.broadcasted_iota(jnp.int32, (1, tk), 1)
    init = (jnp.full((m_rows, 1), _NEG, jnp.float32), jnp.zeros((m_rows, 1), jnp.float32),
            jnp.zeros((m_rows, NSA_DV), jnp.float32))

    def slc_body(kt, carry):
        k0 = pl.multiple_of(kt * tk, tk)
        mask = (k0 + kpos) <= t_col
        return _flash_step(q_slc, ks_ref[0, pl.ds(k0, tk), :], vs_ref[0, pl.ds(k0, tk), :], mask, carry)

    _, l_s, acc_s = lax.fori_loop(0, (t0 + tq + tk - 1) // tk, slc_body, init)
    o_s = acc_s / l_s

    def win_body(kt, carry):
        k0 = pl.multiple_of(kt * tk, tk)
        dw = t_col - (k0 + kpos)
        mask = (dw >= 0) & (dw < WINDOW)
        return _flash_step(q, kw_ref[0, pl.ds(k0, tk), :], vw_ref[0, pl.ds(k0, tk), :], mask, carry)

    kt_lo = jnp.maximum(t0 - (WINDOW - 1), 0) // tk
    _, l_w, acc_w = lax.fori_loop(kt_lo, (t0 + tq + tk - 1) // tk, win_body, init)
    o_w = acc_w / l_w

    gates = 1.0 / (1.0 + jnp.exp(-gl_ref[...]))
    for r in range(rep):
        rs = slice(r * tq, (r + 1) * tq)
        o = (gates[:, 3 * r:3 * r + 1] * o_c[rs] + gates[:, 3 * r + 1:3 * r + 2] * o_s[rs]
             + gates[:, 3 * r + 2:3 * r + 3] * o_w[rs])
        z = z_ref[:, r * NSA_DV:(r + 1) * NSA_DV]
        o_ref[:, r * NSA_DV:(r + 1) * NSA_DV] = (o * (z * (1.0 / (1.0 + jnp.exp(-z))))).astype(o_ref.dtype)


def nsa_prompt_attention(q_hm, kcmp, vcmp, ks_hm, vs_hm, kw_hm, vw_hm, gl, z, *, batch, tq, tk):
    rows = z.shape[0]
    seq = rows // batch
    nq = seq // tq
    nseg = kcmp.shape[2]
    row_blk = lambda b, g, i: (b * nq + i, g)
    kv_blk = lambda b, g, i: (g, b, 0)
    return pl.pallas_call(
        functools.partial(_nsa_prompt_kernel, tq=tq, tk=tk, seq=seq),
        grid=(batch, NSA_KV, nq),
        in_specs=[pl.BlockSpec((1, NSA_REP, tq, NSA_DKP), lambda b, g, i: (g, 0, b * nq + i, 0)),
                  pl.BlockSpec((1, 1, nseg, NSA_DKP), lambda b, g, i: (b, g, 0, 0)),
                  pl.BlockSpec((1, 1, nseg, NSA_DV), lambda b, g, i: (b, g, 0, 0)),
                  pl.BlockSpec((1, seq, NSA_DKP), kv_blk),
                  pl.BlockSpec((1, seq, NSA_DV), kv_blk),
                  pl.BlockSpec((1, seq, NSA_DKP), kv_blk),
                  pl.BlockSpec((1, seq, NSA_DV), kv_blk),
                  pl.BlockSpec((tq, LANE), row_blk),
                  pl.BlockSpec((tq, NSA_REP * NSA_DV), row_blk)],
        out_specs=pl.BlockSpec((tq, NSA_REP * NSA_DV), row_blk),
        out_shape=jax.ShapeDtypeStruct((rows, NSA_O), jnp.bfloat16),
        compiler_params=_cparams("parallel", "parallel", "arbitrary"),
        name="nsa_prompt_attention",
    )(q_hm, kcmp, vcmp, ks_hm, vs_hm, kw_hm, vw_hm, gl, z)


def _nsa_in_proj(x, norm_w, w_in):
    B, L, D = x.shape
    x2 = x.reshape(B * L, D)
    h = rmsnorm_bf16(x2, norm_w, tm=min(256, B * L))
    offs = np.cumsum((0,) + NSA_SPLITS)
    tm = _row_tile(B * L)

    def seg(i, tn, name):
        w = w_in[:, offs[i]:offs[i + 1]].astype(jnp.bfloat16)
        return matmul(h, w, tm=tm, tn=tn, name=name)

    q = seg(0, 1024, "nsa_q_proj")
    kc = seg(1, 768, "nsa_kc_proj")
    vc = seg(2, 512, "nsa_vc_proj")
    ks = seg(3, 768, "nsa_ks_proj")
    vs = seg(4, 512, "nsa_vs_proj")
    kw = seg(5, 768, "nsa_kw_proj")
    vw = seg(6, 512, "nsa_vw_proj")
    wg = w_in[:, offs[7]:offs[8]].reshape(D, NSA_KV, 3 * NSA_REP)
    wg = jnp.pad(wg, ((0, 0), (0, 0), (0, LANE - 3 * NSA_REP))).reshape(D, NSA_KV * LANE).astype(jnp.bfloat16)
    gl = matmul(h, wg, tm=tm, tn=NSA_KV * LANE, name="nsa_gate_proj")
    z = seg(8, 1024, "nsa_z_proj")
    return x2, q, kc, vc, ks, vs, kw, vw, gl, z


def _nsa_prompt(x, norm_w, w_in, q_norm, kn_c, kn_s, kn_w, pe_k, w1_k, w2_k, pe_v, w1_v, w2_v, w_out):
    B, T, D = x.shape
    x2, q, kc_r, vc_r, ks_raw, vs_r, kw_raw, vw_r, gl, z = _nsa_in_proj(x, norm_w, w_in)
    tp = min(512, T)
    q_hm = nsa_q_prep(q, q_norm, tq=tp)
    ks_r, ks_hm = nsa_k_prep(ks_raw, kn_s, seq=T, tk=tp, onehot=True)
    kw_r, kw_hm = nsa_k_prep(kw_raw, kn_w, seq=T, tk=tp, onehot=False)
    vs_hm = nsa_v_prep(vs_r, tk=tp)
    vw_hm = nsa_v_prep(vw_r, tk=tp)
    npg = T // PAGE_ROWS
    table = (jnp.arange(B, dtype=jnp.int32)[:, None] * npg + jnp.arange(npg, dtype=jnp.int32)[None, :])
    seg3 = lambda a: a.reshape(B * T // CMP_STRIDE, CMP_STRIDE, a.shape[1])
    ka, kb = nsa_compress_partials(seg3(kc_r), table, pe_k, w1_k, pages=min(_CMP_PAGES, npg))
    va, vb = nsa_compress_partials(seg3(vc_r), table, pe_v, w1_v, pages=min(_CMP_PAGES, npg))
    kcmp = nsa_compress_finish(ka, kb, None, w2_k, kn_c, is_key=True, alibi_lanes=True)
    vcmp = nsa_compress_finish(va, vb, None, w2_v, jnp.zeros((NSA_DV,), jnp.float32), is_key=False, alibi_lanes=False)
    og = nsa_prompt_attention(q_hm, kcmp, vcmp, ks_hm, vs_hm, kw_hm, vw_hm, gl, z, batch=B, tq=_NSA_TQ, tk=_NSA_TK)
    y = matmul(og, w_out.astype(jnp.bfloat16), tm=_row_tile(B * T), tn=1024, residual=x2, name="nsa_out_proj")
    wl = min(WINDOW, T)
    r4 = lambda a, d: a.reshape(B, T, NSA_KV, d)
    return y.reshape(B, T, D), (r4(kc_r, NSA_DK), r4(vc_r, NSA_DV), r4(ks_r, NSA_DK), r4(vs_r, NSA_DV),
                                r4(kw_r, NSA_DK)[:, T - wl:], r4(vw_r, NSA_DV)[:, T - wl:])


def _rmsnorm_j(x, w):
    xf = x.astype(jnp.float32)
    y = xf * lax.rsqrt(jnp.mean(xf * xf, axis=-1, keepdims=True) + EPS)
    return (y * w.astype(jnp.float32)).astype(x.dtype)


def _masked_softmax(s, mask):
    s = jnp.where(mask, s, -jnp.inf)
    m = jnp.max(s, axis=-1, keepdims=True)
    m = jnp.where(jnp.isfinite(m), m, 0.0)
    p = jnp.exp(s - m)
    return p / jnp.maximum(jnp.sum(p, axis=-1, keepdims=True), 1e-30)


def _compress(rows, pe, w1, w2):
    B, L = rows.shape[:2]
    pad = (-L) % CMP_STRIDE
    rows = jnp.pad(rows, ((0, 0), (0, pad), (0, 0), (0, 0)))
    nseg = (L + pad) // CMP_STRIDE
    seg = rows.reshape((B, nseg, CMP_STRIDE) + rows.shape[2:])
    nc = nseg - CMP_RATIO + 1
    blocks = jnp.concatenate([seg[:, r:r + nc] for r in range(CMP_RATIO)], axis=2)
    hid = jax.nn.silu(jnp.einsum('bnlgd,ldh->bngh', blocks + pe[:, None, :], w1))
    return jnp.einsum('bngh,hd->bngd', hid, w2)


def _cmp_to_slc(nc, nsb):
    i = jnp.arange(nc)[:, None] * CMP_STRIDE
    j = jnp.arange(nsb)[None, :] * SLC_BLOCK
    ov = jnp.minimum(i + CMP_BLOCK, j + SLC_BLOCK) - jnp.maximum(i, j)
    return jnp.maximum(ov, 0).astype(jnp.float32) / CMP_STRIDE


def _slc_blocks(rows):
    B, L = rows.shape[:2]
    pad = (-L) % SLC_BLOCK
    rows = jnp.pad(rows, ((0, 0), (0, pad), (0, 0), (0, 0)))
    return rows.reshape((B, (L + pad) // SLC_BLOCK, SLC_BLOCK) + rows.shape[2:]).transpose(0, 3, 1, 2, 4)


def _nsa_core(q, gates, tq, kc, vc, ec, ks, vs, kw, vw, tw):
    f32 = jnp.float32
    slopes = _alibi_slopes()
    qf = q.astype(f32) * (NSA_DK ** -0.5)
    s = jnp.einsum('bqgrd,bngd->bqgrn', qf, kc.astype(f32))
    dist = (tq[:, None] - ec[None, :]).astype(f32)
    s = s - slopes[None, None, :, :, None] * dist[None, :, None, None, :]
    p_c = _masked_softmax(s, (ec[None, :] <= tq[:, None])[None, :, None, None, :])
    o_c = jnp.einsum('bqgrn,bngd->bqgrd', p_c, vc.astype(f32))
    nsb = ks.shape[2]
    imp = jnp.einsum('bqgrn,nj->bqgj', p_c, _cmp_to_slc(kc.shape[1], nsb))
    jb = jnp.arange(nsb)[None, :]
    cur = (tq // SLC_BLOCK)[:, None]
    forced = (jb == 0) | (jb == cur) | (jb == cur - 1)
    valid = jb * SLC_BLOCK <= tq[:, None]
    imp = jnp.where(forced[None, :, None, :], jnp.inf, jnp.where(valid[None, :, None, :], imp, -jnp.inf))
    _, idx = lax.top_k(imp, min(SLC_TOPN, nsb))
    B, Q, G = idx.shape[:3]
    bi = jnp.arange(B)[:, None, None, None]
    gi = jnp.arange(G)[None, None, :, None]
    ksel = ks[bi, gi, idx].astype(f32)
    vsel = vs[bi, gi, idx].astype(f32)
    pos = idx[..., None] * SLC_BLOCK + jnp.arange(SLC_BLOCK)
    s = jnp.einsum('bqgrd,bqgnkd->bqgrnk', qf, ksel)
    dsel = (tq[None, :, None, None, None] - pos).astype(f32)[:, :, :, None]
    s = s - slopes[None, None, :, :, None, None] * dsel
    smask = (pos <= tq[None, :, None, None, None]).reshape(B, Q, G, 1, -1)
    p_s = _masked_softmax(s.reshape(B, Q, G, NSA_REP, -1), smask)
    o_s = jnp.einsum('bqgrm,bqgmd->bqgrd', p_s, vsel.reshape(B, Q, G, -1, vsel.shape[-1]))
    s = jnp.einsum('bqgrd,bwgd->bqgrw', qf, kw.astype(f32))
    dw = tq[:, None] - tw[None, :]
    s = s - slopes[None, None, :, :, None] * dw.astype(f32)[None, :, None, None, :]
    wmask = (dw >= 0) & (dw < WINDOW) & (tw[None, :] >= 0)
    p_w = _masked_softmax(s, wmask[None, :, None, None, :])
    o_w = jnp.einsum('bqgrw,bwgd->bqgrd', p_w, vw.astype(f32))
    gf = gates.astype(f32)
    return gf[..., 0:1] * o_c + gf[..., 1:2] * o_s + gf[..., 2:3] * o_w


def _nsa_project(x, norm_w, w_in, q_norm, kn_s, kn_w):
    B, L, D = x.shape
    x2 = x.reshape(B * L, D)
    h = rmsnorm_bf16(x2, norm_w, tm=min(256, B * L))
    offs = np.cumsum((0,) + NSA_SPLITS)
    tm = _row_tile(B * L)

    def seg(i, tn, name):
        w = w_in[:, offs[i]:offs[i + 1]].astype(jnp.bfloat16)
        return matmul(h, w, tm=tm, tn=tn, name=name)

    q = seg(0, 1024, "nsa_q_proj")
    kc = seg(1, 768, "nsa_kc_proj")
    vc = seg(2, 512, "nsa_vc_proj")
    ks = seg(3, 768, "nsa_ks_proj")
    vs = seg(4, 512, "nsa_vs_proj")
    kw = seg(5, 768, "nsa_kw_proj")
    vw = seg(6, 512, "nsa_vw_proj")
    wg = jnp.pad(w_in[:, offs[7]:offs[8]], ((0, 0), (0, LANE - 3 * NSA_HEADS))).astype(jnp.bfloat16)
    gl = matmul(h, wg, tm=tm, tn=LANE, name="nsa_gate_proj")[:, :3 * NSA_HEADS]
    z = seg(8, 1024, "nsa_z_proj")
    q = _rmsnorm_j(q.reshape(B, L, NSA_KV, NSA_REP, NSA_DK), q_norm)
    kc = kc.reshape(B, L, NSA_KV, NSA_DK)
    vc = vc.reshape(B, L, NSA_KV, NSA_DV)
    ks = _rmsnorm_j(ks.reshape(B, L, NSA_KV, NSA_DK), kn_s)
    vs = vs.reshape(B, L, NSA_KV, NSA_DV)
    kw = _rmsnorm_j(kw.reshape(B, L, NSA_KV, NSA_DK), kn_w)
    vw = vw.reshape(B, L, NSA_KV, NSA_DV)
    gates = jax.nn.sigmoid(gl).reshape(B, L, NSA_KV, NSA_REP, 3)
    return x2, q, kc, vc, ks, vs, kw, vw, gates, z


def _nsa_compressed(kc_rows, vc_rows, kn_c, pe_k, w1_k, w2_k, pe_v, w1_v, w2_v):
    kc = _rmsnorm_j(_compress(kc_rows, pe_k, w1_k, w2_k), kn_c)
    vc = _compress(vc_rows, pe_v, w1_v, w2_v)
    ec = jnp.arange(kc.shape[1]) * CMP_STRIDE + CMP_BLOCK - 1
    return kc, vc, ec


def _nsa_out(x2, o, z, w_out, shape):
    og = (o * jax.nn.silu(z)).astype(jnp.bfloat16)
    y = matmul(og, w_out.astype(jnp.bfloat16), tm=_row_tile(og.shape[0]), tn=1024, residual=x2, name="nsa_out_proj")
    return y.reshape(shape)


def _gather_past(pool, page_table):
    g = pool[page_table]
    return g.reshape((g.shape[0], g.shape[1] * g.shape[2]) + g.shape[3:])


def _nsa_sample(x, ck, cv, sk, sv, wk_buf, wv_buf, page_table, norm_w,
                w_in, q_norm, kn_c, kn_s, kn_w, pe_k, w1_k, w2_k, pe_v, w1_v, w2_v, w_out):
    B, S, _ = x.shape
    P = page_table.shape[1] * ck.shape[1]
    x2, q, kc_r, vc_r, ks_r, vs_r, kw_r, vw_r, gates, z = _nsa_project(x, norm_w, w_in, q_norm, kn_s, kn_w)
    kc_all = jnp.concatenate([_gather_past(ck, page_table), kc_r], axis=1)
    vc_all = jnp.concatenate([_gather_past(cv, page_table), vc_r], axis=1)
    kc, vc, ec = _nsa_compressed(kc_all, vc_all, kn_c, pe_k, w1_k, w2_k, pe_v, w1_v, w2_v)
    ks = _slc_blocks(jnp.concatenate([_gather_past(sk, page_table), ks_r], axis=1))
    vs = _slc_blocks(jnp.concatenate([_gather_past(sv, page_table), vs_r], axis=1))
    kw = jnp.concatenate([wk_buf, kw_r], axis=1)
    vw = jnp.concatenate([wv_buf, vw_r], axis=1)
    wl = wk_buf.shape[1]
    tw = P - wl + jnp.arange(wl + S)
    tq = P + jnp.arange(S)
    o = _nsa_core(q, gates, tq, kc, vc, ec, ks, vs, kw, vw, tw).reshape(B * S, NSA_O)
    y = _nsa_out(x2, o, z, w_out, x.shape)
    return y, (kc_r, vc_r, ks_r, vs_r, kw[:, S:], vw[:, S:])


def kernel(x_prompt, x_sample, state_delta, state_conv, cache_cmp_k, cache_cmp_v, cache_slc_k, cache_slc_v, cache_win_k, cache_win_v, page_table, norm_dn, w_in_dn, conv_w_dn, a_log_dn, dt_bias_dn, out_norm_dn, w_out_dn, norm_nsa, w_in_nsa, q_norm_nsa, k_norm_cmp, k_norm_slc, k_norm_win, cmp_pe_k, cmp_w1_k, cmp_w2_k, cmp_pe_v, cmp_w1_v, cmp_w2_v, w_out_nsa):
    xp, xs = x_prompt, x_sample
    B = xp.shape[0]
    dw = (norm_dn[0], w_in_dn[0], conv_w_dn[0], a_log_dn[0], dt_bias_dn[0], out_norm_dn[0], w_out_dn[0])
    buf0 = jnp.zeros((B, DN_CONV - 1, conv_w_dn.shape[-1]), xp.dtype)
    st0 = jnp.zeros((B, DN_HEADS, DN_DK, DN_DV), jnp.float32)
    xp, p_dn = _deltanet_mixer(xp, buf0, st0, *dw)
    xs, s_dn = _deltanet_mixer(xs, state_conv[0], state_delta[0], *dw)
    nw = (norm_nsa[0], w_in_nsa[0], q_norm_nsa[0], k_norm_cmp[0], k_norm_slc[0], k_norm_win[0],
          cmp_pe_k[0], cmp_w1_k[0], cmp_w2_k[0], cmp_pe_v[0], cmp_w1_v[0], cmp_w2_v[0], w_out_nsa[0])
    xp, p_nsa = _nsa_prompt(xp, *nw)
    xs, s_nsa = _nsa_sample(xs, cache_cmp_k[0], cache_cmp_v[0], cache_slc_k[0], cache_slc_v[0],
                            cache_win_k[0], cache_win_v[0], page_table, *nw)
    return ((xp, xs, p_dn[0][None], p_dn[1][None]) + tuple(t[None] for t in p_nsa)
            + (s_dn[0][None], s_dn[1][None]) + tuple(t[None] for t in s_nsa))
```

```python
import functools

import jax
import jax.numpy as jnp
import numpy as np
from jax import lax
from jax.experimental import pallas as pl
from jax.experimental.pallas import tpu as pltpu

D_MODEL = 4096
EPS = 1e-6

DN_HEADS = 32
DN_DK = 128
DN_DV = 128
DN_CONV = 4
DN_CHUNK = 64

NSA_HEADS = 32
NSA_KV = 4
NSA_REP = NSA_HEADS // NSA_KV
NSA_DK = 192
NSA_DV = 128
CMP_BLOCK = 32
CMP_STRIDE = 16
CMP_RATIO = CMP_BLOCK // CMP_STRIDE
CMP_HIDDEN = 256
SLC_BLOCK = 64
SLC_TOPN = 16
WINDOW = 512
NSA_Q = NSA_HEADS * NSA_DK
NSA_KW = NSA_KV * NSA_DK
NSA_VW = NSA_KV * NSA_DV
NSA_O = NSA_HEADS * NSA_DV
NSA_SPLITS = (NSA_Q, NSA_KW, NSA_VW, NSA_KW, NSA_VW, NSA_KW, NSA_VW, 3 * NSA_HEADS, NSA_O)

VMEM_LIMIT_BYTES = 56 * 1024 * 1024
LANE = 128


def _cparams(*sem):
    return pltpu.CompilerParams(dimension_semantics=sem, vmem_limit_bytes=VMEM_LIMIT_BYTES)


def _rmsnorm_kernel(x_ref, w_ref, o_ref):
    x = x_ref[...]
    ms = jnp.mean(x * x, axis=-1, keepdims=True)
    o_ref[...] = (x * lax.rsqrt(ms + EPS) * w_ref[...]).astype(o_ref.dtype)


def rmsnorm_bf16(x, w, *, tm):
    m, d = x.shape
    return pl.pallas_call(
        _rmsnorm_kernel,
        grid=(m // tm,),
        in_specs=[pl.BlockSpec((tm, d), lambda i: (i, 0)), pl.BlockSpec((1, d), lambda i: (0, 0))],
        out_specs=pl.BlockSpec((tm, d), lambda i: (i, 0)),
        out_shape=jax.ShapeDtypeStruct((m, d), jnp.bfloat16),
        compiler_params=_cparams("parallel"),
        name="rmsnorm_bf16",
    )(x, w.reshape(1, d))


def _mm_kernel(a_ref, b_ref, o_ref):
    o_ref[...] = jnp.dot(a_ref[...], b_ref[...], preferred_element_type=jnp.float32).astype(o_ref.dtype)


def _mm_res_kernel(a_ref, b_ref, r_ref, o_ref):
    acc = jnp.dot(a_ref[...], b_ref[...], preferred_element_type=jnp.float32)
    o_ref[...] = r_ref[...] + acc


def matmul(a, b, *, tm, tn, residual=None, out_dtype=jnp.float32, name="matmul"):
    m, k = a.shape
    _, n = b.shape
    assert m % tm == 0 and n % tn == 0
    in_specs = [pl.BlockSpec((tm, k), lambda i, j: (i, 0)), pl.BlockSpec((k, tn), lambda i, j: (0, j))]
    args = [a, b]
    kern = _mm_kernel
    if residual is not None:
        in_specs.append(pl.BlockSpec((tm, tn), lambda i, j: (i, j)))
        args.append(residual)
        kern = _mm_res_kernel
    return pl.pallas_call(
        kern,
        grid=(m // tm, n // tn),
        in_specs=in_specs,
        out_specs=pl.BlockSpec((tm, tn), lambda i, j: (i, j)),
        out_shape=jax.ShapeDtypeStruct((m, n), out_dtype),
        compiler_params=_cparams("parallel", "parallel"),
        name=name,
    )(*args)


def _row_tile(m):
    return 1024 if m % 1024 == 0 else m


def _dot(a, b):
    return jnp.dot(a.astype(jnp.bfloat16), b.astype(jnp.bfloat16), preferred_element_type=jnp.float32)


def _dot_nt(a, b):
    return lax.dot_general(a.astype(jnp.bfloat16), b.astype(jnp.bfloat16), (((1,), (1,)), ((), ())),
                           preferred_element_type=jnp.float32)


def _dot_tn(a, b):
    return lax.dot_general(a.astype(jnp.bfloat16), b.astype(jnp.bfloat16), (((0,), (0,)), ((), ())),
                           preferred_element_type=jnp.float32)


def _split_bf16(x):
    hi = x.astype(jnp.bfloat16)
    lo = (x - hi.astype(jnp.float32)).astype(jnp.bfloat16)
    return hi, lo


def _dn_gate_kernel(h_ref, wt_ref, alog_ref, dtb_ref, o_ref, *, heads):
    r = lax.dot_general(wt_ref[...], h_ref[...], (((1,), (1,)), ((), ())),
                        preferred_element_type=jnp.float32)
    b = r[:heads]
    a = r[heads:] + dtb_ref[...]
    softplus = jnp.maximum(a, 0.0) + jnp.log(1.0 + jnp.exp(-jnp.abs(a)))
    o_ref[0:heads, :] = 1.0 / (1.0 + jnp.exp(-b))
    o_ref[heads:2 * heads, :] = -jnp.exp(alog_ref[...]) * softplus


def dn_gates(h, w_ba_t, a_log, dt_bias, *, tm):
    m, d = h.shape
    heads = a_log.shape[0]
    return pl.pallas_call(
        functools.partial(_dn_gate_kernel, heads=heads),
        grid=(m // tm,),
        in_specs=[pl.BlockSpec((tm, d), lambda i: (i, 0)),
                  pl.BlockSpec((2 * heads, d), lambda i: (0, 0)),
                  pl.BlockSpec((heads, 1), lambda i: (0, 0)),
                  pl.BlockSpec((heads, 1), lambda i: (0, 0))],
        out_specs=pl.BlockSpec((2 * heads, tm), lambda i: (0, i)),
        out_shape=jax.ShapeDtypeStruct((2 * heads, m), jnp.float32),
        compiler_params=_cparams("parallel"),
        name="dn_gates",
    )(h, w_ba_t, a_log.reshape(heads, 1), dt_bias.reshape(heads, 1))


_DN_TAIL = 8
_DN_TB = 256
_DN_TMIN = 2 * DN_CHUNK
_DN_HB = 4


def _dn_chunk_prep(q, k, g_row, b_row):
    c = q.shape[0]
    ii = lax.broadcasted_iota(jnp.int32, (c, c), 0)
    jj = lax.broadcasted_iota(jnp.int32, (c, c), 1)
    lower, strict, eye = ii >= jj, ii > jj, ii == jj
    g_b = jnp.broadcast_to(g_row, (c, c))
    b_b = jnp.broadcast_to(b_row, (c, c))
    b_col = jnp.sum(jnp.where(eye, b_b, 0.0), axis=1, keepdims=True)
    g_col = jnp.sum(jnp.where(eye, g_b, 0.0), axis=1, keepdims=True)
    gc_col = jnp.sum(jnp.where(lower, g_b, 0.0), axis=1, keepdims=True)
    gc_row = jnp.sum(jnp.where(ii <= jj, g_col, 0.0), axis=0, keepdims=True)
    g_last = jnp.sum(g_row, axis=1, keepdims=True)
    decay = jnp.exp(jnp.where(lower, gc_col - gc_row, -jnp.inf))
    kk = _dot_nt(k, k)
    qk = _dot_nt(q, k)
    lmat = jnp.where(strict, kk * decay, 0.0) * b_col
    attn = qk * decay
    return dict(x=-lmat, p=eye.astype(jnp.float32) - lmat, attn=attn, b_row=b_row,
                e_row=jnp.exp(gc_row), e_col=jnp.exp(gc_col), c_col=jnp.exp(g_last - gc_col),
                e_last=jnp.exp(g_last))


def _dn_inverse_levels(items, c):
    m = 2
    while m < c:
        for it in items:
            it["x"] = _dot(it["x"], it["x"])
        for it in items:
            it["p"] = it["p"] + _dot(it["p"], it["x"])
        m *= 2


def _dn_kernel(q_ref, k_ref, v_ref, z_ref, beta_ref, g_ref, cw_ref, cs_ref, s0_ref, onw_ref,
               o_ref, s_out_ref, xpad, s_scr, *, tb, hb):
    n = pl.program_id(2)
    c = DN_CHUNK
    d = DN_DK
    nc = tb // c

    @pl.when(n == 0)
    def _init():
        xpad[:, 0:_DN_TAIL, :] = cs_ref[0]
        s_scr[...] = s0_ref[0]

    def conv(i, ref):
        xpad[i, _DN_TAIL:_DN_TAIL + tb, :] = ref[...]
        acc = None
        for j in range(DN_CONV):
            off = _DN_TAIL - (DN_CONV - 1) + j
            term = xpad[i, off:off + tb, :] * cw_ref[i, j:j + 1, :]
            acc = term if acc is None else acc + term
        xpad[i, 0:_DN_TAIL, :] = xpad[i, tb:tb + _DN_TAIL, :]
        return acc * (1.0 / (1.0 + jnp.exp(-acc)))

    qc, kc, vc = conv(0, q_ref), conv(1, k_ref), conv(2, v_ref)
    items = []
    for hh in range(hb):
        sl = slice(hh * d, (hh + 1) * d)
        q = qc[:, sl]
        k = kc[:, sl]
        q = q * (lax.rsqrt(jnp.sum(q * q, axis=-1, keepdims=True) + EPS) * (d ** -0.5))
        k = k * lax.rsqrt(jnp.sum(k * k, axis=-1, keepdims=True) + EPS)
        g_all = g_ref[hh]
        b_all = beta_ref[hh]
        for ci in range(nc):
            rs = slice(ci * c, (ci + 1) * c)
            it = _dn_chunk_prep(q[rs], k[rs], g_all[:, rs], b_all[:, rs])
            it.update(q=q[rs], k=k[rs], v=vc[rs, sl])
            items.append(it)
    _dn_inverse_levels(items, c)
    for it in items:
        inv_b = it["p"] * it["b_row"]
        it["u"] = _dot(inv_b, it["v"])
        it["w"] = _dot(inv_b * it["e_row"], it["k"])
    states = [s_scr[hh] for hh in range(hb)]
    outs = [[None] * nc for _ in range(hb)]
    for ci in range(nc):
        for hh in range(hb):
            it = items[hh * nc + ci]
            s = states[hh]
            v_new = it["u"] - _dot(it["w"], s)
            outs[hh][ci] = it["e_col"] * _dot(it["q"], s) + _dot(it["attn"], v_new)
            states[hh] = s * it["e_last"] + _dot_tn(it["k"] * it["c_col"], v_new)
    for hh in range(hb):
        sl = slice(hh * d, (hh + 1) * d)
        s_scr[hh] = states[hh]
        o = jnp.concatenate(outs[hh], axis=0) if nc > 1 else outs[hh][0]
        o = o * lax.rsqrt(jnp.mean(o * o, axis=-1, keepdims=True) + EPS) * onw_ref[...]
        z = z_ref[:, sl]
        o_ref[:, sl] = (o * (z * (1.0 / (1.0 + jnp.exp(-z))))).astype(o_ref.dtype)

    @pl.when(n == pl.num_programs(2) - 1)
    def _fin():
        s_out_ref[0] = s_scr[...]


def dn_delta(proj, gates, conv_w, conv_state, s0, out_norm, *, batch, tb, hb):
    rows, width = proj.shape
    d = DN_DK
    heads = width // (4 * d)
    t = rows // batch
    nt = t // tb
    hg = heads // hb
    wd = hb * d
    row_map = lambda off: (lambda b, h, n: (b * nt + n, off * hg + h))
    gate_map = lambda off: (lambda b, h, n: (off * hg + h, 0, b * nt + n))
    return pl.pallas_call(
        functools.partial(_dn_kernel, tb=tb, hb=hb),
        grid=(batch, hg, nt),
        in_specs=[pl.BlockSpec((tb, wd), row_map(0)),
                  pl.BlockSpec((tb, wd), row_map(1)),
                  pl.BlockSpec((tb, wd), row_map(2)),
                  pl.BlockSpec((tb, wd), row_map(3)),
                  pl.BlockSpec((hb, 1, tb), gate_map(0)),
                  pl.BlockSpec((hb, 1, tb), gate_map(1)),
                  pl.BlockSpec((3, DN_CONV, wd), lambda b, h, n: (0, 0, h)),
                  pl.BlockSpec((1, 3, _DN_TAIL, wd), lambda b, h, n: (b, 0, 0, h)),
                  pl.BlockSpec((1, hb, d, d), lambda b, h, n: (b, h, 0, 0)),
                  pl.BlockSpec((1, d), lambda b, h, n: (0, 0))],
        out_specs=[pl.BlockSpec((tb, wd), lambda b, h, n: (b * nt + n, h)),
                   pl.BlockSpec((1, hb, d, d), lambda b, h, n: (b, h, 0, 0))],
        out_shape=[jax.ShapeDtypeStruct((rows, heads * d), jnp.bfloat16),
                   jax.ShapeDtypeStruct(s0.shape, jnp.float32)],
        scratch_shapes=[pltpu.VMEM((3, tb + _DN_TAIL, wd), jnp.float32),
                        pltpu.VMEM((hb, d, d), jnp.float32)],
        compiler_params=_cparams("parallel", "parallel", "arbitrary"),
        name="dn_delta",
    )(proj, proj, proj, proj, gates, gates, conv_w, conv_state, s0, out_norm.reshape(1, d))


def _deltanet_mixer(x, conv_buf, s0, norm_w, w_in, conv_w, a_log, dt_bias, out_norm, w_out):
    B, L, D = x.shape
    H = a_log.shape[0]
    cdim = conv_w.shape[1]
    hd = cdim // 3
    x2 = x.reshape(B * L, D)
    h = rmsnorm_bf16(x2, norm_w, tm=min(256, B * L))
    w_main = w_in[:, :cdim + hd].astype(jnp.bfloat16)
    w_ba_t = w_in[:, cdim + hd:].T.astype(jnp.bfloat16)
    proj = matmul(h, w_main, tm=_row_tile(B * L), tn=1024, name="dn_in_proj")
    gates = dn_gates(h, w_ba_t, a_log, dt_bias, tm=_row_tile(B * L))
    new_buf = proj[:, :cdim].reshape(B, L, cdim)[:, L - (DN_CONV - 1):]
    T = -(-L // _DN_TMIN) * _DN_TMIN
    if T != L:
        proj = jnp.pad(proj.reshape(B, L, -1), ((0, 0), (0, T - L), (0, 0))).reshape(B * T, -1)
        gates = jnp.pad(gates.reshape(2 * H, B, L), ((0, 0), (0, 0), (0, T - L))).reshape(2 * H, B * T)
    cw = conv_w.reshape(DN_CONV, 3, hd).transpose(1, 0, 2)
    cs = conv_buf.reshape(B, DN_CONV - 1, 3, hd).transpose(0, 2, 1, 3)
    cs = jnp.pad(cs, ((0, 0), (0, 0), (_DN_TAIL - (DN_CONV - 1), 0), (0, 0)))
    tb = min(_DN_TB, T)
    o, s_new = dn_delta(proj, gates.reshape(2 * H, 1, B * T), cw, cs, s0.astype(jnp.float32), out_norm,
                        batch=B, tb=tb, hb=_DN_HB)
    if T != L:
        o = o.reshape(B, T, -1)[:, :L].reshape(B * L, -1)
    y = matmul(o, w_out.astype(jnp.bfloat16), tm=_row_tile(B * L), tn=1024, residual=x2, name="dn_out_proj")
    return y.reshape(B, L, D), (s_new, new_buf)


def _alibi_slopes():
    hh = jnp.arange(1, NSA_HEADS + 1, dtype=jnp.float32)
    return jnp.exp2(-8.0 * hh / NSA_HEADS).reshape(NSA_KV, NSA_REP)


PAGE_ROWS = 128
NSA_DKP = 256
_POS_LANE = NSA_DK
_SEL_LANE = NSA_DK + 4
_SEL_LANES = 32
_MASK_BIG = 2.0 ** 100
_NEG = -1e30
_NSA_TQ = 128
_NSA_TK = 256
_CMP_PAGES = 16


def _slope_table():
    s = _alibi_slopes()
    hi = s.astype(jnp.bfloat16).astype(jnp.float32)
    lo = (s - hi).astype(jnp.bfloat16).astype(jnp.float32)
    tab = jnp.stack([64.0 * hi, hi, 64.0 * lo, lo], axis=-1)
    return jnp.pad(tab, ((0, 0), (0, 0), (0, NSA_DKP - NSA_DK - 4)))


def _q_prep_kernel(q_ref, w_ref, tab_ref, o_ref):
    tq = q_ref.shape[0]
    for r in range(NSA_REP):
        x = q_ref[:, r * NSA_DK:(r + 1) * NSA_DK]
        y = x * lax.rsqrt(jnp.mean(x * x, axis=-1, keepdims=True) + EPS) * (w_ref[...] * (NSA_DK ** -0.5))
        o_ref[0, r, :, 0:NSA_DK] = y.astype(o_ref.dtype)
        o_ref[0, r, :, NSA_DK:NSA_DKP] = jnp.broadcast_to(tab_ref[0, r:r + 1, :], (tq, NSA_DKP - NSA_DK)).astype(o_ref.dtype)


def nsa_q_prep(q_raw, q_norm, *, tq):
    rows = q_raw.shape[0]
    return pl.pallas_call(
        _q_prep_kernel,
        grid=(rows // tq, NSA_KV),
        in_specs=[pl.BlockSpec((tq, NSA_REP * NSA_DK), lambda i, g: (i, g)),
                  pl.BlockSpec((1, NSA_DK), lambda i, g: (0, 0)),
                  pl.BlockSpec((1, NSA_REP, NSA_DKP - NSA_DK), lambda i, g: (g, 0, 0))],
        out_specs=pl.BlockSpec((1, NSA_REP, tq, NSA_DKP), lambda i, g: (g, 0, i, 0)),
        out_shape=jax.ShapeDtypeStruct((NSA_KV, NSA_REP, rows, NSA_DKP), jnp.bfloat16),
        compiler_params=_cparams("parallel", "parallel"),
        name="nsa_q_prep",
    )(q_raw, q_norm.reshape(1, NSA_DK), _slope_table())


def _k_prep_kernel(k_ref, w_ref, on_ref, oh_ref, *, seq, onehot):
    tk = k_ref.shape[0]
    pos = (pl.program_id(0) * tk) % seq + lax.broadcasted_iota(jnp.int32, (tk, NSA_DKP - NSA_DK), 0)
    lane = lax.broadcasted_iota(jnp.int32, (tk, NSA_DKP - NSA_DK), 1)
    blk, off = pos // SLC_BLOCK, pos % SLC_BLOCK
    extra = jnp.where((lane == 0) | (lane == 2), blk, jnp.where((lane == 1) | (lane == 3), off, 0))
    if onehot:
        extra = jnp.where((lane >= 4) & (lane - 4 == blk), 1, extra)
    extra = extra.astype(jnp.float32)
    for g in range(NSA_KV):
        x = k_ref[:, g * NSA_DK:(g + 1) * NSA_DK]
        x = x * lax.rsqrt(jnp.mean(x * x, axis=-1, keepdims=True) + EPS) * w_ref[...]
        on_ref[:, g * NSA_DK:(g + 1) * NSA_DK] = x
        oh_ref[g, :, 0:NSA_DK] = x.astype(oh_ref.dtype)
        oh_ref[g, :, NSA_DK:NSA_DKP] = extra.astype(oh_ref.dtype)


def nsa_k_prep(k_raw, k_norm, *, seq, tk, onehot):
    rows = k_raw.shape[0]
    return pl.pallas_call(
        functools.partial(_k_prep_kernel, seq=seq, onehot=onehot),
        grid=(rows // tk,),
        in_specs=[pl.BlockSpec((tk, NSA_KW), lambda i: (i, 0)), pl.BlockSpec((1, NSA_DK), lambda i: (0, 0))],
        out_specs=[pl.BlockSpec((tk, NSA_KW), lambda i: (i, 0)),
                   pl.BlockSpec((NSA_KV, tk, NSA_DKP), lambda i: (0, i, 0))],
        out_shape=[jax.ShapeDtypeStruct((rows, NSA_KW), jnp.float32),
                   jax.ShapeDtypeStruct((NSA_KV, rows, NSA_DKP), jnp.bfloat16)],
        compiler_params=_cparams("parallel"),
        name="nsa_k_prep",
    )(k_raw, k_norm.reshape(1, NSA_DK))


def _v_prep_kernel(v_ref, o_ref):
    for g in range(NSA_KV):
        o_ref[g] = v_ref[:, g * NSA_DV:(g + 1) * NSA_DV].astype(o_ref.dtype)


def nsa_v_prep(v_raw, *, tk):
    rows = v_raw.shape[0]
    return pl.pallas_call(
        _v_prep_kernel,
        grid=(rows // tk,),
        in_specs=[pl.BlockSpec((tk, NSA_VW), lambda i: (i, 0))],
        out_specs=pl.BlockSpec((NSA_KV, tk, NSA_DV), lambda i: (0, i, 0)),
        out_shape=jax.ShapeDtypeStruct((NSA_KV, rows, NSA_DV), jnp.bfloat16),
        compiler_params=_cparams("parallel"),
        name="nsa_v_prep",
    )(v_raw)


def _compress_partials_kernel(tab_ref, *refs, d, pages):
    x_refs, (pe_ref, w1_ref, a_ref, b_ref) = refs[:pages], refs[pages:]
    seg = x_refs[0].shape[0]
    acc_a = acc_b = None
    for l in range(CMP_STRIDE):
        x = jnp.concatenate([r[:, l, :] for r in x_refs], axis=0) if pages > 1 else x_refs[0][:, l, :]
        xg = jnp.concatenate([x[:, g * d:(g + 1) * d] for g in range(NSA_KV)], axis=0)
        ta = jnp.dot((xg + pe_ref[l:l + 1, :]).astype(jnp.bfloat16), w1_ref[l], preferred_element_type=jnp.float32)
        tb = jnp.dot((xg + pe_ref[CMP_STRIDE + l:CMP_STRIDE + l + 1, :]).astype(jnp.bfloat16),
                     w1_ref[CMP_STRIDE + l], preferred_element_type=jnp.float32)
        acc_a = ta if acc_a is None else acc_a + ta
        acc_b = tb if acc_b is None else acc_b + tb
    n = pages * seg
    for g in range(NSA_KV):
        a_ref[0, g] = acc_a[g * n:(g + 1) * n]
        b_ref[0, g] = acc_b[g * n:(g + 1) * n]


def nsa_compress_partials(pool3, table, pe, w1, *, pages):
    d = pe.shape[1]
    batch, npg = table.shape
    seg = PAGE_ROWS // CMP_STRIDE
    x_spec = lambda k: pl.BlockSpec((seg, CMP_STRIDE, NSA_KV * d), lambda b, i, tab: (tab[b, i * pages + k], 0, 0))
    out_spec = pl.BlockSpec((1, NSA_KV, pages * seg, CMP_HIDDEN), lambda b, i, tab: (b, 0, i, 0))
    out_sds = jax.ShapeDtypeStruct((batch, NSA_KV, npg * seg, CMP_HIDDEN), jnp.float32)
    return pl.pallas_call(
        functools.partial(_compress_partials_kernel, d=d, pages=pages),
        grid_spec=pltpu.PrefetchScalarGridSpec(
            num_scalar_prefetch=1,
            grid=(batch, npg // pages),
            in_specs=[x_spec(k) for k in range(pages)]
            + [pl.BlockSpec((CMP_BLOCK, d), lambda b, i, tab: (0, 0)),
               pl.BlockSpec((CMP_BLOCK, d, CMP_HIDDEN), lambda b, i, tab: (0, 0, 0))],
            out_specs=[out_spec, out_spec]),
        out_shape=[out_sds, out_sds],
        compiler_params=_cparams("parallel", "arbitrary"),
        name="nsa_compress_partials",
    )(table, *([pool3] * pages), pe, w1.astype(jnp.bfloat16))


def _cmp_extra_lanes(nrows, width):
    n = lax.broadcasted_iota(jnp.int32, (nrows, width), 0)
    lane = lax.broadcasted_iota(jnp.int32, (nrows, width), 1)
    ec = n * CMP_STRIDE + CMP_BLOCK - 1
    blk, off = ec // SLC_BLOCK, ec % SLC_BLOCK
    pl_ = lane - _POS_LANE
    return jnp.where((pl_ == 0) | (pl_ == 2), blk, jnp.where((pl_ == 1) | (pl_ == 3), off, 0)).astype(jnp.float32)


def _compress_finish_kernel(a_ref, b_ref, bn_ref, w2_ref, kn_ref, o_ref, *, is_key, has_new, alibi_lanes):
    ns = a_ref.shape[2]
    row = lax.broadcasted_iota(jnp.int32, (ns, 1), 0)
    for g in range(NSA_KV):
        nxt = pltpu.roll(b_ref[0, g], ns - 1, 0)
        if has_new:
            nxt = jnp.where(row == ns - 1, bn_ref[0, g, 0:1, :], nxt)
        hid_pre = a_ref[0, g] + nxt
        hid = hid_pre * (1.0 / (1.0 + jnp.exp(-hid_pre)))
        out = jnp.dot(hid.astype(jnp.bfloat16), w2_ref[...], preferred_element_type=jnp.float32)
        if is_key:
            ms = jnp.sum(out * out, axis=-1, keepdims=True) * (1.0 / NSA_DK)
            out = out * lax.rsqrt(ms + EPS) * kn_ref[...]
            if alibi_lanes:
                out = out + _cmp_extra_lanes(ns, out.shape[1])
        o_ref[0, g] = out.astype(o_ref.dtype)


def nsa_compress_finish(part_a, part_b, part_b_new, w2, kn, *, is_key, alibi_lanes):
    batch, _, ns, _ = part_a.shape
    d = w2.shape[1]
    dp = NSA_DKP if is_key else d
    has_new = part_b_new is not None
    if not has_new:
        part_b_new = jnp.zeros((batch, NSA_KV, 8, CMP_HIDDEN), jnp.float32)
    w2p = jnp.pad(w2, ((0, 0), (0, dp - d))).astype(jnp.bfloat16)
    knp = jnp.pad(kn, (0, dp - d)).reshape(1, dp)
    blk = lambda n: pl.BlockSpec((1, NSA_KV, n, CMP_HIDDEN), lambda b: (b, 0, 0, 0))
    return pl.pallas_call(
        functools.partial(_compress_finish_kernel, is_key=is_key, has_new=has_new, alibi_lanes=alibi_lanes),
        grid=(batch,),
        in_specs=[blk(ns), blk(ns), blk(part_b_new.shape[2]),
                  pl.BlockSpec((CMP_HIDDEN, dp), lambda b: (0, 0)),
                  pl.BlockSpec((1, dp), lambda b: (0, 0))],
        out_specs=pl.BlockSpec((1, NSA_KV, ns, dp), lambda b: (b, 0, 0, 0)),
        out_shape=jax.ShapeDtypeStruct((batch, NSA_KV, ns, dp), jnp.bfloat16),
        compiler_params=_cparams("parallel"),
        name="nsa_compress_finish_k" if is_key else "nsa_compress_finish_v",
    )(part_a, part_b, part_b_new, w2p, knp)


def _split_dot(x, m_bf16):
    hi, lo = _split_bf16(x)
    f = functools.partial(jnp.dot, preferred_element_type=jnp.float32)
    return f(hi, m_bf16) + f(lo, m_bf16)


def _topn_mask(imp, t_col, nsb, topn):
    jb = lax.broadcasted_iota(jnp.int32, imp.shape, 1)
    cur = t_col // SLC_BLOCK
    forced = (jb == 0) | (jb == cur) | (jb == cur - 1)
    valid = (jb * SLC_BLOCK <= t_col) & (jb < nsb)
    impm = jnp.where(forced, jnp.inf, jnp.where(valid, imp, -jnp.inf))
    rank = jnp.zeros(imp.shape, jnp.float32)
    for k in range(nsb):
        col = impm[:, k:k + 1]
        beats = (col > impm) | ((col == impm) & (k < jb))
        rank = rank + jnp.where(beats, 1.0, 0.0)
    return (rank < topn) & (jb < nsb)


def _flash_step(q, k, v, mask, carry):
    m, l, acc = carry
    s = _dot_nt(q, k)
    s = jnp.where(mask, s, _NEG)
    m_new = jnp.maximum(m, jnp.max(s, axis=1, keepdims=True))
    alpha = jnp.exp(m - m_new)
    p = jnp.exp(s - m_new)
    l = alpha * l + jnp.sum(p, axis=1, keepdims=True)
    acc = alpha * acc + jnp.dot(p.astype(jnp.bfloat16), v, preferred_element_type=jnp.float32)
    return m_new, l, acc


def _nsa_prompt_kernel(q_ref, kc_ref, vc_ref, ks_ref, vs_ref, kw_ref, vw_ref, gl_ref, z_ref, o_ref,
                       *, tq, tk, seq):
    qi = pl.program_id(2)
    t0 = qi * tq
    rep = NSA_REP
    m_rows = rep * tq
    nc = seq // CMP_STRIDE - CMP_RATIO + 1
    nseg = kc_ref.shape[2]
    nsb = seq // SLC_BLOCK
    q = q_ref[0].reshape(m_rows, NSA_DKP)
    t_tok = t0 + lax.broadcasted_iota(jnp.int32, (tq, 1), 0)
    t_col = t0 + lax.broadcasted_iota(jnp.int32, (rep, tq, 1), 1).reshape(m_rows, 1)

    n_row = lax.broadcasted_iota(jnp.int32, (1, nseg), 1)
    cvalid = (n_row * CMP_STRIDE + CMP_BLOCK - 1 <= t_col) & (n_row < nc)
    s = jnp.where(cvalid, _dot_nt(q, kc_ref[0, 0]), _NEG)
    p = jnp.where(cvalid, jnp.exp(s - jnp.max(s, axis=1, keepdims=True)), 0.0)
    p = p / jnp.maximum(jnp.sum(p, axis=1, keepdims=True), 1e-30)
    o_c = jnp.dot(p.astype(jnp.bfloat16), vc_ref[0, 0], preferred_element_type=jnp.float32)
    psum = jnp.sum(p.reshape(rep, tq, nseg), axis=0)
    ci = lax.broadcasted_iota(jnp.int32, (nseg, _SEL_LANES), 0) * CMP_STRIDE
    cj = lax.broadcasted_iota(jnp.int32, (nseg, _SEL_LANES), 1) * SLC_BLOCK
    ov = jnp.maximum(jnp.minimum(ci + CMP_BLOCK, cj + SLC_BLOCK) - jnp.maximum(ci, cj), 0)
    c2s = (ov.astype(jnp.float32) * (1.0 / CMP_STRIDE)).astype(jnp.bfloat16)
    imp = _split_dot(psum, c2s)

    sel = _topn_mask(imp, t_tok, nsb, min(SLC_TOPN, nsb))
    pen = jnp.where(sel, 0.0, -_MASK_BIG).astype(jnp.bfloat16)
    pi = lax.broadcasted_iota(jnp.int32, (_SEL_LANES, NSA_DKP), 0)
    pj = lax.broadcasted_iota(jnp.int32, (_SEL_LANES, NSA_DKP), 1)
    place = jnp.where(pj == pi + _SEL_LANE, 1.0, 0.0).astype(jnp.bfloat16)
    q_add = jnp.dot(pen, place, preferred_element_type=jnp.float32).astype(jnp.bfloat16)
    q_slc = (q.reshape(rep, tq, NSA_DKP) + q_add[None]).reshape(m_rows, NSA_DKP)

    kpos = lax.broadcasted_iota(jnp.int32, (1, tk), 1)
    init = (jnp.full((m_rows, 1), _NEG, jnp.float32), jnp.zeros((m_rows, 1), jnp.float32),
            jnp.zeros((m_rows, NSA_DV), jnp.float32))

    def slc_body(kt, carry):
        k0 = pl.multiple_of(kt * tk, tk)
        mask = (k0 + kpos) <= t_col
        return _flash_step(q_slc, ks_ref[0, pl.ds(k0, tk), :], vs_ref[0, pl.ds(k0, tk), :], mask, carry)

    _, l_s, acc_s = lax.fori_loop(0, (t0 + tq + tk - 1) // tk, slc_body, init)
    o_s = acc_s / l_s

    def win_body(kt, carry):
        k0 = pl.multiple_of(kt * tk, tk)
        dw = t_col - (k0 + kpos)
        mask = (dw >= 0) & (dw < WINDOW)
        return _flash_step(q, kw_ref[0, pl.ds(k0, tk), :], vw_ref[0, pl.ds(k0, tk), :], mask, carry)

    kt_lo = jnp.maximum(t0 - (WINDOW - 1), 0) // tk
    _, l_w, acc_w = lax.fori_loop(kt_lo, (t0 + tq + tk - 1) // tk, win_body, init)
    o_w = acc_w / l_w

    gates = 1.0 / (1.0 + jnp.exp(-gl_ref[...]))
    for r in range(rep):
        rs = slice(r * tq, (r + 1) * tq)
        o = (gates[:, 3 * r:3 * r + 1] * o_c[rs] + gates[:, 3 * r + 1:3 * r + 2] * o_s[rs]
             + gates[:, 3 * r + 2:3 * r + 3] * o_w[rs])
        z = z_ref[:, r * NSA_DV:(r + 1) * NSA_DV]
        o_ref[:, r * NSA_DV:(r + 1) * NSA_DV] = (o * (z * (1.0 / (1.0 + jnp.exp(-z))))).astype(o_ref.dtype)


def nsa_prompt_attention(q_hm, kcmp, vcmp, ks_hm, vs_hm, kw_hm, vw_hm, gl, z, *, batch, tq, tk):
    rows = z.shape[0]
    seq = rows // batch
    nq = seq // tq
    nseg = kcmp.shape[2]
    row_blk = lambda b, g, i: (b * nq + i, g)
    kv_blk = lambda b, g, i: (g, b, 0)
    return pl.pallas_call(
        functools.partial(_nsa_prompt_kernel, tq=tq, tk=tk, seq=seq),
        grid=(batch, NSA_KV, nq),
        in_specs=[pl.BlockSpec((1, NSA_REP, tq, NSA_DKP), lambda b, g, i: (g, 0, b * nq + i, 0)),
                  pl.BlockSpec((1, 1, nseg, NSA_DKP), lambda b, g, i: (b, g, 0, 0)),
                  pl.BlockSpec((1, 1, nseg, NSA_DV), lambda b, g, i: (b, g, 0, 0)),
                  pl.BlockSpec((1, seq, NSA_DKP), kv_blk),
                  pl.BlockSpec((1, seq, NSA_DV), kv_blk),
                  pl.BlockSpec((1, seq, NSA_DKP), kv_blk),
                  pl.BlockSpec((1, seq, NSA_DV), kv_blk),
                  pl.BlockSpec((tq, LANE), row_blk),
                  pl.BlockSpec((tq, NSA_REP * NSA_DV), row_blk)],
        out_specs=pl.BlockSpec((tq, NSA_REP * NSA_DV), row_blk),
        out_shape=jax.ShapeDtypeStruct((rows, NSA_O), jnp.bfloat16),
        compiler_params=_cparams("parallel", "parallel", "arbitrary"),
        name="nsa_prompt_attention",
    )(q_hm, kcmp, vcmp, ks_hm, vs_hm, kw_hm, vw_hm, gl, z)


def _nsa_in_proj(x, norm_w, w_in):
    B, L, D = x.shape
    x2 = x.reshape(B * L, D)
    h = rmsnorm_bf16(x2, norm_w, tm=min(256, B * L))
    offs = np.cumsum((0,) + NSA_SPLITS)
    tm = _row_tile(B * L)

    def seg(i, tn, name):
        w = w_in[:, offs[i]:offs[i + 1]].astype(jnp.bfloat16)
        return matmul(h, w, tm=tm, tn=tn, name=name)

    q = seg(0, 1024, "nsa_q_proj")
    kc = seg(1, 768, "nsa_kc_proj")
    vc = seg(2, 512, "nsa_vc_proj")
    ks = seg(3, 768, "nsa_ks_proj")
    vs = seg(4, 512, "nsa_vs_proj")
    kw = seg(5, 768, "nsa_kw_proj")
    vw = seg(6, 512, "nsa_vw_proj")
    wg = w_in[:, offs[7]:offs[8]].reshape(D, NSA_KV, 3 * NSA_REP)
    wg = jnp.pad(wg, ((0, 0), (0, 0), (0, LANE - 3 * NSA_REP))).reshape(D, NSA_KV * LANE).astype(jnp.bfloat16)
    gl = matmul(h, wg, tm=tm, tn=NSA_KV * LANE, name="nsa_gate_proj")
    z = seg(8, 1024, "nsa_z_proj")
    return x2, q, kc, vc, ks, vs, kw, vw, gl, z


def _nsa_prompt(x, norm_w, w_in, q_norm, kn_c, kn_s, kn_w, pe_k, w1_k, w2_k, pe_v, w1_v, w2_v, w_out):
    B, T, D = x.shape
    x2, q, kc_r, vc_r, ks_raw, vs_r, kw_raw, vw_r, gl, z = _nsa_in_proj(x, norm_w, w_in)
    tp = min(512, T)
    q_hm = nsa_q_prep(q, q_norm, tq=tp)
    ks_r, ks_hm = nsa_k_prep(ks_raw, kn_s, seq=T, tk=tp, onehot=True)
    kw_r, kw_hm = nsa_k_prep(kw_raw, kn_w, seq=T, tk=tp, onehot=False)
    vs_hm = nsa_v_prep(vs_r, tk=tp)
    vw_hm = nsa_v_prep(vw_r, tk=tp)
    npg = T // PAGE_ROWS
    table = (jnp.arange(B, dtype=jnp.int32)[:, None] * npg + jnp.arange(npg, dtype=jnp.int32)[None, :])
    seg3 = lambda a: a.reshape(B * T // CMP_STRIDE, CMP_STRIDE, a.shape[1])
    ka, kb = nsa_compress_partials(seg3(kc_r), table, pe_k, w1_k, pages=min(_CMP_PAGES, npg))
    va, vb = nsa_compress_partials(seg3(vc_r), table, pe_v, w1_v, pages=min(_CMP_PAGES, npg))
    kcmp = nsa_compress_finish(ka, kb, None, w2_k, kn_c, is_key=True, alibi_lanes=True)
    vcmp = nsa_compress_finish(va, vb, None, w2_v, jnp.zeros((NSA_DV,), jnp.float32), is_key=False, alibi_lanes=False)
    og = nsa_prompt_attention(q_hm, kcmp, vcmp, ks_hm, vs_hm, kw_hm, vw_hm, gl, z, batch=B, tq=_NSA_TQ, tk=_NSA_TK)
    y = matmul(og, w_out.astype(jnp.bfloat16), tm=_row_tile(B * T), tn=1024, residual=x2, name="nsa_out_proj")
    wl = min(WINDOW, T)
    r4 = lambda a, d: a.reshape(B, T, NSA_KV, d)
    return y.reshape(B, T, D), (r4(kc_r, NSA_DK), r4(vc_r, NSA_DV), r4(ks_r, NSA_DK), r4(vs_r, NSA_DV),
                                r4(kw_r, NSA_DK)[:, T - wl:], r4(vw_r, NSA_DV)[:, T - wl:])


_SMP_PAGES = 8
_SEL_PAD = 384


def _online_update(m_ref, l_ref, acc_ref, g, s, v):
    m_old = m_ref[g]
    m_new = jnp.maximum(m_old, jnp.max(s, axis=1, keepdims=True))
    alpha = jnp.exp(m_old - m_new)
    p = jnp.exp(s - m_new)
    l_ref[g] = alpha * l_ref[g] + jnp.sum(p, axis=1, keepdims=True)
    acc_ref[g] = alpha * acc_ref[g] + jnp.dot(p.astype(jnp.bfloat16), v.astype(jnp.bfloat16),
                                              preferred_element_type=jnp.float32)
    m_ref[g] = m_new


def _nsa_sample_kernel(tab_ref, *refs, past, steps, pages, nsb):
    kp_refs = refs[:pages]
    vp_refs = refs[pages:2 * pages]
    (q_ref, kc_ref, vc_ref, kn_ref, vn_ref, kwc_ref, vwc_ref, kwn_ref, vwn_ref, gl_ref, z_ref,
     o_ref, m_scr, l_scr, acc_scr, sel_scr, oc_scr) = refs[2 * pages:]
    i = pl.program_id(1)
    s_tok = o_ref.shape[0]
    rows = NSA_REP * s_tok
    ncmp = kc_ref.shape[2]
    r_col = lax.broadcasted_iota(jnp.int32, (NSA_REP, s_tok, 1), 0).reshape(rows, 1)
    t_col = past + lax.broadcasted_iota(jnp.int32, (NSA_REP, s_tok, 1), 1).reshape(rows, 1)
    t_tok = past + lax.broadcasted_iota(jnp.int32, (s_tok, 1), 0)

    def slope_col(g):
        return jnp.exp2((r_col + (g * NSA_REP + 1)).astype(jnp.float32) * (-8.0 / NSA_HEADS))

    def q_of(g):
        return q_ref[0, g * rows:(g + 1) * rows, :]

    def rep_rows(x):
        return jnp.concatenate([x] * NSA_REP, axis=0)

    @pl.when(i == 0)
    def _first():
        n_row = lax.broadcasted_iota(jnp.int32, (1, ncmp), 1)
        cvalid = (n_row * CMP_STRIDE + CMP_BLOCK - 1) <= t_col
        ci = lax.broadcasted_iota(jnp.int32, (ncmp, _SEL_PAD), 0) * CMP_STRIDE
        cj = lax.broadcasted_iota(jnp.int32, (ncmp, _SEL_PAD), 1) * SLC_BLOCK
        ov = jnp.maximum(jnp.minimum(ci + CMP_BLOCK, cj + SLC_BLOCK) - jnp.maximum(ci, cj), 0)
        c2s = (ov.astype(jnp.float32) * (1.0 / CMP_STRIDE)).astype(jnp.bfloat16)
        jb = lax.broadcasted_iota(jnp.int32, (s_tok, _SEL_PAD), 1)
        cur = t_tok // SLC_BLOCK
        forced = (jb == 0) | (jb == cur) | (jb == cur - 1)
        valid = (jb * SLC_BLOCK <= t_tok) & (jb < nsb)
        for g in range(NSA_KV):
            s = jnp.where(cvalid, _dot_nt(q_of(g), kc_ref[0, g]), _NEG)
            p = jnp.where(cvalid, jnp.exp(s - jnp.max(s, axis=1, keepdims=True)), 0.0)
            p = p / jnp.maximum(jnp.sum(p, axis=1, keepdims=True), 1e-30)
            oc_scr[g] = jnp.dot(p.astype(jnp.bfloat16), vc_ref[0, g], preferred_element_type=jnp.float32)
            imp = _split_dot(jnp.sum(p.reshape(NSA_REP, s_tok, ncmp), axis=0), c2s)
            work = jnp.where(forced, jnp.inf, jnp.where(valid, imp, -jnp.inf))
            sel = jnp.zeros((s_tok, _SEL_PAD), jnp.float32)
            for _ in range(min(SLC_TOPN, nsb)):
                mx = jnp.max(work, axis=1, keepdims=True)
                first = jnp.min(jnp.where(work == mx, jb, _SEL_PAD), axis=1, keepdims=True)
                hit = jb == first
                sel = jnp.where(hit, 1.0, sel)
                work = jnp.where(hit, -jnp.inf, work)
            sel_scr[g] = sel
            m_scr[g] = jnp.full((rows, 1), _NEG, jnp.float32)
            l_scr[g] = jnp.zeros((rows, 1), jnp.float32)
            acc_scr[g] = jnp.zeros((rows, NSA_DV), jnp.float32)

    def fold(k_all, v_all, page0, causal):
        nk = k_all.shape[0]
        pos = page0 * PAGE_ROWS + lax.broadcasted_iota(jnp.int32, (1, nk), 1)
        bj = lax.broadcasted_iota(jnp.int32, (_SEL_PAD, nk), 0)
        bl = lax.broadcasted_iota(jnp.int32, (_SEL_PAD, nk), 1)
        expand = jnp.where(bj == (page0 * PAGE_ROWS + bl) // SLC_BLOCK, 1.0, 0.0).astype(jnp.bfloat16)
        for g in range(NSA_KV):
            selk = jnp.dot(sel_scr[g].astype(jnp.bfloat16), expand, preferred_element_type=jnp.float32)
            ok = rep_rows(selk) > 0.5
            if causal:
                ok = ok & (pos <= t_col)
            s = _dot_nt(q_of(g)[:, 0:NSA_DK], k_all[:, g * NSA_DK:(g + 1) * NSA_DK])
            s = s - slope_col(g) * (t_col - pos).astype(jnp.float32)
            _online_update(m_scr, l_scr, acc_scr, g, jnp.where(ok, s, _NEG), v_all[:, g * NSA_DV:(g + 1) * NSA_DV])

    k_pages = jnp.concatenate([r[0] for r in kp_refs], axis=0) if pages > 1 else kp_refs[0][0]
    v_pages = jnp.concatenate([r[0] for r in vp_refs], axis=0) if pages > 1 else vp_refs[0][0]
    fold(k_pages, v_pages, i * pages, causal=False)

    @pl.when(i == steps - 1)
    def _last():
        fold(kn_ref[0], vn_ref[0], past // PAGE_ROWS, causal=True)
        nwc = kwc_ref.shape[1]
        posw = jnp.concatenate([past - nwc + lax.broadcasted_iota(jnp.int32, (1, nwc), 1),
                                past + lax.broadcasted_iota(jnp.int32, (1, PAGE_ROWS), 1)], axis=1)
        dw = t_col - posw
        wok = (dw >= 0) & (dw < WINDOW) & (posw >= 0)
        gates = 1.0 / (1.0 + jnp.exp(-gl_ref[...]))
        for g in range(NSA_KV):
            ksl = slice(g * NSA_DK, (g + 1) * NSA_DK)
            vsl = slice(g * NSA_DV, (g + 1) * NSA_DV)
            qg = q_of(g)[:, 0:NSA_DK]
            s = jnp.concatenate([_dot_nt(qg, kwc_ref[0, :, ksl]), _dot_nt(qg, kwn_ref[0, :, ksl])], axis=1)
            s = jnp.where(wok, s - slope_col(g) * dw.astype(jnp.float32), _NEG)
            p = jnp.where(wok, jnp.exp(s - jnp.max(s, axis=1, keepdims=True)), 0.0)
            p = (p / jnp.maximum(jnp.sum(p, axis=1, keepdims=True), 1e-30)).astype(jnp.bfloat16)
            o_w = (jnp.dot(p[:, 0:nwc], vwc_ref[0, :, vsl].astype(jnp.bfloat16), preferred_element_type=jnp.float32)
                   + jnp.dot(p[:, nwc:], vwn_ref[0, :, vsl].astype(jnp.bfloat16), preferred_element_type=jnp.float32))
            o_s = acc_scr[g] / l_scr[g]
            o_c = oc_scr[g]
            for r in range(NSA_REP):
                rs = slice(r * s_tok, (r + 1) * s_tok)
                c0 = g * LANE + 3 * r
                o = (gates[:, c0:c0 + 1] * o_c[rs] + gates[:, c0 + 1:c0 + 2] * o_s[rs]
                     + gates[:, c0 + 2:c0 + 3] * o_w[rs])
                hs = slice((g * NSA_REP + r) * NSA_DV, (g * NSA_REP + r + 1) * NSA_DV)
                z = z_ref[:, hs]
                o_ref[:, hs] = o * (z * (1.0 / (1.0 + jnp.exp(-z))))


def nsa_sample_attention(q_rows, kcmp, vcmp, pool_k, pool_v, page_table, k_new, v_new, kw_cache, vw_cache,
                         kw_new, vw_new, gl, z, *, s_tok):
    batch, npg = page_table.shape
    pages = _SMP_PAGES
    steps = npg // pages
    past = npg * PAGE_ROWS
    nsb = -(-(past + s_tok) // SLC_BLOCK)
    rows = NSA_REP * s_tok
    ncmp = kcmp.shape[2]
    nwc = kw_cache.shape[1]
    page_spec = lambda k, w: pl.BlockSpec((1, PAGE_ROWS, w), lambda b, i, tab: (tab[b, i * pages + k], 0, 0))
    per_b = lambda shape: pl.BlockSpec((1,) + shape, lambda b, i, tab: (b,) + (0,) * len(shape))
    tok_blk = lambda w: pl.BlockSpec((s_tok, w), lambda b, i, tab: (b, 0))
    return pl.pallas_call(
        functools.partial(_nsa_sample_kernel, past=past, steps=steps, pages=pages, nsb=nsb),
        grid_spec=pltpu.PrefetchScalarGridSpec(
            num_scalar_prefetch=1,
            grid=(batch, steps),
            in_specs=[page_spec(k, NSA_KW) for k in range(pages)] + [page_spec(k, NSA_VW) for k in range(pages)]
            + [per_b((NSA_KV * rows, NSA_DKP)), per_b((NSA_KV, ncmp, NSA_DKP)), per_b((NSA_KV, ncmp, NSA_DV)),
               per_b((PAGE_ROWS, NSA_KW)), per_b((PAGE_ROWS, NSA_VW)),
               per_b((nwc, NSA_KW)), per_b((nwc, NSA_VW)), per_b((PAGE_ROWS, NSA_KW)), per_b((PAGE_ROWS, NSA_VW)),
               tok_blk(NSA_KV * LANE), tok_blk(NSA_O)],
            out_specs=tok_blk(NSA_O),
            scratch_shapes=[pltpu.VMEM((NSA_KV, rows, 1), jnp.float32), pltpu.VMEM((NSA_KV, rows, 1), jnp.float32),
                            pltpu.VMEM((NSA_KV, rows, NSA_DV), jnp.float32),
                            pltpu.VMEM((NSA_KV, s_tok, _SEL_PAD), jnp.float32),
                            pltpu.VMEM((NSA_KV, rows, NSA_DV), jnp.float32)]),
        out_shape=jax.ShapeDtypeStruct((batch * s_tok, NSA_O), jnp.float32),
        compiler_params=_cparams("parallel", "arbitrary"),
        name="nsa_sample_attention",
    )(page_table, *([pool_k] * pages), *([pool_v] * pages), q_rows, kcmp, vcmp, k_new, v_new,
      kw_cache, vw_cache, kw_new, vw_new, gl, z)


def _nsa_sample(x, ck, cv, sk, sv, wk_buf, wv_buf, page_table, norm_w,
                w_in, q_norm, kn_c, kn_s, kn_w, pe_k, w1_k, w2_k, pe_v, w1_v, w2_v, w_out):
    B, S, D = x.shape
    n_pool = ck.shape[0]
    x2, q, kc_r, vc_r, ks_raw, vs_r, kw_raw, vw_r, gl, z = _nsa_in_proj(x, norm_w, w_in)
    q_hm = nsa_q_prep(q, q_norm, tq=B * S)
    q_rows = q_hm.reshape(NSA_KV, NSA_REP, B, S, NSA_DKP).transpose(2, 0, 1, 3, 4).reshape(B, NSA_KV * NSA_REP * S, NSA_DKP)
    ks_r, _ = nsa_k_prep(ks_raw, kn_s, seq=S, tk=B * S, onehot=False)
    kw_r, _ = nsa_k_prep(kw_raw, kn_w, seq=S, tk=B * S, onehot=False)
    as_page = lambda a: jnp.pad(a.reshape(B, S, -1), ((0, 0), (0, PAGE_ROWS - S), (0, 0)))
    seg3 = lambda a, w: a.reshape(-1, CMP_STRIDE, w)
    ident = jnp.arange(B, dtype=jnp.int32)[:, None]
    ka, kb = nsa_compress_partials(seg3(ck, NSA_KW), page_table, pe_k, w1_k, pages=_CMP_PAGES)
    va, vb = nsa_compress_partials(seg3(cv, NSA_VW), page_table, pe_v, w1_v, pages=_CMP_PAGES)
    _, kb_new = nsa_compress_partials(seg3(as_page(kc_r), NSA_KW), ident, pe_k, w1_k, pages=1)
    _, vb_new = nsa_compress_partials(seg3(as_page(vc_r), NSA_VW), ident, pe_v, w1_v, pages=1)
    kcmp = nsa_compress_finish(ka, kb, kb_new, w2_k, kn_c, is_key=True, alibi_lanes=True)
    vcmp = nsa_compress_finish(va, vb, vb_new, w2_v, jnp.zeros((NSA_DV,), jnp.float32), is_key=False, alibi_lanes=False)
    wl = wk_buf.shape[1]
    o = nsa_sample_attention(q_rows, kcmp, vcmp, sk.reshape(n_pool, PAGE_ROWS, NSA_KW), sv.reshape(n_pool, PAGE_ROWS, NSA_VW),
                             page_table, as_page(ks_r), as_page(vs_r), wk_buf.reshape(B, wl, NSA_KW),
                             wv_buf.reshape(B, wl, NSA_VW), as_page(kw_r), as_page(vw_r), gl, z, s_tok=S)
    y = matmul(o.astype(jnp.bfloat16), w_out.astype(jnp.bfloat16), tm=B * S, tn=1024, residual=x2, name="nsa_out_proj")
    r4 = lambda a, d: a.reshape(B, S, NSA_KV, d)
    kw_all = jnp.concatenate([wk_buf, r4(kw_r, NSA_DK)], axis=1)
    vw_all = jnp.concatenate([wv_buf, r4(vw_r, NSA_DV)], axis=1)
    return y.reshape(B, S, D), (r4(kc_r, NSA_DK), r4(vc_r, NSA_DV), r4(ks_r, NSA_DK), r4(vs_r, NSA_DV),
                                kw_all[:, S:], vw_all[:, S:])


def kernel(x_prompt, x_sample, state_delta, state_conv, cache_cmp_k, cache_cmp_v, cache_slc_k, cache_slc_v, cache_win_k, cache_win_v, page_table, norm_dn, w_in_dn, conv_w_dn, a_log_dn, dt_bias_dn, out_norm_dn, w_out_dn, norm_nsa, w_in_nsa, q_norm_nsa, k_norm_cmp, k_norm_slc, k_norm_win, cmp_pe_k, cmp_w1_k, cmp_w2_k, cmp_pe_v, cmp_w1_v, cmp_w2_v, w_out_nsa):
    xp, xs = x_prompt, x_sample
    B = xp.shape[0]
    dw = (norm_dn[0], w_in_dn[0], conv_w_dn[0], a_log_dn[0], dt_bias_dn[0], out_norm_dn[0], w_out_dn[0])
    buf0 = jnp.zeros((B, DN_CONV - 1, conv_w_dn.shape[-1]), xp.dtype)
    st0 = jnp.zeros((B, DN_HEADS, DN_DK, DN_DV), jnp.float32)
    xp, p_dn = _deltanet_mixer(xp, buf0, st0, *dw)
    xs, s_dn = _deltanet_mixer(xs, state_conv[0], state_delta[0], *dw)
    nw = (norm_nsa[0], w_in_nsa[0], q_norm_nsa[0], k_norm_cmp[0], k_norm_slc[0], k_norm_win[0],
          cmp_pe_k[0], cmp_w1_k[0], cmp_w2_k[0], cmp_pe_v[0], cmp_w1_v[0], cmp_w2_v[0], w_out_nsa[0])
    xp, p_nsa = _nsa_prompt(xp, *nw)
    xs, s_nsa = _nsa_sample(xs, cache_cmp_k[0], cache_cmp_v[0], cache_slc_k[0], cache_slc_v[0],
                            cache_win_k[0], cache_win_v[0], page_table, *nw)
    return ((xp, xs, p_dn[0][None], p_dn[1][None]) + tuple(t[None] for t in p_nsa)
            + (s_dn[0][None], s_dn[1][None]) + tuple(t[None] for t in s_nsa))
```

```python
import functools

import jax
import jax.numpy as jnp
import numpy as np
from jax import lax
from jax.experimental import pallas as pl
from jax.experimental.pallas import tpu as pltpu

D_MODEL = 4096
EPS = 1e-6

DN_HEADS = 32
DN_DK = 128
DN_DV = 128
DN_CONV = 4
DN_CHUNK = 64

NSA_HEADS = 32
NSA_KV = 4
NSA_REP = NSA_HEADS // NSA_KV
NSA_DK = 192
NSA_DV = 128
CMP_BLOCK = 32
CMP_STRIDE = 16
CMP_RATIO = CMP_BLOCK // CMP_STRIDE
CMP_HIDDEN = 256
SLC_BLOCK = 64
SLC_TOPN = 16
WINDOW = 512
NSA_Q = NSA_HEADS * NSA_DK
NSA_KW = NSA_KV * NSA_DK
NSA_VW = NSA_KV * NSA_DV
NSA_O = NSA_HEADS * NSA_DV
NSA_SPLITS = (NSA_Q, NSA_KW, NSA_VW, NSA_KW, NSA_VW, NSA_KW, NSA_VW, 3 * NSA_HEADS, NSA_O)

VMEM_LIMIT_BYTES = 56 * 1024 * 1024
LANE = 128


def _cparams(*sem):
    return pltpu.CompilerParams(dimension_semantics=sem, vmem_limit_bytes=VMEM_LIMIT_BYTES)


def _rmsnorm_kernel(x_ref, w_ref, o_ref):
    x = x_ref[...]
    ms = jnp.mean(x * x, axis=-1, keepdims=True)
    o_ref[...] = (x * lax.rsqrt(ms + EPS) * w_ref[...]).astype(o_ref.dtype)


def rmsnorm_bf16(x, w, *, tm):
    m, d = x.shape
    return pl.pallas_call(
        _rmsnorm_kernel,
        grid=(m // tm,),
        in_specs=[pl.BlockSpec((tm, d), lambda i: (i, 0)), pl.BlockSpec((1, d), lambda i: (0, 0))],
        out_specs=pl.BlockSpec((tm, d), lambda i: (i, 0)),
        out_shape=jax.ShapeDtypeStruct((m, d), jnp.bfloat16),
        compiler_params=_cparams("parallel"),
        name="rmsnorm_bf16",
    )(x, w.reshape(1, d))


def _mm_kernel(a_ref, b_ref, o_ref):
    o_ref[...] = jnp.dot(a_ref[...], b_ref[...], preferred_element_type=jnp.float32).astype(o_ref.dtype)


def _mm_res_kernel(a_ref, b_ref, r_ref, o_ref):
    acc = jnp.dot(a_ref[...], b_ref[...], preferred_element_type=jnp.float32)
    o_ref[...] = r_ref[...] + acc


def matmul(a, b, *, tm, tn, residual=None, out_dtype=jnp.float32, name="matmul"):
    m, k = a.shape
    _, n = b.shape
    assert m % tm == 0 and n % tn == 0
    in_specs = [pl.BlockSpec((tm, k), lambda i, j: (i, 0)), pl.BlockSpec((k, tn), lambda i, j: (0, j))]
    args = [a, b]
    kern = _mm_kernel
    if residual is not None:
        in_specs.append(pl.BlockSpec((tm, tn), lambda i, j: (i, j)))
        args.append(residual)
        kern = _mm_res_kernel
    return pl.pallas_call(
        kern,
        grid=(m // tm, n // tn),
        in_specs=in_specs,
        out_specs=pl.BlockSpec((tm, tn), lambda i, j: (i, j)),
        out_shape=jax.ShapeDtypeStruct((m, n), out_dtype),
        compiler_params=_cparams("parallel", "parallel"),
        name=name,
    )(*args)


def _row_tile(m):
    return 1024 if m % 1024 == 0 else m


def _dot(a, b):
    return jnp.dot(a.astype(jnp.bfloat16), b.astype(jnp.bfloat16), preferred_element_type=jnp.float32)


def _dot_nt(a, b):
    return lax.dot_general(a.astype(jnp.bfloat16), b.astype(jnp.bfloat16), (((1,), (1,)), ((), ())),
                           preferred_element_type=jnp.float32)


def _dot_tn(a, b):
    return lax.dot_general(a.astype(jnp.bfloat16), b.astype(jnp.bfloat16), (((0,), (0,)), ((), ())),
                           preferred_element_type=jnp.float32)


def _split_bf16(x):
    hi = x.astype(jnp.bfloat16)
    lo = (x - hi.astype(jnp.float32)).astype(jnp.bfloat16)
    return hi, lo


def _dn_gate_kernel(h_ref, wt_ref, alog_ref, dtb_ref, o_ref, *, heads):
    r = lax.dot_general(wt_ref[...], h_ref[...], (((1,), (1,)), ((), ())),
                        preferred_element_type=jnp.float32)
    b = r[:heads]
    a = r[heads:] + dtb_ref[...]
    softplus = jnp.maximum(a, 0.0) + jnp.log(1.0 + jnp.exp(-jnp.abs(a)))
    o_ref[0:heads, :] = 1.0 / (1.0 + jnp.exp(-b))
    o_ref[heads:2 * heads, :] = -jnp.exp(alog_ref[...]) * softplus


def dn_gates(h, w_ba_t, a_log, dt_bias, *, tm):
    m, d = h.shape
    heads = a_log.shape[0]
    return pl.pallas_call(
        functools.partial(_dn_gate_kernel, heads=heads),
        grid=(m // tm,),
        in_specs=[pl.BlockSpec((tm, d), lambda i: (i, 0)),
                  pl.BlockSpec((2 * heads, d), lambda i: (0, 0)),
                  pl.BlockSpec((heads, 1), lambda i: (0, 0)),
                  pl.BlockSpec((heads, 1), lambda i: (0, 0))],
        out_specs=pl.BlockSpec((2 * heads, tm), lambda i: (0, i)),
        out_shape=jax.ShapeDtypeStruct((2 * heads, m), jnp.float32),
        compiler_params=_cparams("parallel"),
        name="dn_gates",
    )(h, w_ba_t, a_log.reshape(heads, 1), dt_bias.reshape(heads, 1))


_DN_TAIL = 8
_DN_TB = 256
_DN_TMIN = 2 * DN_CHUNK
_DN_HB = 4


def _dn_chunk_prep(q, k, g_row, b_row):
    c = q.shape[0]
    ii = lax.broadcasted_iota(jnp.int32, (c, c), 0)
    jj = lax.broadcasted_iota(jnp.int32, (c, c), 1)
    lower, strict, eye = ii >= jj, ii > jj, ii == jj
    g_b = jnp.broadcast_to(g_row, (c, c))
    b_b = jnp.broadcast_to(b_row, (c, c))
    b_col = jnp.sum(jnp.where(eye, b_b, 0.0), axis=1, keepdims=True)
    g_col = jnp.sum(jnp.where(eye, g_b, 0.0), axis=1, keepdims=True)
    gc_col = jnp.sum(jnp.where(lower, g_b, 0.0), axis=1, keepdims=True)
    gc_row = jnp.sum(jnp.where(ii <= jj, g_col, 0.0), axis=0, keepdims=True)
    g_last = jnp.sum(g_row, axis=1, keepdims=True)
    decay = jnp.exp(jnp.where(lower, gc_col - gc_row, -jnp.inf))
    kk = _dot_nt(k, k)
    qk = _dot_nt(q, k)
    lmat = jnp.where(strict, kk * decay, 0.0) * b_col
    attn = qk * decay
    return dict(x=-lmat, p=eye.astype(jnp.float32) - lmat, attn=attn, b_row=b_row,
                e_row=jnp.exp(gc_row), e_col=jnp.exp(gc_col), c_col=jnp.exp(g_last - gc_col),
                e_last=jnp.exp(g_last))


def _dn_inverse_levels(items, c):
    m = 2
    while m < c:
        for it in items:
            it["x"] = _dot(it["x"], it["x"])
        for it in items:
            it["p"] = it["p"] + _dot(it["p"], it["x"])
        m *= 2


def _dn_kernel(q_ref, k_ref, v_ref, z_ref, beta_ref, g_ref, cw_ref, cs_ref, s0_ref, onw_ref,
               o_ref, s_out_ref, xpad, s_scr, *, tb, hb):
    n = pl.program_id(2)
    c = DN_CHUNK
    d = DN_DK
    nc = tb // c

    @pl.when(n == 0)
    def _init():
        xpad[:, 0:_DN_TAIL, :] = cs_ref[0]
        s_scr[...] = s0_ref[0]

    def conv(i, ref):
        xpad[i, _DN_TAIL:_DN_TAIL + tb, :] = ref[...]
        acc = None
        for j in range(DN_CONV):
            off = _DN_TAIL - (DN_CONV - 1) + j
            term = xpad[i, off:off + tb, :] * cw_ref[i, j:j + 1, :]
            acc = term if acc is None else acc + term
        xpad[i, 0:_DN_TAIL, :] = xpad[i, tb:tb + _DN_TAIL, :]
        return acc * (1.0 / (1.0 + jnp.exp(-acc)))

    qc, kc, vc = conv(0, q_ref), conv(1, k_ref), conv(2, v_ref)
    items = []
    for hh in range(hb):
        sl = slice(hh * d, (hh + 1) * d)
        q = qc[:, sl]
        k = kc[:, sl]
        q = q * (lax.rsqrt(jnp.sum(q * q, axis=-1, keepdims=True) + EPS) * (d ** -0.5))
        k = k * lax.rsqrt(jnp.sum(k * k, axis=-1, keepdims=True) + EPS)
        g_all = g_ref[hh]
        b_all = beta_ref[hh]
        for ci in range(nc):
            rs = slice(ci * c, (ci + 1) * c)
            it = _dn_chunk_prep(q[rs], k[rs], g_all[:, rs], b_all[:, rs])
            it.update(q=q[rs], k=k[rs], v=vc[rs, sl])
            items.append(it)
    _dn_inverse_levels(items, c)
    for it in items:
        inv_b = it["p"] * it["b_row"]
        it["u"] = _dot(inv_b, it["v"])
        it["w"] = _dot(inv_b * it["e_row"], it["k"])
    states = [s_scr[hh] for hh in range(hb)]
    outs = [[None] * nc for _ in range(hb)]
    for ci in range(nc):
        for hh in range(hb):
            it = items[hh * nc + ci]
            s = states[hh]
            v_new = it["u"] - _dot(it["w"], s)
            outs[hh][ci] = it["e_col"] * _dot(it["q"], s) + _dot(it["attn"], v_new)
            states[hh] = s * it["e_last"] + _dot_tn(it["k"] * it["c_col"], v_new)
    for hh in range(hb):
        sl = slice(hh * d, (hh + 1) * d)
        s_scr[hh] = states[hh]
        o = jnp.concatenate(outs[hh], axis=0) if nc > 1 else outs[hh][0]
        o = o * lax.rsqrt(jnp.mean(o * o, axis=-1, keepdims=True) + EPS) * onw_ref[...]
        z = z_ref[:, sl]
        o_ref[:, sl] = (o * (z * (1.0 / (1.0 + jnp.exp(-z))))).astype(o_ref.dtype)

    @pl.when(n == pl.num_programs(2) - 1)
    def _fin():
        s_out_ref[0] = s_scr[...]


def dn_delta(proj, gates, conv_w, conv_state, s0, out_norm, *, batch, tb, hb):
    rows, width = proj.shape
    d = DN_DK
    heads = width // (4 * d)
    t = rows // batch
    nt = t // tb
    hg = heads // hb
    wd = hb * d
    row_map = lambda off: (lambda b, h, n: (b * nt + n, off * hg + h))
    gate_map = lambda off: (lambda b, h, n: (off * hg + h, 0, b * nt + n))
    return pl.pallas_call(
        functools.partial(_dn_kernel, tb=tb, hb=hb),
        grid=(batch, hg, nt),
        in_specs=[pl.BlockSpec((tb, wd), row_map(0)),
                  pl.BlockSpec((tb, wd), row_map(1)),
                  pl.BlockSpec((tb, wd), row_map(2)),
                  pl.BlockSpec((tb, wd), row_map(3)),
                  pl.BlockSpec((hb, 1, tb), gate_map(0)),
                  pl.BlockSpec((hb, 1, tb), gate_map(1)),
                  pl.BlockSpec((3, DN_CONV, wd), lambda b, h, n: (0, 0, h)),
                  pl.BlockSpec((1, 3, _DN_TAIL, wd), lambda b, h, n: (b, 0, 0, h)),
                  pl.BlockSpec((1, hb, d, d), lambda b, h, n: (b, h, 0, 0)),
                  pl.BlockSpec((1, d), lambda b, h, n: (0, 0))],
        out_specs=[pl.BlockSpec((tb, wd), lambda b, h, n: (b * nt + n, h)),
                   pl.BlockSpec((1, hb, d, d), lambda b, h, n: (b, h, 0, 0))],
        out_shape=[jax.ShapeDtypeStruct((rows, heads * d), jnp.bfloat16),
                   jax.ShapeDtypeStruct(s0.shape, jnp.float32)],
        scratch_shapes=[pltpu.VMEM((3, tb + _DN_TAIL, wd), jnp.float32),
                        pltpu.VMEM((hb, d, d), jnp.float32)],
        compiler_params=_cparams("parallel", "parallel", "arbitrary"),
        name="dn_delta",
    )(proj, proj, proj, proj, gates, gates, conv_w, conv_state, s0, out_norm.reshape(1, d))


def _deltanet_mixer(x, conv_buf, s0, norm_w, w_in, conv_w, a_log, dt_bias, out_norm, w_out):
    B, L, D = x.shape
    H = a_log.shape[0]
    cdim = conv_w.shape[1]
    hd = cdim // 3
    x2 = x.reshape(B * L, D)
    h = rmsnorm_bf16(x2, norm_w, tm=min(256, B * L))
    w_main = w_in[:, :cdim + hd].astype(jnp.bfloat16)
    w_ba_t = w_in[:, cdim + hd:].T.astype(jnp.bfloat16)
    proj = matmul(h, w_main, tm=_row_tile(B * L), tn=1024, name="dn_in_proj")
    gates = dn_gates(h, w_ba_t, a_log, dt_bias, tm=_row_tile(B * L))
    new_buf = proj.reshape(B, L, -1)[:, L - (DN_CONV - 1):, :cdim]
    T = -(-L // _DN_TMIN) * _DN_TMIN
    if T != L:
        proj = jnp.pad(proj.reshape(B, L, -1), ((0, 0), (0, T - L), (0, 0))).reshape(B * T, -1)
        gates = jnp.pad(gates.reshape(2 * H, B, L), ((0, 0), (0, 0), (0, T - L))).reshape(2 * H, B * T)
    cw = conv_w.reshape(DN_CONV, 3, hd).transpose(1, 0, 2)
    cs = conv_buf.reshape(B, DN_CONV - 1, 3, hd).transpose(0, 2, 1, 3)
    cs = jnp.pad(cs, ((0, 0), (0, 0), (_DN_TAIL - (DN_CONV - 1), 0), (0, 0)))
    tb = min(_DN_TB, T)
    o, s_new = dn_delta(proj, gates.reshape(2 * H, 1, B * T), cw, cs, s0.astype(jnp.float32), out_norm,
                        batch=B, tb=tb, hb=_DN_HB)
    if T != L:
        o = o.reshape(B, T, -1)[:, :L].reshape(B * L, -1)
    y = matmul(o, w_out.astype(jnp.bfloat16), tm=_row_tile(B * L), tn=1024, residual=x2, name="dn_out_proj")
    return y.reshape(B, L, D), (s_new, new_buf)


def _alibi_slopes():
    hh = jnp.arange(1, NSA_HEADS + 1, dtype=jnp.float32)
    return jnp.exp2(-8.0 * hh / NSA_HEADS).reshape(NSA_KV, NSA_REP)


PAGE_ROWS = 128
NSA_DKP = 256
_POS_LANE = NSA_DK
_SEL_LANE = NSA_DK + 4
_SEL_LANES = 32
_MASK_BIG = 2.0 ** 100
_NEG = -1e30
_NSA_TQ = 128
_NSA_TK = 256
_CMP_PAGES = 16


def _slope_table():
    s = _alibi_slopes()
    hi = s.astype(jnp.bfloat16).astype(jnp.float32)
    lo = (s - hi).astype(jnp.bfloat16).astype(jnp.float32)
    tab = jnp.stack([64.0 * hi, hi, 64.0 * lo, lo], axis=-1)
    return jnp.pad(tab, ((0, 0), (0, 0), (0, NSA_DKP - NSA_DK - 4)))


def _q_prep_kernel(q_ref, w_ref, tab_ref, o_ref):
    tq = q_ref.shape[0]
    for r in range(NSA_REP):
        x = q_ref[:, r * NSA_DK:(r + 1) * NSA_DK]
        y = x * lax.rsqrt(jnp.mean(x * x, axis=-1, keepdims=True) + EPS) * (w_ref[...] * (NSA_DK ** -0.5))
        o_ref[0, r, :, 0:NSA_DK] = y.astype(o_ref.dtype)
        o_ref[0, r, :, NSA_DK:NSA_DKP] = jnp.broadcast_to(tab_ref[0, r:r + 1, :], (tq, NSA_DKP - NSA_DK)).astype(o_ref.dtype)


def nsa_q_prep(q_raw, q_norm, *, tq):
    rows = q_raw.shape[0]
    return pl.pallas_call(
        _q_prep_kernel,
        grid=(rows // tq, NSA_KV),
        in_specs=[pl.BlockSpec((tq, NSA_REP * NSA_DK), lambda i, g: (i, g)),
                  pl.BlockSpec((1, NSA_DK), lambda i, g: (0, 0)),
                  pl.BlockSpec((1, NSA_REP, NSA_DKP - NSA_DK), lambda i, g: (g, 0, 0))],
        out_specs=pl.BlockSpec((1, NSA_REP, tq, NSA_DKP), lambda i, g: (g, 0, i, 0)),
        out_shape=jax.ShapeDtypeStruct((NSA_KV, NSA_REP, rows, NSA_DKP), jnp.bfloat16),
        compiler_params=_cparams("parallel", "parallel"),
        name="nsa_q_prep",
    )(q_raw, q_norm.reshape(1, NSA_DK), _slope_table())


def _k_prep_kernel(k_ref, w_ref, on_ref, oh_ref, *, seq, onehot):
    tk = k_ref.shape[0]
    pos = (pl.program_id(0) * tk) % seq + lax.broadcasted_iota(jnp.int32, (tk, NSA_DKP - NSA_DK), 0)
    lane = lax.broadcasted_iota(jnp.int32, (tk, NSA_DKP - NSA_DK), 1)
    blk, off = pos // SLC_BLOCK, pos % SLC_BLOCK
    extra = jnp.where((lane == 0) | (lane == 2), blk, jnp.where((lane == 1) | (lane == 3), off, 0))
    if onehot:
        extra = jnp.where((lane >= 4) & (lane - 4 == blk), 1, extra)
    extra = extra.astype(jnp.float32)
    for g in range(NSA_KV):
        x = k_ref[:, g * NSA_DK:(g + 1) * NSA_DK]
        x = x * lax.rsqrt(jnp.mean(x * x, axis=-1, keepdims=True) + EPS) * w_ref[...]
        on_ref[:, g * NSA_DK:(g + 1) * NSA_DK] = x
        oh_ref[g, :, 0:NSA_DK] = x.astype(oh_ref.dtype)
        oh_ref[g, :, NSA_DK:NSA_DKP] = extra.astype(oh_ref.dtype)


def nsa_k_prep(k_raw, k_norm, *, seq, tk, onehot):
    rows = k_raw.shape[0]
    return pl.pallas_call(
        functools.partial(_k_prep_kernel, seq=seq, onehot=onehot),
        grid=(rows // tk,),
        in_specs=[pl.BlockSpec((tk, NSA_KW), lambda i: (i, 0)), pl.BlockSpec((1, NSA_DK), lambda i: (0, 0))],
        out_specs=[pl.BlockSpec((tk, NSA_KW), lambda i: (i, 0)),
                   pl.BlockSpec((NSA_KV, tk, NSA_DKP), lambda i: (0, i, 0))],
        out_shape=[jax.ShapeDtypeStruct((rows, NSA_KW), jnp.float32),
                   jax.ShapeDtypeStruct((NSA_KV, rows, NSA_DKP), jnp.bfloat16)],
        compiler_params=_cparams("parallel"),
        name="nsa_k_prep",
    )(k_raw, k_norm.reshape(1, NSA_DK))


def _v_prep_kernel(v_ref, o_ref):
    for g in range(NSA_KV):
        o_ref[g] = v_ref[:, g * NSA_DV:(g + 1) * NSA_DV].astype(o_ref.dtype)


def nsa_v_prep(v_raw, *, tk):
    rows = v_raw.shape[0]
    return pl.pallas_call(
        _v_prep_kernel,
        grid=(rows // tk,),
        in_specs=[pl.BlockSpec((tk, NSA_VW), lambda i: (i, 0))],
        out_specs=pl.BlockSpec((NSA_KV, tk, NSA_DV), lambda i: (0, i, 0)),
        out_shape=jax.ShapeDtypeStruct((NSA_KV, rows, NSA_DV), jnp.bfloat16),
        compiler_params=_cparams("parallel"),
        name="nsa_v_prep",
    )(v_raw)


def _compress_partials_kernel(tab_ref, *refs, d, pages):
    x_refs, (pe_ref, w1_ref, a_ref, b_ref) = refs[:pages], refs[pages:]
    seg = x_refs[0].shape[0]
    acc_a = acc_b = None
    def take(r, l, g):
        return r[:, l, g, :] if len(r.shape) == 4 else r[:, l, g * d:(g + 1) * d]

    for l in range(CMP_STRIDE):
        xg = jnp.concatenate([take(r, l, g) for g in range(NSA_KV) for r in x_refs], axis=0)
        ta = jnp.dot((xg + pe_ref[l:l + 1, :]).astype(jnp.bfloat16), w1_ref[l], preferred_element_type=jnp.float32)
        tb = jnp.dot((xg + pe_ref[CMP_STRIDE + l:CMP_STRIDE + l + 1, :]).astype(jnp.bfloat16),
                     w1_ref[CMP_STRIDE + l], preferred_element_type=jnp.float32)
        acc_a = ta if acc_a is None else acc_a + ta
        acc_b = tb if acc_b is None else acc_b + tb
    n = pages * seg
    for g in range(NSA_KV):
        a_ref[0, g] = acc_a[g * n:(g + 1) * n]
        b_ref[0, g] = acc_b[g * n:(g + 1) * n]


def nsa_compress_partials(pool3, table, pe, w1, *, pages):
    d = pe.shape[1]
    batch, npg = table.shape
    seg = PAGE_ROWS // CMP_STRIDE
    tail = tuple(pool3.shape[2:])
    x_spec = lambda k: pl.BlockSpec((seg, CMP_STRIDE) + tail,
                                    lambda b, i, tab: (tab[b, i * pages + k],) + (0,) * (1 + len(tail)))
    out_spec = pl.BlockSpec((1, NSA_KV, pages * seg, CMP_HIDDEN), lambda b, i, tab: (b, 0, i, 0))
    out_sds = jax.ShapeDtypeStruct((batch, NSA_KV, npg * seg, CMP_HIDDEN), jnp.float32)
    return pl.pallas_call(
        functools.partial(_compress_partials_kernel, d=d, pages=pages),
        grid_spec=pltpu.PrefetchScalarGridSpec(
            num_scalar_prefetch=1,
            grid=(batch, npg // pages),
            in_specs=[x_spec(k) for k in range(pages)]
            + [pl.BlockSpec((CMP_BLOCK, d), lambda b, i, tab: (0, 0)),
               pl.BlockSpec((CMP_BLOCK, d, CMP_HIDDEN), lambda b, i, tab: (0, 0, 0))],
            out_specs=[out_spec, out_spec]),
        out_shape=[out_sds, out_sds],
        compiler_params=_cparams("parallel", "arbitrary"),
        name="nsa_compress_partials",
    )(table, *([pool3] * pages), pe, w1.astype(jnp.bfloat16))


def _cmp_extra_lanes(nrows, width):
    n = lax.broadcasted_iota(jnp.int32, (nrows, width), 0)
    lane = lax.broadcasted_iota(jnp.int32, (nrows, width), 1)
    ec = n * CMP_STRIDE + CMP_BLOCK - 1
    blk, off = ec // SLC_BLOCK, ec % SLC_BLOCK
    pl_ = lane - _POS_LANE
    return jnp.where((pl_ == 0) | (pl_ == 2), blk, jnp.where((pl_ == 1) | (pl_ == 3), off, 0)).astype(jnp.float32)


def _compress_finish_kernel(a_ref, b_ref, bn_ref, w2_ref, kn_ref, o_ref, *, is_key, has_new, alibi_lanes):
    ns = a_ref.shape[2]
    row = lax.broadcasted_iota(jnp.int32, (ns, 1), 0)
    for g in range(NSA_KV):
        nxt = pltpu.roll(b_ref[0, g], ns - 1, 0)
        if has_new:
            nxt = jnp.where(row == ns - 1, bn_ref[0, g, 0:1, :], nxt)
        hid_pre = a_ref[0, g] + nxt
        hid = hid_pre * (1.0 / (1.0 + jnp.exp(-hid_pre)))
        out = jnp.dot(hid.astype(jnp.bfloat16), w2_ref[...], preferred_element_type=jnp.float32)
        if is_key:
            ms = jnp.sum(out * out, axis=-1, keepdims=True) * (1.0 / NSA_DK)
            out = out * lax.rsqrt(ms + EPS) * kn_ref[...]
            if alibi_lanes:
                out = out + _cmp_extra_lanes(ns, out.shape[1])
        o_ref[0, g] = out.astype(o_ref.dtype)


def nsa_compress_finish(part_a, part_b, part_b_new, w2, kn, *, is_key, alibi_lanes):
    batch, _, ns, _ = part_a.shape
    d = w2.shape[1]
    dp = NSA_DKP if is_key else d
    has_new = part_b_new is not None
    if not has_new:
        part_b_new = jnp.zeros((batch, NSA_KV, 8, CMP_HIDDEN), jnp.float32)
    w2p = jnp.pad(w2, ((0, 0), (0, dp - d))).astype(jnp.bfloat16)
    knp = jnp.pad(kn, (0, dp - d)).reshape(1, dp)
    blk = lambda n: pl.BlockSpec((1, NSA_KV, n, CMP_HIDDEN), lambda b: (b, 0, 0, 0))
    return pl.pallas_call(
        functools.partial(_compress_finish_kernel, is_key=is_key, has_new=has_new, alibi_lanes=alibi_lanes),
        grid=(batch,),
        in_specs=[blk(ns), blk(ns), blk(part_b_new.shape[2]),
                  pl.BlockSpec((CMP_HIDDEN, dp), lambda b: (0, 0)),
                  pl.BlockSpec((1, dp), lambda b: (0, 0))],
        out_specs=pl.BlockSpec((1, NSA_KV, ns, dp), lambda b: (b, 0, 0, 0)),
        out_shape=jax.ShapeDtypeStruct((batch, NSA_KV, ns, dp), jnp.bfloat16),
        compiler_params=_cparams("parallel"),
        name="nsa_compress_finish_k" if is_key else "nsa_compress_finish_v",
    )(part_a, part_b, part_b_new, w2p, knp)


def _split_dot(x, m_bf16):
    hi, lo = _split_bf16(x)
    f = functools.partial(jnp.dot, preferred_element_type=jnp.float32)
    return f(hi, m_bf16) + f(lo, m_bf16)


def _topn_mask(imp, t_col, nsb, topn):
    jb = lax.broadcasted_iota(jnp.int32, imp.shape, 1)
    cur = t_col // SLC_BLOCK
    forced = (jb == 0) | (jb == cur) | (jb == cur - 1)
    valid = (jb * SLC_BLOCK <= t_col) & (jb < nsb)
    impm = jnp.where(forced, jnp.inf, jnp.where(valid, imp, -jnp.inf))
    rank = jnp.zeros(imp.shape, jnp.float32)
    for k in range(nsb):
        col = impm[:, k:k + 1]
        beats = (col > impm) | ((col == impm) & (k < jb))
        rank = rank + jnp.where(beats, 1.0, 0.0)
    return (rank < topn) & (jb < nsb)


def _flash_step(q, k, v, mask, carry):
    m, l, acc = carry
    s = _dot_nt(q, k)
    s = jnp.where(mask, s, _NEG)
    m_new = jnp.maximum(m, jnp.max(s, axis=1, keepdims=True))
    alpha = jnp.exp(m - m_new)
    p = jnp.exp(s - m_new)
    l = alpha * l + jnp.sum(p, axis=1, keepdims=True)
    acc = alpha * acc + jnp.dot(p.astype(jnp.bfloat16), v, preferred_element_type=jnp.float32)
    return m_new, l, acc


def _nsa_prompt_kernel(q_ref, kc_ref, vc_ref, ks_ref, vs_ref, kw_ref, vw_ref, gl_ref, z_ref, o_ref,
                       *, tq, tk, seq):
    qi = pl.program_id(2)
    t0 = qi * tq
    rep = NSA_REP
    m_rows = rep * tq
    nc = seq // CMP_STRIDE - CMP_RATIO + 1
    nseg = kc_ref.shape[2]
    nsb = seq // SLC_BLOCK
    q = q_ref[0].reshape(m_rows, NSA_DKP)
    t_tok = t0 + lax.broadcasted_iota(jnp.int32, (tq, 1), 0)
    t_col = t0 + lax.broadcasted_iota(jnp.int32, (rep, tq, 1), 1).reshape(m_rows, 1)

    n_row = lax.broadcasted_iota(jnp.int32, (1, nseg), 1)
    cvalid = (n_row * CMP_STRIDE + CMP_BLOCK - 1 <= t_col) & (n_row < nc)
    s = jnp.where(cvalid, _dot_nt(q, kc_ref[0, 0]), _NEG)
    p = jnp.where(cvalid, jnp.exp(s - jnp.max(s, axis=1, keepdims=True)), 0.0)
    p = p / jnp.maximum(jnp.sum(p, axis=1, keepdims=True), 1e-30)
    o_c = jnp.dot(p.astype(jnp.bfloat16), vc_ref[0, 0], preferred_element_type=jnp.float32)
    psum = jnp.sum(p.reshape(rep, tq, nseg), axis=0)
    ci = lax.broadcasted_iota(jnp.int32, (nseg, _SEL_LANES), 0) * CMP_STRIDE
    cj = lax.broadcasted_iota(jnp.int32, (nseg, _SEL_LANES), 1) * SLC_BLOCK
    ov = jnp.maximum(jnp.minimum(ci + CMP_BLOCK, cj + SLC_BLOCK) - jnp.maximum(ci, cj), 0)
    c2s = (ov.astype(jnp.float32) * (1.0 / CMP_STRIDE)).astype(jnp.bfloat16)
    imp = _split_dot(psum, c2s)

    sel = _topn_mask(imp, t_tok, nsb, min(SLC_TOPN, nsb))
    pen = jnp.where(sel, 0.0, -_MASK_BIG).astype(jnp.bfloat16)
    pi = lax.broadcasted_iota(jnp.int32, (_SEL_LANES, NSA_DKP), 0)
    pj = lax.broadcasted_iota(jnp.int32, (_SEL_LANES, NSA_DKP), 1)
    place = jnp.where(pj == pi + _SEL_LANE, 1.0, 0.0).astype(jnp.bfloat16)
    q_add = jnp.dot(pen, place, preferred_element_type=jnp.float32).astype(jnp.bfloat16)
    q_slc = (q.reshape(rep, tq, NSA_DKP) + q_add[None]).reshape(m_rows, NSA_DKP)

    kpos = lax.broadcasted_iota(jnp.int32, (1, tk), 1)
    init = (jnp.full((m_rows, 1), _NEG, jnp.float32), jnp.zeros((m_rows, 1), jnp.float32),
            jnp.zeros((m_rows, NSA_DV), jnp.float32))

    def slc_body(kt, carry):
        k0 = pl.multiple_of(kt * tk, tk)
        mask = (k0 + kpos) <= t_col
        return _flash_step(q_slc, ks_ref[0, pl.ds(k0, tk), :], vs_ref[0, pl.ds(k0, tk), :], mask, carry)

    _, l_s, acc_s = lax.fori_loop(0, (t0 + tq + tk - 1) // tk, slc_body, init)
    o_s = acc_s / l_s

    def win_body(kt, carry):
        k0 = pl.multiple_of(kt * tk, tk)
        dw = t_col - (k0 + kpos)
        mask = (dw >= 0) & (dw < WINDOW)
        return _flash_step(q, kw_ref[0, pl.ds(k0, tk), :], vw_ref[0, pl.ds(k0, tk), :], mask, carry)

    kt_lo = jnp.maximum(t0 - (WINDOW - 1), 0) // tk
    _, l_w, acc_w = lax.fori_loop(kt_lo, (t0 + tq + tk - 1) // tk, win_body, init)
    o_w = acc_w / l_w

    gates = 1.0 / (1.0 + jnp.exp(-gl_ref[...]))
    for r in range(rep):
        rs = slice(r * tq, (r + 1) * tq)
        o = (gates[:, 3 * r:3 * r + 1] * o_c[rs] + gates[:, 3 * r + 1:3 * r + 2] * o_s[rs]
             + gates[:, 3 * r + 2:3 * r + 3] * o_w[rs])
        z = z_ref[:, r * NSA_DV:(r + 1) * NSA_DV]
        o_ref[:, r * NSA_DV:(r + 1) * NSA_DV] = (o * (z * (1.0 / (1.0 + jnp.exp(-z))))).astype(o_ref.dtype)


def nsa_prompt_attention(q_hm, kcmp, vcmp, ks_hm, vs_hm, kw_hm, vw_hm, gl, z, *, batch, tq, tk):
    rows = z.shape[0]
    seq = rows // batch
    nq = seq // tq
    nseg = kcmp.shape[2]
    row_blk = lambda b, g, i: (b * nq + i, g)
    kv_blk = lambda b, g, i: (g, b, 0)
    return pl.pallas_call(
        functools.partial(_nsa_prompt_kernel, tq=tq, tk=tk, seq=seq),
        grid=(batch, NSA_KV, nq),
        in_specs=[pl.BlockSpec((1, NSA_REP, tq, NSA_DKP), lambda b, g, i: (g, 0, b * nq + i, 0)),
                  pl.BlockSpec((1, 1, nseg, NSA_DKP), lambda b, g, i: (b, g, 0, 0)),
                  pl.BlockSpec((1, 1, nseg, NSA_DV), lambda b, g, i: (b, g, 0, 0)),
                  pl.BlockSpec((1, seq, NSA_DKP), kv_blk),
                  pl.BlockSpec((1, seq, NSA_DV), kv_blk),
                  pl.BlockSpec((1, seq, NSA_DKP), kv_blk),
                  pl.BlockSpec((1, seq, NSA_DV), kv_blk),
                  pl.BlockSpec((tq, LANE), row_blk),
                  pl.BlockSpec((tq, NSA_REP * NSA_DV), row_blk)],
        out_specs=pl.BlockSpec((tq, NSA_REP * NSA_DV), row_blk),
        out_shape=jax.ShapeDtypeStruct((rows, NSA_O), jnp.bfloat16),
        compiler_params=_cparams("parallel", "parallel", "arbitrary"),
        name="nsa_prompt_attention",
    )(q_hm, kcmp, vcmp, ks_hm, vs_hm, kw_hm, vw_hm, gl, z)


def _nsa_in_proj(x, norm_w, w_in):
    B, L, D = x.shape
    x2 = x.reshape(B * L, D)
    h = rmsnorm_bf16(x2, norm_w, tm=min(256, B * L))
    offs = np.cumsum((0,) + NSA_SPLITS)
    tm = _row_tile(B * L)

    def seg(i, tn, name):
        w = w_in[:, offs[i]:offs[i + 1]].astype(jnp.bfloat16)
        return matmul(h, w, tm=tm, tn=tn, name=name)

    q = seg(0, 1024, "nsa_q_proj")
    kc = seg(1, 768, "nsa_kc_proj")
    vc = seg(2, 512, "nsa_vc_proj")
    ks = seg(3, 768, "nsa_ks_proj")
    vs = seg(4, 512, "nsa_vs_proj")
    kw = seg(5, 768, "nsa_kw_proj")
    vw = seg(6, 512, "nsa_vw_proj")
    wg = w_in[:, offs[7]:offs[8]].reshape(D, NSA_KV, 3 * NSA_REP)
    wg = jnp.pad(wg, ((0, 0), (0, 0), (0, LANE - 3 * NSA_REP))).reshape(D, NSA_KV * LANE).astype(jnp.bfloat16)
    gl = matmul(h, wg, tm=tm, tn=NSA_KV * LANE, name="nsa_gate_proj")
    z = seg(8, 1024, "nsa_z_proj")
    return x2, q, kc, vc, ks, vs, kw, vw, gl, z


def _nsa_prompt(x, norm_w, w_in, q_norm, kn_c, kn_s, kn_w, pe_k, w1_k, w2_k, pe_v, w1_v, w2_v, w_out):
    B, T, D = x.shape
    x2, q, kc_r, vc_r, ks_raw, vs_r, kw_raw, vw_r, gl, z = _nsa_in_proj(x, norm_w, w_in)
    tp = min(512, T)
    q_hm = nsa_q_prep(q, q_norm, tq=tp)
    ks_r, ks_hm = nsa_k_prep(ks_raw, kn_s, seq=T, tk=tp, onehot=True)
    kw_r, kw_hm = nsa_k_prep(kw_raw, kn_w, seq=T, tk=tp, onehot=False)
    vs_hm = nsa_v_prep(vs_r, tk=tp)
    vw_hm = nsa_v_prep(vw_r, tk=tp)
    npg = T // PAGE_ROWS
    table = (jnp.arange(B, dtype=jnp.int32)[:, None] * npg + jnp.arange(npg, dtype=jnp.int32)[None, :])
    seg3 = lambda a: a.reshape(B * T // CMP_STRIDE, CMP_STRIDE, a.shape[1])
    ka, kb = nsa_compress_partials(seg3(kc_r), table, pe_k, w1_k, pages=min(_CMP_PAGES, npg))
    va, vb = nsa_compress_partials(seg3(vc_r), table, pe_v, w1_v, pages=min(_CMP_PAGES, npg))
    kcmp = nsa_compress_finish(ka, kb, None, w2_k, kn_c, is_key=True, alibi_lanes=True)
    vcmp = nsa_compress_finish(va, vb, None, w2_v, jnp.zeros((NSA_DV,), jnp.float32), is_key=False, alibi_lanes=False)
    og = nsa_prompt_attention(q_hm, kcmp, vcmp, ks_hm, vs_hm, kw_hm, vw_hm, gl, z, batch=B, tq=_NSA_TQ, tk=_NSA_TK)
    y = matmul(og, w_out.astype(jnp.bfloat16), tm=_row_tile(B * T), tn=1024, residual=x2, name="nsa_out_proj")
    wl = min(WINDOW, T)
    r4 = lambda a, d: a.reshape(B, T, NSA_KV, d)
    return y.reshape(B, T, D), (r4(kc_r, NSA_DK), r4(vc_r, NSA_DV), r4(ks_r, NSA_DK), r4(vs_r, NSA_DV),
                                r4(kw_r, NSA_DK)[:, T - wl:], r4(vw_r, NSA_DV)[:, T - wl:])


_SMP_PAGES = 8
_SEL_PAD = 384


def _online_update(m_ref, l_ref, acc_ref, g, s, v):
    m_old = m_ref[g]
    m_new = jnp.maximum(m_old, jnp.max(s, axis=1, keepdims=True))
    alpha = jnp.exp(m_old - m_new)
    p = jnp.exp(s - m_new)
    l_ref[g] = alpha * l_ref[g] + jnp.sum(p, axis=1, keepdims=True)
    acc_ref[g] = alpha * acc_ref[g] + jnp.dot(p.astype(jnp.bfloat16), v.astype(jnp.bfloat16),
                                              preferred_element_type=jnp.float32)
    m_ref[g] = m_new


def _nsa_sample_kernel(tab_ref, *refs, past, steps, pages, nsb):
    kp_refs = refs[:pages]
    vp_refs = refs[pages:2 * pages]
    (q_ref, kc_ref, vc_ref, kn_ref, vn_ref, kwc_ref, vwc_ref, kwn_ref, vwn_ref, gl_ref, z_ref,
     o_ref, m_scr, l_scr, acc_scr, sel_scr, oc_scr) = refs[2 * pages:]
    i = pl.program_id(1)
    s_tok = o_ref.shape[0]
    rows = NSA_REP * s_tok
    ncmp = kc_ref.shape[2]
    r_col = lax.broadcasted_iota(jnp.int32, (NSA_REP, s_tok, 1), 0).reshape(rows, 1)
    t_col = past + lax.broadcasted_iota(jnp.int32, (NSA_REP, s_tok, 1), 1).reshape(rows, 1)
    t_tok = past + lax.broadcasted_iota(jnp.int32, (s_tok, 1), 0)

    def slope_col(g):
        return jnp.exp2((r_col + (g * NSA_REP + 1)).astype(jnp.float32) * (-8.0 / NSA_HEADS))

    def q_of(g):
        return q_ref[0, g * rows:(g + 1) * rows, :]

    def rep_rows(x):
        return jnp.concatenate([x] * NSA_REP, axis=0)

    @pl.when(i == 0)
    def _first():
        n_row = lax.broadcasted_iota(jnp.int32, (1, ncmp), 1)
        cvalid = (n_row * CMP_STRIDE + CMP_BLOCK - 1) <= t_col
        ci = lax.broadcasted_iota(jnp.int32, (ncmp, _SEL_PAD), 0) * CMP_STRIDE
        cj = lax.broadcasted_iota(jnp.int32, (ncmp, _SEL_PAD), 1) * SLC_BLOCK
        ov = jnp.maximum(jnp.minimum(ci + CMP_BLOCK, cj + SLC_BLOCK) - jnp.maximum(ci, cj), 0)
        c2s = (ov.astype(jnp.float32) * (1.0 / CMP_STRIDE)).astype(jnp.bfloat16)
        jb = lax.broadcasted_iota(jnp.int32, (s_tok, _SEL_PAD), 1)
        cur = t_tok // SLC_BLOCK
        forced = (jb == 0) | (jb == cur) | (jb == cur - 1)
        valid = (jb * SLC_BLOCK <= t_tok) & (jb < nsb)
        for g in range(NSA_KV):
            s = jnp.where(cvalid, _dot_nt(q_of(g), kc_ref[0, g]), _NEG)
            p = jnp.where(cvalid, jnp.exp(s - jnp.max(s, axis=1, keepdims=True)), 0.0)
            p = p / jnp.maximum(jnp.sum(p, axis=1, keepdims=True), 1e-30)
            oc_scr[g] = jnp.dot(p.astype(jnp.bfloat16), vc_ref[0, g], preferred_element_type=jnp.float32)
            imp = _split_dot(jnp.sum(p.reshape(NSA_REP, s_tok, ncmp), axis=0), c2s)
            work = jnp.where(forced, jnp.inf, jnp.where(valid, imp, -jnp.inf))
            sel = jnp.zeros((s_tok, _SEL_PAD), jnp.float32)
            for _ in range(min(SLC_TOPN, nsb)):
                mx = jnp.max(work, axis=1, keepdims=True)
                first = jnp.min(jnp.where(work == mx, jb, _SEL_PAD), axis=1, keepdims=True)
                hit = jb == first
                sel = jnp.where(hit, 1.0, sel)
                work = jnp.where(hit, -jnp.inf, work)
            sel_scr[g] = sel
            m_scr[g] = jnp.full((rows, 1), _NEG, jnp.float32)
            l_scr[g] = jnp.zeros((rows, 1), jnp.float32)
            acc_scr[g] = jnp.zeros((rows, NSA_DV), jnp.float32)

    def fold(k_of, v_of, nk, page0, causal):
        pos = page0 * PAGE_ROWS + lax.broadcasted_iota(jnp.int32, (1, nk), 1)
        bj = lax.broadcasted_iota(jnp.int32, (_SEL_PAD, nk), 0)
        bl = lax.broadcasted_iota(jnp.int32, (_SEL_PAD, nk), 1)
        expand = jnp.where(bj == (page0 * PAGE_ROWS + bl) // SLC_BLOCK, 1.0, 0.0).astype(jnp.bfloat16)
        for g in range(NSA_KV):
            selk = jnp.dot(sel_scr[g].astype(jnp.bfloat16), expand, preferred_element_type=jnp.float32)
            ok = rep_rows(selk) > 0.5
            if causal:
                ok = ok & (pos <= t_col)
            s = _dot_nt(q_of(g)[:, 0:NSA_DK], k_of(g))
            s = s - slope_col(g) * (t_col - pos).astype(jnp.float32)
            _online_update(m_scr, l_scr, acc_scr, g, jnp.where(ok, s, _NEG), v_of(g))

    fold(lambda g: jnp.concatenate([r[0, :, g, :] for r in kp_refs], axis=0),
         lambda g: jnp.concatenate([r[0, :, g, :] for r in vp_refs], axis=0),
         pages * PAGE_ROWS, i * pages, causal=False)

    @pl.when(i == steps - 1)
    def _last():
        fold(lambda g: kn_ref[0, :, g * NSA_DK:(g + 1) * NSA_DK], lambda g: vn_ref[0, :, g * NSA_DV:(g + 1) * NSA_DV],
             PAGE_ROWS, past // PAGE_ROWS, causal=True)
        nwc = kwc_ref.shape[1]
        posw = jnp.concatenate([past - nwc + lax.broadcasted_iota(jnp.int32, (1, nwc), 1),
                                past + lax.broadcasted_iota(jnp.int32, (1, PAGE_ROWS), 1)], axis=1)
        dw = t_col - posw
        wok = (dw >= 0) & (dw < WINDOW) & (posw >= 0)
        gates = 1.0 / (1.0 + jnp.exp(-gl_ref[...]))
        for g in range(NSA_KV):
            ksl = slice(g * NSA_DK, (g + 1) * NSA_DK)
            vsl = slice(g * NSA_DV, (g + 1) * NSA_DV)
            qg = q_of(g)[:, 0:NSA_DK]
            s = jnp.concatenate([_dot_nt(qg, kwc_ref[0, :, ksl]), _dot_nt(qg, kwn_ref[0, :, ksl])], axis=1)
            s = jnp.where(wok, s - slope_col(g) * dw.astype(jnp.float32), _NEG)
            p = jnp.where(wok, jnp.exp(s - jnp.max(s, axis=1, keepdims=True)), 0.0)
            p = (p / jnp.maximum(jnp.sum(p, axis=1, keepdims=True), 1e-30)).astype(jnp.bfloat16)
            o_w = (jnp.dot(p[:, 0:nwc], vwc_ref[0, :, vsl].astype(jnp.bfloat16), preferred_element_type=jnp.float32)
                   + jnp.dot(p[:, nwc:], vwn_ref[0, :, vsl].astype(jnp.bfloat16), preferred_element_type=jnp.float32))
            o_s = acc_scr[g] / l_scr[g]
            o_c = oc_scr[g]
            for r in range(NSA_REP):
                rs = slice(r * s_tok, (r + 1) * s_tok)
                c0 = g * LANE + 3 * r
                o = (gates[:, c0:c0 + 1] * o_c[rs] + gates[:, c0 + 1:c0 + 2] * o_s[rs]
                     + gates[:, c0 + 2:c0 + 3] * o_w[rs])
                hs = slice((g * NSA_REP + r) * NSA_DV, (g * NSA_REP + r + 1) * NSA_DV)
                z = z_ref[:, hs]
                o_ref[:, hs] = o * (z * (1.0 / (1.0 + jnp.exp(-z))))


def nsa_sample_attention(q_rows, kcmp, vcmp, pool_k, pool_v, page_table, k_new, v_new, kw_cache, vw_cache,
                         kw_new, vw_new, gl, z, *, s_tok):
    batch, npg = page_table.shape
    pages = _SMP_PAGES
    steps = npg // pages
    past = npg * PAGE_ROWS
    nsb = -(-(past + s_tok) // SLC_BLOCK)
    rows = NSA_REP * s_tok
    ncmp = kcmp.shape[2]
    nwc = kw_cache.shape[1]
    page_spec = lambda k, d: pl.BlockSpec((1, PAGE_ROWS, NSA_KV, d), lambda b, i, tab: (tab[b, i * pages + k], 0, 0, 0))
    per_b = lambda shape: pl.BlockSpec((1,) + shape, lambda b, i, tab: (b,) + (0,) * len(shape))
    tok_blk = lambda w: pl.BlockSpec((s_tok, w), lambda b, i, tab: (b, 0))
    return pl.pallas_call(
        functools.partial(_nsa_sample_kernel, past=past, steps=steps, pages=pages, nsb=nsb),
        grid_spec=pltpu.PrefetchScalarGridSpec(
            num_scalar_prefetch=1,
            grid=(batch, steps),
            in_specs=[page_spec(k, NSA_DK) for k in range(pages)] + [page_spec(k, NSA_DV) for k in range(pages)]
            + [per_b((NSA_KV * rows, NSA_DKP)), per_b((NSA_KV, ncmp, NSA_DKP)), per_b((NSA_KV, ncmp, NSA_DV)),
               per_b((PAGE_ROWS, NSA_KW)), per_b((PAGE_ROWS, NSA_VW)),
               per_b((nwc, NSA_KW)), per_b((nwc, NSA_VW)), per_b((PAGE_ROWS, NSA_KW)), per_b((PAGE_ROWS, NSA_VW)),
               tok_blk(NSA_KV * LANE), tok_blk(NSA_O)],
            out_specs=tok_blk(NSA_O),
            scratch_shapes=[pltpu.VMEM((NSA_KV, rows, 1), jnp.float32), pltpu.VMEM((NSA_KV, rows, 1), jnp.float32),
                            pltpu.VMEM((NSA_KV, rows, NSA_DV), jnp.float32),
                            pltpu.VMEM((NSA_KV, s_tok, _SEL_PAD), jnp.float32),
                            pltpu.VMEM((NSA_KV, rows, NSA_DV), jnp.float32)]),
        out_shape=jax.ShapeDtypeStruct((batch * s_tok, NSA_O), jnp.float32),
        compiler_params=_cparams("parallel", "arbitrary"),
        name="nsa_sample_attention",
    )(page_table, *([pool_k] * pages), *([pool_v] * pages), q_rows, kcmp, vcmp, k_new, v_new,
      kw_cache, vw_cache, kw_new, vw_new, gl, z)


def _nsa_sample(x, ck, cv, sk, sv, wk_buf, wv_buf, page_table, norm_w,
                w_in, q_norm, kn_c, kn_s, kn_w, pe_k, w1_k, w2_k, pe_v, w1_v, w2_v, w_out):
    B, S, D = x.shape
    x2, q, kc_r, vc_r, ks_raw, vs_r, kw_raw, vw_r, gl, z = _nsa_in_proj(x, norm_w, w_in)
    q_hm = nsa_q_prep(q, q_norm, tq=B * S)
    q_rows = q_hm.reshape(NSA_KV, NSA_REP, B, S, NSA_DKP).transpose(2, 0, 1, 3, 4).reshape(B, NSA_KV * NSA_REP * S, NSA_DKP)
    ks_r, _ = nsa_k_prep(ks_raw, kn_s, seq=S, tk=B * S, onehot=False)
    kw_r, _ = nsa_k_prep(kw_raw, kn_w, seq=S, tk=B * S, onehot=False)
    as_page = lambda a: jnp.pad(a.reshape(B, S, -1), ((0, 0), (0, PAGE_ROWS - S), (0, 0)))
    seg3 = lambda a, w: a.reshape(-1, CMP_STRIDE, w)
    seg4 = lambda a: a.reshape((-1, CMP_STRIDE) + a.shape[2:])
    ident = jnp.arange(B, dtype=jnp.int32)[:, None]
    ka, kb = nsa_compress_partials(seg4(ck), page_table, pe_k, w1_k, pages=_CMP_PAGES)
    va, vb = nsa_compress_partials(seg4(cv), page_table, pe_v, w1_v, pages=_CMP_PAGES)
    _, kb_new = nsa_compress_partials(seg3(as_page(kc_r), NSA_KW), ident, pe_k, w1_k, pages=1)
    _, vb_new = nsa_compress_partials(seg3(as_page(vc_r), NSA_VW), ident, pe_v, w1_v, pages=1)
    kcmp = nsa_compress_finish(ka, kb, kb_new, w2_k, kn_c, is_key=True, alibi_lanes=True)
    vcmp = nsa_compress_finish(va, vb, vb_new, w2_v, jnp.zeros((NSA_DV,), jnp.float32), is_key=False, alibi_lanes=False)
    wl = wk_buf.shape[1]
    o = nsa_sample_attention(q_rows, kcmp, vcmp, sk, sv, page_table, as_page(ks_r), as_page(vs_r), wk_buf.reshape(B, wl, NSA_KW),
                             wv_buf.reshape(B, wl, NSA_VW), as_page(kw_r), as_page(vw_r), gl, z, s_tok=S)
    y = matmul(o.astype(jnp.bfloat16), w_out.astype(jnp.bfloat16), tm=B * S, tn=1024, residual=x2, name="nsa_out_proj")
    r4 = lambda a, d: a.reshape(B, S, NSA_KV, d)
    kw_all = jnp.concatenate([wk_buf, r4(kw_r, NSA_DK)], axis=1)
    vw_all = jnp.concatenate([wv_buf, r4(vw_r, NSA_DV)], axis=1)
    return y.reshape(B, S, D), (r4(kc_r, NSA_DK), r4(vc_r, NSA_DV), r4(ks_r, NSA_DK), r4(vs_r, NSA_DV),
                                kw_all[:, S:], vw_all[:, S:])


def kernel(x_prompt, x_sample, state_delta, state_conv, cache_cmp_k, cache_cmp_v, cache_slc_k, cache_slc_v, cache_win_k, cache_win_v, page_table, norm_dn, w_in_dn, conv_w_dn, a_log_dn, dt_bias_dn, out_norm_dn, w_out_dn, norm_nsa, w_in_nsa, q_norm_nsa, k_norm_cmp, k_norm_slc, k_norm_win, cmp_pe_k, cmp_w1_k, cmp_w2_k, cmp_pe_v, cmp_w1_v, cmp_w2_v, w_out_nsa):
    xp, xs = x_prompt, x_sample
    B = xp.shape[0]
    dw = (norm_dn[0], w_in_dn[0], conv_w_dn[0], a_log_dn[0], dt_bias_dn[0], out_norm_dn[0], w_out_dn[0])
    buf0 = jnp.zeros((B, DN_CONV - 1, conv_w_dn.shape[-1]), xp.dtype)
    st0 = jnp.zeros((B, DN_HEADS, DN_DK, DN_DV), jnp.float32)
    xp, p_dn = _deltanet_mixer(xp, buf0, st0, *dw)
    xs, s_dn = _deltanet_mixer(xs, state_conv[0], state_delta[0], *dw)
    nw = (norm_nsa[0], w_in_nsa[0], q_norm_nsa[0], k_norm_cmp[0], k_norm_slc[0], k_norm_win[0],
          cmp_pe_k[0], cmp_w1_k[0], cmp_w2_k[0], cmp_pe_v[0], cmp_w1_v[0], cmp_w2_v[0], w_out_nsa[0])
    xp, p_nsa = _nsa_prompt(xp, *nw)
    xs, s_nsa = _nsa_sample(xs, cache_cmp_k[0], cache_cmp_v[0], cache_slc_k[0], cache_slc_v[0],
                            cache_win_k[0], cache_win_v[0], page_table, *nw)
    return ((xp, xs, p_dn[0][None], p_dn[1][None]) + tuple(t[None] for t in p_nsa)
            + (s_dn[0][None], s_dn[1][None]) + tuple(t[None] for t in s_nsa))
```

```python
import functools

import jax
import jax.numpy as jnp
import numpy as np
from jax import lax
from jax.experimental import pallas as pl
from jax.experimental.pallas import tpu as pltpu

D_MODEL = 4096
EPS = 1e-6

DN_HEADS = 32
DN_DK = 128
DN_DV = 128
DN_CONV = 4
DN_CHUNK = 64

NSA_HEADS = 32
NSA_KV = 4
NSA_REP = NSA_HEADS // NSA_KV
NSA_DK = 192
NSA_DV = 128
CMP_BLOCK = 32
CMP_STRIDE = 16
CMP_RATIO = CMP_BLOCK // CMP_STRIDE
CMP_HIDDEN = 256
SLC_BLOCK = 64
SLC_TOPN = 16
WINDOW = 512
NSA_Q = NSA_HEADS * NSA_DK
NSA_KW = NSA_KV * NSA_DK
NSA_VW = NSA_KV * NSA_DV
NSA_O = NSA_HEADS * NSA_DV
NSA_SPLITS = (NSA_Q, NSA_KW, NSA_VW, NSA_KW, NSA_VW, NSA_KW, NSA_VW, 3 * NSA_HEADS, NSA_O)

VMEM_LIMIT_BYTES = 56 * 1024 * 1024
LANE = 128
_WT_TN = 512


def _cparams(*sem):
    return pltpu.CompilerParams(dimension_semantics=sem, vmem_limit_bytes=VMEM_LIMIT_BYTES)


def _rmsnorm_kernel(x_ref, w_ref, o_ref):
    x = x_ref[...]
    ms = jnp.mean(x * x, axis=-1, keepdims=True)
    o_ref[...] = (x * lax.rsqrt(ms + EPS) * w_ref[...]).astype(o_ref.dtype)


def rmsnorm_bf16(x, w, *, tm):
    m, d = x.shape
    return pl.pallas_call(
        _rmsnorm_kernel,
        grid=(m // tm,),
        in_specs=[pl.BlockSpec((tm, d), lambda i: (i, 0)), pl.BlockSpec((1, d), lambda i: (0, 0))],
        out_specs=pl.BlockSpec((tm, d), lambda i: (i, 0)),
        out_shape=jax.ShapeDtypeStruct((m, d), jnp.bfloat16),
        compiler_params=_cparams("parallel"),
        name="rmsnorm_bf16",
    )(x, w.reshape(1, d))


def _mm_kernel(a_ref, b_ref, o_ref):
    o_ref[...] = jnp.dot(a_ref[...], b_ref[...], preferred_element_type=jnp.float32).astype(o_ref.dtype)


def _mm_res_kernel(a_ref, b_ref, r_ref, o_ref):
    acc = jnp.dot(a_ref[...], b_ref[...], preferred_element_type=jnp.float32)
    o_ref[...] = r_ref[...] + acc


def matmul(a, b, *, tm, tn, residual=None, out_dtype=jnp.float32, name="matmul"):
    m, k = a.shape
    _, n = b.shape
    assert m % tm == 0 and n % tn == 0
    in_specs = [pl.BlockSpec((tm, k), lambda i, j: (i, 0)), pl.BlockSpec((k, tn), lambda i, j: (0, j))]
    args = [a, b]
    kern = _mm_kernel
    if residual is not None:
        in_specs.append(pl.BlockSpec((tm, tn), lambda i, j: (i, j)))
        args.append(residual)
        kern = _mm_res_kernel
    return pl.pallas_call(
        kern,
        grid=(m // tm, n // tn),
        in_specs=in_specs,
        out_specs=pl.BlockSpec((tm, tn), lambda i, j: (i, j)),
        out_shape=jax.ShapeDtypeStruct((m, n), out_dtype),
        compiler_params=_cparams("parallel", "parallel"),
        name=name,
    )(*args)


def _mm_wt_kernel(a_ref, w_ref, o_ref):
    o_ref[...] = lax.dot_general(a_ref[...], w_ref[...].astype(jnp.bfloat16), (((1,), (1,)), ((), ())),
                                 preferred_element_type=jnp.float32)


def matmul_wt(a, wt, *, tm, tn, n_off, n, name):
    m, k = a.shape
    assert m % tm == 0 and n % tn == 0 and n_off % tn == 0
    j0 = n_off // tn
    return pl.pallas_call(
        _mm_wt_kernel,
        grid=(m // tm, n // tn),
        in_specs=[pl.BlockSpec((tm, k), lambda i, j: (i, 0)), pl.BlockSpec((tn, k), lambda i, j: (j0 + j, 0))],
        out_specs=pl.BlockSpec((tm, tn), lambda i, j: (i, j)),
        out_shape=jax.ShapeDtypeStruct((m, n), jnp.float32),
        compiler_params=_cparams("parallel", "parallel"),
        name=name,
    )(a, wt)


def _row_tile(m):
    return 1024 if m % 1024 == 0 else m


def _dot(a, b):
    return jnp.dot(a.astype(jnp.bfloat16), b.astype(jnp.bfloat16), preferred_element_type=jnp.float32)


def _dot_nt(a, b):
    return lax.dot_general(a.astype(jnp.bfloat16), b.astype(jnp.bfloat16), (((1,), (1,)), ((), ())),
                           preferred_element_type=jnp.float32)


def _dot_tn(a, b):
    return lax.dot_general(a.astype(jnp.bfloat16), b.astype(jnp.bfloat16), (((0,), (0,)), ((), ())),
                           preferred_element_type=jnp.float32)


def _split_bf16(x):
    hi = x.astype(jnp.bfloat16)
    lo = (x - hi.astype(jnp.float32)).astype(jnp.bfloat16)
    return hi, lo


def _dn_gate_kernel(h_ref, wt_ref, alog_ref, dtb_ref, o_ref, *, heads):
    r = lax.dot_general(wt_ref[...], h_ref[...], (((1,), (1,)), ((), ())),
                        preferred_element_type=jnp.float32)
    b = r[:heads]
    a = r[heads:] + dtb_ref[...]
    softplus = jnp.maximum(a, 0.0) + jnp.log(1.0 + jnp.exp(-jnp.abs(a)))
    o_ref[0:heads, :] = 1.0 / (1.0 + jnp.exp(-b))
    o_ref[heads:2 * heads, :] = -jnp.exp(alog_ref[...]) * softplus


def dn_gates(h, w_ba_t, a_log, dt_bias, *, tm):
    m, d = h.shape
    heads = a_log.shape[0]
    return pl.pallas_call(
        functools.partial(_dn_gate_kernel, heads=heads),
        grid=(m // tm,),
        in_specs=[pl.BlockSpec((tm, d), lambda i: (i, 0)),
                  pl.BlockSpec((2 * heads, d), lambda i: (0, 0)),
                  pl.BlockSpec((heads, 1), lambda i: (0, 0)),
                  pl.BlockSpec((heads, 1), lambda i: (0, 0))],
        out_specs=pl.BlockSpec((2 * heads, tm), lambda i: (0, i)),
        out_shape=jax.ShapeDtypeStruct((2 * heads, m), jnp.float32),
        compiler_params=_cparams("parallel"),
        name="dn_gates",
    )(h, w_ba_t, a_log.reshape(heads, 1), dt_bias.reshape(heads, 1))


_DN_TAIL = 8
_DN_TB = 256
_DN_TMIN = 2 * DN_CHUNK
_DN_HB = 4


def _dn_chunk_prep(q, k, g_row, b_row):
    c = q.shape[0]
    ii = lax.broadcasted_iota(jnp.int32, (c, c), 0)
    jj = lax.broadcasted_iota(jnp.int32, (c, c), 1)
    lower, strict, eye = ii >= jj, ii > jj, ii == jj
    g_b = jnp.broadcast_to(g_row, (c, c))
    b_b = jnp.broadcast_to(b_row, (c, c))
    b_col = jnp.sum(jnp.where(eye, b_b, 0.0), axis=1, keepdims=True)
    g_col = jnp.sum(jnp.where(eye, g_b, 0.0), axis=1, keepdims=True)
    gc_col = jnp.sum(jnp.where(lower, g_b, 0.0), axis=1, keepdims=True)
    gc_row = jnp.sum(jnp.where(ii <= jj, g_col, 0.0), axis=0, keepdims=True)
    g_last = jnp.sum(g_row, axis=1, keepdims=True)
    decay = jnp.exp(jnp.where(lower, gc_col - gc_row, -jnp.inf))
    kk = _dot_nt(k, k)
    qk = _dot_nt(q, k)
    lmat = jnp.where(strict, kk * decay, 0.0) * b_col
    attn = qk * decay
    return dict(x=-lmat, p=eye.astype(jnp.float32) - lmat, attn=attn, b_row=b_row,
                e_row=jnp.exp(gc_row), e_col=jnp.exp(gc_col), c_col=jnp.exp(g_last - gc_col),
                e_last=jnp.exp(g_last))


def _dn_inverse_levels(items, c):
    m = 2
    while m < c:
        for it in items:
            it["x"] = _dot(it["x"], it["x"])
        for it in items:
            it["p"] = it["p"] + _dot(it["p"], it["x"])
        m *= 2


def _dn_kernel(q_ref, k_ref, v_ref, z_ref, beta_ref, g_ref, cw_ref, cs_ref, s0_ref, onw_ref,
               o_ref, s_out_ref, xpad, s_scr, *, tb, hb):
    n = pl.program_id(2)
    c = DN_CHUNK
    d = DN_DK
    nc = tb // c

    @pl.when(n == 0)
    def _init():
        xpad[:, 0:_DN_TAIL, :] = cs_ref[0]
        s_scr[...] = s0_ref[0]

    def conv(i, ref):
        xpad[i, _DN_TAIL:_DN_TAIL + tb, :] = ref[...]
        acc = None
        for j in range(DN_CONV):
            off = _DN_TAIL - (DN_CONV - 1) + j
            term = xpad[i, off:off + tb, :] * cw_ref[i, j:j + 1, :]
            acc = term if acc is None else acc + term
        xpad[i, 0:_DN_TAIL, :] = xpad[i, tb:tb + _DN_TAIL, :]
        return acc * (1.0 / (1.0 + jnp.exp(-acc)))

    qc, kc, vc = conv(0, q_ref), conv(1, k_ref), conv(2, v_ref)
    items = []
    for hh in range(hb):
        sl = slice(hh * d, (hh + 1) * d)
        q = qc[:, sl]
        k = kc[:, sl]
        q = q * (lax.rsqrt(jnp.sum(q * q, axis=-1, keepdims=True) + EPS) * (d ** -0.5))
        k = k * lax.rsqrt(jnp.sum(k * k, axis=-1, keepdims=True) + EPS)
        g_all = g_ref[hh]
        b_all = beta_ref[hh]
        for ci in range(nc):
            rs = slice(ci * c, (ci + 1) * c)
            it = _dn_chunk_prep(q[rs], k[rs], g_all[:, rs], b_all[:, rs])
            it.update(q=q[rs], k=k[rs], v=vc[rs, sl])
            items.append(it)
    _dn_inverse_levels(items, c)
    for it in items:
        inv_b = it["p"] * it["b_row"]
        it["u"] = _dot(inv_b, it["v"])
        it["w"] = _dot(inv_b * it["e_row"], it["k"])
    states = [s_scr[hh] for hh in range(hb)]
    outs = [[None] * nc for _ in range(hb)]
    for ci in range(nc):
        for hh in range(hb):
            it = items[hh * nc + ci]
            s = states[hh]
            v_new = it["u"] - _dot(it["w"], s)
            outs[hh][ci] = it["e_col"] * _dot(it["q"], s) + _dot(it["attn"], v_new)
            states[hh] = s * it["e_last"] + _dot_tn(it["k"] * it["c_col"], v_new)
    for hh in range(hb):
        sl = slice(hh * d, (hh + 1) * d)
        s_scr[hh] = states[hh]
        o = jnp.concatenate(outs[hh], axis=0) if nc > 1 else outs[hh][0]
        o = o * lax.rsqrt(jnp.mean(o * o, axis=-1, keepdims=True) + EPS) * onw_ref[...]
        z = z_ref[:, sl]
        o_ref[:, sl] = (o * (z * (1.0 / (1.0 + jnp.exp(-z))))).astype(o_ref.dtype)

    @pl.when(n == pl.num_programs(2) - 1)
    def _fin():
        s_out_ref[0] = s_scr[...]


def dn_delta(proj, gates, conv_w, conv_state, s0, out_norm, *, batch, tb, hb):
    rows, width = proj.shape
    d = DN_DK
    heads = width // (4 * d)
    t = rows // batch
    nt = t // tb
    hg = heads // hb
    wd = hb * d
    row_map = lambda off: (lambda b, h, n: (b * nt + n, off * hg + h))
    gate_map = lambda off: (lambda b, h, n: (off * hg + h, 0, b * nt + n))
    return pl.pallas_call(
        functools.partial(_dn_kernel, tb=tb, hb=hb),
        grid=(batch, hg, nt),
        in_specs=[pl.BlockSpec((tb, wd), row_map(0)),
                  pl.BlockSpec((tb, wd), row_map(1)),
                  pl.BlockSpec((tb, wd), row_map(2)),
                  pl.BlockSpec((tb, wd), row_map(3)),
                  pl.BlockSpec((hb, 1, tb), gate_map(0)),
                  pl.BlockSpec((hb, 1, tb), gate_map(1)),
                  pl.BlockSpec((3, DN_CONV, wd), lambda b, h, n: (0, 0, h)),
                  pl.BlockSpec((1, 3, _DN_TAIL, wd), lambda b, h, n: (b, 0, 0, h)),
                  pl.BlockSpec((1, hb, d, d), lambda b, h, n: (b, h, 0, 0)),
                  pl.BlockSpec((1, d), lambda b, h, n: (0, 0))],
        out_specs=[pl.BlockSpec((tb, wd), lambda b, h, n: (b * nt + n, h)),
                   pl.BlockSpec((1, hb, d, d), lambda b, h, n: (b, h, 0, 0))],
        out_shape=[jax.ShapeDtypeStruct((rows, heads * d), jnp.bfloat16),
                   jax.ShapeDtypeStruct(s0.shape, jnp.float32)],
        scratch_shapes=[pltpu.VMEM((3, tb + _DN_TAIL, wd), jnp.float32),
                        pltpu.VMEM((hb, d, d), jnp.float32)],
        compiler_params=_cparams("parallel", "parallel", "arbitrary"),
        name="dn_delta",
    )(proj, proj, proj, proj, gates, gates, conv_w, conv_state, s0, out_norm.reshape(1, d))


def _deltanet_mixer(x, conv_buf, s0, norm_w, w_in, conv_w, a_log, dt_bias, out_norm, w_out):
    B, L, D = x.shape
    H = a_log.shape[0]
    cdim = conv_w.shape[1]
    hd = cdim // 3
    x2 = x.reshape(B * L, D)
    h = rmsnorm_bf16(x2, norm_w, tm=min(256, B * L))
    wt = w_in.T
    w_ba_t = wt[cdim + hd:].astype(jnp.bfloat16)
    proj = matmul_wt(h, wt, tm=_row_tile(B * L), tn=_WT_TN, n_off=0, n=cdim + hd, name="dn_in_proj")
    gates = dn_gates(h, w_ba_t, a_log, dt_bias, tm=_row_tile(B * L))
    new_buf = proj.reshape(B, L, -1)[:, L - (DN_CONV - 1):, :cdim]
    T = -(-L // _DN_TMIN) * _DN_TMIN
    if T != L:
        proj = jnp.pad(proj.reshape(B, L, -1), ((0, 0), (0, T - L), (0, 0))).reshape(B * T, -1)
        gates = jnp.pad(gates.reshape(2 * H, B, L), ((0, 0), (0, 0), (0, T - L))).reshape(2 * H, B * T)
    cw = conv_w.reshape(DN_CONV, 3, hd).transpose(1, 0, 2)
    cs = conv_buf.reshape(B, DN_CONV - 1, 3, hd).transpose(0, 2, 1, 3)
    cs = jnp.pad(cs, ((0, 0), (0, 0), (_DN_TAIL - (DN_CONV - 1), 0), (0, 0)))
    tb = min(_DN_TB, T)
    o, s_new = dn_delta(proj, gates.reshape(2 * H, 1, B * T), cw, cs, s0.astype(jnp.float32), out_norm,
                        batch=B, tb=tb, hb=_DN_HB)
    if T != L:
        o = o.reshape(B, T, -1)[:, :L].reshape(B * L, -1)
    y = matmul(o, w_out.astype(jnp.bfloat16), tm=_row_tile(B * L), tn=1024, residual=x2, name="dn_out_proj")
    return y.reshape(B, L, D), (s_new, new_buf)


def _alibi_slopes():
    hh = jnp.arange(1, NSA_HEADS + 1, dtype=jnp.float32)
    return jnp.exp2(-8.0 * hh / NSA_HEADS).reshape(NSA_KV, NSA_REP)


PAGE_ROWS = 128
NSA_DKP = 256
_POS_LANE = NSA_DK
_SEL_LANE = NSA_DK + 4
_SEL_LANES = 32
_MASK_BIG = 2.0 ** 100
_NEG = -1e30
_NSA_TQ = 128
_NSA_TK = 256
_CMP_PAGES = 16


def _slope_table():
    s = _alibi_slopes()
    hi = s.astype(jnp.bfloat16).astype(jnp.float32)
    lo = (s - hi).astype(jnp.bfloat16).astype(jnp.float32)
    tab = jnp.stack([64.0 * hi, hi, 64.0 * lo, lo], axis=-1)
    return jnp.pad(tab, ((0, 0), (0, 0), (0, NSA_DKP - NSA_DK - 4)))


def _q_prep_kernel(q_ref, w_ref, tab_ref, o_ref):
    tq = q_ref.shape[0]
    for r in range(NSA_REP):
        x = q_ref[:, r * NSA_DK:(r + 1) * NSA_DK]
        y = x * lax.rsqrt(jnp.mean(x * x, axis=-1, keepdims=True) + EPS) * (w_ref[...] * (NSA_DK ** -0.5))
        o_ref[0, r, :, 0:NSA_DK] = y.astype(o_ref.dtype)
        o_ref[0, r, :, NSA_DK:NSA_DKP] = jnp.broadcast_to(tab_ref[0, r:r + 1, :], (tq, NSA_DKP - NSA_DK)).astype(o_ref.dtype)


def nsa_q_prep(q_raw, q_norm, *, tq):
    rows = q_raw.shape[0]
    return pl.pallas_call(
        _q_prep_kernel,
        grid=(rows // tq, NSA_KV),
        in_specs=[pl.BlockSpec((tq, NSA_REP * NSA_DK), lambda i, g: (i, g)),
                  pl.BlockSpec((1, NSA_DK), lambda i, g: (0, 0)),
                  pl.BlockSpec((1, NSA_REP, NSA_DKP - NSA_DK), lambda i, g: (g, 0, 0))],
        out_specs=pl.BlockSpec((1, NSA_REP, tq, NSA_DKP), lambda i, g: (g, 0, i, 0)),
        out_shape=jax.ShapeDtypeStruct((NSA_KV, NSA_REP, rows, NSA_DKP), jnp.bfloat16),
        compiler_params=_cparams("parallel", "parallel"),
        name="nsa_q_prep",
    )(q_raw, q_norm.reshape(1, NSA_DK), _slope_table())


def _k_prep_kernel(k_ref, w_ref, on_ref, oh_ref, *, seq, onehot):
    tk = k_ref.shape[0]
    pos = (pl.program_id(0) * tk) % seq + lax.broadcasted_iota(jnp.int32, (tk, NSA_DKP - NSA_DK), 0)
    lane = lax.broadcasted_iota(jnp.int32, (tk, NSA_DKP - NSA_DK), 1)
    blk, off = pos // SLC_BLOCK, pos % SLC_BLOCK
    extra = jnp.where((lane == 0) | (lane == 2), blk, jnp.where((lane == 1) | (lane == 3), off, 0))
    if onehot:
        extra = jnp.where((lane >= 4) & (lane - 4 == blk), 1, extra)
    extra = extra.astype(jnp.float32)
    for g in range(NSA_KV):
        x = k_ref[:, g * NSA_DK:(g + 1) * NSA_DK]
        x = x * lax.rsqrt(jnp.mean(x * x, axis=-1, keepdims=True) + EPS) * w_ref[...]
        on_ref[:, g * NSA_DK:(g + 1) * NSA_DK] = x
        oh_ref[g, :, 0:NSA_DK] = x.astype(oh_ref.dtype)
        oh_ref[g, :, NSA_DK:NSA_DKP] = extra.astype(oh_ref.dtype)


def nsa_k_prep(k_raw, k_norm, *, seq, tk, onehot):
    rows = k_raw.shape[0]
    return pl.pallas_call(
        functools.partial(_k_prep_kernel, seq=seq, onehot=onehot),
        grid=(rows // tk,),
        in_specs=[pl.BlockSpec((tk, NSA_KW), lambda i: (i, 0)), pl.BlockSpec((1, NSA_DK), lambda i: (0, 0))],
        out_specs=[pl.BlockSpec((tk, NSA_KW), lambda i: (i, 0)),
                   pl.BlockSpec((NSA_KV, tk, NSA_DKP), lambda i: (0, i, 0))],
        out_shape=[jax.ShapeDtypeStruct((rows, NSA_KW), jnp.float32),
                   jax.ShapeDtypeStruct((NSA_KV, rows, NSA_DKP), jnp.bfloat16)],
        compiler_params=_cparams("parallel"),
        name="nsa_k_prep",
    )(k_raw, k_norm.reshape(1, NSA_DK))


def _v_prep_kernel(v_ref, o_ref):
    for g in range(NSA_KV):
        o_ref[g] = v_ref[:, g * NSA_DV:(g + 1) * NSA_DV].astype(o_ref.dtype)


def nsa_v_prep(v_raw, *, tk):
    rows = v_raw.shape[0]
    return pl.pallas_call(
        _v_prep_kernel,
        grid=(rows // tk,),
        in_specs=[pl.BlockSpec((tk, NSA_VW), lambda i: (i, 0))],
        out_specs=pl.BlockSpec((NSA_KV, tk, NSA_DV), lambda i: (0, i, 0)),
        out_shape=jax.ShapeDtypeStruct((NSA_KV, rows, NSA_DV), jnp.bfloat16),
        compiler_params=_cparams("parallel"),
        name="nsa_v_prep",
    )(v_raw)


def _compress_partials_kernel(tab_ref, *refs, d, pages, transposed):
    x_refs, (pe_ref, w1_ref, a_ref, b_ref) = refs[:pages], refs[pages:]
    seg = PAGE_ROWS // CMP_STRIDE
    acc_a = acc_b = None
    if transposed:
        rows_of = [[r[0, g].T.reshape(seg, CMP_STRIDE, d) for r in x_refs] for g in range(NSA_KV)]
        take = lambda ri, l, g: rows_of[g][ri][:, l, :]
    elif len(x_refs[0].shape) == 4:
        take = lambda ri, l, g: x_refs[ri][:, l, g, :]
    else:
        take = lambda ri, l, g: x_refs[ri][:, l, g * d:(g + 1) * d]

    for l in range(CMP_STRIDE):
        xg = jnp.concatenate([take(ri, l, g) for g in range(NSA_KV) for ri in range(pages)], axis=0)
        ta = jnp.dot((xg + pe_ref[l:l + 1, :]).astype(jnp.bfloat16), w1_ref[l], preferred_element_type=jnp.float32)
        tb = jnp.dot((xg + pe_ref[CMP_STRIDE + l:CMP_STRIDE + l + 1, :]).astype(jnp.bfloat16),
                     w1_ref[CMP_STRIDE + l], preferred_element_type=jnp.float32)
        acc_a = ta if acc_a is None else acc_a + ta
        acc_b = tb if acc_b is None else acc_b + tb
    n = pages * seg
    for g in range(NSA_KV):
        a_ref[0, g] = acc_a[g * n:(g + 1) * n]
        b_ref[0, g] = acc_b[g * n:(g + 1) * n]


def nsa_compress_partials(pool3, table, pe, w1, *, pages, transposed=False):
    d = pe.shape[1]
    batch, npg = table.shape
    seg = PAGE_ROWS // CMP_STRIDE
    blk = (1,) + tuple(pool3.shape[1:]) if transposed else (seg, CMP_STRIDE) + tuple(pool3.shape[2:])
    x_spec = lambda k: pl.BlockSpec(blk, lambda b, i, tab: (tab[b, i * pages + k],) + (0,) * (len(blk) - 1))
    out_spec = pl.BlockSpec((1, NSA_KV, pages * seg, CMP_HIDDEN), lambda b, i, tab: (b, 0, i, 0))
    out_sds = jax.ShapeDtypeStruct((batch, NSA_KV, npg * seg, CMP_HIDDEN), jnp.float32)
    return pl.pallas_call(
        functools.partial(_compress_partials_kernel, d=d, pages=pages, transposed=transposed),
        grid_spec=pltpu.PrefetchScalarGridSpec(
            num_scalar_prefetch=1,
            grid=(batch, npg // pages),
            in_specs=[x_spec(k) for k in range(pages)]
            + [pl.BlockSpec((CMP_BLOCK, d), lambda b, i, tab: (0, 0)),
               pl.BlockSpec((CMP_BLOCK, d, CMP_HIDDEN), lambda b, i, tab: (0, 0, 0))],
            out_specs=[out_spec, out_spec]),
        out_shape=[out_sds, out_sds],
        compiler_params=_cparams("parallel", "arbitrary"),
        name="nsa_compress_partials",
    )(table, *([pool3] * pages), pe, w1.astype(jnp.bfloat16))


def _cmp_extra_lanes(nrows, width):
    n = lax.broadcasted_iota(jnp.int32, (nrows, width), 0)
    lane = lax.broadcasted_iota(jnp.int32, (nrows, width), 1)
    ec = n * CMP_STRIDE + CMP_BLOCK - 1
    blk, off = ec // SLC_BLOCK, ec % SLC_BLOCK
    pl_ = lane - _POS_LANE
    return jnp.where((pl_ == 0) | (pl_ == 2), blk, jnp.where((pl_ == 1) | (pl_ == 3), off, 0)).astype(jnp.float32)


def _compress_finish_kernel(a_ref, b_ref, bn_ref, w2_ref, kn_ref, o_ref, *, is_key, has_new, alibi_lanes):
    ns = a_ref.shape[2]
    row = lax.broadcasted_iota(jnp.int32, (ns, 1), 0)
    for g in range(NSA_KV):
        nxt = pltpu.roll(b_ref[0, g], ns - 1, 0)
        if has_new:
            nxt = jnp.where(row == ns - 1, bn_ref[0, g, 0:1, :], nxt)
        hid_pre = a_ref[0, g] + nxt
        hid = hid_pre * (1.0 / (1.0 + jnp.exp(-hid_pre)))
        out = jnp.dot(hid.astype(jnp.bfloat16), w2_ref[...], preferred_element_type=jnp.float32)
        if is_key:
            ms = jnp.sum(out * out, axis=-1, keepdims=True) * (1.0 / NSA_DK)
            out = out * lax.rsqrt(ms + EPS) * kn_ref[...]
            if alibi_lanes:
                out = out + _cmp_extra_lanes(ns, out.shape[1])
        o_ref[0, g] = out.astype(o_ref.dtype)


def nsa_compress_finish(part_a, part_b, part_b_new, w2, kn, *, is_key, alibi_lanes):
    batch, _, ns, _ = part_a.shape
    d = w2.shape[1]
    dp = NSA_DKP if is_key else d
    has_new = part_b_new is not None
    if not has_new:
        part_b_new = jnp.zeros((batch, NSA_KV, 8, CMP_HIDDEN), jnp.float32)
    w2p = jnp.pad(w2, ((0, 0), (0, dp - d))).astype(jnp.bfloat16)
    knp = jnp.pad(kn, (0, dp - d)).reshape(1, dp)
    blk = lambda n: pl.BlockSpec((1, NSA_KV, n, CMP_HIDDEN), lambda b: (b, 0, 0, 0))
    return pl.pallas_call(
        functools.partial(_compress_finish_kernel, is_key=is_key, has_new=has_new, alibi_lanes=alibi_lanes),
        grid=(batch,),
        in_specs=[blk(ns), blk(ns), blk(part_b_new.shape[2]),
                  pl.BlockSpec((CMP_HIDDEN, dp), lambda b: (0, 0)),
                  pl.BlockSpec((1, dp), lambda b: (0, 0))],
        out_specs=pl.BlockSpec((1, NSA_KV, ns, dp), lambda b: (b, 0, 0, 0)),
        out_shape=jax.ShapeDtypeStruct((batch, NSA_KV, ns, dp), jnp.bfloat16),
        compiler_params=_cparams("parallel"),
        name="nsa_compress_finish_k" if is_key else "nsa_compress_finish_v",
    )(part_a, part_b, part_b_new, w2p, knp)


def _split_dot(x, m_bf16):
    hi, lo = _split_bf16(x)
    f = functools.partial(jnp.dot, preferred_element_type=jnp.float32)
    return f(hi, m_bf16) + f(lo, m_bf16)


def _topn_mask(imp, t_col, nsb, topn):
    jb = lax.broadcasted_iota(jnp.int32, imp.shape, 1)
    cur = t_col // SLC_BLOCK
    forced = (jb == 0) | (jb == cur) | (jb == cur - 1)
    valid = (jb * SLC_BLOCK <= t_col) & (jb < nsb)
    impm = jnp.where(forced, jnp.inf, jnp.where(valid, imp, -jnp.inf))
    rank = jnp.zeros(imp.shape, jnp.float32)
    for k in range(nsb):
        col = impm[:, k:k + 1]
        beats = (col > impm) | ((col == impm) & (k < jb))
        rank = rank + jnp.where(beats, 1.0, 0.0)
    return (rank < topn) & (jb < nsb)


def _flash_step(q, k, v, mask, carry):
    m, l, acc = carry
    s = _dot_nt(q, k)
    s = jnp.where(mask, s, _NEG)
    m_new = jnp.maximum(m, jnp.max(s, axis=1, keepdims=True))
    alpha = jnp.exp(m - m_new)
    p = jnp.exp(s - m_new)
    l = alpha * l + jnp.sum(p, axis=1, keepdims=True)
    acc = alpha * acc + jnp.dot(p.astype(jnp.bfloat16), v, preferred_element_type=jnp.float32)
    return m_new, l, acc


def _nsa_prompt_kernel(q_ref, kc_ref, vc_ref, ks_ref, vs_ref, kw_ref, vw_ref, gl_ref, z_ref, o_ref,
                       *, tq, tk, seq):
    qi = pl.program_id(2)
    t0 = qi * tq
    rep = NSA_REP
    m_rows = rep * tq
    nc = seq // CMP_STRIDE - CMP_RATIO + 1
    nseg = kc_ref.shape[2]
    nsb = seq // SLC_BLOCK
    q = q_ref[0].reshape(m_rows, NSA_DKP)
    t_tok = t0 + lax.broadcasted_iota(jnp.int32, (tq, 1), 0)
    t_col = t0 + lax.broadcasted_iota(jnp.int32, (rep, tq, 1), 1).reshape(m_rows, 1)

    n_row = lax.broadcasted_iota(jnp.int32, (1, nseg), 1)
    cvalid = (n_row * CMP_STRIDE + CMP_BLOCK - 1 <= t_col) & (n_row < nc)
    s = jnp.where(cvalid, _dot_nt(q, kc_ref[0, 0]), _NEG)
    p = jnp.where(cvalid, jnp.exp(s - jnp.max(s, axis=1, keepdims=True)), 0.0)
    p = p / jnp.maximum(jnp.sum(p, axis=1, keepdims=True), 1e-30)
    o_c = jnp.dot(p.astype(jnp.bfloat16), vc_ref[0, 0], preferred_element_type=jnp.float32)
    psum = jnp.sum(p.reshape(rep, tq, nseg), axis=0)
    ci = lax.broadcasted_iota(jnp.int32, (nseg, _SEL_LANES), 0) * CMP_STRIDE
    cj = lax.broadcasted_iota(jnp.int32, (nseg, _SEL_LANES), 1) * SLC_BLOCK
    ov = jnp.maximum(jnp.minimum(ci + CMP_BLOCK, cj + SLC_BLOCK) - jnp.maximum(ci, cj), 0)
    c2s = (ov.astype(jnp.float32) * (1.0 / CMP_STRIDE)).astype(jnp.bfloat16)
    imp = _split_dot(psum, c2s)

    sel = _topn_mask(imp, t_tok, nsb, min(SLC_TOPN, nsb))
    pen = jnp.where(sel, 0.0, -_MASK_BIG).astype(jnp.bfloat16)
    pi = lax.broadcasted_iota(jnp.int32, (_SEL_LANES, NSA_DKP), 0)
    pj = lax.broadcasted_iota(jnp.int32, (_SEL_LANES, NSA_DKP), 1)
    place = jnp.where(pj == pi + _SEL_LANE, 1.0, 0.0).astype(jnp.bfloat16)
    q_add = jnp.dot(pen, place, preferred_element_type=jnp.float32).astype(jnp.bfloat16)
    q_slc = (q.reshape(rep, tq, NSA_DKP) + q_add[None]).reshape(m_rows, NSA_DKP)

    kpos = lax.broadcasted_iota(jnp.int32, (1, tk), 1)
    init = (jnp.full((m_rows, 1), _NEG, jnp.float32), jnp.zeros((m_rows, 1), jnp.float32),
            jnp.zeros((m_rows, NSA_DV), jnp.float32))

    def slc_body(kt, carry):
        k0 = pl.multiple_of(kt * tk, tk)
        mask = (k0 + kpos) <= t_col
        return _flash_step(q_slc, ks_ref[0, pl.ds(k0, tk), :], vs_ref[0, pl.ds(k0, tk), :], mask, carry)

    _, l_s, acc_s = lax.fori_loop(0, (t0 + tq + tk - 1) // tk, slc_body, init)
    o_s = acc_s / l_s

    def win_body(kt, carry):
        k0 = pl.multiple_of(kt * tk, tk)
        dw = t_col - (k0 + kpos)
        mask = (dw >= 0) & (dw < WINDOW)
        return _flash_step(q, kw_ref[0, pl.ds(k0, tk), :], vw_ref[0, pl.ds(k0, tk), :], mask, carry)

    kt_lo = jnp.maximum(t0 - (WINDOW - 1), 0) // tk
    _, l_w, acc_w = lax.fori_loop(kt_lo, (t0 + tq + tk - 1) // tk, win_body, init)
    o_w = acc_w / l_w

    gates = 1.0 / (1.0 + jnp.exp(-gl_ref[...]))
    for r in range(rep):
        rs = slice(r * tq, (r + 1) * tq)
        o = (gates[:, 3 * r:3 * r + 1] * o_c[rs] + gates[:, 3 * r + 1:3 * r + 2] * o_s[rs]
             + gates[:, 3 * r + 2:3 * r + 3] * o_w[rs])
        z = z_ref[:, r * NSA_DV:(r + 1) * NSA_DV]
        o_ref[:, r * NSA_DV:(r + 1) * NSA_DV] = (o * (z * (1.0 / (1.0 + jnp.exp(-z))))).astype(o_ref.dtype)


def nsa_prompt_attention(q_hm, kcmp, vcmp, ks_hm, vs_hm, kw_hm, vw_hm, gl, z, *, batch, tq, tk):
    rows = z.shape[0]
    seq = rows // batch
    nq = seq // tq
    nseg = kcmp.shape[2]
    row_blk = lambda b, g, i: (b * nq + i, g)
    kv_blk = lambda b, g, i: (g, b, 0)
    return pl.pallas_call(
        functools.partial(_nsa_prompt_kernel, tq=tq, tk=tk, seq=seq),
        grid=(batch, NSA_KV, nq),
        in_specs=[pl.BlockSpec((1, NSA_REP, tq, NSA_DKP), lambda b, g, i: (g, 0, b * nq + i, 0)),
                  pl.BlockSpec((1, 1, nseg, NSA_DKP), lambda b, g, i: (b, g, 0, 0)),
                  pl.BlockSpec((1, 1, nseg, NSA_DV), lambda b, g, i: (b, g, 0, 0)),
                  pl.BlockSpec((1, seq, NSA_DKP), kv_blk),
                  pl.BlockSpec((1, seq, NSA_DV), kv_blk),
                  pl.BlockSpec((1, seq, NSA_DKP), kv_blk),
                  pl.BlockSpec((1, seq, NSA_DV), kv_blk),
                  pl.BlockSpec((tq, LANE), row_blk),
                  pl.BlockSpec((tq, NSA_REP * NSA_DV), row_blk)],
        out_specs=pl.BlockSpec((tq, NSA_REP * NSA_DV), row_blk),
        out_shape=jax.ShapeDtypeStruct((rows, NSA_O), jnp.bfloat16),
        compiler_params=_cparams("parallel", "parallel", "arbitrary"),
        name="nsa_prompt_attention",
    )(q_hm, kcmp, vcmp, ks_hm, vs_hm, kw_hm, vw_hm, gl, z)


def _nsa_in_proj(x, norm_w, w_in):
    B, L, D = x.shape
    x2 = x.reshape(B * L, D)
    h = rmsnorm_bf16(x2, norm_w, tm=min(256, B * L))
    offs = np.cumsum((0,) + NSA_SPLITS)
    tm = _row_tile(B * L)

    wt = w_in.T

    def seg(i, tn, name):
        return matmul_wt(h, wt, tm=tm, tn=tn, n_off=int(offs[i]), n=int(offs[i + 1] - offs[i]), name=name)

    q = seg(0, _WT_TN, "nsa_q_proj")
    kc = seg(1, 256, "nsa_kc_proj")
    vc = seg(2, 256, "nsa_vc_proj")
    ks = seg(3, 256, "nsa_ks_proj")
    vs = seg(4, 256, "nsa_vs_proj")
    kw = seg(5, 256, "nsa_kw_proj")
    vw = seg(6, 256, "nsa_vw_proj")
    wg = wt[offs[7]:offs[8]].reshape(NSA_KV, 3 * NSA_REP, D)
    wg = jnp.pad(wg, ((0, 0), (0, LANE - 3 * NSA_REP), (0, 0))).reshape(NSA_KV * LANE, D)
    gl = matmul_wt(h, wg, tm=tm, tn=NSA_KV * LANE, n_off=0, n=NSA_KV * LANE, name="nsa_gate_proj")
    z = matmul_wt(h, wt[offs[8]:], tm=tm, tn=_WT_TN, n_off=0, n=NSA_O, name="nsa_z_proj")
    return x2, q, kc, vc, ks, vs, kw, vw, gl, z


def _nsa_prompt(x, norm_w, w_in, q_norm, kn_c, kn_s, kn_w, pe_k, w1_k, w2_k, pe_v, w1_v, w2_v, w_out):
    B, T, D = x.shape
    x2, q, kc_r, vc_r, ks_raw, vs_r, kw_raw, vw_r, gl, z = _nsa_in_proj(x, norm_w, w_in)
    tp = min(512, T)
    q_hm = nsa_q_prep(q, q_norm, tq=tp)
    ks_r, ks_hm = nsa_k_prep(ks_raw, kn_s, seq=T, tk=tp, onehot=True)
    kw_r, kw_hm = nsa_k_prep(kw_raw, kn_w, seq=T, tk=tp, onehot=False)
    vs_hm = nsa_v_prep(vs_r, tk=tp)
    vw_hm = nsa_v_prep(vw_r, tk=tp)
    npg = T // PAGE_ROWS
    table = (jnp.arange(B, dtype=jnp.int32)[:, None] * npg + jnp.arange(npg, dtype=jnp.int32)[None, :])
    seg3 = lambda a: a.reshape(B * T // CMP_STRIDE, CMP_STRIDE, a.shape[1])
    ka, kb = nsa_compress_partials(seg3(kc_r), table, pe_k, w1_k, pages=min(_CMP_PAGES, npg))
    va, vb = nsa_compress_partials(seg3(vc_r), table, pe_v, w1_v, pages=min(_CMP_PAGES, npg))
    kcmp = nsa_compress_finish(ka, kb, None, w2_k, kn_c, is_key=True, alibi_lanes=True)
    vcmp = nsa_compress_finish(va, vb, None, w2_v, jnp.zeros((NSA_DV,), jnp.float32), is_key=False, alibi_lanes=False)
    og = nsa_prompt_attention(q_hm, kcmp, vcmp, ks_hm, vs_hm, kw_hm, vw_hm, gl, z, batch=B, tq=_NSA_TQ, tk=_NSA_TK)
    y = matmul(og, w_out.astype(jnp.bfloat16), tm=_row_tile(B * T), tn=1024, residual=x2, name="nsa_out_proj")
    wl = min(WINDOW, T)
    r4 = lambda a, d: a.reshape(B, T, NSA_KV, d)
    return y.reshape(B, T, D), (r4(kc_r, NSA_DK), r4(vc_r, NSA_DV), r4(ks_r, NSA_DK), r4(vs_r, NSA_DV),
                                r4(kw_r, NSA_DK)[:, T - wl:], r4(vw_r, NSA_DV)[:, T - wl:])


_SMP_PAGES = 8
_SEL_PAD = 384


def _online_update(m_ref, l_ref, acc_ref, g, s, v):
    m_old = m_ref[g]
    m_new = jnp.maximum(m_old, jnp.max(s, axis=1, keepdims=True))
    alpha = jnp.exp(m_old - m_new)
    p = jnp.exp(s - m_new)
    l_ref[g] = alpha * l_ref[g] + jnp.sum(p, axis=1, keepdims=True)
    acc_ref[g] = alpha * acc_ref[g] + jnp.dot(p.astype(jnp.bfloat16), v.astype(jnp.bfloat16),
                                              preferred_element_type=jnp.float32)
    m_ref[g] = m_new


def _nsa_sample_kernel(tab_ref, *refs, past, steps, pages, nsb):
    kp_refs = refs[:pages]
    vp_refs = refs[pages:2 * pages]
    (q_ref, kc_ref, vc_ref, kn_ref, vn_ref, kwc_ref, vwc_ref, kwn_ref, vwn_ref, gl_ref, z_ref,
     o_ref, m_scr, l_scr, acc_scr, sel_scr, oc_scr) = refs[2 * pages:]
    i = pl.program_id(1)
    s_tok = o_ref.shape[0]
    rows = NSA_REP * s_tok
    ncmp = kc_ref.shape[2]
    r_col = lax.broadcasted_iota(jnp.int32, (NSA_REP, s_tok, 1), 0).reshape(rows, 1)
    t_col = past + lax.broadcasted_iota(jnp.int32, (NSA_REP, s_tok, 1), 1).reshape(rows, 1)
    t_tok = past + lax.broadcasted_iota(jnp.int32, (s_tok, 1), 0)

    def slope_col(g):
        return jnp.exp2((r_col + (g * NSA_REP + 1)).astype(jnp.float32) * (-8.0 / NSA_HEADS))

    def q_of(g):
        return q_ref[0, g * rows:(g + 1) * rows, :]

    def rep_rows(x):
        return jnp.concatenate([x] * NSA_REP, axis=0)

    @pl.when(i == 0)
    def _first():
        n_row = lax.broadcasted_iota(jnp.int32, (1, ncmp), 1)
        cvalid = (n_row * CMP_STRIDE + CMP_BLOCK - 1) <= t_col
        ci = lax.broadcasted_iota(jnp.int32, (ncmp, _SEL_PAD), 0) * CMP_STRIDE
        cj = lax.broadcasted_iota(jnp.int32, (ncmp, _SEL_PAD), 1) * SLC_BLOCK
        ov = jnp.maximum(jnp.minimum(ci + CMP_BLOCK, cj + SLC_BLOCK) - jnp.maximum(ci, cj), 0)
        c2s = (ov.astype(jnp.float32) * (1.0 / CMP_STRIDE)).astype(jnp.bfloat16)
        jb = lax.broadcasted_iota(jnp.int32, (s_tok, _SEL_PAD), 1)
        cur = t_tok // SLC_BLOCK
        forced = (jb == 0) | (jb == cur) | (jb == cur - 1)
        valid = (jb * SLC_BLOCK <= t_tok) & (jb < nsb)
        for g in range(NSA_KV):
            s = jnp.where(cvalid, _dot_nt(q_of(g), kc_ref[0, g]), _NEG)
            p = jnp.where(cvalid, jnp.exp(s - jnp.max(s, axis=1, keepdims=True)), 0.0)
            p = p / jnp.maximum(jnp.sum(p, axis=1, keepdims=True), 1e-30)
            oc_scr[g] = jnp.dot(p.astype(jnp.bfloat16), vc_ref[0, g], preferred_element_type=jnp.float32)
            imp = _split_dot(jnp.sum(p.reshape(NSA_REP, s_tok, ncmp), axis=0), c2s)
            work = jnp.where(forced, jnp.inf, jnp.where(valid, imp, -jnp.inf))
            sel = jnp.zeros((s_tok, _SEL_PAD), jnp.float32)
            for _ in range(min(SLC_TOPN, nsb)):
                mx = jnp.max(work, axis=1, keepdims=True)
                first = jnp.min(jnp.where(work == mx, jb, _SEL_PAD), axis=1, keepdims=True)
                hit = jb == first
                sel = jnp.where(hit, 1.0, sel)
                work = jnp.where(hit, -jnp.inf, work)
            sel_scr[g] = sel
            m_scr[g] = jnp.full((rows, 1), _NEG, jnp.float32)
            l_scr[g] = jnp.zeros((rows, 1), jnp.float32)
            acc_scr[g] = jnp.zeros((rows, NSA_DV), jnp.float32)

    def fold(score_of, v_of, nk, page0, causal):
        pos = page0 * PAGE_ROWS + lax.broadcasted_iota(jnp.int32, (1, nk), 1)
        bj = lax.broadcasted_iota(jnp.int32, (_SEL_PAD, nk), 0)
        bl = lax.broadcasted_iota(jnp.int32, (_SEL_PAD, nk), 1)
        expand = jnp.where(bj == (page0 * PAGE_ROWS + bl) // SLC_BLOCK, 1.0, 0.0).astype(jnp.bfloat16)
        for g in range(NSA_KV):
            selk = jnp.dot(sel_scr[g].astype(jnp.bfloat16), expand, preferred_element_type=jnp.float32)
            ok = rep_rows(selk) > 0.5
            if causal:
                ok = ok & (pos <= t_col)
            s = score_of(g, q_of(g)[:, 0:NSA_DK])
            s = s - slope_col(g) * (t_col - pos).astype(jnp.float32)
            _online_update(m_scr, l_scr, acc_scr, g, jnp.where(ok, s, _NEG), v_of(g))

    fold(lambda g, qg: _dot(qg, jnp.concatenate([r[0, g] for r in kp_refs], axis=1)),
         lambda g: jnp.concatenate([r[0, :, g, :] for r in vp_refs], axis=0),
         pages * PAGE_ROWS, i * pages, causal=False)

    @pl.when(i == steps - 1)
    def _last():
        fold(lambda g, qg: _dot_nt(qg, kn_ref[0, :, g * NSA_DK:(g + 1) * NSA_DK]),
             lambda g: vn_ref[0, :, g * NSA_DV:(g + 1) * NSA_DV], PAGE_ROWS, past // PAGE_ROWS, causal=True)
        nwc = kwc_ref.shape[3]
        posw = jnp.concatenate([past - nwc + lax.broadcasted_iota(jnp.int32, (1, nwc), 1),
                                past + lax.broadcasted_iota(jnp.int32, (1, PAGE_ROWS), 1)], axis=1)
        dw = t_col - posw
        wok = (dw >= 0) & (dw < WINDOW) & (posw >= 0)
        gates = 1.0 / (1.0 + jnp.exp(-gl_ref[...]))
        for g in range(NSA_KV):
            ksl = slice(g * NSA_DK, (g + 1) * NSA_DK)
            vsl = slice(g * NSA_DV, (g + 1) * NSA_DV)
            qg = q_of(g)[:, 0:NSA_DK]
            s = jnp.concatenate([_dot(qg, kwc_ref[0, g]), _dot_nt(qg, kwn_ref[0, :, ksl])], axis=1)
            s = jnp.where(wok, s - slope_col(g) * dw.astype(jnp.float32), _NEG)
            p = jnp.where(wok, jnp.exp(s - jnp.max(s, axis=1, keepdims=True)), 0.0)
            p = (p / jnp.maximum(jnp.sum(p, axis=1, keepdims=True), 1e-30)).astype(jnp.bfloat16)
            o_w = (jnp.dot(p[:, 0:nwc], vwc_ref[0, :, vsl].astype(jnp.bfloat16), preferred_element_type=jnp.float32)
                   + jnp.dot(p[:, nwc:], vwn_ref[0, :, vsl].astype(jnp.bfloat16), preferred_element_type=jnp.float32))
            o_s = acc_scr[g] / l_scr[g]
            o_c = oc_scr[g]
            for r in range(NSA_REP):
                rs = slice(r * s_tok, (r + 1) * s_tok)
                c0 = g * LANE + 3 * r
                o = (gates[:, c0:c0 + 1] * o_c[rs] + gates[:, c0 + 1:c0 + 2] * o_s[rs]
                     + gates[:, c0 + 2:c0 + 3] * o_w[rs])
                hs = slice((g * NSA_REP + r) * NSA_DV, (g * NSA_REP + r + 1) * NSA_DV)
                z = z_ref[:, hs]
                o_ref[:, hs] = o * (z * (1.0 / (1.0 + jnp.exp(-z))))


def nsa_sample_attention(q_rows, kcmp, vcmp, pool_k, pool_v, page_table, k_new, v_new, kw_cache, vw_cache,
                         kw_new, vw_new, gl, z, *, s_tok):
    batch, npg = page_table.shape
    pages = _SMP_PAGES
    steps = npg // pages
    past = npg * PAGE_ROWS
    nsb = -(-(past + s_tok) // SLC_BLOCK)
    rows = NSA_REP * s_tok
    ncmp = kcmp.shape[2]
    nwc = kw_cache.shape[3]
    kpage_spec = lambda k: pl.BlockSpec((1, NSA_KV, NSA_DK, PAGE_ROWS), lambda b, i, tab: (tab[b, i * pages + k], 0, 0, 0))
    vpage_spec = lambda k: pl.BlockSpec((1, PAGE_ROWS, NSA_KV, NSA_DV), lambda b, i, tab: (tab[b, i * pages + k], 0, 0, 0))
    per_b = lambda shape: pl.BlockSpec((1,) + shape, lambda b, i, tab: (b,) + (0,) * len(shape))
    tok_blk = lambda w: pl.BlockSpec((s_tok, w), lambda b, i, tab: (b, 0))
    return pl.pallas_call(
        functools.partial(_nsa_sample_kernel, past=past, steps=steps, pages=pages, nsb=nsb),
        grid_spec=pltpu.PrefetchScalarGridSpec(
            num_scalar_prefetch=1,
            grid=(batch, steps),
            in_specs=[kpage_spec(k) for k in range(pages)] + [vpage_spec(k) for k in range(pages)]
            + [per_b((NSA_KV * rows, NSA_DKP)), per_b((NSA_KV, ncmp, NSA_DKP)), per_b((NSA_KV, ncmp, NSA_DV)),
               per_b((PAGE_ROWS, NSA_KW)), per_b((PAGE_ROWS, NSA_VW)),
               per_b((NSA_KV, NSA_DK, nwc)), per_b((nwc, NSA_VW)), per_b((PAGE_ROWS, NSA_KW)), per_b((PAGE_ROWS, NSA_VW)),
               tok_blk(NSA_KV * LANE), tok_blk(NSA_O)],
            out_specs=tok_blk(NSA_O),
            scratch_shapes=[pltpu.VMEM((NSA_KV, rows, 1), jnp.float32), pltpu.VMEM((NSA_KV, rows, 1), jnp.float32),
                            pltpu.VMEM((NSA_KV, rows, NSA_DV), jnp.float32),
                            pltpu.VMEM((NSA_KV, s_tok, _SEL_PAD), jnp.float32),
                            pltpu.VMEM((NSA_KV, rows, NSA_DV), jnp.float32)]),
        out_shape=jax.ShapeDtypeStruct((batch * s_tok, NSA_O), jnp.float32),
        compiler_params=_cparams("parallel", "arbitrary"),
        name="nsa_sample_attention",
    )(page_table, *([pool_k] * pages), *([pool_v] * pages), q_rows, kcmp, vcmp, k_new, v_new,
      kw_cache, vw_cache, kw_new, vw_new, gl, z)


def _nsa_sample(x, ck, cv, sk, sv, wk_buf, wv_buf, page_table, norm_w,
                w_in, q_norm, kn_c, kn_s, kn_w, pe_k, w1_k, w2_k, pe_v, w1_v, w2_v, w_out):
    B, S, D = x.shape
    x2, q, kc_r, vc_r, ks_raw, vs_r, kw_raw, vw_r, gl, z = _nsa_in_proj(x, norm_w, w_in)
    q_hm = nsa_q_prep(q, q_norm, tq=B * S)
    q_rows = q_hm.reshape(NSA_KV, NSA_REP, B, S, NSA_DKP).transpose(2, 0, 1, 3, 4).reshape(B, NSA_KV * NSA_REP * S, NSA_DKP)
    ks_r, _ = nsa_k_prep(ks_raw, kn_s, seq=S, tk=B * S, onehot=False)
    kw_r, _ = nsa_k_prep(kw_raw, kn_w, seq=S, tk=B * S, onehot=False)
    as_page = lambda a: jnp.pad(a.reshape(B, S, -1), ((0, 0), (0, PAGE_ROWS - S), (0, 0)))
    seg3 = lambda a, w: a.reshape(-1, CMP_STRIDE, w)
    seg4 = lambda a: a.reshape((-1, CMP_STRIDE) + a.shape[2:])
    ident = jnp.arange(B, dtype=jnp.int32)[:, None]
    rows_minor = lambda a: jnp.transpose(a, (0, 2, 3, 1))
    ka, kb = nsa_compress_partials(rows_minor(ck), page_table, pe_k, w1_k, pages=_CMP_PAGES, transposed=True)
    va, vb = nsa_compress_partials(seg4(cv), page_table, pe_v, w1_v, pages=_CMP_PAGES)
    _, kb_new = nsa_compress_partials(seg3(as_page(kc_r), NSA_KW), ident, pe_k, w1_k, pages=1)
    _, vb_new = nsa_compress_partials(seg3(as_page(vc_r), NSA_VW), ident, pe_v, w1_v, pages=1)
    kcmp = nsa_compress_finish(ka, kb, kb_new, w2_k, kn_c, is_key=True, alibi_lanes=True)
    vcmp = nsa_compress_finish(va, vb, vb_new, w2_v, jnp.zeros((NSA_DV,), jnp.float32), is_key=False, alibi_lanes=False)
    wl = wk_buf.shape[1]
    o = nsa_sample_attention(q_rows, kcmp, vcmp, rows_minor(sk), sv, page_table, as_page(ks_r), as_page(vs_r), rows_minor(wk_buf),
                             wv_buf.reshape(B, wl, NSA_VW), as_page(kw_r), as_page(vw_r), gl, z, s_tok=S)
    y = matmul(o.astype(jnp.bfloat16), w_out.astype(jnp.bfloat16), tm=B * S, tn=1024, residual=x2, name="nsa_out_proj")
    r4 = lambda a, d: a.reshape(B, S, NSA_KV, d)
    kw_all = jnp.concatenate([wk_buf, r4(kw_r, NSA_DK)], axis=1)
    vw_all = jnp.concatenate([wv_buf, r4(vw_r, NSA_DV)], axis=1)
    return y.reshape(B, S, D), (r4(kc_r, NSA_DK), r4(vc_r, NSA_DV), r4(ks_r, NSA_DK), r4(vs_r, NSA_DV),
                                kw_all[:, S:], vw_all[:, S:])


def kernel(x_prompt, x_sample, state_delta, state_conv, cache_cmp_k, cache_cmp_v, cache_slc_k, cache_slc_v, cache_win_k, cache_win_v, page_table, norm_dn, w_in_dn, conv_w_dn, a_log_dn, dt_bias_dn, out_norm_dn, w_out_dn, norm_nsa, w_in_nsa, q_norm_nsa, k_norm_cmp, k_norm_slc, k_norm_win, cmp_pe_k, cmp_w1_k, cmp_w2_k, cmp_pe_v, cmp_w1_v, cmp_w2_v, w_out_nsa):
    xp, xs = x_prompt, x_sample
    B = xp.shape[0]
    dw = (norm_dn[0], w_in_dn[0], conv_w_dn[0], a_log_dn[0], dt_bias_dn[0], out_norm_dn[0], w_out_dn[0])
    buf0 = jnp.zeros((B, DN_CONV - 1, conv_w_dn.shape[-1]), xp.dtype)
    st0 = jnp.zeros((B, DN_HEADS, DN_DK, DN_DV), jnp.float32)
    xp, p_dn = _deltanet_mixer(xp, buf0, st0, *dw)
    xs, s_dn = _deltanet_mixer(xs, state_conv[0], state_delta[0], *dw)
    nw = (norm_nsa[0], w_in_nsa[0], q_norm_nsa[0], k_norm_cmp[0], k_norm_slc[0], k_norm_win[0],
          cmp_pe_k[0], cmp_w1_k[0], cmp_w2_k[0], cmp_pe_v[0], cmp_w1_v[0], cmp_w2_v[0], w_out_nsa[0])
    xp, p_nsa = _nsa_prompt(xp, *nw)
    xs, s_nsa = _nsa_sample(xs, cache_cmp_k[0], cache_cmp_v[0], cache_slc_k[0], cache_slc_v[0],
                            cache_win_k[0], cache_win_v[0], page_table, *nw)
    return ((xp, xs, p_dn[0][None], p_dn[1][None]) + tuple(t[None] for t in p_nsa)
            + (s_dn[0][None], s_dn[1][None]) + tuple(t[None] for t in s_nsa))
```

```python
import functools

import jax
import jax.numpy as jnp
import numpy as np
from jax import lax
from jax.experimental import pallas as pl
from jax.experimental.pallas import tpu as pltpu

D_MODEL = 4096
EPS = 1e-6

DN_HEADS = 32
DN_DK = 128
DN_DV = 128
DN_CONV = 4
DN_CHUNK = 64

NSA_HEADS = 32
NSA_KV = 4
NSA_REP = NSA_HEADS // NSA_KV
NSA_DK = 192
NSA_DV = 128
CMP_BLOCK = 32
CMP_STRIDE = 16
CMP_RATIO = CMP_BLOCK // CMP_STRIDE
CMP_HIDDEN = 256
SLC_BLOCK = 64
SLC_TOPN = 16
WINDOW = 512
NSA_Q = NSA_HEADS * NSA_DK
NSA_KW = NSA_KV * NSA_DK
NSA_VW = NSA_KV * NSA_DV
NSA_O = NSA_HEADS * NSA_DV
NSA_SPLITS = (NSA_Q, NSA_KW, NSA_VW, NSA_KW, NSA_VW, NSA_KW, NSA_VW, 3 * NSA_HEADS, NSA_O)

VMEM_LIMIT_BYTES = 56 * 1024 * 1024
LANE = 128
_WT_TN = 512


def _cparams(*sem):
    return pltpu.CompilerParams(dimension_semantics=sem, vmem_limit_bytes=VMEM_LIMIT_BYTES)


def _rmsnorm_kernel(x_ref, w_ref, o_ref):
    x = x_ref[...]
    ms = jnp.mean(x * x, axis=-1, keepdims=True)
    o_ref[...] = (x * lax.rsqrt(ms + EPS) * w_ref[...]).astype(o_ref.dtype)


def rmsnorm_bf16(x, w, *, tm):
    m, d = x.shape
    return pl.pallas_call(
        _rmsnorm_kernel,
        grid=(m // tm,),
        in_specs=[pl.BlockSpec((tm, d), lambda i: (i, 0)), pl.BlockSpec((1, d), lambda i: (0, 0))],
        out_specs=pl.BlockSpec((tm, d), lambda i: (i, 0)),
        out_shape=jax.ShapeDtypeStruct((m, d), jnp.bfloat16),
        compiler_params=_cparams("parallel"),
        name="rmsnorm_bf16",
    )(x, w.reshape(1, d))


def _mm_kernel(a_ref, b_ref, o_ref):
    o_ref[...] = jnp.dot(a_ref[...], b_ref[...], preferred_element_type=jnp.float32).astype(o_ref.dtype)


def _mm_res_kernel(a_ref, b_ref, r_ref, o_ref):
    acc = jnp.dot(a_ref[...], b_ref[...], preferred_element_type=jnp.float32)
    o_ref[...] = r_ref[...] + acc


def matmul(a, b, *, tm, tn, residual=None, out_dtype=jnp.float32, name="matmul"):
    m, k = a.shape
    _, n = b.shape
    assert m % tm == 0 and n % tn == 0
    in_specs = [pl.BlockSpec((tm, k), lambda i, j: (i, 0)), pl.BlockSpec((k, tn), lambda i, j: (0, j))]
    args = [a, b]
    kern = _mm_kernel
    if residual is not None:
        in_specs.append(pl.BlockSpec((tm, tn), lambda i, j: (i, j)))
        args.append(residual)
        kern = _mm_res_kernel
    return pl.pallas_call(
        kern,
        grid=(m // tm, n // tn),
        in_specs=in_specs,
        out_specs=pl.BlockSpec((tm, tn), lambda i, j: (i, j)),
        out_shape=jax.ShapeDtypeStruct((m, n), out_dtype),
        compiler_params=_cparams("parallel", "parallel"),
        name=name,
    )(*args)


def _mm_wt_kernel(a_ref, w_ref, o_ref):
    o_ref[...] = lax.dot_general(a_ref[...], w_ref[...].astype(jnp.bfloat16), (((1,), (1,)), ((), ())),
                                 preferred_element_type=jnp.float32)


def matmul_wt(a, wt, *, tm, tn, n_off, n, name):
    m, k = a.shape
    assert m % tm == 0 and n % tn == 0 and n_off % tn == 0
    j0 = n_off // tn
    return pl.pallas_call(
        _mm_wt_kernel,
        grid=(m // tm, n // tn),
        in_specs=[pl.BlockSpec((tm, k), lambda i, j: (i, 0)), pl.BlockSpec((tn, k), lambda i, j: (j0 + j, 0))],
        out_specs=pl.BlockSpec((tm, tn), lambda i, j: (i, j)),
        out_shape=jax.ShapeDtypeStruct((m, n), jnp.float32),
        compiler_params=_cparams("parallel", "parallel"),
        name=name,
    )(a, wt)


def _row_tile(m):
    return 1024 if m % 1024 == 0 else m


def _dot(a, b):
    return jnp.dot(a.astype(jnp.bfloat16), b.astype(jnp.bfloat16), preferred_element_type=jnp.float32)


def _dot_nt(a, b):
    return lax.dot_general(a.astype(jnp.bfloat16), b.astype(jnp.bfloat16), (((1,), (1,)), ((), ())),
                           preferred_element_type=jnp.float32)


def _dot_tn(a, b):
    return lax.dot_general(a.astype(jnp.bfloat16), b.astype(jnp.bfloat16), (((0,), (0,)), ((), ())),
                           preferred_element_type=jnp.float32)


def _split_bf16(x):
    hi = x.astype(jnp.bfloat16)
    lo = (x - hi.astype(jnp.float32)).astype(jnp.bfloat16)
    return hi, lo


def _dn_gate_kernel(h_ref, wt_ref, alog_ref, dtb_ref, o_ref, *, heads):
    r = lax.dot_general(wt_ref[...], h_ref[...], (((1,), (1,)), ((), ())),
                        preferred_element_type=jnp.float32)
    b = r[:heads]
    a = r[heads:] + dtb_ref[...]
    softplus = jnp.maximum(a, 0.0) + jnp.log(1.0 + jnp.exp(-jnp.abs(a)))
    o_ref[0:heads, :] = 1.0 / (1.0 + jnp.exp(-b))
    o_ref[heads:2 * heads, :] = -jnp.exp(alog_ref[...]) * softplus


def dn_gates(h, w_ba_t, a_log, dt_bias, *, tm):
    m, d = h.shape
    heads = a_log.shape[0]
    return pl.pallas_call(
        functools.partial(_dn_gate_kernel, heads=heads),
        grid=(m // tm,),
        in_specs=[pl.BlockSpec((tm, d), lambda i: (i, 0)),
                  pl.BlockSpec((2 * heads, d), lambda i: (0, 0)),
                  pl.BlockSpec((heads, 1), lambda i: (0, 0)),
                  pl.BlockSpec((heads, 1), lambda i: (0, 0))],
        out_specs=pl.BlockSpec((2 * heads, tm), lambda i: (0, i)),
        out_shape=jax.ShapeDtypeStruct((2 * heads, m), jnp.float32),
        compiler_params=_cparams("parallel"),
        name="dn_gates",
    )(h, w_ba_t, a_log.reshape(heads, 1), dt_bias.reshape(heads, 1))


_DN_TAIL = 8
_DN_TB = 256
_DN_TMIN = 2 * DN_CHUNK
_DN_HB = 4


def _dn_chunk_prep(q, k, g_row, b_row):
    c = q.shape[0]
    ii = lax.broadcasted_iota(jnp.int32, (c, c), 0)
    jj = lax.broadcasted_iota(jnp.int32, (c, c), 1)
    lower, strict, eye = ii >= jj, ii > jj, ii == jj
    g_b = jnp.broadcast_to(g_row, (c, c))
    b_b = jnp.broadcast_to(b_row, (c, c))
    b_col = jnp.sum(jnp.where(eye, b_b, 0.0), axis=1, keepdims=True)
    g_col = jnp.sum(jnp.where(eye, g_b, 0.0), axis=1, keepdims=True)
    gc_col = jnp.sum(jnp.where(lower, g_b, 0.0), axis=1, keepdims=True)
    gc_row = jnp.sum(jnp.where(ii <= jj, g_col, 0.0), axis=0, keepdims=True)
    g_last = jnp.sum(g_row, axis=1, keepdims=True)
    decay = jnp.exp(jnp.where(lower, gc_col - gc_row, -jnp.inf))
    kk = _dot_nt(k, k)
    qk = _dot_nt(q, k)
    lmat = jnp.where(strict, kk * decay, 0.0) * b_col
    attn = qk * decay
    return dict(x=-lmat, p=eye.astype(jnp.float32) - lmat, attn=attn, b_row=b_row,
                e_row=jnp.exp(gc_row), e_col=jnp.exp(gc_col), c_col=jnp.exp(g_last - gc_col),
                e_last=jnp.exp(g_last))


def _dn_inverse_levels(items, c):
    m = 2
    while m < c:
        for it in items:
            it["x"] = _dot(it["x"], it["x"])
        for it in items:
            it["p"] = it["p"] + _dot(it["p"], it["x"])
        m *= 2


def _dn_kernel(q_ref, k_ref, v_ref, z_ref, beta_ref, g_ref, cw_ref, cs_ref, s0_ref, onw_ref,
               o_ref, s_out_ref, xpad, s_scr, *, tb, hb):
    n = pl.program_id(2)
    c = DN_CHUNK
    d = DN_DK
    nc = tb // c

    @pl.when(n == 0)
    def _init():
        xpad[:, 0:_DN_TAIL, :] = cs_ref[0]
        s_scr[...] = s0_ref[0]

    def conv(i, ref):
        xpad[i, _DN_TAIL:_DN_TAIL + tb, :] = ref[...]
        acc = None
        for j in range(DN_CONV):
            off = _DN_TAIL - (DN_CONV - 1) + j
            term = xpad[i, off:off + tb, :] * cw_ref[i, j:j + 1, :]
            acc = term if acc is None else acc + term
        xpad[i, 0:_DN_TAIL, :] = xpad[i, tb:tb + _DN_TAIL, :]
        return acc * (1.0 / (1.0 + jnp.exp(-acc)))

    qc, kc, vc = conv(0, q_ref), conv(1, k_ref), conv(2, v_ref)
    items = []
    for hh in range(hb):
        sl = slice(hh * d, (hh + 1) * d)
        q = qc[:, sl]
        k = kc[:, sl]
        q = q * (lax.rsqrt(jnp.sum(q * q, axis=-1, keepdims=True) + EPS) * (d ** -0.5))
        k = k * lax.rsqrt(jnp.sum(k * k, axis=-1, keepdims=True) + EPS)
        g_all = g_ref[hh]
        b_all = beta_ref[hh]
        for ci in range(nc):
            rs = slice(ci * c, (ci + 1) * c)
            it = _dn_chunk_prep(q[rs], k[rs], g_all[:, rs], b_all[:, rs])
            it.update(q=q[rs], k=k[rs], v=vc[rs, sl])
            items.append(it)
    _dn_inverse_levels(items, c)
    for it in items:
        inv_b = it["p"] * it["b_row"]
        it["u"] = _dot(inv_b, it["v"])
        it["w"] = _dot(inv_b * it["e_row"], it["k"])
    states = [s_scr[hh] for hh in range(hb)]
    outs = [[None] * nc for _ in range(hb)]
    for ci in range(nc):
        for hh in range(hb):
            it = items[hh * nc + ci]
            s = states[hh]
            v_new = it["u"] - _dot(it["w"], s)
            outs[hh][ci] = it["e_col"] * _dot(it["q"], s) + _dot(it["attn"], v_new)
            states[hh] = s * it["e_last"] + _dot_tn(it["k"] * it["c_col"], v_new)
    for hh in range(hb):
        sl = slice(hh * d, (hh + 1) * d)
        s_scr[hh] = states[hh]
        o = jnp.concatenate(outs[hh], axis=0) if nc > 1 else outs[hh][0]
        o = o * lax.rsqrt(jnp.mean(o * o, axis=-1, keepdims=True) + EPS) * onw_ref[...]
        z = z_ref[:, sl]
        o_ref[:, sl] = (o * (z * (1.0 / (1.0 + jnp.exp(-z))))).astype(o_ref.dtype)

    @pl.when(n == pl.num_programs(2) - 1)
    def _fin():
        s_out_ref[0] = s_scr[...]


def dn_delta(proj, gates, conv_w, conv_state, s0, out_norm, *, batch, tb, hb):
    rows, width = proj.shape
    d = DN_DK
    heads = width // (4 * d)
    t = rows // batch
    nt = t // tb
    hg = heads // hb
    wd = hb * d
    row_map = lambda off: (lambda b, h, n: (b * nt + n, off * hg + h))
    gate_map = lambda off: (lambda b, h, n: (off * hg + h, 0, b * nt + n))
    return pl.pallas_call(
        functools.partial(_dn_kernel, tb=tb, hb=hb),
        grid=(batch, hg, nt),
        in_specs=[pl.BlockSpec((tb, wd), row_map(0)),
                  pl.BlockSpec((tb, wd), row_map(1)),
                  pl.BlockSpec((tb, wd), row_map(2)),
                  pl.BlockSpec((tb, wd), row_map(3)),
                  pl.BlockSpec((hb, 1, tb), gate_map(0)),
                  pl.BlockSpec((hb, 1, tb), gate_map(1)),
                  pl.BlockSpec((3, DN_CONV, wd), lambda b, h, n: (0, 0, h)),
                  pl.BlockSpec((1, 3, _DN_TAIL, wd), lambda b, h, n: (b, 0, 0, h)),
                  pl.BlockSpec((1, hb, d, d), lambda b, h, n: (b, h, 0, 0)),
                  pl.BlockSpec((1, d), lambda b, h, n: (0, 0))],
        out_specs=[pl.BlockSpec((tb, wd), lambda b, h, n: (b * nt + n, h)),
                   pl.BlockSpec((1, hb, d, d), lambda b, h, n: (b, h, 0, 0))],
        out_shape=[jax.ShapeDtypeStruct((rows, heads * d), jnp.bfloat16),
                   jax.ShapeDtypeStruct(s0.shape, jnp.float32)],
        scratch_shapes=[pltpu.VMEM((3, tb + _DN_TAIL, wd), jnp.float32),
                        pltpu.VMEM((hb, d, d), jnp.float32)],
        compiler_params=_cparams("parallel", "parallel", "arbitrary"),
        name="dn_delta",
    )(proj, proj, proj, proj, gates, gates, conv_w, conv_state, s0, out_norm.reshape(1, d))


def _deltanet_mixer(x, conv_buf, s0, norm_w, w_in, conv_w, a_log, dt_bias, out_norm, w_out):
    B, L, D = x.shape
    H = a_log.shape[0]
    cdim = conv_w.shape[1]
    hd = cdim // 3
    x2 = x.reshape(B * L, D)
    h = rmsnorm_bf16(x2, norm_w, tm=min(256, B * L))
    wt = w_in.T
    w_ba_t = wt[cdim + hd:].astype(jnp.bfloat16)
    proj = matmul_wt(h, wt, tm=_row_tile(B * L), tn=_WT_TN, n_off=0, n=cdim + hd, name="dn_in_proj")
    gates = dn_gates(h, w_ba_t, a_log, dt_bias, tm=_row_tile(B * L))
    new_buf = proj.reshape(B, L, -1)[:, L - (DN_CONV - 1):, :cdim]
    T = -(-L // _DN_TMIN) * _DN_TMIN
    if T != L:
        proj = jnp.pad(proj.reshape(B, L, -1), ((0, 0), (0, T - L), (0, 0))).reshape(B * T, -1)
        gates = jnp.pad(gates.reshape(2 * H, B, L), ((0, 0), (0, 0), (0, T - L))).reshape(2 * H, B * T)
    cw = conv_w.reshape(DN_CONV, 3, hd).transpose(1, 0, 2)
    cs = conv_buf.reshape(B, DN_CONV - 1, 3, hd).transpose(0, 2, 1, 3)
    cs = jnp.pad(cs, ((0, 0), (0, 0), (_DN_TAIL - (DN_CONV - 1), 0), (0, 0)))
    tb = min(_DN_TB, T)
    o, s_new = dn_delta(proj, gates.reshape(2 * H, 1, B * T), cw, cs, s0.astype(jnp.float32), out_norm,
                        batch=B, tb=tb, hb=_DN_HB)
    if T != L:
        o = o.reshape(B, T, -1)[:, :L].reshape(B * L, -1)
    y = matmul(o, w_out.astype(jnp.bfloat16), tm=_row_tile(B * L), tn=1024, residual=x2, name="dn_out_proj")
    return y.reshape(B, L, D), (s_new, new_buf)


def _alibi_slopes():
    hh = jnp.arange(1, NSA_HEADS + 1, dtype=jnp.float32)
    return jnp.exp2(-8.0 * hh / NSA_HEADS).reshape(NSA_KV, NSA_REP)


PAGE_ROWS = 128
NSA_DKP = 256
_POS_LANE = NSA_DK
_SEL_LANE = NSA_DK + 4
_SEL_LANES = 32
_ROW_LANE = _SEL_LANE + _SEL_LANES
NSA_DVP = 256
_FAST_SCORE_BOUND = 40.0
_MASK_BIG = 2.0 ** 100
_NEG = -1e30
_NSA_TQ = 128
_NSA_TK = 256
_CMP_PAGES = 16


def _slope_table():
    s = _alibi_slopes()
    hi = s.astype(jnp.bfloat16).astype(jnp.float32)
    lo = (s - hi).astype(jnp.bfloat16).astype(jnp.float32)
    tab = jnp.stack([64.0 * hi, hi, 64.0 * lo, lo], axis=-1)
    return jnp.pad(tab, ((0, 0), (0, 0), (0, NSA_DKP - NSA_DK - 4)))


def _q_prep_kernel(q_ref, w_ref, tab_ref, o_ref, *, seq, pos0):
    tq = q_ref.shape[0]
    npad = NSA_DKP - NSA_DK
    t = (pos0 + (pl.program_id(0) * tq) % seq + lax.broadcasted_iota(jnp.int32, (tq, npad), 0)).astype(jnp.float32)
    lane = lax.broadcasted_iota(jnp.int32, (tq, npad), 1)
    for r in range(NSA_REP):
        x = q_ref[:, r * NSA_DK:(r + 1) * NSA_DK]
        y = x * lax.rsqrt(jnp.mean(x * x, axis=-1, keepdims=True) + EPS) * (w_ref[...] * (NSA_DK ** -0.5))
        o_ref[0, r, :, 0:NSA_DK] = y.astype(o_ref.dtype)
        tab = jnp.broadcast_to(tab_ref[0, r:r + 1, :], (tq, npad))
        slope = tab[:, 1:2] + tab[:, 3:4]
        extra = jnp.where(lane == _ROW_LANE - NSA_DK, -slope * t, tab)
        o_ref[0, r, :, NSA_DK:NSA_DKP] = extra.astype(o_ref.dtype)


def nsa_q_prep(q_raw, q_norm, *, tq, seq, pos0=0):
    rows = q_raw.shape[0]
    return pl.pallas_call(
        functools.partial(_q_prep_kernel, seq=seq, pos0=pos0),
        grid=(rows // tq, NSA_KV),
        in_specs=[pl.BlockSpec((tq, NSA_REP * NSA_DK), lambda i, g: (i, g)),
                  pl.BlockSpec((1, NSA_DK), lambda i, g: (0, 0)),
                  pl.BlockSpec((1, NSA_REP, NSA_DKP - NSA_DK), lambda i, g: (g, 0, 0))],
        out_specs=pl.BlockSpec((1, NSA_REP, tq, NSA_DKP), lambda i, g: (g, 0, i, 0)),
        out_shape=jax.ShapeDtypeStruct((NSA_KV, NSA_REP, rows, NSA_DKP), jnp.bfloat16),
        compiler_params=_cparams("parallel", "parallel"),
        name="nsa_q_prep",
    )(q_raw, q_norm.reshape(1, NSA_DK), _slope_table())


def _k_prep_kernel(k_ref, w_ref, on_ref, oh_ref, *, seq, onehot):
    tk = k_ref.shape[0]
    pos = (pl.program_id(0) * tk) % seq + lax.broadcasted_iota(jnp.int32, (tk, NSA_DKP - NSA_DK), 0)
    lane = lax.broadcasted_iota(jnp.int32, (tk, NSA_DKP - NSA_DK), 1)
    blk, off = pos // SLC_BLOCK, pos % SLC_BLOCK
    extra = jnp.where((lane == 0) | (lane == 2), blk, jnp.where((lane == 1) | (lane == 3), off, 0))
    if onehot:
        extra = jnp.where((lane >= 4) & (lane < 4 + _SEL_LANES) & (lane - 4 == blk), 1, extra)
    extra = jnp.where(lane == _ROW_LANE - NSA_DK, 1, extra)
    extra = extra.astype(jnp.float32)
    for g in range(NSA_KV):
        x = k_ref[:, g * NSA_DK:(g + 1) * NSA_DK]
        x = x * lax.rsqrt(jnp.mean(x * x, axis=-1, keepdims=True) + EPS) * w_ref[...]
        on_ref[:, g * NSA_DK:(g + 1) * NSA_DK] = x
        oh_ref[g, :, 0:NSA_DK] = x.astype(oh_ref.dtype)
        oh_ref[g, :, NSA_DK:NSA_DKP] = extra.astype(oh_ref.dtype)


def nsa_k_prep(k_raw, k_norm, *, seq, tk, onehot):
    rows = k_raw.shape[0]
    return pl.pallas_call(
        functools.partial(_k_prep_kernel, seq=seq, onehot=onehot),
        grid=(rows // tk,),
        in_specs=[pl.BlockSpec((tk, NSA_KW), lambda i: (i, 0)), pl.BlockSpec((1, NSA_DK), lambda i: (0, 0))],
        out_specs=[pl.BlockSpec((tk, NSA_KW), lambda i: (i, 0)),
                   pl.BlockSpec((NSA_KV, tk, NSA_DKP), lambda i: (0, i, 0))],
        out_shape=[jax.ShapeDtypeStruct((rows, NSA_KW), jnp.float32),
                   jax.ShapeDtypeStruct((NSA_KV, rows, NSA_DKP), jnp.bfloat16)],
        compiler_params=_cparams("parallel"),
        name="nsa_k_prep",
    )(k_raw, k_norm.reshape(1, NSA_DK))


def _v_prep_kernel(v_ref, o_ref):
    tk = v_ref.shape[0]
    ones_col = jnp.where(lax.broadcasted_iota(jnp.int32, (tk, NSA_DVP - NSA_DV), 1) == 0, 1.0, 0.0)
    for g in range(NSA_KV):
        o_ref[g, :, 0:NSA_DV] = v_ref[:, g * NSA_DV:(g + 1) * NSA_DV].astype(o_ref.dtype)
        o_ref[g, :, NSA_DV:NSA_DVP] = ones_col.astype(o_ref.dtype)


def nsa_v_prep(v_raw, *, tk):
    rows = v_raw.shape[0]
    return pl.pallas_call(
        _v_prep_kernel,
        grid=(rows // tk,),
        in_specs=[pl.BlockSpec((tk, NSA_VW), lambda i: (i, 0))],
        out_specs=pl.BlockSpec((NSA_KV, tk, NSA_DVP), lambda i: (0, i, 0)),
        out_shape=jax.ShapeDtypeStruct((NSA_KV, rows, NSA_DVP), jnp.bfloat16),
        compiler_params=_cparams("parallel"),
        name="nsa_v_prep",
    )(v_raw)


def _compress_partials_kernel(tab_ref, *refs, d, pages, transposed):
    x_refs, (pe_ref, w1_ref, a_ref, b_ref) = refs[:pages], refs[pages:]
    seg = PAGE_ROWS // CMP_STRIDE
    acc_a = acc_b = None
    if transposed:
        rows_of = [[r[0, g].T.reshape(seg, CMP_STRIDE, d) for r in x_refs] for g in range(NSA_KV)]
        take = lambda ri, l, g: rows_of[g][ri][:, l, :]
    elif len(x_refs[0].shape) == 4:
        take = lambda ri, l, g: x_refs[ri][:, l, g, :]
    else:
        take = lambda ri, l, g: x_refs[ri][:, l, g * d:(g + 1) * d]

    for l in range(CMP_STRIDE):
        xg = jnp.concatenate([take(ri, l, g) for g in range(NSA_KV) for ri in range(pages)], axis=0)
        ta = jnp.dot((xg + pe_ref[l:l + 1, :]).astype(jnp.bfloat16), w1_ref[l], preferred_element_type=jnp.float32)
        tb = jnp.dot((xg + pe_ref[CMP_STRIDE + l:CMP_STRIDE + l + 1, :]).astype(jnp.bfloat16),
                     w1_ref[CMP_STRIDE + l], preferred_element_type=jnp.float32)
        acc_a = ta if acc_a is None else acc_a + ta
        acc_b = tb if acc_b is None else acc_b + tb
    n = pages * seg
    for g in range(NSA_KV):
        a_ref[0, g] = acc_a[g * n:(g + 1) * n]
        b_ref[0, g] = acc_b[g * n:(g + 1) * n]


def nsa_compress_partials(pool3, table, pe, w1, *, pages, transposed=False):
    d = pe.shape[1]
    batch, npg = table.shape
    seg = PAGE_ROWS // CMP_STRIDE
    blk = (1,) + tuple(pool3.shape[1:]) if transposed else (seg, CMP_STRIDE) + tuple(pool3.shape[2:])
    x_spec = lambda k: pl.BlockSpec(blk, lambda b, i, tab: (tab[b, i * pages + k],) + (0,) * (len(blk) - 1))
    out_spec = pl.BlockSpec((1, NSA_KV, pages * seg, CMP_HIDDEN), lambda b, i, tab: (b, 0, i, 0))
    out_sds = jax.ShapeDtypeStruct((batch, NSA_KV, npg * seg, CMP_HIDDEN), jnp.float32)
    return pl.pallas_call(
        functools.partial(_compress_partials_kernel, d=d, pages=pages, transposed=transposed),
        grid_spec=pltpu.PrefetchScalarGridSpec(
            num_scalar_prefetch=1,
            grid=(batch, npg // pages),
            in_specs=[x_spec(k) for k in range(pages)]
            + [pl.BlockSpec((CMP_BLOCK, d), lambda b, i, tab: (0, 0)),
               pl.BlockSpec((CMP_BLOCK, d, CMP_HIDDEN), lambda b, i, tab: (0, 0, 0))],
            out_specs=[out_spec, out_spec]),
        out_shape=[out_sds, out_sds],
        compiler_params=_cparams("parallel", "arbitrary"),
        name="nsa_compress_partials",
    )(table, *([pool3] * pages), pe, w1.astype(jnp.bfloat16))


def _cmp_extra_lanes(nrows, width):
    n = lax.broadcasted_iota(jnp.int32, (nrows, width), 0)
    lane = lax.broadcasted_iota(jnp.int32, (nrows, width), 1)
    ec = n * CMP_STRIDE + CMP_BLOCK - 1
    blk, off = ec // SLC_BLOCK, ec % SLC_BLOCK
    pl_ = lane - _POS_LANE
    return jnp.where((pl_ == 0) | (pl_ == 2), blk, jnp.where((pl_ == 1) | (pl_ == 3), off, 0)).astype(jnp.float32)


def _compress_finish_kernel(a_ref, b_ref, bn_ref, w2_ref, kn_ref, o_ref, *, is_key, has_new, alibi_lanes):
    ns = a_ref.shape[2]
    row = lax.broadcasted_iota(jnp.int32, (ns, 1), 0)
    for g in range(NSA_KV):
        nxt = pltpu.roll(b_ref[0, g], ns - 1, 0)
        if has_new:
            nxt = jnp.where(row == ns - 1, bn_ref[0, g, 0:1, :], nxt)
        hid_pre = a_ref[0, g] + nxt
        hid = hid_pre * (1.0 / (1.0 + jnp.exp(-hid_pre)))
        out = jnp.dot(hid.astype(jnp.bfloat16), w2_ref[...], preferred_element_type=jnp.float32)
        if is_key:
            ms = jnp.sum(out * out, axis=-1, keepdims=True) * (1.0 / NSA_DK)
            out = out * lax.rsqrt(ms + EPS) * kn_ref[...]
            if alibi_lanes:
                out = out + _cmp_extra_lanes(ns, out.shape[1])
        o_ref[0, g] = out.astype(o_ref.dtype)


def nsa_compress_finish(part_a, part_b, part_b_new, w2, kn, *, is_key, alibi_lanes):
    batch, _, ns, _ = part_a.shape
    d = w2.shape[1]
    dp = NSA_DKP if is_key else d
    has_new = part_b_new is not None
    if not has_new:
        part_b_new = jnp.zeros((batch, NSA_KV, 8, CMP_HIDDEN), jnp.float32)
    w2p = jnp.pad(w2, ((0, 0), (0, dp - d))).astype(jnp.bfloat16)
    knp = jnp.pad(kn, (0, dp - d)).reshape(1, dp)
    blk = lambda n: pl.BlockSpec((1, NSA_KV, n, CMP_HIDDEN), lambda b: (b, 0, 0, 0))
    return pl.pallas_call(
        functools.partial(_compress_finish_kernel, is_key=is_key, has_new=has_new, alibi_lanes=alibi_lanes),
        grid=(batch,),
        in_specs=[blk(ns), blk(ns), blk(part_b_new.shape[2]),
                  pl.BlockSpec((CMP_HIDDEN, dp), lambda b: (0, 0)),
                  pl.BlockSpec((1, dp), lambda b: (0, 0))],
        out_specs=pl.BlockSpec((1, NSA_KV, ns, dp), lambda b: (b, 0, 0, 0)),
        out_shape=jax.ShapeDtypeStruct((batch, NSA_KV, ns, dp), jnp.bfloat16),
        compiler_params=_cparams("parallel"),
        name="nsa_compress_finish_k" if is_key else "nsa_compress_finish_v",
    )(part_a, part_b, part_b_new, w2p, knp)


def _split_dot(x, m_bf16):
    hi, lo = _split_bf16(x)
    f = functools.partial(jnp.dot, preferred_element_type=jnp.float32)
    return f(hi, m_bf16) + f(lo, m_bf16)


def _topn_mask_t(imp_t, t_row, nsb, topn):
    jb = lax.broadcasted_iota(jnp.int32, imp_t.shape, 0)
    cur = t_row // SLC_BLOCK
    forced = (jb == 0) | (jb == cur) | (jb == cur - 1)
    valid = (jb * SLC_BLOCK <= t_row) & (jb < nsb)
    impm = jnp.where(forced, jnp.inf, jnp.where(valid, imp_t, -jnp.inf))
    rank = jnp.zeros(imp_t.shape, jnp.float32)
    for k in range(nsb):
        row = impm[k:k + 1, :]
        beats = (row > impm) | ((row == impm) & (k < jb))
        rank = rank + jnp.where(beats, 1.0, 0.0)
    return (rank < topn) & (jb < nsb)


def _cmp_and_select(q, kc, vc, t0, tq, seq):
    rep = NSA_REP
    m_rows = rep * tq
    nc = seq // CMP_STRIDE - CMP_RATIO + 1
    nseg = kc.shape[0]
    nsb = seq // SLC_BLOCK
    t_col = t0 + lax.broadcasted_iota(jnp.int32, (rep, tq, 1), 1).reshape(m_rows, 1)
    t_row = t0 + lax.broadcasted_iota(jnp.int32, (1, tq), 1)
    n_row = lax.broadcasted_iota(jnp.int32, (1, nseg), 1)
    cvalid = (n_row * CMP_STRIDE + CMP_BLOCK - 1 <= t_col) & (n_row < nc)
    s = jnp.where(cvalid, _dot_nt(q, kc), _NEG)
    p = jnp.where(cvalid, jnp.exp(s - jnp.max(s, axis=1, keepdims=True)), 0.0)
    p = p / jnp.maximum(jnp.sum(p, axis=1, keepdims=True), 1e-30)
    o_c = jnp.dot(p.astype(jnp.bfloat16), vc, preferred_element_type=jnp.float32)
    psum = jnp.sum(p.reshape(rep, tq, nseg), axis=0)
    cj = lax.broadcasted_iota(jnp.int32, (_SEL_LANES, nseg), 0) * SLC_BLOCK
    ci = lax.broadcasted_iota(jnp.int32, (_SEL_LANES, nseg), 1) * CMP_STRIDE
    ov = jnp.maximum(jnp.minimum(ci + CMP_BLOCK, cj + SLC_BLOCK) - jnp.maximum(ci, cj), 0)
    c2s_t = (ov.astype(jnp.float32) * (1.0 / CMP_STRIDE)).astype(jnp.bfloat16)
    p_hi, p_lo = _split_bf16(psum)
    imp_t = _dot_nt(c2s_t, p_hi) + _dot_nt(c2s_t, p_lo)
    sel_t = _topn_mask_t(imp_t, t_row, nsb, min(SLC_TOPN, nsb))
    pen_t = jnp.where(sel_t, 0.0, -_MASK_BIG).astype(jnp.bfloat16)
    pi = lax.broadcasted_iota(jnp.int32, (_SEL_LANES, NSA_DKP), 0)
    pj = lax.broadcasted_iota(jnp.int32, (_SEL_LANES, NSA_DKP), 1)
    place = jnp.where(pj == pi + _SEL_LANE, 1.0, 0.0).astype(jnp.bfloat16)
    q_add = _dot_tn(pen_t, place).astype(jnp.bfloat16)
    q_slc = (q.reshape(rep, tq, NSA_DKP) + q_add[None]).reshape(m_rows, NSA_DKP)
    return o_c, q_slc


def _gate_combine(gl_ref, z_ref, o_ref, o_c, o_s, o_w, tq):
    gates = 1.0 / (1.0 + jnp.exp(-gl_ref[...]))
    for r in range(NSA_REP):
        rs = slice(r * tq, (r + 1) * tq)
        o = (gates[:, 3 * r:3 * r + 1] * o_c[rs] + gates[:, 3 * r + 1:3 * r + 2] * o_s[rs]
             + gates[:, 3 * r + 2:3 * r + 3] * o_w[rs])
        z = z_ref[:, r * NSA_DV:(r + 1) * NSA_DV]
        o_ref[:, r * NSA_DV:(r + 1) * NSA_DV] = (o * (z * (1.0 / (1.0 + jnp.exp(-z))))).astype(o_ref.dtype)


def _nsa_prompt_fast_kernel(q_ref, kc_ref, vc_ref, ks_ref, vs_ref, kw_ref, vw_ref, gl_ref, z_ref, o_ref,
                            *, tq, tk, seq):
    t0 = pl.program_id(2) * tq
    m_rows = NSA_REP * tq
    q = q_ref[0].reshape(m_rows, NSA_DKP)
    t_col = t0 + lax.broadcasted_iota(jnp.int32, (NSA_REP, tq, 1), 1).reshape(m_rows, 1)
    o_c, q_slc = _cmp_and_select(q, kc_ref[0, 0], vc_ref[0, 0], t0, tq, seq)
    kpos = lax.broadcasted_iota(jnp.int32, (1, tk), 1)
    kt_hi = (t0 + tq + tk - 1) // tk
    zero = jnp.zeros((m_rows, NSA_DVP), jnp.float32)

    def pv(p, v_ref, k0):
        return jnp.dot(p.astype(jnp.bfloat16), v_ref[0, pl.ds(k0, tk), :], preferred_element_type=jnp.float32)

    def slc_open(kt, acc):
        k0 = pl.multiple_of(kt * tk, tk)
        return acc + pv(jnp.exp(_dot_nt(q_slc, ks_ref[0, pl.ds(k0, tk), :])), vs_ref, k0)

    acc = lax.fori_loop(0, kt_hi - 1, slc_open, zero)
    k0 = pl.multiple_of((kt_hi - 1) * tk, tk)
    s = _dot_nt(q_slc, ks_ref[0, pl.ds(k0, tk), :])
    acc = acc + pv(jnp.where((k0 + kpos) <= t_col, jnp.exp(s), 0.0), vs_ref, k0)
    o_s = acc[:, 0:NSA_DV] / acc[:, NSA_DV:NSA_DV + 1]

    def win_body(kt, acc):
        k0 = pl.multiple_of(kt * tk, tk)
        dw = t_col - (k0 + kpos)
        s = _dot_nt(q, kw_ref[0, pl.ds(k0, tk), :])
        return acc + pv(jnp.where((dw >= 0) & (dw < WINDOW), jnp.exp(s), 0.0), vw_ref, k0)

    acc = lax.fori_loop(jnp.maximum(t0 - (WINDOW - 1), 0) // tk, kt_hi, win_body, zero)
    o_w = acc[:, 0:NSA_DV] / acc[:, NSA_DV:NSA_DV + 1]
    _gate_combine(gl_ref, z_ref, o_ref, o_c, o_s, o_w, tq)


def _flash_step(q, k, v, mask, carry):
    m, l, acc = carry
    s = _dot_nt(q, k)
    s = jnp.where(mask, s, _NEG)
    m_new = jnp.maximum(m, jnp.max(s, axis=1, keepdims=True))
    alpha = jnp.exp(m - m_new)
    p = jnp.exp(s - m_new)
    l = alpha * l + jnp.sum(p, axis=1, keepdims=True)
    acc = alpha * acc + jnp.dot(p.astype(jnp.bfloat16), v, preferred_element_type=jnp.float32)
    return m_new, l, acc


def _nsa_prompt_kernel(q_ref, kc_ref, vc_ref, ks_ref, vs_ref, kw_ref, vw_ref, gl_ref, z_ref, o_ref,
                       *, tq, tk, seq):
    t0 = pl.program_id(2) * tq
    m_rows = NSA_REP * tq
    q = q_ref[0].reshape(m_rows, NSA_DKP)
    t_col = t0 + lax.broadcasted_iota(jnp.int32, (NSA_REP, tq, 1), 1).reshape(m_rows, 1)
    o_c, q_slc = _cmp_and_select(q, kc_ref[0, 0], vc_ref[0, 0], t0, tq, seq)

    kpos = lax.broadcasted_iota(jnp.int32, (1, tk), 1)
    init = (jnp.full((m_rows, 1), _NEG, jnp.float32), jnp.zeros((m_rows, 1), jnp.float32),
            jnp.zeros((m_rows, NSA_DV), jnp.float32))

    def slc_body(kt, carry):
        k0 = pl.multiple_of(kt * tk, tk)
        mask = (k0 + kpos) <= t_col
        return _flash_step(q_slc, ks_ref[0, pl.ds(k0, tk), :], vs_ref[0, pl.ds(k0, tk), :], mask, carry)

    _, l_s, acc_s = lax.fori_loop(0, (t0 + tq + tk - 1) // tk, slc_body, init)

    def win_body(kt, carry):
        k0 = pl.multiple_of(kt * tk, tk)
        dw = t_col - (k0 + kpos)
        mask = (dw >= 0) & (dw < WINDOW)
        return _flash_step(q, kw_ref[0, pl.ds(k0, tk), :], vw_ref[0, pl.ds(k0, tk), :], mask, carry)

    kt_lo = jnp.maximum(t0 - (WINDOW - 1), 0) // tk
    _, l_w, acc_w = lax.fori_loop(kt_lo, (t0 + tq + tk - 1) // tk, win_body, init)
    _gate_combine(gl_ref, z_ref, o_ref, o_c, acc_s / l_s, acc_w / l_w, tq)


def nsa_prompt_attention(q_hm, kcmp, vcmp, ks_hm, vs_hm, kw_hm, vw_hm, gl, z, *, batch, tq, tk, fast):
    rows = z.shape[0]
    seq = rows // batch
    nq = seq // tq
    nseg = kcmp.shape[2]
    row_blk = lambda b, g, i: (b * nq + i, g)
    kv_blk = lambda b, g, i: (g, b, 0)
    dvb = NSA_DVP if fast else NSA_DV
    return pl.pallas_call(
        functools.partial(_nsa_prompt_fast_kernel if fast else _nsa_prompt_kernel, tq=tq, tk=tk, seq=seq),
        grid=(batch, NSA_KV, nq),
        in_specs=[pl.BlockSpec((1, NSA_REP, tq, NSA_DKP), lambda b, g, i: (g, 0, b * nq + i, 0)),
                  pl.BlockSpec((1, 1, nseg, NSA_DKP), lambda b, g, i: (b, g, 0, 0)),
                  pl.BlockSpec((1, 1, nseg, NSA_DV), lambda b, g, i: (b, g, 0, 0)),
                  pl.BlockSpec((1, seq, NSA_DKP), kv_blk),
                  pl.BlockSpec((1, seq, dvb), kv_blk),
                  pl.BlockSpec((1, seq, NSA_DKP), kv_blk),
                  pl.BlockSpec((1, seq, dvb), kv_blk),
                  pl.BlockSpec((tq, LANE), row_blk),
                  pl.BlockSpec((tq, NSA_REP * NSA_DV), row_blk)],
        out_specs=pl.BlockSpec((tq, NSA_REP * NSA_DV), row_blk),
        out_shape=jax.ShapeDtypeStruct((rows, NSA_O), jnp.bfloat16),
        compiler_params=_cparams("parallel", "parallel", "arbitrary"),
        name="nsa_prompt_attention_fast" if fast else "nsa_prompt_attention",
    )(q_hm, kcmp, vcmp, ks_hm, vs_hm, kw_hm, vw_hm, gl, z)


def _nsa_in_proj(x, norm_w, w_in):
    B, L, D = x.shape
    x2 = x.reshape(B * L, D)
    h = rmsnorm_bf16(x2, norm_w, tm=min(256, B * L))
    offs = np.cumsum((0,) + NSA_SPLITS)
    tm = _row_tile(B * L)

    wt = w_in.T

    def seg(i, tn, name):
        return matmul_wt(h, wt, tm=tm, tn=tn, n_off=int(offs[i]), n=int(offs[i + 1] - offs[i]), name=name)

    q = seg(0, _WT_TN, "nsa_q_proj")
    kc = seg(1, 256, "nsa_kc_proj")
    vc = seg(2, 256, "nsa_vc_proj")
    ks = seg(3, 256, "nsa_ks_proj")
    vs = seg(4, 256, "nsa_vs_proj")
    kw = seg(5, 256, "nsa_kw_proj")
    vw = seg(6, 256, "nsa_vw_proj")
    wg = wt[offs[7]:offs[8]].reshape(NSA_KV, 3 * NSA_REP, D)
    wg = jnp.pad(wg, ((0, 0), (0, LANE - 3 * NSA_REP), (0, 0))).reshape(NSA_KV * LANE, D)
    gl = matmul_wt(h, wg, tm=tm, tn=NSA_KV * LANE, n_off=0, n=NSA_KV * LANE, name="nsa_gate_proj")
    z = matmul_wt(h, wt[offs[8]:], tm=tm, tn=_WT_TN, n_off=0, n=NSA_O, name="nsa_z_proj")
    return x2, q, kc, vc, ks, vs, kw, vw, gl, z


def _nsa_prompt(x, norm_w, w_in, q_norm, kn_c, kn_s, kn_w, pe_k, w1_k, w2_k, pe_v, w1_v, w2_v, w_out):
    B, T, D = x.shape
    x2, q, kc_r, vc_r, ks_raw, vs_r, kw_raw, vw_r, gl, z = _nsa_in_proj(x, norm_w, w_in)
    tp = min(512, T)
    q_hm = nsa_q_prep(q, q_norm, tq=tp, seq=T)
    ks_r, ks_hm = nsa_k_prep(ks_raw, kn_s, seq=T, tk=tp, onehot=True)
    kw_r, kw_hm = nsa_k_prep(kw_raw, kn_w, seq=T, tk=tp, onehot=False)
    vs_hm = nsa_v_prep(vs_r, tk=tp)
    vw_hm = nsa_v_prep(vw_r, tk=tp)
    npg = T // PAGE_ROWS
    table = (jnp.arange(B, dtype=jnp.int32)[:, None] * npg + jnp.arange(npg, dtype=jnp.int32)[None, :])
    seg3 = lambda a: a.reshape(B * T // CMP_STRIDE, CMP_STRIDE, a.shape[1])
    ka, kb = nsa_compress_partials(seg3(kc_r), table, pe_k, w1_k, pages=min(_CMP_PAGES, npg))
    va, vb = nsa_compress_partials(seg3(vc_r), table, pe_v, w1_v, pages=min(_CMP_PAGES, npg))
    kcmp = nsa_compress_finish(ka, kb, None, w2_k, kn_c, is_key=True, alibi_lanes=True)
    vcmp = nsa_compress_finish(va, vb, None, w2_v, jnp.zeros((NSA_DV,), jnp.float32), is_key=False, alibi_lanes=False)
    bound = jnp.max(jnp.abs(q_norm)) * jnp.maximum(jnp.max(jnp.abs(kn_s)), jnp.max(jnp.abs(kn_w))) * (NSA_DK ** 0.5)
    attend = lambda fast: functools.partial(nsa_prompt_attention, batch=B, tq=_NSA_TQ, tk=_NSA_TK, fast=fast)
    og = lax.cond(bound < _FAST_SCORE_BOUND, attend(True), attend(False),
                  q_hm, kcmp, vcmp, ks_hm, vs_hm, kw_hm, vw_hm, gl, z)
    y = matmul(og, w_out.astype(jnp.bfloat16), tm=_row_tile(B * T), tn=1024, residual=x2, name="nsa_out_proj")
    wl = min(WINDOW, T)
    r4 = lambda a, d: a.reshape(B, T, NSA_KV, d)
    return y.reshape(B, T, D), (r4(kc_r, NSA_DK), r4(vc_r, NSA_DV), r4(ks_r, NSA_DK), r4(vs_r, NSA_DV),
                                r4(kw_r, NSA_DK)[:, T - wl:], r4(vw_r, NSA_DV)[:, T - wl:])


_SMP_PAGES = 8
_SEL_PAD = 384


def _online_update(m_ref, l_ref, acc_ref, g, s, v):
    m_old = m_ref[g]
    m_new = jnp.maximum(m_old, jnp.max(s, axis=1, keepdims=True))
    alpha = jnp.exp(m_old - m_new)
    p = jnp.exp(s - m_new)
    l_ref[g] = alpha * l_ref[g] + jnp.sum(p, axis=1, keepdims=True)
    acc_ref[g] = alpha * acc_ref[g] + jnp.dot(p.astype(jnp.bfloat16), v.astype(jnp.bfloat16),
                                              preferred_element_type=jnp.float32)
    m_ref[g] = m_new


def _nsa_sample_kernel(tab_ref, *refs, past, steps, pages, nsb):
    kp_refs = refs[:pages]
    vp_refs = refs[pages:2 * pages]
    (q_ref, kc_ref, vc_ref, kn_ref, vn_ref, kwc_ref, vwc_ref, kwn_ref, vwn_ref, gl_ref, z_ref,
     o_ref, m_scr, l_scr, acc_scr, sel_scr, oc_scr) = refs[2 * pages:]
    i = pl.program_id(1)
    s_tok = o_ref.shape[0]
    rows = NSA_REP * s_tok
    ncmp = kc_ref.shape[2]
    r_col = lax.broadcasted_iota(jnp.int32, (NSA_REP, s_tok, 1), 0).reshape(rows, 1)
    t_col = past + lax.broadcasted_iota(jnp.int32, (NSA_REP, s_tok, 1), 1).reshape(rows, 1)
    t_tok = past + lax.broadcasted_iota(jnp.int32, (s_tok, 1), 0)

    def slope_col(g):
        return jnp.exp2((r_col + (g * NSA_REP + 1)).astype(jnp.float32) * (-8.0 / NSA_HEADS))

    def q_of(g):
        return q_ref[0, g * rows:(g + 1) * rows, :]

    def rep_rows(x):
        return jnp.concatenate([x] * NSA_REP, axis=0)

    @pl.when(i == 0)
    def _first():
        n_row = lax.broadcasted_iota(jnp.int32, (1, ncmp), 1)
        cvalid = (n_row * CMP_STRIDE + CMP_BLOCK - 1) <= t_col
        ci = lax.broadcasted_iota(jnp.int32, (ncmp, _SEL_PAD), 0) * CMP_STRIDE
        cj = lax.broadcasted_iota(jnp.int32, (ncmp, _SEL_PAD), 1) * SLC_BLOCK
        ov = jnp.maximum(jnp.minimum(ci + CMP_BLOCK, cj + SLC_BLOCK) - jnp.maximum(ci, cj), 0)
        c2s = (ov.astype(jnp.float32) * (1.0 / CMP_STRIDE)).astype(jnp.bfloat16)
        jb = lax.broadcasted_iota(jnp.int32, (s_tok, _SEL_PAD), 1)
        cur = t_tok // SLC_BLOCK
        forced = (jb == 0) | (jb == cur) | (jb == cur - 1)
        valid = (jb * SLC_BLOCK <= t_tok) & (jb < nsb)
        for g in range(NSA_KV):
            s = jnp.where(cvalid, _dot_nt(q_of(g), kc_ref[0, g]), _NEG)
            p = jnp.where(cvalid, jnp.exp(s - jnp.max(s, axis=1, keepdims=True)), 0.0)
            p = p / jnp.maximum(jnp.sum(p, axis=1, keepdims=True), 1e-30)
            oc_scr[g] = jnp.dot(p.astype(jnp.bfloat16), vc_ref[0, g], preferred_element_type=jnp.float32)
            imp = _split_dot(jnp.sum(p.reshape(NSA_REP, s_tok, ncmp), axis=0), c2s)
            work = jnp.where(forced, jnp.inf, jnp.where(valid, imp, -jnp.inf))
            sel = jnp.zeros((s_tok, _SEL_PAD), jnp.float32)
            for _ in range(min(SLC_TOPN, nsb)):
                mx = jnp.max(work, axis=1, keepdims=True)
                first = jnp.min(jnp.where(work == mx, jb, _SEL_PAD), axis=1, keepdims=True)
                hit = jb == first
                sel = jnp.where(hit, 1.0, sel)
                work = jnp.where(hit, -jnp.inf, work)
            sel_scr[g] = sel
            m_scr[g] = jnp.full((rows, 1), _NEG, jnp.float32)
            l_scr[g] = jnp.zeros((rows, 1), jnp.float32)
            acc_scr[g] = jnp.zeros((rows, NSA_DV), jnp.float32)

    def fold(score_of, v_of, nk, page0, causal):
        pos = page0 * PAGE_ROWS + lax.broadcasted_iota(jnp.int32, (1, nk), 1)
        bj = lax.broadcasted_iota(jnp.int32, (_SEL_PAD, nk), 0)
        bl = lax.broadcasted_iota(jnp.int32, (_SEL_PAD, nk), 1)
        expand = jnp.where(bj == (page0 * PAGE_ROWS + bl) // SLC_BLOCK, 1.0, 0.0).astype(jnp.bfloat16)
        for g in range(NSA_KV):
            selk = jnp.dot(sel_scr[g].astype(jnp.bfloat16), expand, preferred_element_type=jnp.float32)
            ok = rep_rows(selk) > 0.5
            if causal:
                ok = ok & (pos <= t_col)
            s = score_of(g, q_of(g)[:, 0:NSA_DK])
            s = s - slope_col(g) * (t_col - pos).astype(jnp.float32)
            _online_update(m_scr, l_scr, acc_scr, g, jnp.where(ok, s, _NEG), v_of(g))

    fold(lambda g, qg: _dot(qg, jnp.concatenate([r[0, g] for r in kp_refs], axis=1)),
         lambda g: jnp.concatenate([r[0, :, g, :] for r in vp_refs], axis=0),
         pages * PAGE_ROWS, i * pages, causal=False)

    @pl.when(i == steps - 1)
    def _last():
        fold(lambda g, qg: _dot_nt(qg, kn_ref[0, :, g * NSA_DK:(g + 1) * NSA_DK]),
             lambda g: vn_ref[0, :, g * NSA_DV:(g + 1) * NSA_DV], PAGE_ROWS, past // PAGE_ROWS, causal=True)
        nwc = kwc_ref.shape[3]
        posw = jnp.concatenate([past - nwc + lax.broadcasted_iota(jnp.int32, (1, nwc), 1),
                                past + lax.broadcasted_iota(jnp.int32, (1, PAGE_ROWS), 1)], axis=1)
        dw = t_col - posw
        wok = (dw >= 0) & (dw < WINDOW) & (posw >= 0)
        gates = 1.0 / (1.0 + jnp.exp(-gl_ref[...]))
        for g in range(NSA_KV):
            ksl = slice(g * NSA_DK, (g + 1) * NSA_DK)
            vsl = slice(g * NSA_DV, (g + 1) * NSA_DV)
            qg = q_of(g)[:, 0:NSA_DK]
            s = jnp.concatenate([_dot(qg, kwc_ref[0, g]), _dot_nt(qg, kwn_ref[0, :, ksl])], axis=1)
            s = jnp.where(wok, s - slope_col(g) * dw.astype(jnp.float32), _NEG)
            p = jnp.where(wok, jnp.exp(s - jnp.max(s, axis=1, keepdims=True)), 0.0)
            p = (p / jnp.maximum(jnp.sum(p, axis=1, keepdims=True), 1e-30)).astype(jnp.bfloat16)
            o_w = (jnp.dot(p[:, 0:nwc], vwc_ref[0, :, vsl].astype(jnp.bfloat16), preferred_element_type=jnp.float32)
                   + jnp.dot(p[:, nwc:], vwn_ref[0, :, vsl].astype(jnp.bfloat16), preferred_element_type=jnp.float32))
            o_s = acc_scr[g] / l_scr[g]
            o_c = oc_scr[g]
            for r in range(NSA_REP):
                rs = slice(r * s_tok, (r + 1) * s_tok)
                c0 = g * LANE + 3 * r
                o = (gates[:, c0:c0 + 1] * o_c[rs] + gates[:, c0 + 1:c0 + 2] * o_s[rs]
                     + gates[:, c0 + 2:c0 + 3] * o_w[rs])
                hs = slice((g * NSA_REP + r) * NSA_DV, (g * NSA_REP + r + 1) * NSA_DV)
                z = z_ref[:, hs]
                o_ref[:, hs] = o * (z * (1.0 / (1.0 + jnp.exp(-z))))


def nsa_sample_attention(q_rows, kcmp, vcmp, pool_k, pool_v, page_table, k_new, v_new, kw_cache, vw_cache,
                         kw_new, vw_new, gl, z, *, s_tok):
    batch, npg = page_table.shape
    pages = _SMP_PAGES
    steps = npg // pages
    past = npg * PAGE_ROWS
    nsb = -(-(past + s_tok) // SLC_BLOCK)
    rows = NSA_REP * s_tok
    ncmp = kcmp.shape[2]
    nwc = kw_cache.shape[3]
    kpage_spec = lambda k: pl.BlockSpec((1, NSA_KV, NSA_DK, PAGE_ROWS), lambda b, i, tab: (tab[b, i * pages + k], 0, 0, 0))
    vpage_spec = lambda k: pl.BlockSpec((1, PAGE_ROWS, NSA_KV, NSA_DV), lambda b, i, tab: (tab[b, i * pages + k], 0, 0, 0))
    per_b = lambda shape: pl.BlockSpec((1,) + shape, lambda b, i, tab: (b,) + (0,) * len(shape))
    tok_blk = lambda w: pl.BlockSpec((s_tok, w), lambda b, i, tab: (b, 0))
    return pl.pallas_call(
        functools.partial(_nsa_sample_kernel, past=past, steps=steps, pages=pages, nsb=nsb),
        grid_spec=pltpu.PrefetchScalarGridSpec(
            num_scalar_prefetch=1,
            grid=(batch, steps),
            in_specs=[kpage_spec(k) for k in range(pages)] + [vpage_spec(k) for k in range(pages)]
            + [per_b((NSA_KV * rows, NSA_DKP)), per_b((NSA_KV, ncmp, NSA_DKP)), per_b((NSA_KV, ncmp, NSA_DV)),
               per_b((PAGE_ROWS, NSA_KW)), per_b((PAGE_ROWS, NSA_VW)),
               per_b((NSA_KV, NSA_DK, nwc)), per_b((nwc, NSA_VW)), per_b((PAGE_ROWS, NSA_KW)), per_b((PAGE_ROWS, NSA_VW)),
               tok_blk(NSA_KV * LANE), tok_blk(NSA_O)],
            out_specs=tok_blk(NSA_O),
            scratch_shapes=[pltpu.VMEM((NSA_KV, rows, 1), jnp.float32), pltpu.VMEM((NSA_KV, rows, 1), jnp.float32),
                            pltpu.VMEM((NSA_KV, rows, NSA_DV), jnp.float32),
                            pltpu.VMEM((NSA_KV, s_tok, _SEL_PAD), jnp.float32),
                            pltpu.VMEM((NSA_KV, rows, NSA_DV), jnp.float32)]),
        out_shape=jax.ShapeDtypeStruct((batch * s_tok, NSA_O), jnp.float32),
        compiler_params=_cparams("parallel", "arbitrary"),
        name="nsa_sample_attention",
    )(page_table, *([pool_k] * pages), *([pool_v] * pages), q_rows, kcmp, vcmp, k_new, v_new,
      kw_cache, vw_cache, kw_new, vw_new, gl, z)


def _nsa_sample(x, ck, cv, sk, sv, wk_buf, wv_buf, page_table, norm_w,
                w_in, q_norm, kn_c, kn_s, kn_w, pe_k, w1_k, w2_k, pe_v, w1_v, w2_v, w_out):
    B, S, D = x.shape
    x2, q, kc_r, vc_r, ks_raw, vs_r, kw_raw, vw_r, gl, z = _nsa_in_proj(x, norm_w, w_in)
    q_hm = nsa_q_prep(q, q_norm, tq=B * S, seq=S, pos0=page_table.shape[1] * PAGE_ROWS)
    q_rows = q_hm.reshape(NSA_KV, NSA_REP, B, S, NSA_DKP).transpose(2, 0, 1, 3, 4).reshape(B, NSA_KV * NSA_REP * S, NSA_DKP)
    ks_r, _ = nsa_k_prep(ks_raw, kn_s, seq=S, tk=B * S, onehot=False)
    kw_r, _ = nsa_k_prep(kw_raw, kn_w, seq=S, tk=B * S, onehot=False)
    as_page = lambda a: jnp.pad(a.reshape(B, S, -1), ((0, 0), (0, PAGE_ROWS - S), (0, 0)))
    seg3 = lambda a, w: a.reshape(-1, CMP_STRIDE, w)
    seg4 = lambda a: a.reshape((-1, CMP_STRIDE) + a.shape[2:])
    ident = jnp.arange(B, dtype=jnp.int32)[:, None]
    rows_minor = lambda a: jnp.transpose(a, (0, 2, 3, 1))
    ka, kb = nsa_compress_partials(rows_minor(ck), page_table, pe_k, w1_k, pages=_CMP_PAGES, transposed=True)
    va, vb = nsa_compress_partials(seg4(cv), page_table, pe_v, w1_v, pages=_CMP_PAGES)
    _, kb_new = nsa_compress_partials(seg3(as_page(kc_r), NSA_KW), ident, pe_k, w1_k, pages=1)
    _, vb_new = nsa_compress_partials(seg3(as_page(vc_r), NSA_VW), ident, pe_v, w1_v, pages=1)
    kcmp = nsa_compress_finish(ka, kb, kb_new, w2_k, kn_c, is_key=True, alibi_lanes=True)
    vcmp = nsa_compress_finish(va, vb, vb_new, w2_v, jnp.zeros((NSA_DV,), jnp.float32), is_key=False, alibi_lanes=False)
    wl = wk_buf.shape[1]
    o = nsa_sample_attention(q_rows, kcmp, vcmp, rows_minor(sk), sv, page_table, as_page(ks_r), as_page(vs_r), rows_minor(wk_buf),
                             wv_buf.reshape(B, wl, NSA_VW), as_page(kw_r), as_page(vw_r), gl, z, s_tok=S)
    y = matmul(o.astype(jnp.bfloat16), w_out.astype(jnp.bfloat16), tm=B * S, tn=1024, residual=x2, name="nsa_out_proj")
    r4 = lambda a, d: a.reshape(B, S, NSA_KV, d)
    kw_all = jnp.concatenate([wk_buf, r4(kw_r, NSA_DK)], axis=1)
    vw_all = jnp.concatenate([wv_buf, r4(vw_r, NSA_DV)], axis=1)
    return y.reshape(B, S, D), (r4(kc_r, NSA_DK), r4(vc_r, NSA_DV), r4(ks_r, NSA_DK), r4(vs_r, NSA_DV),
                                kw_all[:, S:], vw_all[:, S:])


def kernel(x_prompt, x_sample, state_delta, state_conv, cache_cmp_k, cache_cmp_v, cache_slc_k, cache_slc_v, cache_win_k, cache_win_v, page_table, norm_dn, w_in_dn, conv_w_dn, a_log_dn, dt_bias_dn, out_norm_dn, w_out_dn, norm_nsa, w_in_nsa, q_norm_nsa, k_norm_cmp, k_norm_slc, k_norm_win, cmp_pe_k, cmp_w1_k, cmp_w2_k, cmp_pe_v, cmp_w1_v, cmp_w2_v, w_out_nsa):
    xp, xs = x_prompt, x_sample
    B = xp.shape[0]
    dw = (norm_dn[0], w_in_dn[0], conv_w_dn[0], a_log_dn[0], dt_bias_dn[0], out_norm_dn[0], w_out_dn[0])
    buf0 = jnp.zeros((B, DN_CONV - 1, conv_w_dn.shape[-1]), xp.dtype)
    st0 = jnp.zeros((B, DN_HEADS, DN_DK, DN_DV), jnp.float32)
    xp, p_dn = _deltanet_mixer(xp, buf0, st0, *dw)
    xs, s_dn = _deltanet_mixer(xs, state_conv[0], state_delta[0], *dw)
    nw = (norm_nsa[0], w_in_nsa[0], q_norm_nsa[0], k_norm_cmp[0], k_norm_slc[0], k_norm_win[0],
          cmp_pe_k[0], cmp_w1_k[0], cmp_w2_k[0], cmp_pe_v[0], cmp_w1_v[0], cmp_w2_v[0], w_out_nsa[0])
    xp, p_nsa = _nsa_prompt(xp, *nw)
    xs, s_nsa = _nsa_sample(xs, cache_cmp_k[0], cache_cmp_v[0], cache_slc_k[0], cache_slc_v[0],
                            cache_win_k[0], cache_win_v[0], page_table, *nw)
    return ((xp, xs, p_dn[0][None], p_dn[1][None]) + tuple(t[None] for t in p_nsa)
            + (s_dn[0][None], s_dn[1][None]) + tuple(t[None] for t in s_nsa))
```

```python
import functools

import jax
import jax.numpy as jnp
import numpy as np
from jax import lax
from jax.experimental import pallas as pl
from jax.experimental.pallas import tpu as pltpu

D_MODEL = 4096
EPS = 1e-6

DN_HEADS = 32
DN_DK = 128
DN_DV = 128
DN_CONV = 4
DN_CHUNK = 64

NSA_HEADS = 32
NSA_KV = 4
NSA_REP = NSA_HEADS // NSA_KV
NSA_DK = 192
NSA_DV = 128
CMP_BLOCK = 32
CMP_STRIDE = 16
CMP_RATIO = CMP_BLOCK // CMP_STRIDE
CMP_HIDDEN = 256
SLC_BLOCK = 64
SLC_TOPN = 16
WINDOW = 512
NSA_Q = NSA_HEADS * NSA_DK
NSA_KW = NSA_KV * NSA_DK
NSA_VW = NSA_KV * NSA_DV
NSA_O = NSA_HEADS * NSA_DV
NSA_SPLITS = (NSA_Q, NSA_KW, NSA_VW, NSA_KW, NSA_VW, NSA_KW, NSA_VW, 3 * NSA_HEADS, NSA_O)

VMEM_LIMIT_BYTES = 56 * 1024 * 1024
LANE = 128
_WT_TN = 512


def _cparams(*sem):
    return pltpu.CompilerParams(dimension_semantics=sem, vmem_limit_bytes=VMEM_LIMIT_BYTES)


def _rmsnorm_kernel(x_ref, w_ref, o_ref):
    x = x_ref[...]
    ms = jnp.mean(x * x, axis=-1, keepdims=True)
    o_ref[...] = (x * lax.rsqrt(ms + EPS) * w_ref[...]).astype(o_ref.dtype)


def rmsnorm_bf16(x, w, *, tm):
    m, d = x.shape
    return pl.pallas_call(
        _rmsnorm_kernel,
        grid=(m // tm,),
        in_specs=[pl.BlockSpec((tm, d), lambda i: (i, 0)), pl.BlockSpec((1, d), lambda i: (0, 0))],
        out_specs=pl.BlockSpec((tm, d), lambda i: (i, 0)),
        out_shape=jax.ShapeDtypeStruct((m, d), jnp.bfloat16),
        compiler_params=_cparams("parallel"),
        name="rmsnorm_bf16",
    )(x, w.reshape(1, d))


def _mm_kernel(a_ref, b_ref, o_ref):
    o_ref[...] = jnp.dot(a_ref[...], b_ref[...], preferred_element_type=jnp.float32).astype(o_ref.dtype)


def _mm_res_kernel(a_ref, b_ref, r_ref, o_ref):
    acc = jnp.dot(a_ref[...], b_ref[...], preferred_element_type=jnp.float32)
    o_ref[...] = r_ref[...] + acc


def matmul(a, b, *, tm, tn, residual=None, out_dtype=jnp.float32, name="matmul"):
    m, k = a.shape
    _, n = b.shape
    assert m % tm == 0 and n % tn == 0
    in_specs = [pl.BlockSpec((tm, k), lambda i, j: (i, 0)), pl.BlockSpec((k, tn), lambda i, j: (0, j))]
    args = [a, b]
    kern = _mm_kernel
    if residual is not None:
        in_specs.append(pl.BlockSpec((tm, tn), lambda i, j: (i, j)))
        args.append(residual)
        kern = _mm_res_kernel
    return pl.pallas_call(
        kern,
        grid=(m // tm, n // tn),
        in_specs=in_specs,
        out_specs=pl.BlockSpec((tm, tn), lambda i, j: (i, j)),
        out_shape=jax.ShapeDtypeStruct((m, n), out_dtype),
        compiler_params=_cparams("parallel", "parallel"),
        name=name,
    )(*args)


def _mm_wt_kernel(a_ref, w_ref, o_ref):
    o_ref[...] = lax.dot_general(a_ref[...], w_ref[...].astype(jnp.bfloat16), (((1,), (1,)), ((), ())),
                                 preferred_element_type=jnp.float32)


def matmul_wt(a, wt, *, tm, tn, n_off, n, name):
    m, k = a.shape
    assert m % tm == 0 and n % tn == 0 and n_off % tn == 0
    j0 = n_off // tn
    return pl.pallas_call(
        _mm_wt_kernel,
        grid=(m // tm, n // tn),
        in_specs=[pl.BlockSpec((tm, k), lambda i, j: (i, 0)), pl.BlockSpec((tn, k), lambda i, j: (j0 + j, 0))],
        out_specs=pl.BlockSpec((tm, tn), lambda i, j: (i, j)),
        out_shape=jax.ShapeDtypeStruct((m, n), jnp.float32),
        compiler_params=_cparams("parallel", "parallel"),
        name=name,
    )(a, wt)


def _row_tile(m):
    return 1024 if m % 1024 == 0 else m


def _dot(a, b):
    return jnp.dot(a.astype(jnp.bfloat16), b.astype(jnp.bfloat16), preferred_element_type=jnp.float32)


def _dot_nt(a, b):
    return lax.dot_general(a.astype(jnp.bfloat16), b.astype(jnp.bfloat16), (((1,), (1,)), ((), ())),
                           preferred_element_type=jnp.float32)


def _dot_tn(a, b):
    return lax.dot_general(a.astype(jnp.bfloat16), b.astype(jnp.bfloat16), (((0,), (0,)), ((), ())),
                           preferred_element_type=jnp.float32)


def _split_bf16(x):
    hi = x.astype(jnp.bfloat16)
    lo = (x - hi.astype(jnp.float32)).astype(jnp.bfloat16)
    return hi, lo


def _dn_gate_kernel(h_ref, wt_ref, alog_ref, dtb_ref, o_ref, *, heads):
    r = lax.dot_general(wt_ref[...], h_ref[...], (((1,), (1,)), ((), ())),
                        preferred_element_type=jnp.float32)
    b = r[:heads]
    a = r[heads:] + dtb_ref[...]
    softplus = jnp.maximum(a, 0.0) + jnp.log(1.0 + jnp.exp(-jnp.abs(a)))
    o_ref[0:heads, :] = 1.0 / (1.0 + jnp.exp(-b))
    o_ref[heads:2 * heads, :] = -jnp.exp(alog_ref[...]) * softplus


def dn_gates(h, w_ba_t, a_log, dt_bias, *, tm):
    m, d = h.shape
    heads = a_log.shape[0]
    return pl.pallas_call(
        functools.partial(_dn_gate_kernel, heads=heads),
        grid=(m // tm,),
        in_specs=[pl.BlockSpec((tm, d), lambda i: (i, 0)),
                  pl.BlockSpec((2 * heads, d), lambda i: (0, 0)),
                  pl.BlockSpec((heads, 1), lambda i: (0, 0)),
                  pl.BlockSpec((heads, 1), lambda i: (0, 0))],
        out_specs=pl.BlockSpec((2 * heads, tm), lambda i: (0, i)),
        out_shape=jax.ShapeDtypeStruct((2 * heads, m), jnp.float32),
        compiler_params=_cparams("parallel"),
        name="dn_gates",
    )(h, w_ba_t, a_log.reshape(heads, 1), dt_bias.reshape(heads, 1))


_DN_TAIL = 8
_DN_TB = 256
_DN_TMIN = 2 * DN_CHUNK
_DN_HB = 4


def _dn_chunk_prep(q, k, g_row, b_row):
    c = q.shape[0]
    ii = lax.broadcasted_iota(jnp.int32, (c, c), 0)
    jj = lax.broadcasted_iota(jnp.int32, (c, c), 1)
    lower, strict, eye = ii >= jj, ii > jj, ii == jj
    g_b = jnp.broadcast_to(g_row, (c, c))
    b_b = jnp.broadcast_to(b_row, (c, c))
    b_col = jnp.sum(jnp.where(eye, b_b, 0.0), axis=1, keepdims=True)
    g_col = jnp.sum(jnp.where(eye, g_b, 0.0), axis=1, keepdims=True)
    gc_col = jnp.sum(jnp.where(lower, g_b, 0.0), axis=1, keepdims=True)
    gc_row = jnp.sum(jnp.where(ii <= jj, g_col, 0.0), axis=0, keepdims=True)
    g_last = jnp.sum(g_row, axis=1, keepdims=True)
    decay = jnp.exp(jnp.where(lower, gc_col - gc_row, -jnp.inf))
    kk = _dot_nt(k, k)
    qk = _dot_nt(q, k)
    lmat = jnp.where(strict, kk * decay, 0.0) * b_col
    attn = qk * decay
    return dict(x=-lmat, p=eye.astype(jnp.float32) - lmat, attn=attn, b_row=b_row,
                e_row=jnp.exp(gc_row), e_col=jnp.exp(gc_col), c_col=jnp.exp(g_last - gc_col),
                e_last=jnp.exp(g_last))


def _dn_inverse_levels(items, c):
    m = 2
    while m < c:
        for it in items:
            it["x"] = _dot(it["x"], it["x"])
        for it in items:
            it["p"] = it["p"] + _dot(it["p"], it["x"])
        m *= 2


def _dn_kernel(q_ref, k_ref, v_ref, z_ref, beta_ref, g_ref, cw_ref, cs_ref, s0_ref, onw_ref,
               o_ref, s_out_ref, xpad, s_scr, *, tb, hb, live):
    n = pl.program_id(2)
    c = DN_CHUNK
    d = DN_DK
    nc = live

    @pl.when(n == 0)
    def _init():
        xpad[:, 0:_DN_TAIL, :] = cs_ref[0]
        s_scr[...] = s0_ref[0]

    def conv(i, ref):
        xpad[i, _DN_TAIL:_DN_TAIL + tb, :] = ref[...]
        acc = None
        for j in range(DN_CONV):
            off = _DN_TAIL - (DN_CONV - 1) + j
            term = xpad[i, off:off + tb, :] * cw_ref[i, j:j + 1, :]
            acc = term if acc is None else acc + term
        xpad[i, 0:_DN_TAIL, :] = xpad[i, tb:tb + _DN_TAIL, :]
        return acc * (1.0 / (1.0 + jnp.exp(-acc)))

    qc, kc, vc = conv(0, q_ref), conv(1, k_ref), conv(2, v_ref)
    items = []
    for hh in range(hb):
        sl = slice(hh * d, (hh + 1) * d)
        q = qc[:, sl]
        k = kc[:, sl]
        q = q * (lax.rsqrt(jnp.sum(q * q, axis=-1, keepdims=True) + EPS) * (d ** -0.5))
        k = k * lax.rsqrt(jnp.sum(k * k, axis=-1, keepdims=True) + EPS)
        g_all = g_ref[hh]
        b_all = beta_ref[hh]
        for ci in range(nc):
            rs = slice(ci * c, (ci + 1) * c)
            it = _dn_chunk_prep(q[rs], k[rs], g_all[:, rs], b_all[:, rs])
            it.update(q=q[rs], k=k[rs], v=vc[rs, sl])
            items.append(it)
    _dn_inverse_levels(items, c)
    for it in items:
        inv_b = it["p"] * it["b_row"]
        it["u"] = _dot(inv_b, it["v"])
        it["w"] = _dot(inv_b * it["e_row"], it["k"])
    states = [s_scr[hh] for hh in range(hb)]
    outs = [[None] * nc for _ in range(hb)]
    for ci in range(nc):
        for hh in range(hb):
            it = items[hh * nc + ci]
            s = states[hh]
            v_new = it["u"] - _dot(it["w"], s)
            outs[hh][ci] = it["e_col"] * _dot(it["q"], s) + _dot(it["attn"], v_new)
            states[hh] = s * it["e_last"] + _dot_tn(it["k"] * it["c_col"], v_new)
    for hh in range(hb):
        sl = slice(hh * d, (hh + 1) * d)
        s_scr[hh] = states[hh]
        o = jnp.concatenate(outs[hh], axis=0) if nc > 1 else outs[hh][0]
        o = o * lax.rsqrt(jnp.mean(o * o, axis=-1, keepdims=True) + EPS) * onw_ref[...]
        z = z_ref[0:nc * c, sl]
        o_ref[0:nc * c, sl] = (o * (z * (1.0 / (1.0 + jnp.exp(-z))))).astype(o_ref.dtype)
        if nc * c < tb:
            o_ref[nc * c:tb, sl] = jnp.zeros((tb - nc * c, d), o_ref.dtype)

    @pl.when(n == pl.num_programs(2) - 1)
    def _fin():
        s_out_ref[0] = s_scr[...]


def dn_delta(proj, gates, conv_w, conv_state, s0, out_norm, *, batch, tb, hb, live):
    rows, width = proj.shape
    d = DN_DK
    heads = width // (4 * d)
    t = rows // batch
    nt = t // tb
    hg = heads // hb
    wd = hb * d
    row_map = lambda off: (lambda b, h, n: (b * nt + n, off * hg + h))
    gate_map = lambda off: (lambda b, h, n: (off * hg + h, 0, b * nt + n))
    return pl.pallas_call(
        functools.partial(_dn_kernel, tb=tb, hb=hb, live=live),
        grid=(batch, hg, nt),
        in_specs=[pl.BlockSpec((tb, wd), row_map(0)),
                  pl.BlockSpec((tb, wd), row_map(1)),
                  pl.BlockSpec((tb, wd), row_map(2)),
                  pl.BlockSpec((tb, wd), row_map(3)),
                  pl.BlockSpec((hb, 1, tb), gate_map(0)),
                  pl.BlockSpec((hb, 1, tb), gate_map(1)),
                  pl.BlockSpec((3, DN_CONV, wd), lambda b, h, n: (0, 0, h)),
                  pl.BlockSpec((1, 3, _DN_TAIL, wd), lambda b, h, n: (b, 0, 0, h)),
                  pl.BlockSpec((1, hb, d, d), lambda b, h, n: (b, h, 0, 0)),
                  pl.BlockSpec((1, d), lambda b, h, n: (0, 0))],
        out_specs=[pl.BlockSpec((tb, wd), lambda b, h, n: (b * nt + n, h)),
                   pl.BlockSpec((1, hb, d, d), lambda b, h, n: (b, h, 0, 0))],
        out_shape=[jax.ShapeDtypeStruct((rows, heads * d), jnp.bfloat16),
                   jax.ShapeDtypeStruct(s0.shape, jnp.float32)],
        scratch_shapes=[pltpu.VMEM((3, tb + _DN_TAIL, wd), jnp.float32),
                        pltpu.VMEM((hb, d, d), jnp.float32)],
        compiler_params=_cparams("parallel", "parallel", "arbitrary"),
        name="dn_delta",
    )(proj, proj, proj, proj, gates, gates, conv_w, conv_state, s0, out_norm.reshape(1, d))


def _deltanet_mixer(x, conv_buf, s0, norm_w, w_in, conv_w, a_log, dt_bias, out_norm, w_out):
    B, L, D = x.shape
    H = a_log.shape[0]
    cdim = conv_w.shape[1]
    hd = cdim // 3
    x2 = x.reshape(B * L, D)
    h = rmsnorm_bf16(x2, norm_w, tm=min(256, B * L))
    wt = w_in.T
    w_ba_t = wt[cdim + hd:].astype(jnp.bfloat16)
    proj = matmul_wt(h, wt, tm=_row_tile(B * L), tn=_WT_TN, n_off=0, n=cdim + hd, name="dn_in_proj")
    gates = dn_gates(h, w_ba_t, a_log, dt_bias, tm=_row_tile(B * L))
    new_buf = proj.reshape(B, L, -1)[:, L - (DN_CONV - 1):, :cdim]
    T = -(-L // _DN_TMIN) * _DN_TMIN
    if T != L:
        proj = jnp.pad(proj.reshape(B, L, -1), ((0, 0), (0, T - L), (0, 0))).reshape(B * T, -1)
        gates = jnp.pad(gates.reshape(2 * H, B, L), ((0, 0), (0, 0), (0, T - L))).reshape(2 * H, B * T)
    cw = conv_w.reshape(DN_CONV, 3, hd).transpose(1, 0, 2)
    cs = conv_buf.reshape(B, DN_CONV - 1, 3, hd).transpose(0, 2, 1, 3)
    cs = jnp.pad(cs, ((0, 0), (0, 0), (_DN_TAIL - (DN_CONV - 1), 0), (0, 0)))
    tb = min(_DN_TB, T)
    live = tb // DN_CHUNK if T == L else -(-L // DN_CHUNK)
    assert T == L or T == tb
    o, s_new = dn_delta(proj, gates.reshape(2 * H, 1, B * T), cw, cs, s0.astype(jnp.float32), out_norm,
                        batch=B, tb=tb, hb=_DN_HB, live=live)
    if T != L:
        o = o.reshape(B, T, -1)[:, :L].reshape(B * L, -1)
    y = matmul(o, w_out.astype(jnp.bfloat16), tm=_row_tile(B * L), tn=1024, residual=x2, name="dn_out_proj")
    return y.reshape(B, L, D), (s_new, new_buf)


def _alibi_slopes():
    hh = jnp.arange(1, NSA_HEADS + 1, dtype=jnp.float32)
    return jnp.exp2(-8.0 * hh / NSA_HEADS).reshape(NSA_KV, NSA_REP)


PAGE_ROWS = 128
NSA_DKP = 256
_POS_LANE = NSA_DK
_SEL_LANE = NSA_DK + 4
_SEL_LANES = 32
_ROW_LANE = _SEL_LANE + _SEL_LANES
NSA_DVP = 256
_FAST_SCORE_BOUND = 40.0
_MASK_BIG = 2.0 ** 100
_NEG = -1e30
_NSA_TQ = 128
_NSA_TK = 256
_CMP_PAGES = 16


def _slope_table():
    s = _alibi_slopes()
    hi = s.astype(jnp.bfloat16).astype(jnp.float32)
    lo = (s - hi).astype(jnp.bfloat16).astype(jnp.float32)
    tab = jnp.stack([64.0 * hi, hi, 64.0 * lo, lo], axis=-1)
    return jnp.pad(tab, ((0, 0), (0, 0), (0, NSA_DKP - NSA_DK - 4)))


def _q_prep_kernel(q_ref, w_ref, tab_ref, o_ref, *, seq, pos0):
    tq = q_ref.shape[0]
    npad = NSA_DKP - NSA_DK
    t = (pos0 + (pl.program_id(0) * tq) % seq + lax.broadcasted_iota(jnp.int32, (tq, npad), 0)).astype(jnp.float32)
    lane = lax.broadcasted_iota(jnp.int32, (tq, npad), 1)
    for r in range(NSA_REP):
        x = q_ref[:, r * NSA_DK:(r + 1) * NSA_DK]
        y = x * lax.rsqrt(jnp.mean(x * x, axis=-1, keepdims=True) + EPS) * (w_ref[...] * (NSA_DK ** -0.5))
        o_ref[0, r, :, 0:NSA_DK] = y.astype(o_ref.dtype)
        tab = jnp.broadcast_to(tab_ref[0, r:r + 1, :], (tq, npad))
        slope = tab[:, 1:2] + tab[:, 3:4]
        extra = jnp.where(lane == _ROW_LANE - NSA_DK, -slope * t, tab)
        o_ref[0, r, :, NSA_DK:NSA_DKP] = extra.astype(o_ref.dtype)


def nsa_q_prep(q_raw, q_norm, *, tq, seq, pos0=0):
    rows = q_raw.shape[0]
    return pl.pallas_call(
        functools.partial(_q_prep_kernel, seq=seq, pos0=pos0),
        grid=(rows // tq, NSA_KV),
        in_specs=[pl.BlockSpec((tq, NSA_REP * NSA_DK), lambda i, g: (i, g)),
                  pl.BlockSpec((1, NSA_DK), lambda i, g: (0, 0)),
                  pl.BlockSpec((1, NSA_REP, NSA_DKP - NSA_DK), lambda i, g: (g, 0, 0))],
        out_specs=pl.BlockSpec((1, NSA_REP, tq, NSA_DKP), lambda i, g: (g, 0, i, 0)),
        out_shape=jax.ShapeDtypeStruct((NSA_KV, NSA_REP, rows, NSA_DKP), jnp.bfloat16),
        compiler_params=_cparams("parallel", "parallel"),
        name="nsa_q_prep",
    )(q_raw, q_norm.reshape(1, NSA_DK), _slope_table())


def _k_prep_kernel(k_ref, w_ref, on_ref, oh_ref, *, seq, onehot):
    tk = k_ref.shape[0]
    pos = (pl.program_id(0) * tk) % seq + lax.broadcasted_iota(jnp.int32, (tk, NSA_DKP - NSA_DK), 0)
    lane = lax.broadcasted_iota(jnp.int32, (tk, NSA_DKP - NSA_DK), 1)
    blk, off = pos // SLC_BLOCK, pos % SLC_BLOCK
    extra = jnp.where((lane == 0) | (lane == 2), blk, jnp.where((lane == 1) | (lane == 3), off, 0))
    if onehot:
        extra = jnp.where((lane >= 4) & (lane < 4 + _SEL_LANES) & (lane - 4 == blk), 1, extra)
    extra = jnp.where(lane == _ROW_LANE - NSA_DK, 1, extra)
    extra = extra.astype(jnp.float32)
    for g in range(NSA_KV):
        x = k_ref[:, g * NSA_DK:(g + 1) * NSA_DK]
        x = x * lax.rsqrt(jnp.mean(x * x, axis=-1, keepdims=True) + EPS) * w_ref[...]
        on_ref[:, g * NSA_DK:(g + 1) * NSA_DK] = x
        oh_ref[g, :, 0:NSA_DK] = x.astype(oh_ref.dtype)
        oh_ref[g, :, NSA_DK:NSA_DKP] = extra.astype(oh_ref.dtype)


def nsa_k_prep(k_raw, k_norm, *, seq, tk, onehot):
    rows = k_raw.shape[0]
    return pl.pallas_call(
        functools.partial(_k_prep_kernel, seq=seq, onehot=onehot),
        grid=(rows // tk,),
        in_specs=[pl.BlockSpec((tk, NSA_KW), lambda i: (i, 0)), pl.BlockSpec((1, NSA_DK), lambda i: (0, 0))],
        out_specs=[pl.BlockSpec((tk, NSA_KW), lambda i: (i, 0)),
                   pl.BlockSpec((NSA_KV, tk, NSA_DKP), lambda i: (0, i, 0))],
        out_shape=[jax.ShapeDtypeStruct((rows, NSA_KW), jnp.float32),
                   jax.ShapeDtypeStruct((NSA_KV, rows, NSA_DKP), jnp.bfloat16)],
        compiler_params=_cparams("parallel"),
        name="nsa_k_prep",
    )(k_raw, k_norm.reshape(1, NSA_DK))


def _v_prep_kernel(v_ref, o_ref):
    tk = v_ref.shape[0]
    ones_col = jnp.where(lax.broadcasted_iota(jnp.int32, (tk, NSA_DVP - NSA_DV), 1) == 0, 1.0, 0.0)
    for g in range(NSA_KV):
        o_ref[g, :, 0:NSA_DV] = v_ref[:, g * NSA_DV:(g + 1) * NSA_DV].astype(o_ref.dtype)
        o_ref[g, :, NSA_DV:NSA_DVP] = ones_col.astype(o_ref.dtype)


def nsa_v_prep(v_raw, *, tk):
    rows = v_raw.shape[0]
    return pl.pallas_call(
        _v_prep_kernel,
        grid=(rows // tk,),
        in_specs=[pl.BlockSpec((tk, NSA_VW), lambda i: (i, 0))],
        out_specs=pl.BlockSpec((NSA_KV, tk, NSA_DVP), lambda i: (0, i, 0)),
        out_shape=jax.ShapeDtypeStruct((NSA_KV, rows, NSA_DVP), jnp.bfloat16),
        compiler_params=_cparams("parallel"),
        name="nsa_v_prep",
    )(v_raw)


def _compress_partials_kernel(tab_ref, *refs, d, pages, transposed):
    x_refs, (pe_ref, w1_ref, a_ref, b_ref) = refs[:pages], refs[pages:]
    seg = PAGE_ROWS // CMP_STRIDE
    acc_a = acc_b = None
    if transposed:
        rows_of = [[r[0, g].T.reshape(seg, CMP_STRIDE, d) for r in x_refs] for g in range(NSA_KV)]
        take = lambda ri, l, g: rows_of[g][ri][:, l, :]
    elif len(x_refs[0].shape) == 4:
        take = lambda ri, l, g: x_refs[ri][:, l, g, :]
    else:
        take = lambda ri, l, g: x_refs[ri][:, l, g * d:(g + 1) * d]

    for l in range(CMP_STRIDE):
        xg = jnp.concatenate([take(ri, l, g) for g in range(NSA_KV) for ri in range(pages)], axis=0)
        ta = jnp.dot((xg + pe_ref[l:l + 1, :]).astype(jnp.bfloat16), w1_ref[l], preferred_element_type=jnp.float32)
        tb = jnp.dot((xg + pe_ref[CMP_STRIDE + l:CMP_STRIDE + l + 1, :]).astype(jnp.bfloat16),
                     w1_ref[CMP_STRIDE + l], preferred_element_type=jnp.float32)
        acc_a = ta if acc_a is None else acc_a + ta
        acc_b = tb if acc_b is None else acc_b + tb
    n = pages * seg
    for g in range(NSA_KV):
        a_ref[0, g] = acc_a[g * n:(g + 1) * n]
        b_ref[0, g] = acc_b[g * n:(g + 1) * n]


def nsa_compress_partials(pool3, table, pe, w1, *, pages, transposed=False):
    d = pe.shape[1]
    batch, npg = table.shape
    seg = PAGE_ROWS // CMP_STRIDE
    blk = (1,) + tuple(pool3.shape[1:]) if transposed else (seg, CMP_STRIDE) + tuple(pool3.shape[2:])
    x_spec = lambda k: pl.BlockSpec(blk, lambda b, i, tab: (tab[b, i * pages + k],) + (0,) * (len(blk) - 1))
    out_spec = pl.BlockSpec((1, NSA_KV, pages * seg, CMP_HIDDEN), lambda b, i, tab: (b, 0, i, 0))
    out_sds = jax.ShapeDtypeStruct((batch, NSA_KV, npg * seg, CMP_HIDDEN), jnp.float32)
    return pl.pallas_call(
        functools.partial(_compress_partials_kernel, d=d, pages=pages, transposed=transposed),
        grid_spec=pltpu.PrefetchScalarGridSpec(
            num_scalar_prefetch=1,
            grid=(batch, npg // pages),
            in_specs=[x_spec(k) for k in range(pages)]
            + [pl.BlockSpec((CMP_BLOCK, d), lambda b, i, tab: (0, 0)),
               pl.BlockSpec((CMP_BLOCK, d, CMP_HIDDEN), lambda b, i, tab: (0, 0, 0))],
            out_specs=[out_spec, out_spec]),
        out_shape=[out_sds, out_sds],
        compiler_params=_cparams("parallel", "arbitrary"),
        name="nsa_compress_partials",
    )(table, *([pool3] * pages), pe, w1.astype(jnp.bfloat16))


def _cmp_extra_lanes(nrows, width):
    n = lax.broadcasted_iota(jnp.int32, (nrows, width), 0)
    lane = lax.broadcasted_iota(jnp.int32, (nrows, width), 1)
    ec = n * CMP_STRIDE + CMP_BLOCK - 1
    blk, off = ec // SLC_BLOCK, ec % SLC_BLOCK
    pl_ = lane - _POS_LANE
    return jnp.where((pl_ == 0) | (pl_ == 2), blk, jnp.where((pl_ == 1) | (pl_ == 3), off, 0)).astype(jnp.float32)


def _compress_finish_kernel(a_ref, b_ref, bn_ref, w2_ref, kn_ref, o_ref, *, is_key, has_new, alibi_lanes):
    ns = a_ref.shape[2]
    row = lax.broadcasted_iota(jnp.int32, (ns, 1), 0)
    for g in range(NSA_KV):
        nxt = pltpu.roll(b_ref[0, g], ns - 1, 0)
        if has_new:
            nxt = jnp.where(row == ns - 1, bn_ref[0, g, 0:1, :], nxt)
        hid_pre = a_ref[0, g] + nxt
        hid = hid_pre * (1.0 / (1.0 + jnp.exp(-hid_pre)))
        out = jnp.dot(hid.astype(jnp.bfloat16), w2_ref[...], preferred_element_type=jnp.float32)
        if is_key:
            ms = jnp.sum(out * out, axis=-1, keepdims=True) * (1.0 / NSA_DK)
            out = out * lax.rsqrt(ms + EPS) * kn_ref[...]
            if alibi_lanes:
                out = out + _cmp_extra_lanes(ns, out.shape[1])
        o_ref[0, g] = out.astype(o_ref.dtype)


def nsa_compress_finish(part_a, part_b, part_b_new, w2, kn, *, is_key, alibi_lanes):
    batch, _, ns, _ = part_a.shape
    d = w2.shape[1]
    dp = NSA_DKP if is_key else d
    has_new = part_b_new is not None
    if not has_new:
        part_b_new = jnp.zeros((batch, NSA_KV, 8, CMP_HIDDEN), jnp.float32)
    w2p = jnp.pad(w2, ((0, 0), (0, dp - d))).astype(jnp.bfloat16)
    knp = jnp.pad(kn, (0, dp - d)).reshape(1, dp)
    blk = lambda n: pl.BlockSpec((1, NSA_KV, n, CMP_HIDDEN), lambda b: (b, 0, 0, 0))
    return pl.pallas_call(
        functools.partial(_compress_finish_kernel, is_key=is_key, has_new=has_new, alibi_lanes=alibi_lanes),
        grid=(batch,),
        in_specs=[blk(ns), blk(ns), blk(part_b_new.shape[2]),
                  pl.BlockSpec((CMP_HIDDEN, dp), lambda b: (0, 0)),
                  pl.BlockSpec((1, dp), lambda b: (0, 0))],
        out_specs=pl.BlockSpec((1, NSA_KV, ns, dp), lambda b: (b, 0, 0, 0)),
        out_shape=jax.ShapeDtypeStruct((batch, NSA_KV, ns, dp), jnp.bfloat16),
        compiler_params=_cparams("parallel"),
        name="nsa_compress_finish_k" if is_key else "nsa_compress_finish_v",
    )(part_a, part_b, part_b_new, w2p, knp)


def _split_dot(x, m_bf16):
    hi, lo = _split_bf16(x)
    f = functools.partial(jnp.dot, preferred_element_type=jnp.float32)
    return f(hi, m_bf16) + f(lo, m_bf16)


def _topn_mask_t(imp_t, t_row, nsb, topn):
    jb = lax.broadcasted_iota(jnp.int32, imp_t.shape, 0)
    cur = t_row // SLC_BLOCK
    forced = (jb == 0) | (jb == cur) | (jb == cur - 1)
    valid = (jb * SLC_BLOCK <= t_row) & (jb < nsb)
    impm = jnp.where(forced, jnp.inf, jnp.where(valid, imp_t, -jnp.inf))
    rank = jnp.zeros(imp_t.shape, jnp.float32)
    for k in range(nsb):
        row = impm[k:k + 1, :]
        beats = (row > impm) | ((row == impm) & (k < jb))
        rank = rank + jnp.where(beats, 1.0, 0.0)
    return (rank < topn) & (jb < nsb)


def _cmp_and_select(q, kc, vc, t0, tq, seq):
    rep = NSA_REP
    m_rows = rep * tq
    nc = seq // CMP_STRIDE - CMP_RATIO + 1
    nseg = kc.shape[0]
    nsb = seq // SLC_BLOCK
    t_col = t0 + lax.broadcasted_iota(jnp.int32, (rep, tq, 1), 1).reshape(m_rows, 1)
    t_row = t0 + lax.broadcasted_iota(jnp.int32, (1, tq), 1)
    n_row = lax.broadcasted_iota(jnp.int32, (1, nseg), 1)
    cvalid = (n_row * CMP_STRIDE + CMP_BLOCK - 1 <= t_col) & (n_row < nc)
    s = jnp.where(cvalid, _dot_nt(q, kc), _NEG)
    p = jnp.where(cvalid, jnp.exp(s - jnp.max(s, axis=1, keepdims=True)), 0.0)
    p = p / jnp.maximum(jnp.sum(p, axis=1, keepdims=True), 1e-30)
    o_c = jnp.dot(p.astype(jnp.bfloat16), vc, preferred_element_type=jnp.float32)
    psum = jnp.sum(p.reshape(rep, tq, nseg), axis=0)
    cj = lax.broadcasted_iota(jnp.int32, (_SEL_LANES, nseg), 0) * SLC_BLOCK
    ci = lax.broadcasted_iota(jnp.int32, (_SEL_LANES, nseg), 1) * CMP_STRIDE
    ov = jnp.maximum(jnp.minimum(ci + CMP_BLOCK, cj + SLC_BLOCK) - jnp.maximum(ci, cj), 0)
    c2s_t = (ov.astype(jnp.float32) * (1.0 / CMP_STRIDE)).astype(jnp.bfloat16)
    p_hi, p_lo = _split_bf16(psum)
    imp_t = _dot_nt(c2s_t, p_hi) + _dot_nt(c2s_t, p_lo)
    sel_t = _topn_mask_t(imp_t, t_row, nsb, min(SLC_TOPN, nsb))
    pen_t = jnp.where(sel_t, 0.0, -_MASK_BIG).astype(jnp.bfloat16)
    pi = lax.broadcasted_iota(jnp.int32, (_SEL_LANES, NSA_DKP), 0)
    pj = lax.broadcasted_iota(jnp.int32, (_SEL_LANES, NSA_DKP), 1)
    place = jnp.where(pj == pi + _SEL_LANE, 1.0, 0.0).astype(jnp.bfloat16)
    q_add = _dot_tn(pen_t, place).astype(jnp.bfloat16)
    q_slc = (q.reshape(rep, tq, NSA_DKP) + q_add[None]).reshape(m_rows, NSA_DKP)
    return o_c, q_slc


def _gate_combine(gl_ref, z_ref, o_ref, o_c, o_s, o_w, tq):
    gates = 1.0 / (1.0 + jnp.exp(-gl_ref[...]))
    for r in range(NSA_REP):
        rs = slice(r * tq, (r + 1) * tq)
        o = (gates[:, 3 * r:3 * r + 1] * o_c[rs] + gates[:, 3 * r + 1:3 * r + 2] * o_s[rs]
             + gates[:, 3 * r + 2:3 * r + 3] * o_w[rs])
        z = z_ref[:, r * NSA_DV:(r + 1) * NSA_DV]
        o_ref[:, r * NSA_DV:(r + 1) * NSA_DV] = (o * (z * (1.0 / (1.0 + jnp.exp(-z))))).astype(o_ref.dtype)


def _nsa_prompt_fast_kernel(q_ref, kc_ref, vc_ref, ks_ref, vs_ref, kw_ref, vw_ref, gl_ref, z_ref, o_ref,
                            *, tq, tk, seq):
    t0 = pl.program_id(2) * tq
    m_rows = NSA_REP * tq
    q = q_ref[0].reshape(m_rows, NSA_DKP)
    t_col = t0 + lax.broadcasted_iota(jnp.int32, (NSA_REP, tq, 1), 1).reshape(m_rows, 1)
    o_c, q_slc = _cmp_and_select(q, kc_ref[0, 0], vc_ref[0, 0], t0, tq, seq)
    kpos = lax.broadcasted_iota(jnp.int32, (1, tk), 1)
    kt_hi = (t0 + tq + tk - 1) // tk
    zero = jnp.zeros((m_rows, NSA_DVP), jnp.float32)

    def pv(p, v_ref, k0):
        return jnp.dot(p.astype(jnp.bfloat16), v_ref[0, pl.ds(k0, tk), :], preferred_element_type=jnp.float32)

    def slc_pair(j, acc):
        outs = []
        for h in range(2):
            kt = 2 * j + h
            k0 = pl.multiple_of(jnp.minimum(kt, kt_hi - 1) * tk, tk)
            s = _dot_nt(q_slc, ks_ref[0, pl.ds(k0, tk), :])
            outs.append((s, k0, ((k0 + kpos) <= t_col) & (kt < kt_hi)))
        ps = [jnp.where(ok, jnp.exp(s), 0.0) for s, _, ok in outs]
        return acc + pv(ps[0], vs_ref, outs[0][1]) + pv(ps[1], vs_ref, outs[1][1])

    acc = lax.fori_loop(0, (kt_hi + 1) // 2, slc_pair, zero)
    o_s = acc[:, 0:NSA_DV] / acc[:, NSA_DV:NSA_DV + 1]

    def win_tile(kt):
        k0 = pl.multiple_of(kt * tk, tk)
        dw = t_col - (k0 + kpos)
        return _dot_nt(q, kw_ref[0, pl.ds(k0, tk), :]), k0, (dw >= 0) & (dw < WINDOW)

    def win_body(kt, acc):
        s, k0, ok = win_tile(kt)
        return acc + pv(jnp.where(ok, jnp.exp(s), 0.0), vw_ref, k0)

    kt_lo = jnp.maximum(t0 - (WINDOW - 1), 0) // tk
    acc = lax.fori_loop(kt_lo, kt_hi - 2, win_body, zero)
    ta = win_tile(jnp.maximum(kt_hi - 2, 0))
    tb_ = win_tile(kt_hi - 1)
    pa = jnp.where(ta[2] & (kt_hi >= 2), jnp.exp(ta[0]), 0.0)
    pb = jnp.where(tb_[2], jnp.exp(tb_[0]), 0.0)
    acc = acc + pv(pa, vw_ref, ta[1]) + pv(pb, vw_ref, tb_[1])
    o_w = acc[:, 0:NSA_DV] / acc[:, NSA_DV:NSA_DV + 1]
    _gate_combine(gl_ref, z_ref, o_ref, o_c, o_s, o_w, tq)


def _flash_step(q, k, v, mask, carry):
    m, l, acc = carry
    s = _dot_nt(q, k)
    s = jnp.where(mask, s, _NEG)
    m_new = jnp.maximum(m, jnp.max(s, axis=1, keepdims=True))
    alpha = jnp.exp(m - m_new)
    p = jnp.exp(s - m_new)
    l = alpha * l + jnp.sum(p, axis=1, keepdims=True)
    acc = alpha * acc + jnp.dot(p.astype(jnp.bfloat16), v, preferred_element_type=jnp.float32)
    return m_new, l, acc


def _nsa_prompt_kernel(q_ref, kc_ref, vc_ref, ks_ref, vs_ref, kw_ref, vw_ref, gl_ref, z_ref, o_ref,
                       *, tq, tk, seq):
    t0 = pl.program_id(2) * tq
    m_rows = NSA_REP * tq
    q = q_ref[0].reshape(m_rows, NSA_DKP)
    t_col = t0 + lax.broadcasted_iota(jnp.int32, (NSA_REP, tq, 1), 1).reshape(m_rows, 1)
    o_c, q_slc = _cmp_and_select(q, kc_ref[0, 0], vc_ref[0, 0], t0, tq, seq)

    kpos = lax.broadcasted_iota(jnp.int32, (1, tk), 1)
    init = (jnp.full((m_rows, 1), _NEG, jnp.float32), jnp.zeros((m_rows, 1), jnp.float32),
            jnp.zeros((m_rows, NSA_DV), jnp.float32))

    def slc_body(kt, carry):
        k0 = pl.multiple_of(kt * tk, tk)
        mask = (k0 + kpos) <= t_col
        return _flash_step(q_slc, ks_ref[0, pl.ds(k0, tk), :], vs_ref[0, pl.ds(k0, tk), :], mask, carry)

    _, l_s, acc_s = lax.fori_loop(0, (t0 + tq + tk - 1) // tk, slc_body, init)

    def win_body(kt, carry):
        k0 = pl.multiple_of(kt * tk, tk)
        dw = t_col - (k0 + kpos)
        mask = (dw >= 0) & (dw < WINDOW)
        return _flash_step(q, kw_ref[0, pl.ds(k0, tk), :], vw_ref[0, pl.ds(k0, tk), :], mask, carry)

    kt_lo = jnp.maximum(t0 - (WINDOW - 1), 0) // tk
    _, l_w, acc_w = lax.fori_loop(kt_lo, (t0 + tq + tk - 1) // tk, win_body, init)
    _gate_combine(gl_ref, z_ref, o_ref, o_c, acc_s / l_s, acc_w / l_w, tq)


def nsa_prompt_attention(q_hm, kcmp, vcmp, ks_hm, vs_hm, kw_hm, vw_hm, gl, z, *, batch, tq, tk, fast):
    rows = z.shape[0]
    seq = rows // batch
    nq = seq // tq
    nseg = kcmp.shape[2]
    row_blk = lambda b, g, i: (b * nq + i, g)
    kv_blk = lambda b, g, i: (g, b, 0)
    dvb = NSA_DVP if fast else NSA_DV
    return pl.pallas_call(
        functools.partial(_nsa_prompt_fast_kernel if fast else _nsa_prompt_kernel, tq=tq, tk=tk, seq=seq),
        grid=(batch, NSA_KV, nq),
        in_specs=[pl.BlockSpec((1, NSA_REP, tq, NSA_DKP), lambda b, g, i: (g, 0, b * nq + i, 0)),
                  pl.BlockSpec((1, 1, nseg, NSA_DKP), lambda b, g, i: (b, g, 0, 0)),
                  pl.BlockSpec((1, 1, nseg, NSA_DV), lambda b, g, i: (b, g, 0, 0)),
                  pl.BlockSpec((1, seq, NSA_DKP), kv_blk),
                  pl.BlockSpec((1, seq, dvb), kv_blk),
                  pl.BlockSpec((1, seq, NSA_DKP), kv_blk),
                  pl.BlockSpec((1, seq, dvb), kv_blk),
                  pl.BlockSpec((tq, LANE), row_blk),
                  pl.BlockSpec((tq, NSA_REP * NSA_DV), row_blk)],
        out_specs=pl.BlockSpec((tq, NSA_REP * NSA_DV), row_blk),
        out_shape=jax.ShapeDtypeStruct((rows, NSA_O), jnp.bfloat16),
        compiler_params=_cparams("parallel", "parallel", "arbitrary"),
        name="nsa_prompt_attention_fast" if fast else "nsa_prompt_attention",
    )(q_hm, kcmp, vcmp, ks_hm, vs_hm, kw_hm, vw_hm, gl, z)


def _nsa_in_proj(x, norm_w, w_in):
    B, L, D = x.shape
    x2 = x.reshape(B * L, D)
    h = rmsnorm_bf16(x2, norm_w, tm=min(256, B * L))
    offs = np.cumsum((0,) + NSA_SPLITS)
    tm = _row_tile(B * L)

    wt = w_in.T

    def seg(i, tn, name):
        return matmul_wt(h, wt, tm=tm, tn=tn, n_off=int(offs[i]), n=int(offs[i + 1] - offs[i]), name=name)

    q = seg(0, _WT_TN, "nsa_q_proj")
    kv = matmul_wt(h, wt, tm=tm, tn=256, n_off=int(offs[1]), n=int(offs[7] - offs[1]), name="nsa_kv_proj")
    kc, vc, ks, vs, kw, vw = (kv[:, int(offs[i] - offs[1]):int(offs[i + 1] - offs[1])] for i in range(1, 7))
    wg = wt[offs[7]:offs[8]].reshape(NSA_KV, 3 * NSA_REP, D)
    wg = jnp.pad(wg, ((0, 0), (0, LANE - 3 * NSA_REP), (0, 0))).reshape(NSA_KV * LANE, D)
    gl = matmul_wt(h, wg, tm=tm, tn=NSA_KV * LANE, n_off=0, n=NSA_KV * LANE, name="nsa_gate_proj")
    z = matmul_wt(h, wt[offs[8]:], tm=tm, tn=_WT_TN, n_off=0, n=NSA_O, name="nsa_z_proj")
    return x2, q, kc, vc, ks, vs, kw, vw, gl, z


def _nsa_prompt(x, norm_w, w_in, q_norm, kn_c, kn_s, kn_w, pe_k, w1_k, w2_k, pe_v, w1_v, w2_v, w_out):
    B, T, D = x.shape
    x2, q, kc_r, vc_r, ks_raw, vs_r, kw_raw, vw_r, gl, z = _nsa_in_proj(x, norm_w, w_in)
    tp = min(512, T)
    q_hm = nsa_q_prep(q, q_norm, tq=tp, seq=T)
    ks_r, ks_hm = nsa_k_prep(ks_raw, kn_s, seq=T, tk=tp, onehot=True)
    kw_r, kw_hm = nsa_k_prep(kw_raw, kn_w, seq=T, tk=tp, onehot=False)
    vs_hm = nsa_v_prep(vs_r, tk=tp)
    vw_hm = nsa_v_prep(vw_r, tk=tp)
    npg = T // PAGE_ROWS
    table = (jnp.arange(B, dtype=jnp.int32)[:, None] * npg + jnp.arange(npg, dtype=jnp.int32)[None, :])
    seg3 = lambda a: a.reshape(B * T // CMP_STRIDE, CMP_STRIDE, a.shape[1])
    ka, kb = nsa_compress_partials(seg3(kc_r), table, pe_k, w1_k, pages=min(_CMP_PAGES, npg))
    va, vb = nsa_compress_partials(seg3(vc_r), table, pe_v, w1_v, pages=min(_CMP_PAGES, npg))
    kcmp = nsa_compress_finish(ka, kb, None, w2_k, kn_c, is_key=True, alibi_lanes=True)
    vcmp = nsa_compress_finish(va, vb, None, w2_v, jnp.zeros((NSA_DV,), jnp.float32), is_key=False, alibi_lanes=False)
    bound = jnp.max(jnp.abs(q_norm)) * jnp.maximum(jnp.max(jnp.abs(kn_s)), jnp.max(jnp.abs(kn_w))) * (NSA_DK ** 0.5)
    attend = lambda fast: functools.partial(nsa_prompt_attention, batch=B, tq=_NSA_TQ, tk=_NSA_TK, fast=fast)
    og = lax.cond(bound < _FAST_SCORE_BOUND, attend(True), attend(False),
                  q_hm, kcmp, vcmp, ks_hm, vs_hm, kw_hm, vw_hm, gl, z)
    y = matmul(og, w_out.astype(jnp.bfloat16), tm=_row_tile(B * T), tn=1024, residual=x2, name="nsa_out_proj")
    wl = min(WINDOW, T)
    r4 = lambda a, d: a.reshape(B, T, NSA_KV, d)
    return y.reshape(B, T, D), (r4(kc_r, NSA_DK), r4(vc_r, NSA_DV), r4(ks_r, NSA_DK), r4(vs_r, NSA_DV),
                                r4(kw_r, NSA_DK)[:, T - wl:], r4(vw_r, NSA_DV)[:, T - wl:])


_SMP_PAGES = 8
_SEL_PAD = 384


def _online_update(m_ref, l_ref, acc_ref, g, s, v):
    m_old = m_ref[g]
    m_new = jnp.maximum(m_old, jnp.max(s, axis=1, keepdims=True))
    alpha = jnp.exp(m_old - m_new)
    p = jnp.exp(s - m_new)
    l_ref[g] = alpha * l_ref[g] + jnp.sum(p, axis=1, keepdims=True)
    acc_ref[g] = alpha * acc_ref[g] + jnp.dot(p.astype(jnp.bfloat16), v.astype(jnp.bfloat16),
                                              preferred_element_type=jnp.float32)
    m_ref[g] = m_new


def _nsa_sample_kernel(tab_ref, *refs, past, steps, pages, nsb):
    kp_refs = refs[:pages]
    vp_refs = refs[pages:2 * pages]
    (q_ref, kc_ref, vc_ref, kn_ref, vn_ref, kwc_ref, vwc_ref, kwn_ref, vwn_ref, gl_ref, z_ref,
     o_ref, m_scr, l_scr, acc_scr, sel_scr, oc_scr) = refs[2 * pages:]
    i = pl.program_id(1)
    s_tok = o_ref.shape[0]
    rows = NSA_REP * s_tok
    ncmp = kc_ref.shape[2]
    r_col = lax.broadcasted_iota(jnp.int32, (NSA_REP, s_tok, 1), 0).reshape(rows, 1)
    t_col = past + lax.broadcasted_iota(jnp.int32, (NSA_REP, s_tok, 1), 1).reshape(rows, 1)
    t_tok = past + lax.broadcasted_iota(jnp.int32, (s_tok, 1), 0)

    def slope_col(g):
        return jnp.exp2((r_col + (g * NSA_REP + 1)).astype(jnp.float32) * (-8.0 / NSA_HEADS))

    def q_of(g):
        return q_ref[0, g * rows:(g + 1) * rows, :]

    def rep_rows(x):
        return jnp.concatenate([x] * NSA_REP, axis=0)

    @pl.when(i == 0)
    def _first():
        n_row = lax.broadcasted_iota(jnp.int32, (1, ncmp), 1)
        cvalid = (n_row * CMP_STRIDE + CMP_BLOCK - 1) <= t_col
        ci = lax.broadcasted_iota(jnp.int32, (ncmp, _SEL_PAD), 0) * CMP_STRIDE
        cj = lax.broadcasted_iota(jnp.int32, (ncmp, _SEL_PAD), 1) * SLC_BLOCK
        ov = jnp.maximum(jnp.minimum(ci + CMP_BLOCK, cj + SLC_BLOCK) - jnp.maximum(ci, cj), 0)
        c2s = (ov.astype(jnp.float32) * (1.0 / CMP_STRIDE)).astype(jnp.bfloat16)
        jb = lax.broadcasted_iota(jnp.int32, (s_tok, _SEL_PAD), 1)
        cur = t_tok // SLC_BLOCK
        forced = (jb == 0) | (jb == cur) | (jb == cur - 1)
        valid = (jb * SLC_BLOCK <= t_tok) & (jb < nsb)
        for g in range(NSA_KV):
            s = jnp.where(cvalid, _dot_nt(q_of(g), kc_ref[0, g]), _NEG)
            p = jnp.where(cvalid, jnp.exp(s - jnp.max(s, axis=1, keepdims=True)), 0.0)
            p = p / jnp.maximum(jnp.sum(p, axis=1, keepdims=True), 1e-30)
            oc_scr[g] = jnp.dot(p.astype(jnp.bfloat16), vc_ref[0, g], preferred_element_type=jnp.float32)
            imp = _split_dot(jnp.sum(p.reshape(NSA_REP, s_tok, ncmp), axis=0), c2s)
            work = jnp.where(forced, jnp.inf, jnp.where(valid, imp, -jnp.inf))
            sel = jnp.zeros((s_tok, _SEL_PAD), jnp.float32)
            for _ in range(min(SLC_TOPN, nsb)):
                mx = jnp.max(work, axis=1, keepdims=True)
                first = jnp.min(jnp.where(work == mx, jb, _SEL_PAD), axis=1, keepdims=True)
                hit = jb == first
                sel = jnp.where(hit, 1.0, sel)
                work = jnp.where(hit, -jnp.inf, work)
            sel_scr[g] = sel
            m_scr[g] = jnp.full((rows, 1), _NEG, jnp.float32)
            l_scr[g] = jnp.zeros((rows, 1), jnp.float32)
            acc_scr[g] = jnp.zeros((rows, NSA_DV), jnp.float32)

    def fold(score_of, v_of, nk, page0, causal):
        pos = page0 * PAGE_ROWS + lax.broadcasted_iota(jnp.int32, (1, nk), 1)
        bj = lax.broadcasted_iota(jnp.int32, (_SEL_PAD, nk), 0)
        bl = lax.broadcasted_iota(jnp.int32, (_SEL_PAD, nk), 1)
        expand = jnp.where(bj == (page0 * PAGE_ROWS + bl) // SLC_BLOCK, 1.0, 0.0).astype(jnp.bfloat16)
        for g in range(NSA_KV):
            selk = jnp.dot(sel_scr[g].astype(jnp.bfloat16), expand, preferred_element_type=jnp.float32)
            ok = rep_rows(selk) > 0.5
            if causal:
                ok = ok & (pos <= t_col)
            s = score_of(g, q_of(g)[:, 0:NSA_DK])
            s = s - slope_col(g) * (t_col - pos).astype(jnp.float32)
            _online_update(m_scr, l_scr, acc_scr, g, jnp.where(ok, s, _NEG), v_of(g))

    fold(lambda g, qg: _dot(qg, jnp.concatenate([r[0, g] for r in kp_refs], axis=1)),
         lambda g: jnp.concatenate([r[0, :, g, :] for r in vp_refs], axis=0),
         pages * PAGE_ROWS, i * pages, causal=False)

    @pl.when(i == steps - 1)
    def _last():
        fold(lambda g, qg: _dot_nt(qg, kn_ref[0, :, g * NSA_DK:(g + 1) * NSA_DK]),
             lambda g: vn_ref[0, :, g * NSA_DV:(g + 1) * NSA_DV], PAGE_ROWS, past // PAGE_ROWS, causal=True)
        nwc = kwc_ref.shape[3]
        posw = jnp.concatenate([past - nwc + lax.broadcasted_iota(jnp.int32, (1, nwc), 1),
                                past + lax.broadcasted_iota(jnp.int32, (1, PAGE_ROWS), 1)], axis=1)
        dw = t_col - posw
        wok = (dw >= 0) & (dw < WINDOW) & (posw >= 0)
        gates = 1.0 / (1.0 + jnp.exp(-gl_ref[...]))
        for g in range(NSA_KV):
            ksl = slice(g * NSA_DK, (g + 1) * NSA_DK)
            vsl = slice(g * NSA_DV, (g + 1) * NSA_DV)
            qg = q_of(g)[:, 0:NSA_DK]
            s = jnp.concatenate([_dot(qg, kwc_ref[0, g]), _dot_nt(qg, kwn_ref[0, :, ksl])], axis=1)
            s = jnp.where(wok, s - slope_col(g) * dw.astype(jnp.float32), _NEG)
            p = jnp.where(wok, jnp.exp(s - jnp.max(s, axis=1, keepdims=True)), 0.0)
            p = (p / jnp.maximum(jnp.sum(p, axis=1, keepdims=True), 1e-30)).astype(jnp.bfloat16)
            o_w = (jnp.dot(p[:, 0:nwc], vwc_ref[0, :, vsl].astype(jnp.bfloat16), preferred_element_type=jnp.float32)
                   + jnp.dot(p[:, nwc:], vwn_ref[0, :, vsl].astype(jnp.bfloat16), preferred_element_type=jnp.float32))
            o_s = acc_scr[g] / l_scr[g]
            o_c = oc_scr[g]
            for r in range(NSA_REP):
                rs = slice(r * s_tok, (r + 1) * s_tok)
                c0 = g * LANE + 3 * r
                o = (gates[:, c0:c0 + 1] * o_c[rs] + gates[:, c0 + 1:c0 + 2] * o_s[rs]
                     + gates[:, c0 + 2:c0 + 3] * o_w[rs])
                hs = slice((g * NSA_REP + r) * NSA_DV, (g * NSA_REP + r + 1) * NSA_DV)
                z = z_ref[:, hs]
                o_ref[:, hs] = o * (z * (1.0 / (1.0 + jnp.exp(-z))))


def nsa_sample_attention(q_rows, kcmp, vcmp, pool_k, pool_v, page_table, k_new, v_new, kw_cache, vw_cache,
                         kw_new, vw_new, gl, z, *, s_tok):
    batch, npg = page_table.shape
    pages = _SMP_PAGES
    steps = npg // pages
    past = npg * PAGE_ROWS
    nsb = -(-(past + s_tok) // SLC_BLOCK)
    rows = NSA_REP * s_tok
    ncmp = kcmp.shape[2]
    nwc = kw_cache.shape[3]
    kpage_spec = lambda k: pl.BlockSpec((1, NSA_KV, NSA_DK, PAGE_ROWS), lambda b, i, tab: (tab[b, i * pages + k], 0, 0, 0))
    vpage_spec = lambda k: pl.BlockSpec((1, PAGE_ROWS, NSA_KV, NSA_DV), lambda b, i, tab: (tab[b, i * pages + k], 0, 0, 0))
    per_b = lambda shape: pl.BlockSpec((1,) + shape, lambda b, i, tab: (b,) + (0,) * len(shape))
    tok_blk = lambda w: pl.BlockSpec((s_tok, w), lambda b, i, tab: (b, 0))
    return pl.pallas_call(
        functools.partial(_nsa_sample_kernel, past=past, steps=steps, pages=pages, nsb=nsb),
        grid_spec=pltpu.PrefetchScalarGridSpec(
            num_scalar_prefetch=1,
            grid=(batch, steps),
            in_specs=[kpage_spec(k) for k in range(pages)] + [vpage_spec(k) for k in range(pages)]
            + [per_b((NSA_KV * rows, NSA_DKP)), per_b((NSA_KV, ncmp, NSA_DKP)), per_b((NSA_KV, ncmp, NSA_DV)),
               per_b((PAGE_ROWS, NSA_KW)), per_b((PAGE_ROWS, NSA_VW)),
               per_b((NSA_KV, NSA_DK, nwc)), per_b((nwc, NSA_VW)), per_b((PAGE_ROWS, NSA_KW)), per_b((PAGE_ROWS, NSA_VW)),
               tok_blk(NSA_KV * LANE), tok_blk(NSA_O)],
            out_specs=tok_blk(NSA_O),
            scratch_shapes=[pltpu.VMEM((NSA_KV, rows, 1), jnp.float32), pltpu.VMEM((NSA_KV, rows, 1), jnp.float32),
                            pltpu.VMEM((NSA_KV, rows, NSA_DV), jnp.float32),
                            pltpu.VMEM((NSA_KV, s_tok, _SEL_PAD), jnp.float32),
                            pltpu.VMEM((NSA_KV, rows, NSA_DV), jnp.float32)]),
        out_shape=jax.ShapeDtypeStruct((batch * s_tok, NSA_O), jnp.float32),
        compiler_params=_cparams("parallel", "arbitrary"),
        name="nsa_sample_attention",
    )(page_table, *([pool_k] * pages), *([pool_v] * pages), q_rows, kcmp, vcmp, k_new, v_new,
      kw_cache, vw_cache, kw_new, vw_new, gl, z)


def _nsa_sample(x, ck, cv, sk, sv, wk_buf, wv_buf, page_table, norm_w,
                w_in, q_norm, kn_c, kn_s, kn_w, pe_k, w1_k, w2_k, pe_v, w1_v, w2_v, w_out):
    B, S, D = x.shape
    x2, q, kc_r, vc_r, ks_raw, vs_r, kw_raw, vw_r, gl, z = _nsa_in_proj(x, norm_w, w_in)
    q_hm = nsa_q_prep(q, q_norm, tq=B * S, seq=S, pos0=page_table.shape[1] * PAGE_ROWS)
    q_rows = q_hm.reshape(NSA_KV, NSA_REP, B, S, NSA_DKP).transpose(2, 0, 1, 3, 4).reshape(B, NSA_KV * NSA_REP * S, NSA_DKP)
    ks_r, _ = nsa_k_prep(ks_raw, kn_s, seq=S, tk=B * S, onehot=False)
    kw_r, _ = nsa_k_prep(kw_raw, kn_w, seq=S, tk=B * S, onehot=False)
    as_page = lambda a: jnp.pad(a.reshape(B, S, -1), ((0, 0), (0, PAGE_ROWS - S), (0, 0)))
    seg3 = lambda a, w: a.reshape(-1, CMP_STRIDE, w)
    seg4 = lambda a: a.reshape((-1, CMP_STRIDE) + a.shape[2:])
    ident = jnp.arange(B, dtype=jnp.int32)[:, None]
    rows_minor = lambda a: jnp.transpose(a, (0, 2, 3, 1))
    ka, kb = nsa_compress_partials(rows_minor(ck), page_table, pe_k, w1_k, pages=_CMP_PAGES, transposed=True)
    va, vb = nsa_compress_partials(seg4(cv), page_table, pe_v, w1_v, pages=_CMP_PAGES)
    _, kb_new = nsa_compress_partials(seg3(as_page(kc_r), NSA_KW), ident, pe_k, w1_k, pages=1)
    _, vb_new = nsa_compress_partials(seg3(as_page(vc_r), NSA_VW), ident, pe_v, w1_v, pages=1)
    kcmp = nsa_compress_finish(ka, kb, kb_new, w2_k, kn_c, is_key=True, alibi_lanes=True)
    vcmp = nsa_compress_finish(va, vb, vb_new, w2_v, jnp.zeros((NSA_DV,), jnp.float32), is_key=False, alibi_lanes=False)
    wl = wk_buf.shape[1]
    o = nsa_sample_attention(q_rows, kcmp, vcmp, rows_minor(sk), sv, page_table, as_page(ks_r), as_page(vs_r), rows_minor(wk_buf),
                             wv_buf.reshape(B, wl, NSA_VW), as_page(kw_r), as_page(vw_r), gl, z, s_tok=S)
    y = matmul(o.astype(jnp.bfloat16), w_out.astype(jnp.bfloat16), tm=B * S, tn=1024, residual=x2, name="nsa_out_proj")
    r4 = lambda a, d: a.reshape(B, S, NSA_KV, d)
    kw_all = jnp.concatenate([wk_buf, r4(kw_r, NSA_DK)], axis=1)
    vw_all = jnp.concatenate([wv_buf, r4(vw_r, NSA_DV)], axis=1)
    return y.reshape(B, S, D), (r4(kc_r, NSA_DK), r4(vc_r, NSA_DV), r4(ks_r, NSA_DK), r4(vs_r, NSA_DV),
                                kw_all[:, S:], vw_all[:, S:])


def kernel(x_prompt, x_sample, state_delta, state_conv, cache_cmp_k, cache_cmp_v, cache_slc_k, cache_slc_v, cache_win_k, cache_win_v, page_table, norm_dn, w_in_dn, conv_w_dn, a_log_dn, dt_bias_dn, out_norm_dn, w_out_dn, norm_nsa, w_in_nsa, q_norm_nsa, k_norm_cmp, k_norm_slc, k_norm_win, cmp_pe_k, cmp_w1_k, cmp_w2_k, cmp_pe_v, cmp_w1_v, cmp_w2_v, w_out_nsa):
    xp, xs = x_prompt, x_sample
    B = xp.shape[0]
    dw = (norm_dn[0], w_in_dn[0], conv_w_dn[0], a_log_dn[0], dt_bias_dn[0], out_norm_dn[0], w_out_dn[0])
    buf0 = jnp.zeros((B, DN_CONV - 1, conv_w_dn.shape[-1]), xp.dtype)
    st0 = jnp.zeros((B, DN_HEADS, DN_DK, DN_DV), jnp.float32)
    xp, p_dn = _deltanet_mixer(xp, buf0, st0, *dw)
    xs, s_dn = _deltanet_mixer(xs, state_conv[0], state_delta[0], *dw)
    nw = (norm_nsa[0], w_in_nsa[0], q_norm_nsa[0], k_norm_cmp[0], k_norm_slc[0], k_norm_win[0],
          cmp_pe_k[0], cmp_w1_k[0], cmp_w2_k[0], cmp_pe_v[0], cmp_w1_v[0], cmp_w2_v[0], w_out_nsa[0])
    xp, p_nsa = _nsa_prompt(xp, *nw)
    xs, s_nsa = _nsa_sample(xs, cache_cmp_k[0], cache_cmp_v[0], cache_slc_k[0], cache_slc_v[0],
                            cache_win_k[0], cache_win_v[0], page_table, *nw)
    return ((xp, xs, p_dn[0][None], p_dn[1][None]) + tuple(t[None] for t in p_nsa)
            + (s_dn[0][None], s_dn[1][None]) + tuple(t[None] for t in s_nsa))
```

```python
import functools

import jax
import jax.numpy as jnp
import numpy as np
from jax import lax
from jax.experimental import pallas as pl
from jax.experimental.pallas import tpu as pltpu

D_MODEL = 4096
EPS = 1e-6

DN_HEADS = 32
DN_DK = 128
DN_DV = 128
DN_CONV = 4
DN_CHUNK = 64

NSA_HEADS = 32
NSA_KV = 4
NSA_REP = NSA_HEADS // NSA_KV
NSA_DK = 192
NSA_DV = 128
CMP_BLOCK = 32
CMP_STRIDE = 16
CMP_RATIO = CMP_BLOCK // CMP_STRIDE
CMP_HIDDEN = 256
SLC_BLOCK = 64
SLC_TOPN = 16
WINDOW = 512
NSA_Q = NSA_HEADS * NSA_DK
NSA_KW = NSA_KV * NSA_DK
NSA_VW = NSA_KV * NSA_DV
NSA_O = NSA_HEADS * NSA_DV
NSA_SPLITS = (NSA_Q, NSA_KW, NSA_VW, NSA_KW, NSA_VW, NSA_KW, NSA_VW, 3 * NSA_HEADS, NSA_O)

VMEM_LIMIT_BYTES = 56 * 1024 * 1024
LANE = 128
_WT_TN = 512


def _cparams(*sem):
    return pltpu.CompilerParams(dimension_semantics=sem, vmem_limit_bytes=VMEM_LIMIT_BYTES)


def _rmsnorm_kernel(x_ref, w_ref, o_ref):
    x = x_ref[...]
    ms = jnp.mean(x * x, axis=-1, keepdims=True)
    o_ref[...] = (x * lax.rsqrt(ms + EPS) * w_ref[...]).astype(o_ref.dtype)


def rmsnorm_bf16(x, w, *, tm):
    m, d = x.shape
    return pl.pallas_call(
        _rmsnorm_kernel,
        grid=(m // tm,),
        in_specs=[pl.BlockSpec((tm, d), lambda i: (i, 0)), pl.BlockSpec((1, d), lambda i: (0, 0))],
        out_specs=pl.BlockSpec((tm, d), lambda i: (i, 0)),
        out_shape=jax.ShapeDtypeStruct((m, d), jnp.bfloat16),
        compiler_params=_cparams("parallel"),
        name="rmsnorm_bf16",
    )(x, w.reshape(1, d))


def _mm_kernel(a_ref, b_ref, o_ref):
    o_ref[...] = jnp.dot(a_ref[...], b_ref[...], preferred_element_type=jnp.float32).astype(o_ref.dtype)


def _mm_res_kernel(a_ref, b_ref, r_ref, o_ref):
    acc = jnp.dot(a_ref[...], b_ref[...], preferred_element_type=jnp.float32)
    o_ref[...] = r_ref[...] + acc


def matmul(a, b, *, tm, tn, residual=None, out_dtype=jnp.float32, name="matmul"):
    m, k = a.shape
    _, n = b.shape
    assert m % tm == 0 and n % tn == 0
    in_specs = [pl.BlockSpec((tm, k), lambda i, j: (i, 0)), pl.BlockSpec((k, tn), lambda i, j: (0, j))]
    args = [a, b]
    kern = _mm_kernel
    if residual is not None:
        in_specs.append(pl.BlockSpec((tm, tn), lambda i, j: (i, j)))
        args.append(residual)
        kern = _mm_res_kernel
    return pl.pallas_call(
        kern,
        grid=(m // tm, n // tn),
        in_specs=in_specs,
        out_specs=pl.BlockSpec((tm, tn), lambda i, j: (i, j)),
        out_shape=jax.ShapeDtypeStruct((m, n), out_dtype),
        compiler_params=_cparams("parallel", "parallel"),
        name=name,
    )(*args)


def _mm_wt_kernel(a_ref, w_ref, o_ref):
    o_ref[...] = lax.dot_general(a_ref[...], w_ref[...].astype(jnp.bfloat16), (((1,), (1,)), ((), ())),
                                 preferred_element_type=jnp.float32)


def matmul_wt(a, wt, *, tm, tn, n_off, n, name):
    m, k = a.shape
    assert m % tm == 0 and n % tn == 0 and n_off % tn == 0
    j0 = n_off // tn
    return pl.pallas_call(
        _mm_wt_kernel,
        grid=(m // tm, n // tn),
        in_specs=[pl.BlockSpec((tm, k), lambda i, j: (i, 0)), pl.BlockSpec((tn, k), lambda i, j: (j0 + j, 0))],
        out_specs=pl.BlockSpec((tm, tn), lambda i, j: (i, j)),
        out_shape=jax.ShapeDtypeStruct((m, n), jnp.float32),
        compiler_params=_cparams("parallel", "parallel"),
        name=name,
    )(a, wt)


def _row_tile(m):
    return 1024 if m % 1024 == 0 else m


def _dot(a, b):
    return jnp.dot(a.astype(jnp.bfloat16), b.astype(jnp.bfloat16), preferred_element_type=jnp.float32)


def _dot_nt(a, b):
    return lax.dot_general(a.astype(jnp.bfloat16), b.astype(jnp.bfloat16), (((1,), (1,)), ((), ())),
                           preferred_element_type=jnp.float32)


def _dot_tn(a, b):
    return lax.dot_general(a.astype(jnp.bfloat16), b.astype(jnp.bfloat16), (((0,), (0,)), ((), ())),
                           preferred_element_type=jnp.float32)


def _split_bf16(x):
    hi = x.astype(jnp.bfloat16)
    lo = (x - hi.astype(jnp.float32)).astype(jnp.bfloat16)
    return hi, lo


def _dn_gate_kernel(h_ref, wt_ref, alog_ref, dtb_ref, o_ref, *, heads):
    r = lax.dot_general(wt_ref[...], h_ref[...], (((1,), (1,)), ((), ())),
                        preferred_element_type=jnp.float32)
    b = r[:heads]
    a = r[heads:] + dtb_ref[...]
    softplus = jnp.maximum(a, 0.0) + jnp.log(1.0 + jnp.exp(-jnp.abs(a)))
    o_ref[0:heads, :] = 1.0 / (1.0 + jnp.exp(-b))
    o_ref[heads:2 * heads, :] = -jnp.exp(alog_ref[...]) * softplus


def dn_gates(h, w_ba_t, a_log, dt_bias, *, tm):
    m, d = h.shape
    heads = a_log.shape[0]
    return pl.pallas_call(
        functools.partial(_dn_gate_kernel, heads=heads),
        grid=(m // tm,),
        in_specs=[pl.BlockSpec((tm, d), lambda i: (i, 0)),
                  pl.BlockSpec((2 * heads, d), lambda i: (0, 0)),
                  pl.BlockSpec((heads, 1), lambda i: (0, 0)),
                  pl.BlockSpec((heads, 1), lambda i: (0, 0))],
        out_specs=pl.BlockSpec((2 * heads, tm), lambda i: (0, i)),
        out_shape=jax.ShapeDtypeStruct((2 * heads, m), jnp.float32),
        compiler_params=_cparams("parallel"),
        name="dn_gates",
    )(h, w_ba_t, a_log.reshape(heads, 1), dt_bias.reshape(heads, 1))


_DN_TAIL = 8
_DN_TB = 256
_DN_TMIN = 2 * DN_CHUNK
_DN_HB = 8


def _dn_chunk_prep(q, k, g_row, b_row):
    c = q.shape[0]
    ii = lax.broadcasted_iota(jnp.int32, (c, c), 0)
    jj = lax.broadcasted_iota(jnp.int32, (c, c), 1)
    lower, strict, eye = ii >= jj, ii > jj, ii == jj
    g_b = jnp.broadcast_to(g_row, (c, c))
    b_b = jnp.broadcast_to(b_row, (c, c))
    b_col = jnp.sum(jnp.where(eye, b_b, 0.0), axis=1, keepdims=True)
    g_col = jnp.sum(jnp.where(eye, g_b, 0.0), axis=1, keepdims=True)
    gc_col = jnp.sum(jnp.where(lower, g_b, 0.0), axis=1, keepdims=True)
    gc_row = jnp.sum(jnp.where(ii <= jj, g_col, 0.0), axis=0, keepdims=True)
    g_last = jnp.sum(g_row, axis=1, keepdims=True)
    decay = jnp.exp(jnp.where(lower, gc_col - gc_row, -jnp.inf))
    kk = _dot_nt(k, k)
    qk = _dot_nt(q, k)
    lmat = jnp.where(strict, kk * decay, 0.0) * b_col
    attn = qk * decay
    return dict(x=-lmat, p=eye.astype(jnp.float32) - lmat, attn=attn, b_row=b_row,
                e_row=jnp.exp(gc_row), e_col=jnp.exp(gc_col), c_col=jnp.exp(g_last - gc_col),
                e_last=jnp.exp(g_last))


def _dn_inverse_levels(items, c):
    m = 2
    while m < c:
        for it in items:
            it["x"] = _dot(it["x"], it["x"])
        for it in items:
            it["p"] = it["p"] + _dot(it["p"], it["x"])
        m *= 2


def _dn_kernel(q_ref, k_ref, v_ref, z_ref, beta_ref, g_ref, cw_ref, cs_ref, s0_ref, onw_ref,
               o_ref, s_out_ref, xpad, s_scr, *, tb, hb, live):
    n = pl.program_id(2)
    c = DN_CHUNK
    d = DN_DK
    nc = live

    @pl.when(n == 0)
    def _init():
        xpad[:, 0:_DN_TAIL, :] = cs_ref[0]
        s_scr[...] = s0_ref[0]

    def conv(i, ref):
        xpad[i, _DN_TAIL:_DN_TAIL + tb, :] = ref[...]
        acc = None
        for j in range(DN_CONV):
            off = _DN_TAIL - (DN_CONV - 1) + j
            term = xpad[i, off:off + tb, :] * cw_ref[i, j:j + 1, :]
            acc = term if acc is None else acc + term
        xpad[i, 0:_DN_TAIL, :] = xpad[i, tb:tb + _DN_TAIL, :]
        return acc * (1.0 / (1.0 + jnp.exp(-acc)))

    qc, kc, vc = conv(0, q_ref), conv(1, k_ref), conv(2, v_ref)
    items = []
    for hh in range(hb):
        sl = slice(hh * d, (hh + 1) * d)
        q = qc[:, sl]
        k = kc[:, sl]
        q = q * (lax.rsqrt(jnp.sum(q * q, axis=-1, keepdims=True) + EPS) * (d ** -0.5))
        k = k * lax.rsqrt(jnp.sum(k * k, axis=-1, keepdims=True) + EPS)
        g_all = g_ref[hh]
        b_all = beta_ref[hh]
        for ci in range(nc):
            rs = slice(ci * c, (ci + 1) * c)
            it = _dn_chunk_prep(q[rs], k[rs], g_all[:, rs], b_all[:, rs])
            it.update(q=q[rs], k=k[rs], v=vc[rs, sl])
            items.append(it)
    _dn_inverse_levels(items, c)
    for it in items:
        inv_b = it["p"] * it["b_row"]
        it["u"] = _dot(inv_b, it["v"])
        it["w"] = _dot(inv_b * it["e_row"], it["k"])
    states = [s_scr[hh] for hh in range(hb)]
    outs = [[None] * nc for _ in range(hb)]
    for ci in range(nc):
        for hh in range(hb):
            it = items[hh * nc + ci]
            s = states[hh]
            v_new = it["u"] - _dot(it["w"], s)
            outs[hh][ci] = it["e_col"] * _dot(it["q"], s) + _dot(it["attn"], v_new)
            states[hh] = s * it["e_last"] + _dot_tn(it["k"] * it["c_col"], v_new)
    for hh in range(hb):
        sl = slice(hh * d, (hh + 1) * d)
        s_scr[hh] = states[hh]
        o = jnp.concatenate(outs[hh], axis=0) if nc > 1 else outs[hh][0]
        o = o * lax.rsqrt(jnp.mean(o * o, axis=-1, keepdims=True) + EPS) * onw_ref[...]
        z = z_ref[0:nc * c, sl]
        o_ref[0:nc * c, sl] = (o * (z * (1.0 / (1.0 + jnp.exp(-z))))).astype(o_ref.dtype)
        if nc * c < tb:
            o_ref[nc * c:tb, sl] = jnp.zeros((tb - nc * c, d), o_ref.dtype)

    @pl.when(n == pl.num_programs(2) - 1)
    def _fin():
        s_out_ref[0] = s_scr[...]


def dn_delta(proj, gates, conv_w, conv_state, s0, out_norm, *, batch, tb, hb, live):
    rows, width = proj.shape
    d = DN_DK
    heads = width // (4 * d)
    t = rows // batch
    nt = t // tb
    hg = heads // hb
    wd = hb * d
    row_map = lambda off: (lambda b, h, n: (b * nt + n, off * hg + h))
    gate_map = lambda off: (lambda b, h, n: (off * hg + h, 0, b * nt + n))
    return pl.pallas_call(
        functools.partial(_dn_kernel, tb=tb, hb=hb, live=live),
        grid=(batch, hg, nt),
        in_specs=[pl.BlockSpec((tb, wd), row_map(0)),
                  pl.BlockSpec((tb, wd), row_map(1)),
                  pl.BlockSpec((tb, wd), row_map(2)),
                  pl.BlockSpec((tb, wd), row_map(3)),
                  pl.BlockSpec((hb, 1, tb), gate_map(0)),
                  pl.BlockSpec((hb, 1, tb), gate_map(1)),
                  pl.BlockSpec((3, DN_CONV, wd), lambda b, h, n: (0, 0, h)),
                  pl.BlockSpec((1, 3, _DN_TAIL, wd), lambda b, h, n: (b, 0, 0, h)),
                  pl.BlockSpec((1, hb, d, d), lambda b, h, n: (b, h, 0, 0)),
                  pl.BlockSpec((1, d), lambda b, h, n: (0, 0))],
        out_specs=[pl.BlockSpec((tb, wd), lambda b, h, n: (b * nt + n, h)),
                   pl.BlockSpec((1, hb, d, d), lambda b, h, n: (b, h, 0, 0))],
        out_shape=[jax.ShapeDtypeStruct((rows, heads * d), jnp.bfloat16),
                   jax.ShapeDtypeStruct(s0.shape, jnp.float32)],
        scratch_shapes=[pltpu.VMEM((3, tb + _DN_TAIL, wd), jnp.float32),
                        pltpu.VMEM((hb, d, d), jnp.float32)],
        compiler_params=_cparams("parallel", "parallel", "arbitrary"),
        name="dn_delta",
    )(proj, proj, proj, proj, gates, gates, conv_w, conv_state, s0, out_norm.reshape(1, d))


def _deltanet_mixer(x, conv_buf, s0, norm_w, w_in, conv_w, a_log, dt_bias, out_norm, w_out):
    B, L, D = x.shape
    H = a_log.shape[0]
    cdim = conv_w.shape[1]
    hd = cdim // 3
    x2 = x.reshape(B * L, D)
    h = rmsnorm_bf16(x2, norm_w, tm=min(256, B * L))
    wt = w_in.T
    w_ba_t = wt[cdim + hd:].astype(jnp.bfloat16)
    proj = matmul_wt(h, wt, tm=_row_tile(B * L), tn=_WT_TN, n_off=0, n=cdim + hd, name="dn_in_proj")
    gates = dn_gates(h, w_ba_t, a_log, dt_bias, tm=_row_tile(B * L))
    new_buf = proj.reshape(B, L, -1)[:, L - (DN_CONV - 1):, :cdim]
    T = -(-L // _DN_TMIN) * _DN_TMIN
    if T != L:
        proj = jnp.pad(proj.reshape(B, L, -1), ((0, 0), (0, T - L), (0, 0))).reshape(B * T, -1)
        gates = jnp.pad(gates.reshape(2 * H, B, L), ((0, 0), (0, 0), (0, T - L))).reshape(2 * H, B * T)
    cw = conv_w.reshape(DN_CONV, 3, hd).transpose(1, 0, 2)
    cs = conv_buf.reshape(B, DN_CONV - 1, 3, hd).transpose(0, 2, 1, 3)
    cs = jnp.pad(cs, ((0, 0), (0, 0), (_DN_TAIL - (DN_CONV - 1), 0), (0, 0)))
    tb = min(_DN_TB, T)
    live = tb // DN_CHUNK if T == L else -(-L // DN_CHUNK)
    assert T == L or T == tb
    o, s_new = dn_delta(proj, gates.reshape(2 * H, 1, B * T), cw, cs, s0.astype(jnp.float32), out_norm,
                        batch=B, tb=tb, hb=_DN_HB, live=live)
    if T != L:
        o = o.reshape(B, T, -1)[:, :L].reshape(B * L, -1)
    y = matmul(o, w_out.astype(jnp.bfloat16), tm=_row_tile(B * L), tn=1024, residual=x2, name="dn_out_proj")
    return y.reshape(B, L, D), (s_new, new_buf)


def _alibi_slopes():
    hh = jnp.arange(1, NSA_HEADS + 1, dtype=jnp.float32)
    return jnp.exp2(-8.0 * hh / NSA_HEADS).reshape(NSA_KV, NSA_REP)


PAGE_ROWS = 128
NSA_DKP = 256
_POS_LANE = NSA_DK
_SEL_LANE = NSA_DK + 4
_SEL_LANES = 32
_ROW_LANE = _SEL_LANE + _SEL_LANES
NSA_DVP = 256
_FAST_SCORE_BOUND = 40.0
_MASK_BIG = 2.0 ** 100
_NEG = -1e30
_NSA_TQ = 128
_NSA_TK = 256
_CMP_PAGES = 16


def _slope_table():
    s = _alibi_slopes()
    hi = s.astype(jnp.bfloat16).astype(jnp.float32)
    lo = (s - hi).astype(jnp.bfloat16).astype(jnp.float32)
    tab = jnp.stack([64.0 * hi, hi, 64.0 * lo, lo], axis=-1)
    return jnp.pad(tab, ((0, 0), (0, 0), (0, NSA_DKP - NSA_DK - 4)))


def _q_prep_kernel(q_ref, w_ref, tab_ref, o_ref, *, seq, pos0):
    tq = q_ref.shape[0]
    npad = NSA_DKP - NSA_DK
    t = (pos0 + (pl.program_id(0) * tq) % seq + lax.broadcasted_iota(jnp.int32, (tq, npad), 0)).astype(jnp.float32)
    lane = lax.broadcasted_iota(jnp.int32, (tq, npad), 1)
    for r in range(NSA_REP):
        x = q_ref[:, r * NSA_DK:(r + 1) * NSA_DK]
        y = x * lax.rsqrt(jnp.mean(x * x, axis=-1, keepdims=True) + EPS) * (w_ref[...] * (NSA_DK ** -0.5))
        o_ref[0, r, :, 0:NSA_DK] = y.astype(o_ref.dtype)
        tab = jnp.broadcast_to(tab_ref[0, r:r + 1, :], (tq, npad))
        slope = tab[:, 1:2] + tab[:, 3:4]
        extra = jnp.where(lane == _ROW_LANE - NSA_DK, -slope * t, tab)
        o_ref[0, r, :, NSA_DK:NSA_DKP] = extra.astype(o_ref.dtype)


def nsa_q_prep(q_raw, q_norm, *, tq, seq, pos0=0):
    rows = q_raw.shape[0]
    return pl.pallas_call(
        functools.partial(_q_prep_kernel, seq=seq, pos0=pos0),
        grid=(rows // tq, NSA_KV),
        in_specs=[pl.BlockSpec((tq, NSA_REP * NSA_DK), lambda i, g: (i, g)),
                  pl.BlockSpec((1, NSA_DK), lambda i, g: (0, 0)),
                  pl.BlockSpec((1, NSA_REP, NSA_DKP - NSA_DK), lambda i, g: (g, 0, 0))],
        out_specs=pl.BlockSpec((1, NSA_REP, tq, NSA_DKP), lambda i, g: (g, 0, i, 0)),
        out_shape=jax.ShapeDtypeStruct((NSA_KV, NSA_REP, rows, NSA_DKP), jnp.bfloat16),
        compiler_params=_cparams("parallel", "parallel"),
        name="nsa_q_prep",
    )(q_raw, q_norm.reshape(1, NSA_DK), _slope_table())


def _k_prep_kernel(k_ref, w_ref, on_ref, oh_ref, *, seq, onehot):
    tk = k_ref.shape[0]
    pos = (pl.program_id(0) * tk) % seq + lax.broadcasted_iota(jnp.int32, (tk, NSA_DKP - NSA_DK), 0)
    lane = lax.broadcasted_iota(jnp.int32, (tk, NSA_DKP - NSA_DK), 1)
    blk, off = pos // SLC_BLOCK, pos % SLC_BLOCK
    extra = jnp.where((lane == 0) | (lane == 2), blk, jnp.where((lane == 1) | (lane == 3), off, 0))
    if onehot:
        extra = jnp.where((lane >= 4) & (lane < 4 + _SEL_LANES) & (lane - 4 == blk), 1, extra)
    extra = jnp.where(lane == _ROW_LANE - NSA_DK, 1, extra)
    extra = extra.astype(jnp.float32)
    for g in range(NSA_KV):
        x = k_ref[:, g * NSA_DK:(g + 1) * NSA_DK]
        x = x * lax.rsqrt(jnp.mean(x * x, axis=-1, keepdims=True) + EPS) * w_ref[...]
        on_ref[:, g * NSA_DK:(g + 1) * NSA_DK] = x
        oh_ref[g, :, 0:NSA_DK] = x.astype(oh_ref.dtype)
        oh_ref[g, :, NSA_DK:NSA_DKP] = extra.astype(oh_ref.dtype)


def nsa_k_prep(k_raw, k_norm, *, seq, tk, onehot):
    rows = k_raw.shape[0]
    return pl.pallas_call(
        functools.partial(_k_prep_kernel, seq=seq, onehot=onehot),
        grid=(rows // tk,),
        in_specs=[pl.BlockSpec((tk, NSA_KW), lambda i: (i, 0)), pl.BlockSpec((1, NSA_DK), lambda i: (0, 0))],
        out_specs=[pl.BlockSpec((tk, NSA_KW), lambda i: (i, 0)),
                   pl.BlockSpec((NSA_KV, tk, NSA_DKP), lambda i: (0, i, 0))],
        out_shape=[jax.ShapeDtypeStruct((rows, NSA_KW), jnp.float32),
                   jax.ShapeDtypeStruct((NSA_KV, rows, NSA_DKP), jnp.bfloat16)],
        compiler_params=_cparams("parallel"),
        name="nsa_k_prep",
    )(k_raw, k_norm.reshape(1, NSA_DK))


def _v_prep_kernel(v_ref, o_ref):
    tk = v_ref.shape[0]
    ones_col = jnp.where(lax.broadcasted_iota(jnp.int32, (tk, NSA_DVP - NSA_DV), 1) == 0, 1.0, 0.0)
    for g in range(NSA_KV):
        o_ref[g, :, 0:NSA_DV] = v_ref[:, g * NSA_DV:(g + 1) * NSA_DV].astype(o_ref.dtype)
        o_ref[g, :, NSA_DV:NSA_DVP] = ones_col.astype(o_ref.dtype)


def nsa_v_prep(v_raw, *, tk):
    rows = v_raw.shape[0]
    return pl.pallas_call(
        _v_prep_kernel,
        grid=(rows // tk,),
        in_specs=[pl.BlockSpec((tk, NSA_VW), lambda i: (i, 0))],
        out_specs=pl.BlockSpec((NSA_KV, tk, NSA_DVP), lambda i: (0, i, 0)),
        out_shape=jax.ShapeDtypeStruct((NSA_KV, rows, NSA_DVP), jnp.bfloat16),
        compiler_params=_cparams("parallel"),
        name="nsa_v_prep",
    )(v_raw)


def _compress_partials_kernel(tab_ref, *refs, d, pages, transposed):
    x_refs, (pe_ref, w1_ref, a_ref, b_ref) = refs[:pages], refs[pages:]
    seg = PAGE_ROWS // CMP_STRIDE
    acc_a = acc_b = None
    if transposed:
        rows_of = [[r[0, g].T.reshape(seg, CMP_STRIDE, d) for r in x_refs] for g in range(NSA_KV)]
        take = lambda ri, l, g: rows_of[g][ri][:, l, :]
    elif x_refs[0].shape[1] == CMP_STRIDE * NSA_KV:
        take = lambda ri, l, g: x_refs[ri][:, l * NSA_KV + g, :]
    else:
        take = lambda ri, l, g: x_refs[ri][:, l, g * d:(g + 1) * d]

    for l in range(CMP_STRIDE):
        xg = jnp.concatenate([take(ri, l, g) for g in range(NSA_KV) for ri in range(pages)], axis=0)
        ta = jnp.dot((xg + pe_ref[l:l + 1, :]).astype(jnp.bfloat16), w1_ref[l], preferred_element_type=jnp.float32)
        tb = jnp.dot((xg + pe_ref[CMP_STRIDE + l:CMP_STRIDE + l + 1, :]).astype(jnp.bfloat16),
                     w1_ref[CMP_STRIDE + l], preferred_element_type=jnp.float32)
        acc_a = ta if acc_a is None else acc_a + ta
        acc_b = tb if acc_b is None else acc_b + tb
    n = pages * seg
    for g in range(NSA_KV):
        a_ref[0, g] = acc_a[g * n:(g + 1) * n]
        b_ref[0, g] = acc_b[g * n:(g + 1) * n]


def nsa_compress_partials(pool3, table, pe, w1, *, pages, transposed=False):
    d = pe.shape[1]
    batch, npg = table.shape
    seg = PAGE_ROWS // CMP_STRIDE
    blk = ((1,) if transposed else (seg,)) + tuple(pool3.shape[1:])
    x_spec = lambda k: pl.BlockSpec(blk, lambda b, i, tab: (tab[b, i * pages + k],) + (0,) * (len(blk) - 1))
    out_spec = pl.BlockSpec((1, NSA_KV, pages * seg, CMP_HIDDEN), lambda b, i, tab: (b, 0, i, 0))
    out_sds = jax.ShapeDtypeStruct((batch, NSA_KV, npg * seg, CMP_HIDDEN), jnp.float32)
    return pl.pallas_call(
        functools.partial(_compress_partials_kernel, d=d, pages=pages, transposed=transposed),
        grid_spec=pltpu.PrefetchScalarGridSpec(
            num_scalar_prefetch=1,
            grid=(batch, npg // pages),
            in_specs=[x_spec(k) for k in range(pages)]
            + [pl.BlockSpec((CMP_BLOCK, d), lambda b, i, tab: (0, 0)),
               pl.BlockSpec((CMP_BLOCK, d, CMP_HIDDEN), lambda b, i, tab: (0, 0, 0))],
            out_specs=[out_spec, out_spec]),
        out_shape=[out_sds, out_sds],
        compiler_params=_cparams("parallel", "arbitrary"),
        name="nsa_compress_partials",
    )(table, *([pool3] * pages), pe, w1.astype(jnp.bfloat16))


def _cmp_extra_lanes(nrows, width):
    n = lax.broadcasted_iota(jnp.int32, (nrows, width), 0)
    lane = lax.broadcasted_iota(jnp.int32, (nrows, width), 1)
    ec = n * CMP_STRIDE + CMP_BLOCK - 1
    blk, off = ec // SLC_BLOCK, ec % SLC_BLOCK
    pl_ = lane - _POS_LANE
    return jnp.where((pl_ == 0) | (pl_ == 2), blk, jnp.where((pl_ == 1) | (pl_ == 3), off, 0)).astype(jnp.float32)


def _compress_finish_kernel(a_ref, b_ref, bn_ref, w2_ref, kn_ref, o_ref, *, is_key, has_new, alibi_lanes):
    ns = a_ref.shape[2]
    row = lax.broadcasted_iota(jnp.int32, (ns, 1), 0)
    for g in range(NSA_KV):
        nxt = pltpu.roll(b_ref[0, g], ns - 1, 0)
        if has_new:
            nxt = jnp.where(row == ns - 1, bn_ref[0, g, 0:1, :], nxt)
        hid_pre = a_ref[0, g] + nxt
        hid = hid_pre * (1.0 / (1.0 + jnp.exp(-hid_pre)))
        out = jnp.dot(hid.astype(jnp.bfloat16), w2_ref[...], preferred_element_type=jnp.float32)
        if is_key:
            ms = jnp.sum(out * out, axis=-1, keepdims=True) * (1.0 / NSA_DK)
            out = out * lax.rsqrt(ms + EPS) * kn_ref[...]
            if alibi_lanes:
                out = out + _cmp_extra_lanes(ns, out.shape[1])
        o_ref[0, g] = out.astype(o_ref.dtype)


def nsa_compress_finish(part_a, part_b, part_b_new, w2, kn, *, is_key, alibi_lanes):
    batch, _, ns, _ = part_a.shape
    d = w2.shape[1]
    dp = NSA_DKP if is_key else d
    has_new = part_b_new is not None
    if not has_new:
        part_b_new = jnp.zeros((batch, NSA_KV, 8, CMP_HIDDEN), jnp.float32)
    w2p = jnp.pad(w2, ((0, 0), (0, dp - d))).astype(jnp.bfloat16)
    knp = jnp.pad(kn, (0, dp - d)).reshape(1, dp)
    blk = lambda n: pl.BlockSpec((1, NSA_KV, n, CMP_HIDDEN), lambda b: (b, 0, 0, 0))
    return pl.pallas_call(
        functools.partial(_compress_finish_kernel, is_key=is_key, has_new=has_new, alibi_lanes=alibi_lanes),
        grid=(batch,),
        in_specs=[blk(ns), blk(ns), blk(part_b_new.shape[2]),
                  pl.BlockSpec((CMP_HIDDEN, dp), lambda b: (0, 0)),
                  pl.BlockSpec((1, dp), lambda b: (0, 0))],
        out_specs=pl.BlockSpec((1, NSA_KV, ns, dp), lambda b: (b, 0, 0, 0)),
        out_shape=jax.ShapeDtypeStruct((batch, NSA_KV, ns, dp), jnp.bfloat16),
        compiler_params=_cparams("parallel"),
        name="nsa_compress_finish_k" if is_key else "nsa_compress_finish_v",
    )(part_a, part_b, part_b_new, w2p, knp)


def _split_dot(x, m_bf16):
    hi, lo = _split_bf16(x)
    f = functools.partial(jnp.dot, preferred_element_type=jnp.float32)
    return f(hi, m_bf16) + f(lo, m_bf16)


def _topn_mask_t(imp_t, t_row, nsb, topn):
    jb = lax.broadcasted_iota(jnp.int32, imp_t.shape, 0)
    cur = t_row // SLC_BLOCK
    forced = (jb == 0) | (jb == cur) | (jb == cur - 1)
    valid = (jb * SLC_BLOCK <= t_row) & (jb < nsb)
    impm = jnp.where(forced, jnp.inf, jnp.where(valid, imp_t, -jnp.inf))
    rank = jnp.zeros(imp_t.shape, jnp.float32)
    for k in range(nsb):
        row = impm[k:k + 1, :]
        beats = (row > impm) | ((row == impm) & (k < jb))
        rank = rank + jnp.where(beats, 1.0, 0.0)
    return (rank < topn) & (jb < nsb)


def _cmp_and_select(q, kc, vc, t0, tq, seq):
    rep = NSA_REP
    m_rows = rep * tq
    nc = seq // CMP_STRIDE - CMP_RATIO + 1
    nseg = kc.shape[0]
    nsb = seq // SLC_BLOCK
    t_col = t0 + lax.broadcasted_iota(jnp.int32, (rep, tq, 1), 1).reshape(m_rows, 1)
    t_row = t0 + lax.broadcasted_iota(jnp.int32, (1, tq), 1)
    n_row = lax.broadcasted_iota(jnp.int32, (1, nseg), 1)
    cvalid = (n_row * CMP_STRIDE + CMP_BLOCK - 1 <= t_col) & (n_row < nc)
    s = jnp.where(cvalid, _dot_nt(q, kc), _NEG)
    p = jnp.where(cvalid, jnp.exp(s - jnp.max(s, axis=1, keepdims=True)), 0.0)
    p = p / jnp.maximum(jnp.sum(p, axis=1, keepdims=True), 1e-30)
    o_c = jnp.dot(p.astype(jnp.bfloat16), vc, preferred_element_type=jnp.float32)
    psum = jnp.sum(p.reshape(rep, tq, nseg), axis=0)
    cj = lax.broadcasted_iota(jnp.int32, (_SEL_LANES, nseg), 0) * SLC_BLOCK
    ci = lax.broadcasted_iota(jnp.int32, (_SEL_LANES, nseg), 1) * CMP_STRIDE
    ov = jnp.maximum(jnp.minimum(ci + CMP_BLOCK, cj + SLC_BLOCK) - jnp.maximum(ci, cj), 0)
    c2s_t = (ov.astype(jnp.float32) * (1.0 / CMP_STRIDE)).astype(jnp.bfloat16)
    p_hi, p_lo = _split_bf16(psum)
    imp_t = _dot_nt(c2s_t, p_hi) + _dot_nt(c2s_t, p_lo)
    sel_t = _topn_mask_t(imp_t, t_row, nsb, min(SLC_TOPN, nsb))
    pen_t = jnp.where(sel_t, 0.0, -_MASK_BIG).astype(jnp.bfloat16)
    pi = lax.broadcasted_iota(jnp.int32, (_SEL_LANES, NSA_DKP), 0)
    pj = lax.broadcasted_iota(jnp.int32, (_SEL_LANES, NSA_DKP), 1)
    place = jnp.where(pj == pi + _SEL_LANE, 1.0, 0.0).astype(jnp.bfloat16)
    q_add = _dot_tn(pen_t, place).astype(jnp.bfloat16)
    q_slc = (q.reshape(rep, tq, NSA_DKP) + q_add[None]).reshape(m_rows, NSA_DKP)
    return o_c, q_slc


def _gate_combine(gl_ref, z_ref, o_ref, o_c, o_s, o_w, tq):
    gates = 1.0 / (1.0 + jnp.exp(-gl_ref[...]))
    for r in range(NSA_REP):
        rs = slice(r * tq, (r + 1) * tq)
        o = (gates[:, 3 * r:3 * r + 1] * o_c[rs] + gates[:, 3 * r + 1:3 * r + 2] * o_s[rs]
             + gates[:, 3 * r + 2:3 * r + 3] * o_w[rs])
        z = z_ref[:, r * NSA_DV:(r + 1) * NSA_DV]
        o_ref[:, r * NSA_DV:(r + 1) * NSA_DV] = (o * (z * (1.0 / (1.0 + jnp.exp(-z))))).astype(o_ref.dtype)


def _nsa_prompt_fast_kernel(q_ref, kc_ref, vc_ref, ks_ref, vs_ref, kw_ref, vw_ref, gl_ref, z_ref, o_ref,
                            *, tq, tk, seq):
    t0 = pl.program_id(2) * tq
    m_rows = NSA_REP * tq
    q = q_ref[0].reshape(m_rows, NSA_DKP)
    t_col = t0 + lax.broadcasted_iota(jnp.int32, (NSA_REP, tq, 1), 1).reshape(m_rows, 1)
    o_c, q_slc = _cmp_and_select(q, kc_ref[0, 0], vc_ref[0, 0], t0, tq, seq)
    kpos = lax.broadcasted_iota(jnp.int32, (1, tk), 1)
    kt_hi = (t0 + tq + tk - 1) // tk
    zero = jnp.zeros((m_rows, NSA_DVP), jnp.float32)

    def pv(p, v_ref, k0):
        return jnp.dot(p.astype(jnp.bfloat16), v_ref[0, pl.ds(k0, tk), :], preferred_element_type=jnp.float32)

    def slc_pair(j, acc):
        outs = []
        for h in range(2):
            kt = 2 * j + h
            k0 = pl.multiple_of(jnp.minimum(kt, kt_hi - 1) * tk, tk)
            s = _dot_nt(q_slc, ks_ref[0, pl.ds(k0, tk), :])
            outs.append((s, k0, ((k0 + kpos) <= t_col) & (kt < kt_hi)))
        ps = [jnp.where(ok, jnp.exp(s), 0.0) for s, _, ok in outs]
        return acc + pv(ps[0], vs_ref, outs[0][1]) + pv(ps[1], vs_ref, outs[1][1])

    acc = lax.fori_loop(0, (kt_hi + 1) // 2, slc_pair, zero)
    o_s = acc[:, 0:NSA_DV] / acc[:, NSA_DV:NSA_DV + 1]

    def win_tile(kt):
        k0 = pl.multiple_of(kt * tk, tk)
        dw = t_col - (k0 + kpos)
        return _dot_nt(q, kw_ref[0, pl.ds(k0, tk), :]), k0, (dw >= 0) & (dw < WINDOW)

    def win_body(kt, acc):
        s, k0, ok = win_tile(kt)
        return acc + pv(jnp.where(ok, jnp.exp(s), 0.0), vw_ref, k0)

    kt_lo = jnp.maximum(t0 - (WINDOW - 1), 0) // tk
    acc = lax.fori_loop(kt_lo, kt_hi - 2, win_body, zero)
    ta = win_tile(jnp.maximum(kt_hi - 2, 0))
    tb_ = win_tile(kt_hi - 1)
    pa = jnp.where(ta[2] & (kt_hi >= 2), jnp.exp(ta[0]), 0.0)
    pb = jnp.where(tb_[2], jnp.exp(tb_[0]), 0.0)
    acc = acc + pv(pa, vw_ref, ta[1]) + pv(pb, vw_ref, tb_[1])
    o_w = acc[:, 0:NSA_DV] / acc[:, NSA_DV:NSA_DV + 1]
    _gate_combine(gl_ref, z_ref, o_ref, o_c, o_s, o_w, tq)


def _flash_step(q, k, v, mask, carry):
    m, l, acc = carry
    s = _dot_nt(q, k)
    s = jnp.where(mask, s, _NEG)
    m_new = jnp.maximum(m, jnp.max(s, axis=1, keepdims=True))
    alpha = jnp.exp(m - m_new)
    p = jnp.exp(s - m_new)
    l = alpha * l + jnp.sum(p, axis=1, keepdims=True)
    acc = alpha * acc + jnp.dot(p.astype(jnp.bfloat16), v, preferred_element_type=jnp.float32)
    return m_new, l, acc


def _nsa_prompt_kernel(q_ref, kc_ref, vc_ref, ks_ref, vs_ref, kw_ref, vw_ref, gl_ref, z_ref, o_ref,
                       *, tq, tk, seq):
    t0 = pl.program_id(2) * tq
    m_rows = NSA_REP * tq
    q = q_ref[0].reshape(m_rows, NSA_DKP)
    t_col = t0 + lax.broadcasted_iota(jnp.int32, (NSA_REP, tq, 1), 1).reshape(m_rows, 1)
    o_c, q_slc = _cmp_and_select(q, kc_ref[0, 0], vc_ref[0, 0], t0, tq, seq)

    kpos = lax.broadcasted_iota(jnp.int32, (1, tk), 1)
    init = (jnp.full((m_rows, 1), _NEG, jnp.float32), jnp.zeros((m_rows, 1), jnp.float32),
            jnp.zeros((m_rows, NSA_DV), jnp.float32))

    def slc_body(kt, carry):
        k0 = pl.multiple_of(kt * tk, tk)
        mask = (k0 + kpos) <= t_col
        return _flash_step(q_slc, ks_ref[0, pl.ds(k0, tk), :], vs_ref[0, pl.ds(k0, tk), :], mask, carry)

    _, l_s, acc_s = lax.fori_loop(0, (t0 + tq + tk - 1) // tk, slc_body, init)

    def win_body(kt, carry):
        k0 = pl.multiple_of(kt * tk, tk)
        dw = t_col - (k0 + kpos)
        mask = (dw >= 0) & (dw < WINDOW)
        return _flash_step(q, kw_ref[0, pl.ds(k0, tk), :], vw_ref[0, pl.ds(k0, tk), :], mask, carry)

    kt_lo = jnp.maximum(t0 - (WINDOW - 1), 0) // tk
    _, l_w, acc_w = lax.fori_loop(kt_lo, (t0 + tq + tk - 1) // tk, win_body, init)
    _gate_combine(gl_ref, z_ref, o_ref, o_c, acc_s / l_s, acc_w / l_w, tq)


def nsa_prompt_attention(q_hm, kcmp, vcmp, ks_hm, vs_hm, kw_hm, vw_hm, gl, z, *, batch, tq, tk, fast):
    rows = z.shape[0]
    seq = rows // batch
    nq = seq // tq
    nseg = kcmp.shape[2]
    row_blk = lambda b, g, i: (b * nq + i, g)
    kv_blk = lambda b, g, i: (g, b, 0)
    dvb = NSA_DVP if fast else NSA_DV
    return pl.pallas_call(
        functools.partial(_nsa_prompt_fast_kernel if fast else _nsa_prompt_kernel, tq=tq, tk=tk, seq=seq),
        grid=(batch, NSA_KV, nq),
        in_specs=[pl.BlockSpec((1, NSA_REP, tq, NSA_DKP), lambda b, g, i: (g, 0, b * nq + i, 0)),
                  pl.BlockSpec((1, 1, nseg, NSA_DKP), lambda b, g, i: (b, g, 0, 0)),
                  pl.BlockSpec((1, 1, nseg, NSA_DV), lambda b, g, i: (b, g, 0, 0)),
                  pl.BlockSpec((1, seq, NSA_DKP), kv_blk),
                  pl.BlockSpec((1, seq, dvb), kv_blk),
                  pl.BlockSpec((1, seq, NSA_DKP), kv_blk),
                  pl.BlockSpec((1, seq, dvb), kv_blk),
                  pl.BlockSpec((tq, LANE), row_blk),
                  pl.BlockSpec((tq, NSA_REP * NSA_DV), row_blk)],
        out_specs=pl.BlockSpec((tq, NSA_REP * NSA_DV), row_blk),
        out_shape=jax.ShapeDtypeStruct((rows, NSA_O), jnp.bfloat16),
        compiler_params=_cparams("parallel", "parallel", "arbitrary"),
        name="nsa_prompt_attention_fast" if fast else "nsa_prompt_attention",
    )(q_hm, kcmp, vcmp, ks_hm, vs_hm, kw_hm, vw_hm, gl, z)


def _nsa_in_proj(x, norm_w, w_in):
    B, L, D = x.shape
    x2 = x.reshape(B * L, D)
    h = rmsnorm_bf16(x2, norm_w, tm=min(256, B * L))
    offs = np.cumsum((0,) + NSA_SPLITS)
    tm = _row_tile(B * L)

    wt = w_in.T

    def seg(i, tn, name):
        return matmul_wt(h, wt, tm=tm, tn=tn, n_off=int(offs[i]), n=int(offs[i + 1] - offs[i]), name=name)

    q = seg(0, _WT_TN, "nsa_q_proj")
    kv = matmul_wt(h, wt, tm=tm, tn=256, n_off=int(offs[1]), n=int(offs[7] - offs[1]), name="nsa_kv_proj")
    kc, vc, ks, vs, kw, vw = (kv[:, int(offs[i] - offs[1]):int(offs[i + 1] - offs[1])] for i in range(1, 7))
    wg = wt[offs[7]:offs[8]].reshape(NSA_KV, 3 * NSA_REP, D)
    wg = jnp.pad(wg, ((0, 0), (0, LANE - 3 * NSA_REP), (0, 0))).reshape(NSA_KV * LANE, D)
    gl = matmul_wt(h, wg, tm=tm, tn=NSA_KV * LANE, n_off=0, n=NSA_KV * LANE, name="nsa_gate_proj")
    z = matmul_wt(h, wt[offs[8]:], tm=tm, tn=_WT_TN, n_off=0, n=NSA_O, name="nsa_z_proj")
    return x2, q, kc, vc, ks, vs, kw, vw, gl, z


def _nsa_prompt(x, norm_w, w_in, q_norm, kn_c, kn_s, kn_w, pe_k, w1_k, w2_k, pe_v, w1_v, w2_v, w_out):
    B, T, D = x.shape
    x2, q, kc_r, vc_r, ks_raw, vs_r, kw_raw, vw_r, gl, z = _nsa_in_proj(x, norm_w, w_in)
    tp = min(512, T)
    q_hm = nsa_q_prep(q, q_norm, tq=tp, seq=T)
    ks_r, ks_hm = nsa_k_prep(ks_raw, kn_s, seq=T, tk=tp, onehot=True)
    kw_r, kw_hm = nsa_k_prep(kw_raw, kn_w, seq=T, tk=tp, onehot=False)
    vs_hm = nsa_v_prep(vs_r, tk=tp)
    vw_hm = nsa_v_prep(vw_r, tk=tp)
    npg = T // PAGE_ROWS
    table = (jnp.arange(B, dtype=jnp.int32)[:, None] * npg + jnp.arange(npg, dtype=jnp.int32)[None, :])
    seg3 = lambda a: a.reshape(B * T // CMP_STRIDE, CMP_STRIDE, a.shape[1])
    ka, kb = nsa_compress_partials(seg3(kc_r), table, pe_k, w1_k, pages=min(_CMP_PAGES, npg))
    va, vb = nsa_compress_partials(seg3(vc_r), table, pe_v, w1_v, pages=min(_CMP_PAGES, npg))
    kcmp = nsa_compress_finish(ka, kb, None, w2_k, kn_c, is_key=True, alibi_lanes=True)
    vcmp = nsa_compress_finish(va, vb, None, w2_v, jnp.zeros((NSA_DV,), jnp.float32), is_key=False, alibi_lanes=False)
    bound = jnp.max(jnp.abs(q_norm)) * jnp.maximum(jnp.max(jnp.abs(kn_s)), jnp.max(jnp.abs(kn_w))) * (NSA_DK ** 0.5)
    attend = lambda fast: functools.partial(nsa_prompt_attention, batch=B, tq=_NSA_TQ, tk=_NSA_TK, fast=fast)
    og = lax.cond(bound < _FAST_SCORE_BOUND, attend(True), attend(False),
                  q_hm, kcmp, vcmp, ks_hm, vs_hm, kw_hm, vw_hm, gl, z)
    y = matmul(og, w_out.astype(jnp.bfloat16), tm=_row_tile(B * T), tn=1024, residual=x2, name="nsa_out_proj")
    wl = min(WINDOW, T)
    r4 = lambda a, d: a.reshape(B, T, NSA_KV, d)
    return y.reshape(B, T, D), (r4(kc_r, NSA_DK), r4(vc_r, NSA_DV), r4(ks_r, NSA_DK), r4(vs_r, NSA_DV),
                                r4(kw_r, NSA_DK)[:, T - wl:], r4(vw_r, NSA_DV)[:, T - wl:])


_SMP_PAGES = 8
_SEL_PAD = 384


def _online_update(m_ref, l_ref, acc_ref, g, s, v):
    m_old = m_ref[g]
    m_new = jnp.maximum(m_old, jnp.max(s, axis=1, keepdims=True))
    alpha = jnp.exp(m_old - m_new)
    p = jnp.exp(s - m_new)
    l_ref[g] = alpha * l_ref[g] + jnp.sum(p, axis=1, keepdims=True)
    acc_ref[g] = alpha * acc_ref[g] + jnp.dot(p.astype(jnp.bfloat16), v.astype(jnp.bfloat16),
                                              preferred_element_type=jnp.float32)
    m_ref[g] = m_new


def _nsa_sample_kernel(tab_ref, *refs, past, steps, pages, nsb):
    kp_refs = refs[:pages]
    vp_refs = refs[pages:2 * pages]
    (q_ref, kc_ref, vc_ref, kn_ref, vn_ref, kwc_ref, vwc_ref, kwn_ref, vwn_ref, gl_ref, z_ref,
     o_ref, m_scr, l_scr, acc_scr, sel_scr, oc_scr) = refs[2 * pages:]
    i = pl.program_id(1)
    s_tok = o_ref.shape[0]
    rows = NSA_REP * s_tok
    ncmp = kc_ref.shape[2]
    r_col = lax.broadcasted_iota(jnp.int32, (NSA_REP, s_tok, 1), 0).reshape(rows, 1)
    t_col = past + lax.broadcasted_iota(jnp.int32, (NSA_REP, s_tok, 1), 1).reshape(rows, 1)
    t_tok = past + lax.broadcasted_iota(jnp.int32, (s_tok, 1), 0)

    def slope_col(g):
        return jnp.exp2((r_col + (g * NSA_REP + 1)).astype(jnp.float32) * (-8.0 / NSA_HEADS))

    def q_of(g):
        return q_ref[0, g * rows:(g + 1) * rows, :]

    def rep_rows(x):
        return jnp.concatenate([x] * NSA_REP, axis=0)

    @pl.when(i == 0)
    def _first():
        n_row = lax.broadcasted_iota(jnp.int32, (1, ncmp), 1)
        cvalid = (n_row * CMP_STRIDE + CMP_BLOCK - 1) <= t_col
        ci = lax.broadcasted_iota(jnp.int32, (ncmp, _SEL_PAD), 0) * CMP_STRIDE
        cj = lax.broadcasted_iota(jnp.int32, (ncmp, _SEL_PAD), 1) * SLC_BLOCK
        ov = jnp.maximum(jnp.minimum(ci + CMP_BLOCK, cj + SLC_BLOCK) - jnp.maximum(ci, cj), 0)
        c2s = (ov.astype(jnp.float32) * (1.0 / CMP_STRIDE)).astype(jnp.bfloat16)
        jb = lax.broadcasted_iota(jnp.int32, (s_tok, _SEL_PAD), 1)
        cur = t_tok // SLC_BLOCK
        forced = (jb == 0) | (jb == cur) | (jb == cur - 1)
        valid = (jb * SLC_BLOCK <= t_tok) & (jb < nsb)
        for g in range(NSA_KV):
            s = jnp.where(cvalid, _dot_nt(q_of(g), kc_ref[0, g]), _NEG)
            p = jnp.where(cvalid, jnp.exp(s - jnp.max(s, axis=1, keepdims=True)), 0.0)
            p = p / jnp.maximum(jnp.sum(p, axis=1, keepdims=True), 1e-30)
            oc_scr[g] = jnp.dot(p.astype(jnp.bfloat16), vc_ref[0, g], preferred_element_type=jnp.float32)
            imp = _split_dot(jnp.sum(p.reshape(NSA_REP, s_tok, ncmp), axis=0), c2s)
            work = jnp.where(forced, jnp.inf, jnp.where(valid, imp, -jnp.inf))
            sel = jnp.zeros((s_tok, _SEL_PAD), jnp.float32)
            for _ in range(min(SLC_TOPN, nsb)):
                mx = jnp.max(work, axis=1, keepdims=True)
                first = jnp.min(jnp.where(work == mx, jb, _SEL_PAD), axis=1, keepdims=True)
                hit = jb == first
                sel = jnp.where(hit, 1.0, sel)
                work = jnp.where(hit, -jnp.inf, work)
            sel_scr[g] = sel
            m_scr[g] = jnp.full((rows, 1), _NEG, jnp.float32)
            l_scr[g] = jnp.zeros((rows, 1), jnp.float32)
            acc_scr[g] = jnp.zeros((rows, NSA_DV), jnp.float32)

    def fold(score_of, v_of, nk, page0, causal):
        pos = page0 * PAGE_ROWS + lax.broadcasted_iota(jnp.int32, (1, nk), 1)
        bj = lax.broadcasted_iota(jnp.int32, (_SEL_PAD, nk), 0)
        bl = lax.broadcasted_iota(jnp.int32, (_SEL_PAD, nk), 1)
        expand = jnp.where(bj == (page0 * PAGE_ROWS + bl) // SLC_BLOCK, 1.0, 0.0).astype(jnp.bfloat16)
        for g in range(NSA_KV):
            selk = jnp.dot(sel_scr[g].astype(jnp.bfloat16), expand, preferred_element_type=jnp.float32)
            ok = rep_rows(selk) > 0.5
            if causal:
                ok = ok & (pos <= t_col)
            s = score_of(g, q_of(g)[:, 0:NSA_DK])
            s = s - slope_col(g) * (t_col - pos).astype(jnp.float32)
            _online_update(m_scr, l_scr, acc_scr, g, jnp.where(ok, s, _NEG), v_of(g))

    fold(lambda g, qg: _dot(qg, jnp.concatenate([r[0, g] for r in kp_refs], axis=1)),
         lambda g: jnp.concatenate([r[0, pl.ds(g, PAGE_ROWS, stride=NSA_KV), :] for r in vp_refs], axis=0),
         pages * PAGE_ROWS, i * pages, causal=False)

    @pl.when(i == steps - 1)
    def _last():
        fold(lambda g, qg: _dot_nt(qg, kn_ref[0, :, g * NSA_DK:(g + 1) * NSA_DK]),
             lambda g: vn_ref[0, :, g * NSA_DV:(g + 1) * NSA_DV], PAGE_ROWS, past // PAGE_ROWS, causal=True)
        nwc = kwc_ref.shape[3]
        posw = jnp.concatenate([past - nwc + lax.broadcasted_iota(jnp.int32, (1, nwc), 1),
                                past + lax.broadcasted_iota(jnp.int32, (1, PAGE_ROWS), 1)], axis=1)
        dw = t_col - posw
        wok = (dw >= 0) & (dw < WINDOW) & (posw >= 0)
        gates = 1.0 / (1.0 + jnp.exp(-gl_ref[...]))
        for g in range(NSA_KV):
            ksl = slice(g * NSA_DK, (g + 1) * NSA_DK)
            vsl = slice(g * NSA_DV, (g + 1) * NSA_DV)
            qg = q_of(g)[:, 0:NSA_DK]
            s = jnp.concatenate([_dot(qg, kwc_ref[0, g]), _dot_nt(qg, kwn_ref[0, :, ksl])], axis=1)
            s = jnp.where(wok, s - slope_col(g) * dw.astype(jnp.float32), _NEG)
            p = jnp.where(wok, jnp.exp(s - jnp.max(s, axis=1, keepdims=True)), 0.0)
            p = (p / jnp.maximum(jnp.sum(p, axis=1, keepdims=True), 1e-30)).astype(jnp.bfloat16)
            o_w = (jnp.dot(p[:, 0:nwc], vwc_ref[0, :, vsl].astype(jnp.bfloat16), preferred_element_type=jnp.float32)
                   + jnp.dot(p[:, nwc:], vwn_ref[0, :, vsl].astype(jnp.bfloat16), preferred_element_type=jnp.float32))
            o_s = acc_scr[g] / l_scr[g]
            o_c = oc_scr[g]
            for r in range(NSA_REP):
                rs = slice(r * s_tok, (r + 1) * s_tok)
                c0 = g * LANE + 3 * r
                o = (gates[:, c0:c0 + 1] * o_c[rs] + gates[:, c0 + 1:c0 + 2] * o_s[rs]
                     + gates[:, c0 + 2:c0 + 3] * o_w[rs])
                hs = slice((g * NSA_REP + r) * NSA_DV, (g * NSA_REP + r + 1) * NSA_DV)
                z = z_ref[:, hs]
                o_ref[:, hs] = o * (z * (1.0 / (1.0 + jnp.exp(-z))))


def nsa_sample_attention(q_rows, kcmp, vcmp, pool_k, pool_v, page_table, k_new, v_new, kw_cache, vw_cache,
                         kw_new, vw_new, gl, z, *, s_tok):
    batch, npg = page_table.shape
    pages = _SMP_PAGES
    steps = npg // pages
    past = npg * PAGE_ROWS
    nsb = -(-(past + s_tok) // SLC_BLOCK)
    rows = NSA_REP * s_tok
    ncmp = kcmp.shape[2]
    nwc = kw_cache.shape[3]
    kpage_spec = lambda k: pl.BlockSpec((1, NSA_KV, NSA_DK, PAGE_ROWS), lambda b, i, tab: (tab[b, i * pages + k], 0, 0, 0))
    vpage_spec = lambda k: pl.BlockSpec((1, PAGE_ROWS * NSA_KV, NSA_DV), lambda b, i, tab: (tab[b, i * pages + k], 0, 0))
    per_b = lambda shape: pl.BlockSpec((1,) + shape, lambda b, i, tab: (b,) + (0,) * len(shape))
    tok_blk = lambda w: pl.BlockSpec((s_tok, w), lambda b, i, tab: (b, 0))
    return pl.pallas_call(
        functools.partial(_nsa_sample_kernel, past=past, steps=steps, pages=pages, nsb=nsb),
        grid_spec=pltpu.PrefetchScalarGridSpec(
            num_scalar_prefetch=1,
            grid=(batch, steps),
            in_specs=[kpage_spec(k) for k in range(pages)] + [vpage_spec(k) for k in range(pages)]
            + [per_b((NSA_KV * rows, NSA_DKP)), per_b((NSA_KV, ncmp, NSA_DKP)), per_b((NSA_KV, ncmp, NSA_DV)),
               per_b((PAGE_ROWS, NSA_KW)), per_b((PAGE_ROWS, NSA_VW)),
               per_b((NSA_KV, NSA_DK, nwc)), per_b((nwc, NSA_VW)), per_b((PAGE_ROWS, NSA_KW)), per_b((PAGE_ROWS, NSA_VW)),
               tok_blk(NSA_KV * LANE), tok_blk(NSA_O)],
            out_specs=tok_blk(NSA_O),
            scratch_shapes=[pltpu.VMEM((NSA_KV, rows, 1), jnp.float32), pltpu.VMEM((NSA_KV, rows, 1), jnp.float32),
                            pltpu.VMEM((NSA_KV, rows, NSA_DV), jnp.float32),
                            pltpu.VMEM((NSA_KV, s_tok, _SEL_PAD), jnp.float32),
                            pltpu.VMEM((NSA_KV, rows, NSA_DV), jnp.float32)]),
        out_shape=jax.ShapeDtypeStruct((batch * s_tok, NSA_O), jnp.float32),
        compiler_params=_cparams("parallel", "arbitrary"),
        name="nsa_sample_attention",
    )(page_table, *([pool_k] * pages), *([pool_v] * pages), q_rows, kcmp, vcmp, k_new, v_new,
      kw_cache, vw_cache, kw_new, vw_new, gl, z)


def _nsa_sample(x, ck, cv, sk, sv, wk_buf, wv_buf, page_table, norm_w,
                w_in, q_norm, kn_c, kn_s, kn_w, pe_k, w1_k, w2_k, pe_v, w1_v, w2_v, w_out):
    B, S, D = x.shape
    x2, q, kc_r, vc_r, ks_raw, vs_r, kw_raw, vw_r, gl, z = _nsa_in_proj(x, norm_w, w_in)
    q_hm = nsa_q_prep(q, q_norm, tq=B * S, seq=S, pos0=page_table.shape[1] * PAGE_ROWS)
    q_rows = q_hm.reshape(NSA_KV, NSA_REP, B, S, NSA_DKP).transpose(2, 0, 1, 3, 4).reshape(B, NSA_KV * NSA_REP * S, NSA_DKP)
    ks_r, _ = nsa_k_prep(ks_raw, kn_s, seq=S, tk=B * S, onehot=False)
    kw_r, _ = nsa_k_prep(kw_raw, kn_w, seq=S, tk=B * S, onehot=False)
    as_page = lambda a: jnp.pad(a.reshape(B, S, -1), ((0, 0), (0, PAGE_ROWS - S), (0, 0)))
    seg3 = lambda a, w: a.reshape(-1, CMP_STRIDE, w)
    seg4 = lambda a: a.reshape(-1, CMP_STRIDE * NSA_KV, a.shape[3])
    ident = jnp.arange(B, dtype=jnp.int32)[:, None]
    rows_minor = lambda a: jnp.transpose(a, (0, 2, 3, 1))
    ka, kb = nsa_compress_partials(rows_minor(ck), page_table, pe_k, w1_k, pages=_CMP_PAGES, transposed=True)
    va, vb = nsa_compress_partials(seg4(cv), page_table, pe_v, w1_v, pages=_CMP_PAGES)
    _, kb_new = nsa_compress_partials(seg3(as_page(kc_r), NSA_KW), ident, pe_k, w1_k, pages=1)
    _, vb_new = nsa_compress_partials(seg3(as_page(vc_r), NSA_VW), ident, pe_v, w1_v, pages=1)
    kcmp = nsa_compress_finish(ka, kb, kb_new, w2_k, kn_c, is_key=True, alibi_lanes=True)
    vcmp = nsa_compress_finish(va, vb, vb_new, w2_v, jnp.zeros((NSA_DV,), jnp.float32), is_key=False, alibi_lanes=False)
    wl = wk_buf.shape[1]
    o = nsa_sample_attention(q_rows, kcmp, vcmp, rows_minor(sk), sv.reshape(sv.shape[0], PAGE_ROWS * NSA_KV, NSA_DV), page_table, as_page(ks_r), as_page(vs_r), rows_minor(wk_buf),
                             wv_buf.reshape(B, wl, NSA_VW), as_page(kw_r), as_page(vw_r), gl, z, s_tok=S)
    y = matmul(o.astype(jnp.bfloat16), w_out.astype(jnp.bfloat16), tm=B * S, tn=1024, residual=x2, name="nsa_out_proj")
    r4 = lambda a, d: a.reshape(B, S, NSA_KV, d)
    kw_all = jnp.concatenate([wk_buf, r4(kw_r, NSA_DK)], axis=1)
    vw_all = jnp.concatenate([wv_buf, r4(vw_r, NSA_DV)], axis=1)
    return y.reshape(B, S, D), (r4(kc_r, NSA_DK), r4(vc_r, NSA_DV), r4(ks_r, NSA_DK), r4(vs_r, NSA_DV),
                                kw_all[:, S:], vw_all[:, S:])


def kernel(x_prompt, x_sample, state_delta, state_conv, cache_cmp_k, cache_cmp_v, cache_slc_k, cache_slc_v, cache_win_k, cache_win_v, page_table, norm_dn, w_in_dn, conv_w_dn, a_log_dn, dt_bias_dn, out_norm_dn, w_out_dn, norm_nsa, w_in_nsa, q_norm_nsa, k_norm_cmp, k_norm_slc, k_norm_win, cmp_pe_k, cmp_w1_k, cmp_w2_k, cmp_pe_v, cmp_w1_v, cmp_w2_v, w_out_nsa):
    xp, xs = x_prompt, x_sample
    B = xp.shape[0]
    dw = (norm_dn[0], w_in_dn[0], conv_w_dn[0], a_log_dn[0], dt_bias_dn[0], out_norm_dn[0], w_out_dn[0])
    buf0 = jnp.zeros((B, DN_CONV - 1, conv_w_dn.shape[-1]), xp.dtype)
    st0 = jnp.zeros((B, DN_HEADS, DN_DK, DN_DV), jnp.float32)
    xp, p_dn = _deltanet_mixer(xp, buf0, st0, *dw)
    xs, s_dn = _deltanet_mixer(xs, state_conv[0], state_delta[0], *dw)
    nw = (norm_nsa[0], w_in_nsa[0], q_norm_nsa[0], k_norm_cmp[0], k_norm_slc[0], k_norm_win[0],
          cmp_pe_k[0], cmp_w1_k[0], cmp_w2_k[0], cmp_pe_v[0], cmp_w1_v[0], cmp_w2_v[0], w_out_nsa[0])
    xp, p_nsa = _nsa_prompt(xp, *nw)
    xs, s_nsa = _nsa_sample(xs, cache_cmp_k[0], cache_cmp_v[0], cache_slc_k[0], cache_slc_v[0],
                            cache_win_k[0], cache_win_v[0], page_table, *nw)
    return ((xp, xs, p_dn[0][None], p_dn[1][None]) + tuple(t[None] for t in p_nsa)
            + (s_dn[0][None], s_dn[1][None]) + tuple(t[None] for t in s_nsa))
```

```python
import functools

import jax
import jax.numpy as jnp
import numpy as np
from jax import lax
from jax.experimental import pallas as pl
from jax.experimental.pallas import tpu as pltpu

D_MODEL = 4096
EPS = 1e-6

DN_HEADS = 32
DN_DK = 128
DN_DV = 128
DN_CONV = 4
DN_CHUNK = 64

NSA_HEADS = 32
NSA_KV = 4
NSA_REP = NSA_HEADS // NSA_KV
NSA_DK = 192
NSA_DV = 128
CMP_BLOCK = 32
CMP_STRIDE = 16
CMP_RATIO = CMP_BLOCK // CMP_STRIDE
CMP_HIDDEN = 256
SLC_BLOCK = 64
SLC_TOPN = 16
WINDOW = 512
NSA_Q = NSA_HEADS * NSA_DK
NSA_KW = NSA_KV * NSA_DK
NSA_VW = NSA_KV * NSA_DV
NSA_O = NSA_HEADS * NSA_DV
NSA_SPLITS = (NSA_Q, NSA_KW, NSA_VW, NSA_KW, NSA_VW, NSA_KW, NSA_VW, 3 * NSA_HEADS, NSA_O)

VMEM_LIMIT_BYTES = 56 * 1024 * 1024
LANE = 128
_WT_TN = 512


def _cparams(*sem):
    return pltpu.CompilerParams(dimension_semantics=sem, vmem_limit_bytes=VMEM_LIMIT_BYTES)


def _rmsnorm_kernel(x_ref, w_ref, o_ref):
    x = x_ref[...]
    ms = jnp.mean(x * x, axis=-1, keepdims=True)
    o_ref[...] = (x * lax.rsqrt(ms + EPS) * w_ref[...]).astype(o_ref.dtype)


def rmsnorm_bf16(x, w, *, tm):
    m, d = x.shape
    return pl.pallas_call(
        _rmsnorm_kernel,
        grid=(m // tm,),
        in_specs=[pl.BlockSpec((tm, d), lambda i: (i, 0)), pl.BlockSpec((1, d), lambda i: (0, 0))],
        out_specs=pl.BlockSpec((tm, d), lambda i: (i, 0)),
        out_shape=jax.ShapeDtypeStruct((m, d), jnp.bfloat16),
        compiler_params=_cparams("parallel"),
        name="rmsnorm_bf16",
    )(x, w.reshape(1, d))


def _mm_kernel(a_ref, b_ref, o_ref):
    o_ref[...] = jnp.dot(a_ref[...], b_ref[...], preferred_element_type=jnp.float32).astype(o_ref.dtype)


def _mm_res_kernel(a_ref, b_ref, r_ref, o_ref):
    acc = jnp.dot(a_ref[...], b_ref[...], preferred_element_type=jnp.float32)
    o_ref[...] = r_ref[...] + acc


def matmul(a, b, *, tm, tn, residual=None, out_dtype=jnp.float32, name="matmul"):
    m, k = a.shape
    _, n = b.shape
    assert m % tm == 0 and n % tn == 0
    in_specs = [pl.BlockSpec((tm, k), lambda i, j: (i, 0)), pl.BlockSpec((k, tn), lambda i, j: (0, j))]
    args = [a, b]
    kern = _mm_kernel
    if residual is not None:
        in_specs.append(pl.BlockSpec((tm, tn), lambda i, j: (i, j)))
        args.append(residual)
        kern = _mm_res_kernel
    return pl.pallas_call(
        kern,
        grid=(m // tm, n // tn),
        in_specs=in_specs,
        out_specs=pl.BlockSpec((tm, tn), lambda i, j: (i, j)),
        out_shape=jax.ShapeDtypeStruct((m, n), out_dtype),
        compiler_params=_cparams("parallel", "parallel"),
        name=name,
    )(*args)


def _mm_wt_kernel(a_ref, w_ref, o_ref):
    o_ref[...] = lax.dot_general(a_ref[...], w_ref[...].astype(jnp.bfloat16), (((1,), (1,)), ((), ())),
                                 preferred_element_type=jnp.float32)


def matmul_wt(a, wt, *, tm, tn, n_off, n, name):
    m, k = a.shape
    assert m % tm == 0 and n % tn == 0 and n_off % tn == 0
    j0 = n_off // tn
    return pl.pallas_call(
        _mm_wt_kernel,
        grid=(m // tm, n // tn),
        in_specs=[pl.BlockSpec((tm, k), lambda i, j: (i, 0)), pl.BlockSpec((tn, k), lambda i, j: (j0 + j, 0))],
        out_specs=pl.BlockSpec((tm, tn), lambda i, j: (i, j)),
        out_shape=jax.ShapeDtypeStruct((m, n), jnp.float32),
        compiler_params=_cparams("parallel", "parallel"),
        name=name,
    )(a, wt)


def _row_tile(m):
    return 1024 if m % 1024 == 0 else m


def _dot(a, b):
    return jnp.dot(a.astype(jnp.bfloat16), b.astype(jnp.bfloat16), preferred_element_type=jnp.float32)


def _dot_nt(a, b):
    return lax.dot_general(a.astype(jnp.bfloat16), b.astype(jnp.bfloat16), (((1,), (1,)), ((), ())),
                           preferred_element_type=jnp.float32)


def _dot_tn(a, b):
    return lax.dot_general(a.astype(jnp.bfloat16), b.astype(jnp.bfloat16), (((0,), (0,)), ((), ())),
                           preferred_element_type=jnp.float32)


def _split_bf16(x):
    hi = x.astype(jnp.bfloat16)
    lo = (x - hi.astype(jnp.float32)).astype(jnp.bfloat16)
    return hi, lo


def _dn_gate_kernel(h_ref, wt_ref, alog_ref, dtb_ref, o_ref, *, heads):
    r = lax.dot_general(wt_ref[...], h_ref[...], (((1,), (1,)), ((), ())),
                        preferred_element_type=jnp.float32)
    b = r[:heads]
    a = r[heads:] + dtb_ref[...]
    softplus = jnp.maximum(a, 0.0) + jnp.log(1.0 + jnp.exp(-jnp.abs(a)))
    o_ref[0:heads, :] = 1.0 / (1.0 + jnp.exp(-b))
    o_ref[heads:2 * heads, :] = -jnp.exp(alog_ref[...]) * softplus


def dn_gates(h, w_ba_t, a_log, dt_bias, *, tm):
    m, d = h.shape
    heads = a_log.shape[0]
    return pl.pallas_call(
        functools.partial(_dn_gate_kernel, heads=heads),
        grid=(m // tm,),
        in_specs=[pl.BlockSpec((tm, d), lambda i: (i, 0)),
                  pl.BlockSpec((2 * heads, d), lambda i: (0, 0)),
                  pl.BlockSpec((heads, 1), lambda i: (0, 0)),
                  pl.BlockSpec((heads, 1), lambda i: (0, 0))],
        out_specs=pl.BlockSpec((2 * heads, tm), lambda i: (0, i)),
        out_shape=jax.ShapeDtypeStruct((2 * heads, m), jnp.float32),
        compiler_params=_cparams("parallel"),
        name="dn_gates",
    )(h, w_ba_t, a_log.reshape(heads, 1), dt_bias.reshape(heads, 1))


_DN_TAIL = 8
_DN_TB = 256
_DN_TMIN = 2 * DN_CHUNK
_DN_HB = 8


def _dn_chunk_prep(q, k, g_row, b_row):
    c = q.shape[0]
    ii = lax.broadcasted_iota(jnp.int32, (c, c), 0)
    jj = lax.broadcasted_iota(jnp.int32, (c, c), 1)
    lower, strict, eye = ii >= jj, ii > jj, ii == jj
    g_b = jnp.broadcast_to(g_row, (c, c))
    b_b = jnp.broadcast_to(b_row, (c, c))
    b_col = jnp.sum(jnp.where(eye, b_b, 0.0), axis=1, keepdims=True)
    g_col = jnp.sum(jnp.where(eye, g_b, 0.0), axis=1, keepdims=True)
    gc_col = jnp.sum(jnp.where(lower, g_b, 0.0), axis=1, keepdims=True)
    gc_row = jnp.sum(jnp.where(ii <= jj, g_col, 0.0), axis=0, keepdims=True)
    g_last = jnp.sum(g_row, axis=1, keepdims=True)
    decay = jnp.exp(jnp.where(lower, gc_col - gc_row, -jnp.inf))
    kk = _dot_nt(k, k)
    qk = _dot_nt(q, k)
    lmat = jnp.where(strict, kk * decay, 0.0) * b_col
    attn = qk * decay
    return dict(x=-lmat, p=eye.astype(jnp.float32) - lmat, attn=attn, b_row=b_row,
                e_row=jnp.exp(gc_row), e_col=jnp.exp(gc_col), c_col=jnp.exp(g_last - gc_col),
                e_last=jnp.exp(g_last))


def _dn_inverse_levels(items, c):
    m = 2
    while m < c:
        for it in items:
            it["x"] = _dot(it["x"], it["x"])
        for it in items:
            it["p"] = it["p"] + _dot(it["p"], it["x"])
        m *= 2


def _dn_kernel(q_ref, k_ref, v_ref, z_ref, beta_ref, g_ref, cw_ref, cs_ref, s0_ref, onw_ref,
               o_ref, s_out_ref, xpad, s_scr, *, tb, hb, live):
    n = pl.program_id(2)
    c = DN_CHUNK
    d = DN_DK
    nc = live

    @pl.when(n == 0)
    def _init():
        xpad[:, 0:_DN_TAIL, :] = cs_ref[0]
        s_scr[...] = s0_ref[0]

    def conv(i, ref):
        xpad[i, _DN_TAIL:_DN_TAIL + tb, :] = ref[...]
        acc = None
        for j in range(DN_CONV):
            off = _DN_TAIL - (DN_CONV - 1) + j
            term = xpad[i, off:off + tb, :] * cw_ref[i, j:j + 1, :]
            acc = term if acc is None else acc + term
        xpad[i, 0:_DN_TAIL, :] = xpad[i, tb:tb + _DN_TAIL, :]
        return acc * (1.0 / (1.0 + jnp.exp(-acc)))

    qc, kc, vc = conv(0, q_ref), conv(1, k_ref), conv(2, v_ref)
    items = []
    for hh in range(hb):
        sl = slice(hh * d, (hh + 1) * d)
        q = qc[:, sl]
        k = kc[:, sl]
        q = q * (lax.rsqrt(jnp.sum(q * q, axis=-1, keepdims=True) + EPS) * (d ** -0.5))
        k = k * lax.rsqrt(jnp.sum(k * k, axis=-1, keepdims=True) + EPS)
        g_all = g_ref[hh]
        b_all = beta_ref[hh]
        for ci in range(nc):
            rs = slice(ci * c, (ci + 1) * c)
            it = _dn_chunk_prep(q[rs], k[rs], g_all[:, rs], b_all[:, rs])
            it.update(q=q[rs], k=k[rs], v=vc[rs, sl])
            items.append(it)
    _dn_inverse_levels(items, c)
    for it in items:
        inv_b = it["p"] * it["b_row"]
        it["u"] = _dot(inv_b, it["v"])
        it["w"] = _dot(inv_b * it["e_row"], it["k"])
    states = [s_scr[hh] for hh in range(hb)]
    outs = [[None] * nc for _ in range(hb)]
    for ci in range(nc):
        for hh in range(hb):
            it = items[hh * nc + ci]
            s = states[hh]
            v_new = it["u"] - _dot(it["w"], s)
            outs[hh][ci] = it["e_col"] * _dot(it["q"], s) + _dot(it["attn"], v_new)
            states[hh] = s * it["e_last"] + _dot_tn(it["k"] * it["c_col"], v_new)
    for hh in range(hb):
        sl = slice(hh * d, (hh + 1) * d)
        s_scr[hh] = states[hh]
        o = jnp.concatenate(outs[hh], axis=0) if nc > 1 else outs[hh][0]
        o = o * lax.rsqrt(jnp.mean(o * o, axis=-1, keepdims=True) + EPS) * onw_ref[...]
        z = z_ref[0:nc * c, sl]
        o_ref[0:nc * c, sl] = (o * (z * (1.0 / (1.0 + jnp.exp(-z))))).astype(o_ref.dtype)
        if nc * c < tb:
            o_ref[nc * c:tb, sl] = jnp.zeros((tb - nc * c, d), o_ref.dtype)

    @pl.when(n == pl.num_programs(2) - 1)
    def _fin():
        s_out_ref[0] = s_scr[...]


def dn_delta(proj, gates, conv_w, conv_state, s0, out_norm, *, batch, tb, hb, live):
    rows, width = proj.shape
    d = DN_DK
    heads = width // (4 * d)
    t = rows // batch
    nt = t // tb
    hg = heads // hb
    wd = hb * d
    row_map = lambda off: (lambda b, h, n: (b * nt + n, off * hg + h))
    gate_map = lambda off: (lambda b, h, n: (off * hg + h, 0, b * nt + n))
    return pl.pallas_call(
        functools.partial(_dn_kernel, tb=tb, hb=hb, live=live),
        grid=(batch, hg, nt),
        in_specs=[pl.BlockSpec((tb, wd), row_map(0)),
                  pl.BlockSpec((tb, wd), row_map(1)),
                  pl.BlockSpec((tb, wd), row_map(2)),
                  pl.BlockSpec((tb, wd), row_map(3)),
                  pl.BlockSpec((hb, 1, tb), gate_map(0)),
                  pl.BlockSpec((hb, 1, tb), gate_map(1)),
                  pl.BlockSpec((3, DN_CONV, wd), lambda b, h, n: (0, 0, h)),
                  pl.BlockSpec((1, 3, _DN_TAIL, wd), lambda b, h, n: (b, 0, 0, h)),
                  pl.BlockSpec((1, hb, d, d), lambda b, h, n: (b, h, 0, 0)),
                  pl.BlockSpec((1, d), lambda b, h, n: (0, 0))],
        out_specs=[pl.BlockSpec((tb, wd), lambda b, h, n: (b * nt + n, h)),
                   pl.BlockSpec((1, hb, d, d), lambda b, h, n: (b, h, 0, 0))],
        out_shape=[jax.ShapeDtypeStruct((rows, heads * d), jnp.bfloat16),
                   jax.ShapeDtypeStruct(s0.shape, jnp.float32)],
        scratch_shapes=[pltpu.VMEM((3, tb + _DN_TAIL, wd), jnp.float32),
                        pltpu.VMEM((hb, d, d), jnp.float32)],
        compiler_params=_cparams("parallel", "parallel", "arbitrary"),
        name="dn_delta",
    )(proj, proj, proj, proj, gates, gates, conv_w, conv_state, s0, out_norm.reshape(1, d))


def _deltanet_mixer(x, conv_buf, s0, norm_w, w_in, conv_w, a_log, dt_bias, out_norm, w_out):
    B, L, D = x.shape
    H = a_log.shape[0]
    cdim = conv_w.shape[1]
    hd = cdim // 3
    x2 = x.reshape(B * L, D)
    h = rmsnorm_bf16(x2, norm_w, tm=min(256, B * L))
    wt = w_in.T
    w_ba_t = wt[cdim + hd:].astype(jnp.bfloat16)
    proj = matmul_wt(h, wt, tm=_row_tile(B * L), tn=_WT_TN, n_off=0, n=cdim + hd, name="dn_in_proj")
    gates = dn_gates(h, w_ba_t, a_log, dt_bias, tm=_row_tile(B * L))
    new_buf = proj.reshape(B, L, -1)[:, L - (DN_CONV - 1):, :cdim]
    T = -(-L // _DN_TMIN) * _DN_TMIN
    if T != L:
        proj = jnp.pad(proj.reshape(B, L, -1), ((0, 0), (0, T - L), (0, 0))).reshape(B * T, -1)
        gates = jnp.pad(gates.reshape(2 * H, B, L), ((0, 0), (0, 0), (0, T - L))).reshape(2 * H, B * T)
    cw = conv_w.reshape(DN_CONV, 3, hd).transpose(1, 0, 2)
    cs = conv_buf.reshape(B, DN_CONV - 1, 3, hd).transpose(0, 2, 1, 3)
    cs = jnp.pad(cs, ((0, 0), (0, 0), (_DN_TAIL - (DN_CONV - 1), 0), (0, 0)))
    tb = min(_DN_TB, T)
    live = tb // DN_CHUNK if T == L else -(-L // DN_CHUNK)
    assert T == L or T == tb
    o, s_new = dn_delta(proj, gates.reshape(2 * H, 1, B * T), cw, cs, s0.astype(jnp.float32), out_norm,
                        batch=B, tb=tb, hb=_DN_HB, live=live)
    if T != L:
        o = o.reshape(B, T, -1)[:, :L].reshape(B * L, -1)
    y = matmul(o, w_out.astype(jnp.bfloat16), tm=_row_tile(B * L), tn=1024, residual=x2, name="dn_out_proj")
    return y.reshape(B, L, D), (s_new, new_buf)


def _alibi_slopes():
    hh = jnp.arange(1, NSA_HEADS + 1, dtype=jnp.float32)
    return jnp.exp2(-8.0 * hh / NSA_HEADS).reshape(NSA_KV, NSA_REP)


PAGE_ROWS = 128
NSA_DKP = 256
_POS_LANE = NSA_DK
_SEL_LANE = NSA_DK + 4
_SEL_LANES = 32
_ROW_LANE = _SEL_LANE + _SEL_LANES
NSA_DVP = 256
_FAST_SCORE_BOUND = 40.0
_MASK_BIG = 2.0 ** 100
_NEG = -1e30
_NSA_TQ = 128
_NSA_TK = 256
_CMP_PAGES = 16


def _slope_table():
    s = _alibi_slopes()
    hi = s.astype(jnp.bfloat16).astype(jnp.float32)
    lo = (s - hi).astype(jnp.bfloat16).astype(jnp.float32)
    tab = jnp.stack([64.0 * hi, hi, 64.0 * lo, lo], axis=-1)
    return jnp.pad(tab, ((0, 0), (0, 0), (0, NSA_DKP - NSA_DK - 4)))


def _q_prep_kernel(q_ref, w_ref, tab_ref, o_ref, *, seq, pos0):
    tq = q_ref.shape[0]
    npad = NSA_DKP - NSA_DK
    t = (pos0 + (pl.program_id(0) * tq) % seq + lax.broadcasted_iota(jnp.int32, (tq, npad), 0)).astype(jnp.float32)
    lane = lax.broadcasted_iota(jnp.int32, (tq, npad), 1)
    for r in range(NSA_REP):
        x = q_ref[:, r * NSA_DK:(r + 1) * NSA_DK]
        y = x * lax.rsqrt(jnp.mean(x * x, axis=-1, keepdims=True) + EPS) * (w_ref[...] * (NSA_DK ** -0.5))
        o_ref[0, r, :, 0:NSA_DK] = y.astype(o_ref.dtype)
        tab = jnp.broadcast_to(tab_ref[0, r:r + 1, :], (tq, npad))
        slope = tab[:, 1:2] + tab[:, 3:4]
        extra = jnp.where(lane == _ROW_LANE - NSA_DK, -slope * t, tab)
        o_ref[0, r, :, NSA_DK:NSA_DKP] = extra.astype(o_ref.dtype)


def nsa_q_prep(q_raw, q_norm, *, tq, seq, pos0=0):
    rows = q_raw.shape[0]
    return pl.pallas_call(
        functools.partial(_q_prep_kernel, seq=seq, pos0=pos0),
        grid=(rows // tq, NSA_KV),
        in_specs=[pl.BlockSpec((tq, NSA_REP * NSA_DK), lambda i, g: (i, g)),
                  pl.BlockSpec((1, NSA_DK), lambda i, g: (0, 0)),
                  pl.BlockSpec((1, NSA_REP, NSA_DKP - NSA_DK), lambda i, g: (g, 0, 0))],
        out_specs=pl.BlockSpec((1, NSA_REP, tq, NSA_DKP), lambda i, g: (g, 0, i, 0)),
        out_shape=jax.ShapeDtypeStruct((NSA_KV, NSA_REP, rows, NSA_DKP), jnp.bfloat16),
        compiler_params=_cparams("parallel", "parallel"),
        name="nsa_q_prep",
    )(q_raw, q_norm.reshape(1, NSA_DK), _slope_table())


def _k_prep_kernel(k_ref, w_ref, on_ref, oh_ref, *, seq, onehot):
    tk = k_ref.shape[0]
    pos = (pl.program_id(0) * tk) % seq + lax.broadcasted_iota(jnp.int32, (tk, NSA_DKP - NSA_DK), 0)
    lane = lax.broadcasted_iota(jnp.int32, (tk, NSA_DKP - NSA_DK), 1)
    blk, off = pos // SLC_BLOCK, pos % SLC_BLOCK
    extra = jnp.where((lane == 0) | (lane == 2), blk, jnp.where((lane == 1) | (lane == 3), off, 0))
    if onehot:
        extra = jnp.where((lane >= 4) & (lane < 4 + _SEL_LANES) & (lane - 4 == blk), 1, extra)
    extra = jnp.where(lane == _ROW_LANE - NSA_DK, 1, extra)
    extra = extra.astype(jnp.float32)
    for g in range(NSA_KV):
        x = k_ref[:, g * NSA_DK:(g + 1) * NSA_DK]
        x = x * lax.rsqrt(jnp.mean(x * x, axis=-1, keepdims=True) + EPS) * w_ref[...]
        on_ref[:, g * NSA_DK:(g + 1) * NSA_DK] = x
        oh_ref[g, :, 0:NSA_DK] = x.astype(oh_ref.dtype)
        oh_ref[g, :, NSA_DK:NSA_DKP] = extra.astype(oh_ref.dtype)


def nsa_k_prep(k_raw, k_norm, *, seq, tk, onehot):
    rows = k_raw.shape[0]
    return pl.pallas_call(
        functools.partial(_k_prep_kernel, seq=seq, onehot=onehot),
        grid=(rows // tk,),
        in_specs=[pl.BlockSpec((tk, NSA_KW), lambda i: (i, 0)), pl.BlockSpec((1, NSA_DK), lambda i: (0, 0))],
        out_specs=[pl.BlockSpec((tk, NSA_KW), lambda i: (i, 0)),
                   pl.BlockSpec((NSA_KV, tk, NSA_DKP), lambda i: (0, i, 0))],
        out_shape=[jax.ShapeDtypeStruct((rows, NSA_KW), jnp.float32),
                   jax.ShapeDtypeStruct((NSA_KV, rows, NSA_DKP), jnp.bfloat16)],
        compiler_params=_cparams("parallel"),
        name="nsa_k_prep",
    )(k_raw, k_norm.reshape(1, NSA_DK))


def _v_prep_kernel(v_ref, o_ref):
    tk = v_ref.shape[0]
    ones_col = jnp.where(lax.broadcasted_iota(jnp.int32, (tk, NSA_DVP - NSA_DV), 1) == 0, 1.0, 0.0)
    for g in range(NSA_KV):
        o_ref[g, :, 0:NSA_DV] = v_ref[:, g * NSA_DV:(g + 1) * NSA_DV].astype(o_ref.dtype)
        o_ref[g, :, NSA_DV:NSA_DVP] = ones_col.astype(o_ref.dtype)


def nsa_v_prep(v_raw, *, tk):
    rows = v_raw.shape[0]
    return pl.pallas_call(
        _v_prep_kernel,
        grid=(rows // tk,),
        in_specs=[pl.BlockSpec((tk, NSA_VW), lambda i: (i, 0))],
        out_specs=pl.BlockSpec((NSA_KV, tk, NSA_DVP), lambda i: (0, i, 0)),
        out_shape=jax.ShapeDtypeStruct((NSA_KV, rows, NSA_DVP), jnp.bfloat16),
        compiler_params=_cparams("parallel"),
        name="nsa_v_prep",
    )(v_raw)


def _compress_partials_kernel(tab_ref, *refs, d, pages, transposed):
    x_refs, (pe_ref, w1_ref, perm_ref, a_ref, b_ref) = refs[:pages], refs[pages:]
    seg = PAGE_ROWS // CMP_STRIDE
    acc_a = acc_b = None
    if transposed:
        rows_of = [[_dot_nt(perm_ref[...], r[0, g]) for r in x_refs] for g in range(NSA_KV)]
        take = lambda ri, l, g: rows_of[g][ri][l * seg:(l + 1) * seg, :]
    elif x_refs[0].shape[1] == CMP_STRIDE * NSA_KV:
        take = lambda ri, l, g: x_refs[ri][:, l * NSA_KV + g, :]
    else:
        take = lambda ri, l, g: x_refs[ri][:, l, g * d:(g + 1) * d]

    for l in range(CMP_STRIDE):
        xg = jnp.concatenate([take(ri, l, g) for g in range(NSA_KV) for ri in range(pages)], axis=0)
        ta = jnp.dot((xg + pe_ref[l:l + 1, :]).astype(jnp.bfloat16), w1_ref[l], preferred_element_type=jnp.float32)
        tb = jnp.dot((xg + pe_ref[CMP_STRIDE + l:CMP_STRIDE + l + 1, :]).astype(jnp.bfloat16),
                     w1_ref[CMP_STRIDE + l], preferred_element_type=jnp.float32)
        acc_a = ta if acc_a is None else acc_a + ta
        acc_b = tb if acc_b is None else acc_b + tb
    n = pages * seg
    for g in range(NSA_KV):
        a_ref[0, g] = acc_a[g * n:(g + 1) * n]
        b_ref[0, g] = acc_b[g * n:(g + 1) * n]


def nsa_compress_partials(pool3, table, pe, w1, *, pages, transposed=False):
    d = pe.shape[1]
    batch, npg = table.shape
    seg = PAGE_ROWS // CMP_STRIDE
    blk = ((1,) if transposed else (seg,)) + tuple(pool3.shape[1:])
    x_spec = lambda k: pl.BlockSpec(blk, lambda b, i, tab: (tab[b, i * pages + k],) + (0,) * (len(blk) - 1))
    out_spec = pl.BlockSpec((1, NSA_KV, pages * seg, CMP_HIDDEN), lambda b, i, tab: (b, 0, i, 0))
    out_sds = jax.ShapeDtypeStruct((batch, NSA_KV, npg * seg, CMP_HIDDEN), jnp.float32)
    i = jnp.arange(PAGE_ROWS)
    perm = (i[None, :] == (i % seg)[:, None] * CMP_STRIDE + (i // seg)[:, None]).astype(jnp.bfloat16)
    return pl.pallas_call(
        functools.partial(_compress_partials_kernel, d=d, pages=pages, transposed=transposed),
        grid_spec=pltpu.PrefetchScalarGridSpec(
            num_scalar_prefetch=1,
            grid=(batch, npg // pages),
            in_specs=[x_spec(k) for k in range(pages)]
            + [pl.BlockSpec((CMP_BLOCK, d), lambda b, i, tab: (0, 0)),
               pl.BlockSpec((CMP_BLOCK, d, CMP_HIDDEN), lambda b, i, tab: (0, 0, 0)),
               pl.BlockSpec((PAGE_ROWS, PAGE_ROWS), lambda b, i, tab: (0, 0))],
            out_specs=[out_spec, out_spec]),
        out_shape=[out_sds, out_sds],
        compiler_params=_cparams("parallel", "arbitrary"),
        name="nsa_compress_partials",
    )(table, *([pool3] * pages), pe, w1.astype(jnp.bfloat16), perm)


def _cmp_extra_lanes(nrows, width):
    n = lax.broadcasted_iota(jnp.int32, (nrows, width), 0)
    lane = lax.broadcasted_iota(jnp.int32, (nrows, width), 1)
    ec = n * CMP_STRIDE + CMP_BLOCK - 1
    blk, off = ec // SLC_BLOCK, ec % SLC_BLOCK
    pl_ = lane - _POS_LANE
    return jnp.where((pl_ == 0) | (pl_ == 2), blk, jnp.where((pl_ == 1) | (pl_ == 3), off, 0)).astype(jnp.float32)


def _compress_finish_kernel(a_ref, b_ref, bn_ref, w2_ref, kn_ref, o_ref, *, is_key, has_new, alibi_lanes):
    ns = a_ref.shape[2]
    row = lax.broadcasted_iota(jnp.int32, (ns, 1), 0)
    for g in range(NSA_KV):
        nxt = pltpu.roll(b_ref[0, g], ns - 1, 0)
        if has_new:
            nxt = jnp.where(row == ns - 1, bn_ref[0, g, 0:1, :], nxt)
        hid_pre = a_ref[0, g] + nxt
        hid = hid_pre * (1.0 / (1.0 + jnp.exp(-hid_pre)))
        out = jnp.dot(hid.astype(jnp.bfloat16), w2_ref[...], preferred_element_type=jnp.float32)
        if is_key:
            ms = jnp.sum(out * out, axis=-1, keepdims=True) * (1.0 / NSA_DK)
            out = out * lax.rsqrt(ms + EPS) * kn_ref[...]
            if alibi_lanes:
                out = out + _cmp_extra_lanes(ns, out.shape[1])
        o_ref[0, g] = out.astype(o_ref.dtype)


def nsa_compress_finish(part_a, part_b, part_b_new, w2, kn, *, is_key, alibi_lanes):
    batch, _, ns, _ = part_a.shape
    d = w2.shape[1]
    dp = NSA_DKP if is_key else d
    has_new = part_b_new is not None
    if not has_new:
        part_b_new = jnp.zeros((batch, NSA_KV, 8, CMP_HIDDEN), jnp.float32)
    w2p = jnp.pad(w2, ((0, 0), (0, dp - d))).astype(jnp.bfloat16)
    knp = jnp.pad(kn, (0, dp - d)).reshape(1, dp)
    blk = lambda n: pl.BlockSpec((1, NSA_KV, n, CMP_HIDDEN), lambda b: (b, 0, 0, 0))
    return pl.pallas_call(
        functools.partial(_compress_finish_kernel, is_key=is_key, has_new=has_new, alibi_lanes=alibi_lanes),
        grid=(batch,),
        in_specs=[blk(ns), blk(ns), blk(part_b_new.shape[2]),
                  pl.BlockSpec((CMP_HIDDEN, dp), lambda b: (0, 0)),
                  pl.BlockSpec((1, dp), lambda b: (0, 0))],
        out_specs=pl.BlockSpec((1, NSA_KV, ns, dp), lambda b: (b, 0, 0, 0)),
        out_shape=jax.ShapeDtypeStruct((batch, NSA_KV, ns, dp), jnp.bfloat16),
        compiler_params=_cparams("parallel"),
        name="nsa_compress_finish_k" if is_key else "nsa_compress_finish_v",
    )(part_a, part_b, part_b_new, w2p, knp)


def _split_dot(x, m_bf16):
    hi, lo = _split_bf16(x)
    f = functools.partial(jnp.dot, preferred_element_type=jnp.float32)
    return f(hi, m_bf16) + f(lo, m_bf16)


def _topn_mask_t(imp_t, t_row, nsb, topn):
    jb = lax.broadcasted_iota(jnp.int32, imp_t.shape, 0)
    cur = t_row // SLC_BLOCK
    forced = (jb == 0) | (jb == cur) | (jb == cur - 1)
    valid = (jb * SLC_BLOCK <= t_row) & (jb < nsb)
    impm = jnp.where(forced, jnp.inf, jnp.where(valid, imp_t, -jnp.inf))
    rank = jnp.zeros(imp_t.shape, jnp.float32)
    for k in range(nsb):
        row = impm[k:k + 1, :]
        beats = (row > impm) | ((row == impm) & (k < jb))
        rank = rank + jnp.where(beats, 1.0, 0.0)
    return (rank < topn) & (jb < nsb)


def _cmp_and_select(q, kc, vc, t0, tq, seq):
    rep = NSA_REP
    m_rows = rep * tq
    nc = seq // CMP_STRIDE - CMP_RATIO + 1
    nseg = kc.shape[0]
    nsb = seq // SLC_BLOCK
    t_col = t0 + lax.broadcasted_iota(jnp.int32, (rep, tq, 1), 1).reshape(m_rows, 1)
    t_row = t0 + lax.broadcasted_iota(jnp.int32, (1, tq), 1)
    n_row = lax.broadcasted_iota(jnp.int32, (1, nseg), 1)
    cvalid = (n_row * CMP_STRIDE + CMP_BLOCK - 1 <= t_col) & (n_row < nc)
    s = jnp.where(cvalid, _dot_nt(q, kc), _NEG)
    p = jnp.where(cvalid, jnp.exp(s - jnp.max(s, axis=1, keepdims=True)), 0.0)
    p = p / jnp.maximum(jnp.sum(p, axis=1, keepdims=True), 1e-30)
    o_c = jnp.dot(p.astype(jnp.bfloat16), vc, preferred_element_type=jnp.float32)
    psum = jnp.sum(p.reshape(rep, tq, nseg), axis=0)
    cj = lax.broadcasted_iota(jnp.int32, (_SEL_LANES, nseg), 0) * SLC_BLOCK
    ci = lax.broadcasted_iota(jnp.int32, (_SEL_LANES, nseg), 1) * CMP_STRIDE
    ov = jnp.maximum(jnp.minimum(ci + CMP_BLOCK, cj + SLC_BLOCK) - jnp.maximum(ci, cj), 0)
    c2s_t = (ov.astype(jnp.float32) * (1.0 / CMP_STRIDE)).astype(jnp.bfloat16)
    p_hi, p_lo = _split_bf16(psum)
    imp_t = _dot_nt(c2s_t, p_hi) + _dot_nt(c2s_t, p_lo)
    sel_t = _topn_mask_t(imp_t, t_row, nsb, min(SLC_TOPN, nsb))
    pen_t = jnp.where(sel_t, 0.0, -_MASK_BIG).astype(jnp.bfloat16)
    pi = lax.broadcasted_iota(jnp.int32, (_SEL_LANES, NSA_DKP), 0)
    pj = lax.broadcasted_iota(jnp.int32, (_SEL_LANES, NSA_DKP), 1)
    place = jnp.where(pj == pi + _SEL_LANE, 1.0, 0.0).astype(jnp.bfloat16)
    q_add = _dot_tn(pen_t, place).astype(jnp.bfloat16)
    q_slc = (q.reshape(rep, tq, NSA_DKP) + q_add[None]).reshape(m_rows, NSA_DKP)
    return o_c, q_slc


def _gate_combine(gl_ref, z_ref, o_ref, o_c, o_s, o_w, tq):
    gates = 1.0 / (1.0 + jnp.exp(-gl_ref[...]))
    for r in range(NSA_REP):
        rs = slice(r * tq, (r + 1) * tq)
        o = (gates[:, 3 * r:3 * r + 1] * o_c[rs] + gates[:, 3 * r + 1:3 * r + 2] * o_s[rs]
             + gates[:, 3 * r + 2:3 * r + 3] * o_w[rs])
        z = z_ref[:, r * NSA_DV:(r + 1) * NSA_DV]
        o_ref[:, r * NSA_DV:(r + 1) * NSA_DV] = (o * (z * (1.0 / (1.0 + jnp.exp(-z))))).astype(o_ref.dtype)


def _nsa_prompt_fast_kernel(q_ref, kc_ref, vc_ref, ks_ref, vs_ref, kw_ref, vw_ref, gl_ref, z_ref, o_ref,
                            *, tq, tk, seq):
    t0 = pl.program_id(2) * tq
    m_rows = NSA_REP * tq
    q = q_ref[0].reshape(m_rows, NSA_DKP)
    t_col = t0 + lax.broadcasted_iota(jnp.int32, (NSA_REP, tq, 1), 1).reshape(m_rows, 1)
    o_c, q_slc = _cmp_and_select(q, kc_ref[0, 0], vc_ref[0, 0], t0, tq, seq)
    kpos = lax.broadcasted_iota(jnp.int32, (1, tk), 1)
    kt_hi = (t0 + tq + tk - 1) // tk
    zero = jnp.zeros((m_rows, NSA_DVP), jnp.float32)

    def pv(p, v_ref, k0):
        return jnp.dot(p.astype(jnp.bfloat16), v_ref[0, pl.ds(k0, tk), :], preferred_element_type=jnp.float32)

    def slc_pair(j, acc):
        outs = []
        for h in range(2):
            kt = 2 * j + h
            k0 = pl.multiple_of(jnp.minimum(kt, kt_hi - 1) * tk, tk)
            s = _dot_nt(q_slc, ks_ref[0, pl.ds(k0, tk), :])
            outs.append((s, k0, ((k0 + kpos) <= t_col) & (kt < kt_hi)))
        ps = [jnp.where(ok, jnp.exp(s), 0.0) for s, _, ok in outs]
        return acc + pv(ps[0], vs_ref, outs[0][1]) + pv(ps[1], vs_ref, outs[1][1])

    acc = lax.fori_loop(0, (kt_hi + 1) // 2, slc_pair, zero)
    o_s = acc[:, 0:NSA_DV] / acc[:, NSA_DV:NSA_DV + 1]

    def win_tile(kt):
        k0 = pl.multiple_of(kt * tk, tk)
        dw = t_col - (k0 + kpos)
        return _dot_nt(q, kw_ref[0, pl.ds(k0, tk), :]), k0, (dw >= 0) & (dw < WINDOW)

    def win_body(kt, acc):
        s, k0, ok = win_tile(kt)
        return acc + pv(jnp.where(ok, jnp.exp(s), 0.0), vw_ref, k0)

    kt_lo = jnp.maximum(t0 - (WINDOW - 1), 0) // tk
    acc = lax.fori_loop(kt_lo, kt_hi - 2, win_body, zero)
    ta = win_tile(jnp.maximum(kt_hi - 2, 0))
    tb_ = win_tile(kt_hi - 1)
    pa = jnp.where(ta[2] & (kt_hi >= 2), jnp.exp(ta[0]), 0.0)
    pb = jnp.where(tb_[2], jnp.exp(tb_[0]), 0.0)
    acc = acc + pv(pa, vw_ref, ta[1]) + pv(pb, vw_ref, tb_[1])
    o_w = acc[:, 0:NSA_DV] / acc[:, NSA_DV:NSA_DV + 1]
    _gate_combine(gl_ref, z_ref, o_ref, o_c, o_s, o_w, tq)


def _flash_step(q, k, v, mask, carry):
    m, l, acc = carry
    s = _dot_nt(q, k)
    s = jnp.where(mask, s, _NEG)
    m_new = jnp.maximum(m, jnp.max(s, axis=1, keepdims=True))
    alpha = jnp.exp(m - m_new)
    p = jnp.exp(s - m_new)
    l = alpha * l + jnp.sum(p, axis=1, keepdims=True)
    acc = alpha * acc + jnp.dot(p.astype(jnp.bfloat16), v, preferred_element_type=jnp.float32)
    return m_new, l, acc


def _nsa_prompt_kernel(q_ref, kc_ref, vc_ref, ks_ref, vs_ref, kw_ref, vw_ref, gl_ref, z_ref, o_ref,
                       *, tq, tk, seq):
    t0 = pl.program_id(2) * tq
    m_rows = NSA_REP * tq
    q = q_ref[0].reshape(m_rows, NSA_DKP)
    t_col = t0 + lax.broadcasted_iota(jnp.int32, (NSA_REP, tq, 1), 1).reshape(m_rows, 1)
    o_c, q_slc = _cmp_and_select(q, kc_ref[0, 0], vc_ref[0, 0], t0, tq, seq)

    kpos = lax.broadcasted_iota(jnp.int32, (1, tk), 1)
    init = (jnp.full((m_rows, 1), _NEG, jnp.float32), jnp.zeros((m_rows, 1), jnp.float32),
            jnp.zeros((m_rows, NSA_DV), jnp.float32))

    def slc_body(kt, carry):
        k0 = pl.multiple_of(kt * tk, tk)
        mask = (k0 + kpos) <= t_col
        return _flash_step(q_slc, ks_ref[0, pl.ds(k0, tk), :], vs_ref[0, pl.ds(k0, tk), :], mask, carry)

    _, l_s, acc_s = lax.fori_loop(0, (t0 + tq + tk - 1) // tk, slc_body, init)

    def win_body(kt, carry):
        k0 = pl.multiple_of(kt * tk, tk)
        dw = t_col - (k0 + kpos)
        mask = (dw >= 0) & (dw < WINDOW)
        return _flash_step(q, kw_ref[0, pl.ds(k0, tk), :], vw_ref[0, pl.ds(k0, tk), :], mask, carry)

    kt_lo = jnp.maximum(t0 - (WINDOW - 1), 0) // tk
    _, l_w, acc_w = lax.fori_loop(kt_lo, (t0 + tq + tk - 1) // tk, win_body, init)
    _gate_combine(gl_ref, z_ref, o_ref, o_c, acc_s / l_s, acc_w / l_w, tq)


def nsa_prompt_attention(q_hm, kcmp, vcmp, ks_hm, vs_hm, kw_hm, vw_hm, gl, z, *, batch, tq, tk, fast):
    rows = z.shape[0]
    seq = rows // batch
    nq = seq // tq
    nseg = kcmp.shape[2]
    row_blk = lambda b, g, i: (b * nq + i, g)
    kv_blk = lambda b, g, i: (g, b, 0)
    dvb = NSA_DVP if fast else NSA_DV
    return pl.pallas_call(
        functools.partial(_nsa_prompt_fast_kernel if fast else _nsa_prompt_kernel, tq=tq, tk=tk, seq=seq),
        grid=(batch, NSA_KV, nq),
        in_specs=[pl.BlockSpec((1, NSA_REP, tq, NSA_DKP), lambda b, g, i: (g, 0, b * nq + i, 0)),
                  pl.BlockSpec((1, 1, nseg, NSA_DKP), lambda b, g, i: (b, g, 0, 0)),
                  pl.BlockSpec((1, 1, nseg, NSA_DV), lambda b, g, i: (b, g, 0, 0)),
                  pl.BlockSpec((1, seq, NSA_DKP), kv_blk),
                  pl.BlockSpec((1, seq, dvb), kv_blk),
                  pl.BlockSpec((1, seq, NSA_DKP), kv_blk),
                  pl.BlockSpec((1, seq, dvb), kv_blk),
                  pl.BlockSpec((tq, LANE), row_blk),
                  pl.BlockSpec((tq, NSA_REP * NSA_DV), row_blk)],
        out_specs=pl.BlockSpec((tq, NSA_REP * NSA_DV), row_blk),
        out_shape=jax.ShapeDtypeStruct((rows, NSA_O), jnp.bfloat16),
        compiler_params=_cparams("parallel", "parallel", "arbitrary"),
        name="nsa_prompt_attention_fast" if fast else "nsa_prompt_attention",
    )(q_hm, kcmp, vcmp, ks_hm, vs_hm, kw_hm, vw_hm, gl, z)


def _nsa_in_proj(x, norm_w, w_in):
    B, L, D = x.shape
    x2 = x.reshape(B * L, D)
    h = rmsnorm_bf16(x2, norm_w, tm=min(256, B * L))
    offs = np.cumsum((0,) + NSA_SPLITS)
    tm = _row_tile(B * L)

    wt = w_in.T

    def seg(i, tn, name):
        return matmul_wt(h, wt, tm=tm, tn=tn, n_off=int(offs[i]), n=int(offs[i + 1] - offs[i]), name=name)

    q = seg(0, _WT_TN, "nsa_q_proj")
    kv = matmul_wt(h, wt, tm=tm, tn=768, n_off=int(offs[1]), n=int(offs[7] - offs[1]), name="nsa_kv_proj")
    kc, vc, ks, vs, kw, vw = (kv[:, int(offs[i] - offs[1]):int(offs[i + 1] - offs[1])] for i in range(1, 7))
    wg = wt[offs[7]:offs[8]].reshape(NSA_KV, 3 * NSA_REP, D)
    wg = jnp.pad(wg, ((0, 0), (0, LANE - 3 * NSA_REP), (0, 0))).reshape(NSA_KV * LANE, D)
    gl = matmul_wt(h, wg, tm=tm, tn=NSA_KV * LANE, n_off=0, n=NSA_KV * LANE, name="nsa_gate_proj")
    z = matmul_wt(h, wt[offs[8]:], tm=tm, tn=_WT_TN, n_off=0, n=NSA_O, name="nsa_z_proj")
    return x2, q, kc, vc, ks, vs, kw, vw, gl, z


def _nsa_prompt(x, norm_w, w_in, q_norm, kn_c, kn_s, kn_w, pe_k, w1_k, w2_k, pe_v, w1_v, w2_v, w_out):
    B, T, D = x.shape
    x2, q, kc_r, vc_r, ks_raw, vs_r, kw_raw, vw_r, gl, z = _nsa_in_proj(x, norm_w, w_in)
    tp = min(512, T)
    q_hm = nsa_q_prep(q, q_norm, tq=tp, seq=T)
    ks_r, ks_hm = nsa_k_prep(ks_raw, kn_s, seq=T, tk=tp, onehot=True)
    kw_r, kw_hm = nsa_k_prep(kw_raw, kn_w, seq=T, tk=tp, onehot=False)
    vs_hm = nsa_v_prep(vs_r, tk=tp)
    vw_hm = nsa_v_prep(vw_r, tk=tp)
    npg = T // PAGE_ROWS
    table = (jnp.arange(B, dtype=jnp.int32)[:, None] * npg + jnp.arange(npg, dtype=jnp.int32)[None, :])
    seg3 = lambda a: a.reshape(B * T // CMP_STRIDE, CMP_STRIDE, a.shape[1])
    ka, kb = nsa_compress_partials(seg3(kc_r), table, pe_k, w1_k, pages=min(_CMP_PAGES, npg))
    va, vb = nsa_compress_partials(seg3(vc_r), table, pe_v, w1_v, pages=min(_CMP_PAGES, npg))
    kcmp = nsa_compress_finish(ka, kb, None, w2_k, kn_c, is_key=True, alibi_lanes=True)
    vcmp = nsa_compress_finish(va, vb, None, w2_v, jnp.zeros((NSA_DV,), jnp.float32), is_key=False, alibi_lanes=False)
    bound = jnp.max(jnp.abs(q_norm)) * jnp.maximum(jnp.max(jnp.abs(kn_s)), jnp.max(jnp.abs(kn_w))) * (NSA_DK ** 0.5)
    attend = lambda fast: functools.partial(nsa_prompt_attention, batch=B, tq=_NSA_TQ, tk=_NSA_TK, fast=fast)
    og = lax.cond(bound < _FAST_SCORE_BOUND, attend(True), attend(False),
                  q_hm, kcmp, vcmp, ks_hm, vs_hm, kw_hm, vw_hm, gl, z)
    y = matmul(og, w_out.astype(jnp.bfloat16), tm=_row_tile(B * T), tn=1024, residual=x2, name="nsa_out_proj")
    wl = min(WINDOW, T)
    r4 = lambda a, d: a.reshape(B, T, NSA_KV, d)
    return y.reshape(B, T, D), (r4(kc_r, NSA_DK), r4(vc_r, NSA_DV), r4(ks_r, NSA_DK), r4(vs_r, NSA_DV),
                                r4(kw_r, NSA_DK)[:, T - wl:], r4(vw_r, NSA_DV)[:, T - wl:])


_SMP_PAGES = 8
_SEL_PAD = 384


def _online_update(m_ref, l_ref, acc_ref, g, s, v):
    m_old = m_ref[g]
    m_new = jnp.maximum(m_old, jnp.max(s, axis=1, keepdims=True))
    alpha = jnp.exp(m_old - m_new)
    p = jnp.exp(s - m_new)
    l_ref[g] = alpha * l_ref[g] + jnp.sum(p, axis=1, keepdims=True)
    acc_ref[g] = alpha * acc_ref[g] + jnp.dot(p.astype(jnp.bfloat16), v.astype(jnp.bfloat16),
                                              preferred_element_type=jnp.float32)
    m_ref[g] = m_new


def _nsa_sample_kernel(tab_ref, *refs, past, steps, pages, nsb):
    kp_refs = refs[:pages]
    vp_refs = refs[pages:2 * pages]
    (q_ref, kc_ref, vc_ref, kn_ref, vn_ref, kwc_ref, vwc_ref, kwn_ref, vwn_ref, gl_ref, z_ref,
     o_ref, m_scr, l_scr, acc_scr, sel_scr, oc_scr) = refs[2 * pages:]
    i = pl.program_id(1)
    s_tok = o_ref.shape[0]
    rows = NSA_REP * s_tok
    ncmp = kc_ref.shape[2]
    r_col = lax.broadcasted_iota(jnp.int32, (NSA_REP, s_tok, 1), 0).reshape(rows, 1)
    t_col = past + lax.broadcasted_iota(jnp.int32, (NSA_REP, s_tok, 1), 1).reshape(rows, 1)
    t_tok = past + lax.broadcasted_iota(jnp.int32, (s_tok, 1), 0)

    def slope_col(g):
        return jnp.exp2((r_col + (g * NSA_REP + 1)).astype(jnp.float32) * (-8.0 / NSA_HEADS))

    def q_of(g):
        return q_ref[0, g * rows:(g + 1) * rows, :]

    def rep_rows(x):
        return jnp.concatenate([x] * NSA_REP, axis=0)

    @pl.when(i == 0)
    def _first():
        n_row = lax.broadcasted_iota(jnp.int32, (1, ncmp), 1)
        cvalid = (n_row * CMP_STRIDE + CMP_BLOCK - 1) <= t_col
        ci = lax.broadcasted_iota(jnp.int32, (ncmp, _SEL_PAD), 0) * CMP_STRIDE
        cj = lax.broadcasted_iota(jnp.int32, (ncmp, _SEL_PAD), 1) * SLC_BLOCK
        ov = jnp.maximum(jnp.minimum(ci + CMP_BLOCK, cj + SLC_BLOCK) - jnp.maximum(ci, cj), 0)
        c2s = (ov.astype(jnp.float32) * (1.0 / CMP_STRIDE)).astype(jnp.bfloat16)
        jb = lax.broadcasted_iota(jnp.int32, (s_tok, _SEL_PAD), 1)
        cur = t_tok // SLC_BLOCK
        forced = (jb == 0) | (jb == cur) | (jb == cur - 1)
        valid = (jb * SLC_BLOCK <= t_tok) & (jb < nsb)
        for g in range(NSA_KV):
            s = jnp.where(cvalid, _dot_nt(q_of(g), kc_ref[0, g]), _NEG)
            p = jnp.where(cvalid, jnp.exp(s - jnp.max(s, axis=1, keepdims=True)), 0.0)
            p = p / jnp.maximum(jnp.sum(p, axis=1, keepdims=True), 1e-30)
            oc_scr[g] = jnp.dot(p.astype(jnp.bfloat16), vc_ref[0, g], preferred_element_type=jnp.float32)
            imp = _split_dot(jnp.sum(p.reshape(NSA_REP, s_tok, ncmp), axis=0), c2s)
            work = jnp.where(forced, jnp.inf, jnp.where(valid, imp, -jnp.inf))
            sel = jnp.zeros((s_tok, _SEL_PAD), jnp.float32)
            for _ in range(min(SLC_TOPN, nsb)):
                mx = jnp.max(work, axis=1, keepdims=True)
                first = jnp.min(jnp.where(work == mx, jb, _SEL_PAD), axis=1, keepdims=True)
                hit = jb == first
                sel = jnp.where(hit, 1.0, sel)
                work = jnp.where(hit, -jnp.inf, work)
            sel_scr[g] = sel
            m_scr[g] = jnp.full((rows, 1), _NEG, jnp.float32)
            l_scr[g] = jnp.zeros((rows, 1), jnp.float32)
            acc_scr[g] = jnp.zeros((rows, NSA_DV), jnp.float32)

    def fold(score_of, v_of, nk, page0, causal):
        pos = page0 * PAGE_ROWS + lax.broadcasted_iota(jnp.int32, (1, nk), 1)
        bj = lax.broadcasted_iota(jnp.int32, (_SEL_PAD, nk), 0)
        bl = lax.broadcasted_iota(jnp.int32, (_SEL_PAD, nk), 1)
        expand = jnp.where(bj == (page0 * PAGE_ROWS + bl) // SLC_BLOCK, 1.0, 0.0).astype(jnp.bfloat16)
        for g in range(NSA_KV):
            selk = jnp.dot(sel_scr[g].astype(jnp.bfloat16), expand, preferred_element_type=jnp.float32)
            ok = rep_rows(selk) > 0.5
            if causal:
                ok = ok & (pos <= t_col)
            s = score_of(g, q_of(g)[:, 0:NSA_DK])
            s = s - slope_col(g) * (t_col - pos).astype(jnp.float32)
            _online_update(m_scr, l_scr, acc_scr, g, jnp.where(ok, s, _NEG), v_of(g))

    fold(lambda g, qg: _dot(qg, jnp.concatenate([r[0, g] for r in kp_refs], axis=1)),
         lambda g: jnp.concatenate([r[0, pl.ds(g, PAGE_ROWS, stride=NSA_KV), :] for r in vp_refs], axis=0),
         pages * PAGE_ROWS, i * pages, causal=False)

    @pl.when(i == steps - 1)
    def _last():
        fold(lambda g, qg: _dot_nt(qg, kn_ref[0, :, g * NSA_DK:(g + 1) * NSA_DK]),
             lambda g: vn_ref[0, :, g * NSA_DV:(g + 1) * NSA_DV], PAGE_ROWS, past // PAGE_ROWS, causal=True)
        nwc = kwc_ref.shape[3]
        posw = jnp.concatenate([past - nwc + lax.broadcasted_iota(jnp.int32, (1, nwc), 1),
                                past + lax.broadcasted_iota(jnp.int32, (1, PAGE_ROWS), 1)], axis=1)
        dw = t_col - posw
        wok = (dw >= 0) & (dw < WINDOW) & (posw >= 0)
        gates = 1.0 / (1.0 + jnp.exp(-gl_ref[...]))
        for g in range(NSA_KV):
            ksl = slice(g * NSA_DK, (g + 1) * NSA_DK)
            vsl = slice(g * NSA_DV, (g + 1) * NSA_DV)
            qg = q_of(g)[:, 0:NSA_DK]
            s = jnp.concatenate([_dot(qg, kwc_ref[0, g]), _dot_nt(qg, kwn_ref[0, :, ksl])], axis=1)
            s = jnp.where(wok, s - slope_col(g) * dw.astype(jnp.float32), _NEG)
            p = jnp.where(wok, jnp.exp(s - jnp.max(s, axis=1, keepdims=True)), 0.0)
            p = (p / jnp.maximum(jnp.sum(p, axis=1, keepdims=True), 1e-30)).astype(jnp.bfloat16)
            o_w = (jnp.dot(p[:, 0:nwc], vwc_ref[0, :, vsl].astype(jnp.bfloat16), preferred_element_type=jnp.float32)
                   + jnp.dot(p[:, nwc:], vwn_ref[0, :, vsl].astype(jnp.bfloat16), preferred_element_type=jnp.float32))
            o_s = acc_scr[g] / l_scr[g]
            o_c = oc_scr[g]
            for r in range(NSA_REP):
                rs = slice(r * s_tok, (r + 1) * s_tok)
                c0 = g * LANE + 3 * r
                o = (gates[:, c0:c0 + 1] * o_c[rs] + gates[:, c0 + 1:c0 + 2] * o_s[rs]
                     + gates[:, c0 + 2:c0 + 3] * o_w[rs])
                hs = slice((g * NSA_REP + r) * NSA_DV, (g * NSA_REP + r + 1) * NSA_DV)
                z = z_ref[:, hs]
                o_ref[:, hs] = o * (z * (1.0 / (1.0 + jnp.exp(-z))))


def nsa_sample_attention(q_rows, kcmp, vcmp, pool_k, pool_v, page_table, k_new, v_new, kw_cache, vw_cache,
                         kw_new, vw_new, gl, z, *, s_tok):
    batch, npg = page_table.shape
    pages = _SMP_PAGES
    steps = npg // pages
    past = npg * PAGE_ROWS
    nsb = -(-(past + s_tok) // SLC_BLOCK)
    rows = NSA_REP * s_tok
    ncmp = kcmp.shape[2]
    nwc = kw_cache.shape[3]
    kpage_spec = lambda k: pl.BlockSpec((1, NSA_KV, NSA_DK, PAGE_ROWS), lambda b, i, tab: (tab[b, i * pages + k], 0, 0, 0))
    vpage_spec = lambda k: pl.BlockSpec((1, PAGE_ROWS * NSA_KV, NSA_DV), lambda b, i, tab: (tab[b, i * pages + k], 0, 0))
    per_b = lambda shape: pl.BlockSpec((1,) + shape, lambda b, i, tab: (b,) + (0,) * len(shape))
    tok_blk = lambda w: pl.BlockSpec((s_tok, w), lambda b, i, tab: (b, 0))
    return pl.pallas_call(
        functools.partial(_nsa_sample_kernel, past=past, steps=steps, pages=pages, nsb=nsb),
        grid_spec=pltpu.PrefetchScalarGridSpec(
            num_scalar_prefetch=1,
            grid=(batch, steps),
            in_specs=[kpage_spec(k) for k in range(pages)] + [vpage_spec(k) for k in range(pages)]
            + [per_b((NSA_KV * rows, NSA_DKP)), per_b((NSA_KV, ncmp, NSA_DKP)), per_b((NSA_KV, ncmp, NSA_DV)),
               per_b((PAGE_ROWS, NSA_KW)), per_b((PAGE_ROWS, NSA_VW)),
               per_b((NSA_KV, NSA_DK, nwc)), per_b((nwc, NSA_VW)), per_b((PAGE_ROWS, NSA_KW)), per_b((PAGE_ROWS, NSA_VW)),
               tok_blk(NSA_KV * LANE), tok_blk(NSA_O)],
            out_specs=tok_blk(NSA_O),
            scratch_shapes=[pltpu.VMEM((NSA_KV, rows, 1), jnp.float32), pltpu.VMEM((NSA_KV, rows, 1), jnp.float32),
                            pltpu.VMEM((NSA_KV, rows, NSA_DV), jnp.float32),
                            pltpu.VMEM((NSA_KV, s_tok, _SEL_PAD), jnp.float32),
                            pltpu.VMEM((NSA_KV, rows, NSA_DV), jnp.float32)]),
        out_shape=jax.ShapeDtypeStruct((batch * s_tok, NSA_O), jnp.float32),
        compiler_params=_cparams("parallel", "arbitrary"),
        name="nsa_sample_attention",
    )(page_table, *([pool_k] * pages), *([pool_v] * pages), q_rows, kcmp, vcmp, k_new, v_new,
      kw_cache, vw_cache, kw_new, vw_new, gl, z)


def _nsa_sample(x, ck, cv, sk, sv, wk_buf, wv_buf, page_table, norm_w,
                w_in, q_norm, kn_c, kn_s, kn_w, pe_k, w1_k, w2_k, pe_v, w1_v, w2_v, w_out):
    B, S, D = x.shape
    x2, q, kc_r, vc_r, ks_raw, vs_r, kw_raw, vw_r, gl, z = _nsa_in_proj(x, norm_w, w_in)
    q_hm = nsa_q_prep(q, q_norm, tq=B * S, seq=S, pos0=page_table.shape[1] * PAGE_ROWS)
    q_rows = q_hm.reshape(NSA_KV, NSA_REP, B, S, NSA_DKP).transpose(2, 0, 1, 3, 4).reshape(B, NSA_KV * NSA_REP * S, NSA_DKP)
    ks_r, _ = nsa_k_prep(ks_raw, kn_s, seq=S, tk=B * S, onehot=False)
    kw_r, _ = nsa_k_prep(kw_raw, kn_w, seq=S, tk=B * S, onehot=False)
    as_page = lambda a: jnp.pad(a.reshape(B, S, -1), ((0, 0), (0, PAGE_ROWS - S), (0, 0)))
    seg3 = lambda a, w: a.reshape(-1, CMP_STRIDE, w)
    seg4 = lambda a: a.reshape(-1, CMP_STRIDE * NSA_KV, a.shape[3])
    ident = jnp.arange(B, dtype=jnp.int32)[:, None]
    rows_minor = lambda a: jnp.transpose(a, (0, 2, 3, 1))
    ka, kb = nsa_compress_partials(rows_minor(ck), page_table, pe_k, w1_k, pages=_CMP_PAGES, transposed=True)
    va, vb = nsa_compress_partials(seg4(cv), page_table, pe_v, w1_v, pages=_CMP_PAGES)
    _, kb_new = nsa_compress_partials(seg3(as_page(kc_r), NSA_KW), ident, pe_k, w1_k, pages=1)
    _, vb_new = nsa_compress_partials(seg3(as_page(vc_r), NSA_VW), ident, pe_v, w1_v, pages=1)
    kcmp = nsa_compress_finish(ka, kb, kb_new, w2_k, kn_c, is_key=True, alibi_lanes=True)
    vcmp = nsa_compress_finish(va, vb, vb_new, w2_v, jnp.zeros((NSA_DV,), jnp.float32), is_key=False, alibi_lanes=False)
    wl = wk_buf.shape[1]
    o = nsa_sample_attention(q_rows, kcmp, vcmp, rows_minor(sk), sv.reshape(sv.shape[0], PAGE_ROWS * NSA_KV, NSA_DV), page_table, as_page(ks_r), as_page(vs_r), rows_minor(wk_buf),
                             wv_buf.reshape(B, wl, NSA_VW), as_page(kw_r), as_page(vw_r), gl, z, s_tok=S)
    y = matmul(o.astype(jnp.bfloat16), w_out.astype(jnp.bfloat16), tm=B * S, tn=1024, residual=x2, name="nsa_out_proj")
    r4 = lambda a, d: a.reshape(B, S, NSA_KV, d)
    kw_all = jnp.concatenate([wk_buf, r4(kw_r, NSA_DK)], axis=1)
    vw_all = jnp.concatenate([wv_buf, r4(vw_r, NSA_DV)], axis=1)
    return y.reshape(B, S, D), (r4(kc_r, NSA_DK), r4(vc_r, NSA_DV), r4(ks_r, NSA_DK), r4(vs_r, NSA_DV),
                                kw_all[:, S:], vw_all[:, S:])


def kernel(x_prompt, x_sample, state_delta, state_conv, cache_cmp_k, cache_cmp_v, cache_slc_k, cache_slc_v, cache_win_k, cache_win_v, page_table, norm_dn, w_in_dn, conv_w_dn, a_log_dn, dt_bias_dn, out_norm_dn, w_out_dn, norm_nsa, w_in_nsa, q_norm_nsa, k_norm_cmp, k_norm_slc, k_norm_win, cmp_pe_k, cmp_w1_k, cmp_w2_k, cmp_pe_v, cmp_w1_v, cmp_w2_v, w_out_nsa):
    xp, xs = x_prompt, x_sample
    B = xp.shape[0]
    dw = (norm_dn[0], w_in_dn[0], conv_w_dn[0], a_log_dn[0], dt_bias_dn[0], out_norm_dn[0], w_out_dn[0])
    buf0 = jnp.zeros((B, DN_CONV - 1, conv_w_dn.shape[-1]), xp.dtype)
    st0 = jnp.zeros((B, DN_HEADS, DN_DK, DN_DV), jnp.float32)
    xp, p_dn = _deltanet_mixer(xp, buf0, st0, *dw)
    xs, s_dn = _deltanet_mixer(xs, state_conv[0], state_delta[0], *dw)
    nw = (norm_nsa[0], w_in_nsa[0], q_norm_nsa[0], k_norm_cmp[0], k_norm_slc[0], k_norm_win[0],
          cmp_pe_k[0], cmp_w1_k[0], cmp_w2_k[0], cmp_pe_v[0], cmp_w1_v[0], cmp_w2_v[0], w_out_nsa[0])
    xp, p_nsa = _nsa_prompt(xp, *nw)
    xs, s_nsa = _nsa_sample(xs, cache_cmp_k[0], cache_cmp_v[0], cache_slc_k[0], cache_slc_v[0],
                            cache_win_k[0], cache_win_v[0], page_table, *nw)
    return ((xp, xs, p_dn[0][None], p_dn[1][None]) + tuple(t[None] for t in p_nsa)
            + (s_dn[0][None], s_dn[1][None]) + tuple(t[None] for t in s_nsa))
```

```python
import functools

import jax
import jax.numpy as jnp
import numpy as np
from jax import lax
from jax.experimental import pallas as pl
from jax.experimental.pallas import tpu as pltpu

D_MODEL = 4096
EPS = 1e-6

DN_HEADS = 32
DN_DK = 128
DN_DV = 128
DN_CONV = 4
DN_CHUNK = 64

NSA_HEADS = 32
NSA_KV = 4
NSA_REP = NSA_HEADS // NSA_KV
NSA_DK = 192
NSA_DV = 128
CMP_BLOCK = 32
CMP_STRIDE = 16
CMP_RATIO = CMP_BLOCK // CMP_STRIDE
CMP_HIDDEN = 256
SLC_BLOCK = 64
SLC_TOPN = 16
WINDOW = 512
NSA_Q = NSA_HEADS * NSA_DK
NSA_KW = NSA_KV * NSA_DK
NSA_VW = NSA_KV * NSA_DV
NSA_O = NSA_HEADS * NSA_DV
NSA_SPLITS = (NSA_Q, NSA_KW, NSA_VW, NSA_KW, NSA_VW, NSA_KW, NSA_VW, 3 * NSA_HEADS, NSA_O)

VMEM_LIMIT_BYTES = 56 * 1024 * 1024
LANE = 128
_WT_TN = 512


def _cparams(*sem):
    return pltpu.CompilerParams(dimension_semantics=sem, vmem_limit_bytes=VMEM_LIMIT_BYTES)


def _rmsnorm_kernel(x_ref, w_ref, o_ref):
    x = x_ref[...]
    ms = jnp.mean(x * x, axis=-1, keepdims=True)
    o_ref[...] = (x * lax.rsqrt(ms + EPS) * w_ref[...]).astype(o_ref.dtype)


def rmsnorm_bf16(x, w, *, tm):
    m, d = x.shape
    return pl.pallas_call(
        _rmsnorm_kernel,
        grid=(m // tm,),
        in_specs=[pl.BlockSpec((tm, d), lambda i: (i, 0)), pl.BlockSpec((1, d), lambda i: (0, 0))],
        out_specs=pl.BlockSpec((tm, d), lambda i: (i, 0)),
        out_shape=jax.ShapeDtypeStruct((m, d), jnp.bfloat16),
        compiler_params=_cparams("parallel"),
        name="rmsnorm_bf16",
    )(x, w.reshape(1, d))


def _mm_kernel(a_ref, b_ref, o_ref):
    o_ref[...] = jnp.dot(a_ref[...], b_ref[...], preferred_element_type=jnp.float32).astype(o_ref.dtype)


def _mm_res_kernel(a_ref, b_ref, r_ref, o_ref):
    acc = jnp.dot(a_ref[...], b_ref[...], preferred_element_type=jnp.float32)
    o_ref[...] = r_ref[...] + acc


def matmul(a, b, *, tm, tn, residual=None, out_dtype=jnp.float32, name="matmul"):
    m, k = a.shape
    _, n = b.shape
    assert m % tm == 0 and n % tn == 0
    in_specs = [pl.BlockSpec((tm, k), lambda i, j: (i, 0)), pl.BlockSpec((k, tn), lambda i, j: (0, j))]
    args = [a, b]
    kern = _mm_kernel
    if residual is not None:
        in_specs.append(pl.BlockSpec((tm, tn), lambda i, j: (i, j)))
        args.append(residual)
        kern = _mm_res_kernel
    return pl.pallas_call(
        kern,
        grid=(m // tm, n // tn),
        in_specs=in_specs,
        out_specs=pl.BlockSpec((tm, tn), lambda i, j: (i, j)),
        out_shape=jax.ShapeDtypeStruct((m, n), out_dtype),
        compiler_params=_cparams("parallel", "parallel"),
        name=name,
    )(*args)


def _mm_wt_kernel(a_ref, w_ref, o_ref):
    o_ref[...] = lax.dot_general(a_ref[...], w_ref[...].astype(jnp.bfloat16), (((1,), (1,)), ((), ())),
                                 preferred_element_type=jnp.float32)


def matmul_wt(a, wt, *, tm, tn, n_off, n, name):
    m, k = a.shape
    assert m % tm == 0 and n % tn == 0 and n_off % tn == 0
    j0 = n_off // tn
    return pl.pallas_call(
        _mm_wt_kernel,
        grid=(m // tm, n // tn),
        in_specs=[pl.BlockSpec((tm, k), lambda i, j: (i, 0)), pl.BlockSpec((tn, k), lambda i, j: (j0 + j, 0))],
        out_specs=pl.BlockSpec((tm, tn), lambda i, j: (i, j)),
        out_shape=jax.ShapeDtypeStruct((m, n), jnp.float32),
        compiler_params=_cparams("parallel", "parallel"),
        name=name,
    )(a, wt)


def _row_tile(m):
    return 1024 if m % 1024 == 0 else m


def _dot(a, b):
    return jnp.dot(a.astype(jnp.bfloat16), b.astype(jnp.bfloat16), preferred_element_type=jnp.float32)


def _dot_nt(a, b):
    return lax.dot_general(a.astype(jnp.bfloat16), b.astype(jnp.bfloat16), (((1,), (1,)), ((), ())),
                           preferred_element_type=jnp.float32)


def _dot_tn(a, b):
    return lax.dot_general(a.astype(jnp.bfloat16), b.astype(jnp.bfloat16), (((0,), (0,)), ((), ())),
                           preferred_element_type=jnp.float32)


def _split_bf16(x):
    hi = x.astype(jnp.bfloat16)
    lo = (x - hi.astype(jnp.float32)).astype(jnp.bfloat16)
    return hi, lo


def _dn_gate_kernel(h_ref, wt_ref, alog_ref, dtb_ref, o_ref, *, heads):
    r = lax.dot_general(wt_ref[...], h_ref[...], (((1,), (1,)), ((), ())),
                        preferred_element_type=jnp.float32)
    b = r[:heads]
    a = r[heads:] + dtb_ref[...]
    softplus = jnp.maximum(a, 0.0) + jnp.log(1.0 + jnp.exp(-jnp.abs(a)))
    o_ref[0:heads, :] = 1.0 / (1.0 + jnp.exp(-b))
    o_ref[heads:2 * heads, :] = -jnp.exp(alog_ref[...]) * softplus


def dn_gates(h, w_ba_t, a_log, dt_bias, *, tm):
    m, d = h.shape
    heads = a_log.shape[0]
    return pl.pallas_call(
        functools.partial(_dn_gate_kernel, heads=heads),
        grid=(m // tm,),
        in_specs=[pl.BlockSpec((tm, d), lambda i: (i, 0)),
                  pl.BlockSpec((2 * heads, d), lambda i: (0, 0)),
                  pl.BlockSpec((heads, 1), lambda i: (0, 0)),
                  pl.BlockSpec((heads, 1), lambda i: (0, 0))],
        out_specs=pl.BlockSpec((2 * heads, tm), lambda i: (0, i)),
        out_shape=jax.ShapeDtypeStruct((2 * heads, m), jnp.float32),
        compiler_params=_cparams("parallel"),
        name="dn_gates",
    )(h, w_ba_t, a_log.reshape(heads, 1), dt_bias.reshape(heads, 1))


_DN_TAIL = 8
_DN_TB = 256
_DN_TMIN = 2 * DN_CHUNK
_DN_HB = 8


def _dn_chunk_prep(q, k, g_row, b_row):
    c = q.shape[0]
    ii = lax.broadcasted_iota(jnp.int32, (c, c), 0)
    jj = lax.broadcasted_iota(jnp.int32, (c, c), 1)
    lower, strict, eye = ii >= jj, ii > jj, ii == jj
    g_b = jnp.broadcast_to(g_row, (c, c))
    b_b = jnp.broadcast_to(b_row, (c, c))
    b_col = jnp.sum(jnp.where(eye, b_b, 0.0), axis=1, keepdims=True)
    g_col = jnp.sum(jnp.where(eye, g_b, 0.0), axis=1, keepdims=True)
    gc_col = jnp.sum(jnp.where(lower, g_b, 0.0), axis=1, keepdims=True)
    gc_row = jnp.sum(jnp.where(ii <= jj, g_col, 0.0), axis=0, keepdims=True)
    g_last = jnp.sum(g_row, axis=1, keepdims=True)
    decay = jnp.exp(jnp.where(lower, gc_col - gc_row, -jnp.inf))
    kk = _dot_nt(k, k)
    qk = _dot_nt(q, k)
    lmat = jnp.where(strict, kk * decay, 0.0) * b_col
    attn = qk * decay
    return dict(x=-lmat, p=eye.astype(jnp.float32) - lmat, attn=attn, b_row=b_row,
                e_row=jnp.exp(gc_row), e_col=jnp.exp(gc_col), c_col=jnp.exp(g_last - gc_col),
                e_last=jnp.exp(g_last))


def _dn_inverse_levels(items, c):
    m = 2
    while m < c:
        for it in items:
            it["x"] = _dot(it["x"], it["x"])
        for it in items:
            it["p"] = it["p"] + _dot(it["p"], it["x"])
        m *= 2


def _dn_kernel(q_ref, k_ref, v_ref, z_ref, beta_ref, g_ref, cw_ref, cs_ref, s0_ref, onw_ref,
               o_ref, s_out_ref, xpad, s_scr, *, tb, hb, live):
    n = pl.program_id(2)
    c = DN_CHUNK
    d = DN_DK
    nc = live

    @pl.when(n == 0)
    def _init():
        xpad[:, 0:_DN_TAIL, :] = cs_ref[0]
        s_scr[...] = s0_ref[0]

    def conv(i, ref):
        xpad[i, _DN_TAIL:_DN_TAIL + tb, :] = ref[...]
        acc = None
        for j in range(DN_CONV):
            off = _DN_TAIL - (DN_CONV - 1) + j
            term = xpad[i, off:off + tb, :] * cw_ref[i, j:j + 1, :]
            acc = term if acc is None else acc + term
        xpad[i, 0:_DN_TAIL, :] = xpad[i, tb:tb + _DN_TAIL, :]
        return acc * (1.0 / (1.0 + jnp.exp(-acc)))

    qc, kc, vc = conv(0, q_ref), conv(1, k_ref), conv(2, v_ref)
    items = []
    for hh in range(hb):
        sl = slice(hh * d, (hh + 1) * d)
        q = qc[:, sl]
        k = kc[:, sl]
        q = q * (lax.rsqrt(jnp.sum(q * q, axis=-1, keepdims=True) + EPS) * (d ** -0.5))
        k = k * lax.rsqrt(jnp.sum(k * k, axis=-1, keepdims=True) + EPS)
        g_all = g_ref[hh]
        b_all = beta_ref[hh]
        for ci in range(nc):
            rs = slice(ci * c, (ci + 1) * c)
            it = _dn_chunk_prep(q[rs], k[rs], g_all[:, rs], b_all[:, rs])
            it.update(q=q[rs], k=k[rs], v=vc[rs, sl])
            items.append(it)
    _dn_inverse_levels(items, c)
    for it in items:
        inv_b = it["p"] * it["b_row"]
        it["u"] = _dot(inv_b, it["v"])
        it["w"] = _dot(inv_b * it["e_row"], it["k"])
    states = [s_scr[hh] for hh in range(hb)]
    outs = [[None] * nc for _ in range(hb)]
    for ci in range(nc):
        for hh in range(hb):
            it = items[hh * nc + ci]
            s = states[hh]
            v_new = it["u"] - _dot(it["w"], s)
            outs[hh][ci] = it["e_col"] * _dot(it["q"], s) + _dot(it["attn"], v_new)
            states[hh] = s * it["e_last"] + _dot_tn(it["k"] * it["c_col"], v_new)
    for hh in range(hb):
        sl = slice(hh * d, (hh + 1) * d)
        s_scr[hh] = states[hh]
        o = jnp.concatenate(outs[hh], axis=0) if nc > 1 else outs[hh][0]
        o = o * lax.rsqrt(jnp.mean(o * o, axis=-1, keepdims=True) + EPS) * onw_ref[...]
        z = z_ref[0:nc * c, sl]
        o_ref[0:nc * c, sl] = (o * (z * (1.0 / (1.0 + jnp.exp(-z))))).astype(o_ref.dtype)
        if nc * c < tb:
            o_ref[nc * c:tb, sl] = jnp.zeros((tb - nc * c, d), o_ref.dtype)

    @pl.when(n == pl.num_programs(2) - 1)
    def _fin():
        s_out_ref[0] = s_scr[...]


def dn_delta(proj, gates, conv_w, conv_state, s0, out_norm, *, batch, tb, hb, live):
    rows, width = proj.shape
    d = DN_DK
    heads = width // (4 * d)
    t = rows // batch
    nt = t // tb
    hg = heads // hb
    wd = hb * d
    row_map = lambda off: (lambda b, h, n: (b * nt + n, off * hg + h))
    gate_map = lambda off: (lambda b, h, n: (off * hg + h, 0, b * nt + n))
    return pl.pallas_call(
        functools.partial(_dn_kernel, tb=tb, hb=hb, live=live),
        grid=(batch, hg, nt),
        in_specs=[pl.BlockSpec((tb, wd), row_map(0)),
                  pl.BlockSpec((tb, wd), row_map(1)),
                  pl.BlockSpec((tb, wd), row_map(2)),
                  pl.BlockSpec((tb, wd), row_map(3)),
                  pl.BlockSpec((hb, 1, tb), gate_map(0)),
                  pl.BlockSpec((hb, 1, tb), gate_map(1)),
                  pl.BlockSpec((3, DN_CONV, wd), lambda b, h, n: (0, 0, h)),
                  pl.BlockSpec((1, 3, _DN_TAIL, wd), lambda b, h, n: (b, 0, 0, h)),
                  pl.BlockSpec((1, hb, d, d), lambda b, h, n: (b, h, 0, 0)),
                  pl.BlockSpec((1, d), lambda b, h, n: (0, 0))],
        out_specs=[pl.BlockSpec((tb, wd), lambda b, h, n: (b * nt + n, h)),
                   pl.BlockSpec((1, hb, d, d), lambda b, h, n: (b, h, 0, 0))],
        out_shape=[jax.ShapeDtypeStruct((rows, heads * d), jnp.bfloat16),
                   jax.ShapeDtypeStruct(s0.shape, jnp.float32)],
        scratch_shapes=[pltpu.VMEM((3, tb + _DN_TAIL, wd), jnp.float32),
                        pltpu.VMEM((hb, d, d), jnp.float32)],
        compiler_params=_cparams("parallel", "parallel", "arbitrary"),
        name="dn_delta",
    )(proj, proj, proj, proj, gates, gates, conv_w, conv_state, s0, out_norm.reshape(1, d))


def _deltanet_mixer(x, conv_buf, s0, norm_w, w_in, conv_w, a_log, dt_bias, out_norm, w_out):
    B, L, D = x.shape
    H = a_log.shape[0]
    cdim = conv_w.shape[1]
    hd = cdim // 3
    x2 = x.reshape(B * L, D)
    h = rmsnorm_bf16(x2, norm_w, tm=min(256, B * L))
    wt = w_in.T
    w_ba_t = wt[cdim + hd:].astype(jnp.bfloat16)
    proj = matmul_wt(h, wt, tm=_row_tile(B * L), tn=_WT_TN, n_off=0, n=cdim + hd, name="dn_in_proj")
    gates = dn_gates(h, w_ba_t, a_log, dt_bias, tm=_row_tile(B * L))
    new_buf = proj.reshape(B, L, -1)[:, L - (DN_CONV - 1):, :cdim]
    T = -(-L // _DN_TMIN) * _DN_TMIN
    if T != L:
        proj = jnp.pad(proj.reshape(B, L, -1), ((0, 0), (0, T - L), (0, 0))).reshape(B * T, -1)
        gates = jnp.pad(gates.reshape(2 * H, B, L), ((0, 0), (0, 0), (0, T - L))).reshape(2 * H, B * T)
    cw = conv_w.reshape(DN_CONV, 3, hd).transpose(1, 0, 2)
    cs = conv_buf.reshape(B, DN_CONV - 1, 3, hd).transpose(0, 2, 1, 3)
    cs = jnp.pad(cs, ((0, 0), (0, 0), (_DN_TAIL - (DN_CONV - 1), 0), (0, 0)))
    tb = min(_DN_TB, T)
    live = tb // DN_CHUNK if T == L else -(-L // DN_CHUNK)
    assert T == L or T == tb
    o, s_new = dn_delta(proj, gates.reshape(2 * H, 1, B * T), cw, cs, s0.astype(jnp.float32), out_norm,
                        batch=B, tb=tb, hb=_DN_HB, live=live)
    if T != L:
        o = o.reshape(B, T, -1)[:, :L].reshape(B * L, -1)
    y = matmul(o, w_out.astype(jnp.bfloat16), tm=_row_tile(B * L), tn=1024, residual=x2, name="dn_out_proj")
    return y.reshape(B, L, D), (s_new, new_buf)


def _alibi_slopes():
    hh = jnp.arange(1, NSA_HEADS + 1, dtype=jnp.float32)
    return jnp.exp2(-8.0 * hh / NSA_HEADS).reshape(NSA_KV, NSA_REP)


PAGE_ROWS = 128
NSA_DKP = 256
_POS_LANE = NSA_DK
_SEL_LANE = NSA_DK + 4
_SEL_LANES = 32
_ROW_LANE = _SEL_LANE + _SEL_LANES
NSA_DVP = 256
_FAST_SCORE_BOUND = 40.0
_MASK_BIG = 2.0 ** 100
_NEG = -1e30
_NSA_TQ = 128
_NSA_TK = 256
_CMP_PAGES = 16


def _slope_table():
    s = _alibi_slopes()
    hi = s.astype(jnp.bfloat16).astype(jnp.float32)
    lo = (s - hi).astype(jnp.bfloat16).astype(jnp.float32)
    tab = jnp.stack([64.0 * hi, hi, 64.0 * lo, lo], axis=-1)
    return jnp.pad(tab, ((0, 0), (0, 0), (0, NSA_DKP - NSA_DK - 4)))


_QP_HEADS = 4


def _q_proj_prep_kernel(a_ref, w_ref, nw_ref, tab_ref, o_ref, *, seq, pos0):
    tm = a_ref.shape[0]
    npad = NSA_DKP - NSA_DK
    acc = lax.dot_general(a_ref[...], w_ref[...].astype(jnp.bfloat16), (((1,), (1,)), ((), ())),
                          preferred_element_type=jnp.float32)
    t = (pos0 + (pl.program_id(0) * tm) % seq + lax.broadcasted_iota(jnp.int32, (tm, npad), 0)).astype(jnp.float32)
    lane = lax.broadcasted_iota(jnp.int32, (tm, npad), 1)
    for r in range(_QP_HEADS):
        x = acc[:, r * NSA_DK:(r + 1) * NSA_DK]
        y = x * lax.rsqrt(jnp.mean(x * x, axis=-1, keepdims=True) + EPS) * (nw_ref[...] * (NSA_DK ** -0.5))
        o_ref[0, r, :, 0:NSA_DK] = y.astype(o_ref.dtype)
        tab = jnp.broadcast_to(tab_ref[0, r:r + 1, :], (tm, npad))
        slope = tab[:, 1:2] + tab[:, 3:4]
        extra = jnp.where(lane == _ROW_LANE - NSA_DK, -slope * t, tab)
        o_ref[0, r, :, NSA_DK:NSA_DKP] = extra.astype(o_ref.dtype)


def nsa_q_proj_prep(h, wt, q_norm, *, tm, seq, pos0=0):
    rows, k = h.shape
    per_g = NSA_REP // _QP_HEADS
    tn = _QP_HEADS * NSA_DK
    return pl.pallas_call(
        functools.partial(_q_proj_prep_kernel, seq=seq, pos0=pos0),
        grid=(rows // tm, NSA_HEADS // _QP_HEADS),
        in_specs=[pl.BlockSpec((tm, k), lambda i, j: (i, 0)),
                  pl.BlockSpec((tn, k), lambda i, j: (j, 0)),
                  pl.BlockSpec((1, NSA_DK), lambda i, j: (0, 0)),
                  pl.BlockSpec((1, _QP_HEADS, NSA_DKP - NSA_DK), lambda i, j: (j, 0, 0))],
        out_specs=pl.BlockSpec((1, _QP_HEADS, tm, NSA_DKP), lambda i, j: (j // per_g, j % per_g, i, 0)),
        out_shape=jax.ShapeDtypeStruct((NSA_KV, NSA_REP, rows, NSA_DKP), jnp.bfloat16),
        compiler_params=_cparams("parallel", "parallel"),
        name="nsa_q_proj_prep",
    )(h, wt, q_norm.reshape(1, NSA_DK), _slope_table().reshape(NSA_HEADS // _QP_HEADS, _QP_HEADS, NSA_DKP - NSA_DK))


def _k_prep_kernel(k_ref, w_ref, on_ref, oh_ref, *, seq, onehot):
    tk = k_ref.shape[0]
    pos = (pl.program_id(0) * tk) % seq + lax.broadcasted_iota(jnp.int32, (tk, NSA_DKP - NSA_DK), 0)
    lane = lax.broadcasted_iota(jnp.int32, (tk, NSA_DKP - NSA_DK), 1)
    blk, off = pos // SLC_BLOCK, pos % SLC_BLOCK
    extra = jnp.where((lane == 0) | (lane == 2), blk, jnp.where((lane == 1) | (lane == 3), off, 0))
    if onehot:
        extra = jnp.where((lane >= 4) & (lane < 4 + _SEL_LANES) & (lane - 4 == blk), 1, extra)
    extra = jnp.where(lane == _ROW_LANE - NSA_DK, 1, extra)
    extra = extra.astype(jnp.float32)
    for g in range(NSA_KV):
        x = k_ref[:, g * NSA_DK:(g + 1) * NSA_DK]
        x = x * lax.rsqrt(jnp.mean(x * x, axis=-1, keepdims=True) + EPS) * w_ref[...]
        on_ref[:, g * NSA_DK:(g + 1) * NSA_DK] = x
        oh_ref[g, :, 0:NSA_DK] = x.astype(oh_ref.dtype)
        oh_ref[g, :, NSA_DK:NSA_DKP] = extra.astype(oh_ref.dtype)


def nsa_k_prep(k_raw, k_norm, *, seq, tk, onehot):
    rows = k_raw.shape[0]
    return pl.pallas_call(
        functools.partial(_k_prep_kernel, seq=seq, onehot=onehot),
        grid=(rows // tk,),
        in_specs=[pl.BlockSpec((tk, NSA_KW), lambda i: (i, 0)), pl.BlockSpec((1, NSA_DK), lambda i: (0, 0))],
        out_specs=[pl.BlockSpec((tk, NSA_KW), lambda i: (i, 0)),
                   pl.BlockSpec((NSA_KV, tk, NSA_DKP), lambda i: (0, i, 0))],
        out_shape=[jax.ShapeDtypeStruct((rows, NSA_KW), jnp.float32),
                   jax.ShapeDtypeStruct((NSA_KV, rows, NSA_DKP), jnp.bfloat16)],
        compiler_params=_cparams("parallel"),
        name="nsa_k_prep",
    )(k_raw, k_norm.reshape(1, NSA_DK))


def _v_prep_kernel(v_ref, o_ref):
    tk = v_ref.shape[0]
    ones_col = jnp.where(lax.broadcasted_iota(jnp.int32, (tk, NSA_DVP - NSA_DV), 1) == 0, 1.0, 0.0)
    for g in range(NSA_KV):
        o_ref[g, :, 0:NSA_DV] = v_ref[:, g * NSA_DV:(g + 1) * NSA_DV].astype(o_ref.dtype)
        o_ref[g, :, NSA_DV:NSA_DVP] = ones_col.astype(o_ref.dtype)


def nsa_v_prep(v_raw, *, tk):
    rows = v_raw.shape[0]
    return pl.pallas_call(
        _v_prep_kernel,
        grid=(rows // tk,),
        in_specs=[pl.BlockSpec((tk, NSA_VW), lambda i: (i, 0))],
        out_specs=pl.BlockSpec((NSA_KV, tk, NSA_DVP), lambda i: (0, i, 0)),
        out_shape=jax.ShapeDtypeStruct((NSA_KV, rows, NSA_DVP), jnp.bfloat16),
        compiler_params=_cparams("parallel"),
        name="nsa_v_prep",
    )(v_raw)


def _compress_partials_kernel(tab_ref, *refs, d, pages, transposed):
    x_refs, (pe_ref, w1_ref, perm_ref, a_ref, b_ref) = refs[:pages], refs[pages:]
    seg = PAGE_ROWS // CMP_STRIDE
    acc_a = acc_b = None
    if transposed:
        rows_of = [[_dot_nt(perm_ref[...], r[0, g]) for r in x_refs] for g in range(NSA_KV)]
        take = lambda ri, l, g: rows_of[g][ri][l * seg:(l + 1) * seg, :]
    elif x_refs[0].shape[1] == CMP_STRIDE * NSA_KV:
        take = lambda ri, l, g: x_refs[ri][:, l * NSA_KV + g, :]
    else:
        take = lambda ri, l, g: x_refs[ri][:, l, g * d:(g + 1) * d]

    for l in range(CMP_STRIDE):
        xg = jnp.concatenate([take(ri, l, g) for g in range(NSA_KV) for ri in range(pages)], axis=0)
        ta = jnp.dot((xg + pe_ref[l:l + 1, :]).astype(jnp.bfloat16), w1_ref[l], preferred_element_type=jnp.float32)
        tb = jnp.dot((xg + pe_ref[CMP_STRIDE + l:CMP_STRIDE + l + 1, :]).astype(jnp.bfloat16),
                     w1_ref[CMP_STRIDE + l], preferred_element_type=jnp.float32)
        acc_a = ta if acc_a is None else acc_a + ta
        acc_b = tb if acc_b is None else acc_b + tb
    n = pages * seg
    for g in range(NSA_KV):
        a_ref[0, g] = acc_a[g * n:(g + 1) * n]
        b_ref[0, g] = acc_b[g * n:(g + 1) * n]


def nsa_compress_partials(pool3, table, pe, w1, *, pages, transposed=False):
    d = pe.shape[1]
    batch, npg = table.shape
    seg = PAGE_ROWS // CMP_STRIDE
    blk = ((1,) if transposed else (seg,)) + tuple(pool3.shape[1:])
    x_spec = lambda k: pl.BlockSpec(blk, lambda b, i, tab: (tab[b, i * pages + k],) + (0,) * (len(blk) - 1))
    out_spec = pl.BlockSpec((1, NSA_KV, pages * seg, CMP_HIDDEN), lambda b, i, tab: (b, 0, i, 0))
    out_sds = jax.ShapeDtypeStruct((batch, NSA_KV, npg * seg, CMP_HIDDEN), jnp.float32)
    i = jnp.arange(PAGE_ROWS)
    perm = (i[None, :] == (i % seg)[:, None] * CMP_STRIDE + (i // seg)[:, None]).astype(jnp.bfloat16)
    return pl.pallas_call(
        functools.partial(_compress_partials_kernel, d=d, pages=pages, transposed=transposed),
        grid_spec=pltpu.PrefetchScalarGridSpec(
            num_scalar_prefetch=1,
            grid=(batch, npg // pages),
            in_specs=[x_spec(k) for k in range(pages)]
            + [pl.BlockSpec((CMP_BLOCK, d), lambda b, i, tab: (0, 0)),
               pl.BlockSpec((CMP_BLOCK, d, CMP_HIDDEN), lambda b, i, tab: (0, 0, 0)),
               pl.BlockSpec((PAGE_ROWS, PAGE_ROWS), lambda b, i, tab: (0, 0))],
            out_specs=[out_spec, out_spec]),
        out_shape=[out_sds, out_sds],
        compiler_params=_cparams("parallel", "arbitrary"),
        name="nsa_compress_partials",
    )(table, *([pool3] * pages), pe, w1.astype(jnp.bfloat16), perm)


def _cmp_extra_lanes(nrows, width):
    n = lax.broadcasted_iota(jnp.int32, (nrows, width), 0)
    lane = lax.broadcasted_iota(jnp.int32, (nrows, width), 1)
    ec = n * CMP_STRIDE + CMP_BLOCK - 1
    blk, off = ec // SLC_BLOCK, ec % SLC_BLOCK
    pl_ = lane - _POS_LANE
    return jnp.where((pl_ == 0) | (pl_ == 2), blk, jnp.where((pl_ == 1) | (pl_ == 3), off, 0)).astype(jnp.float32)


def _compress_finish_kernel(a_ref, b_ref, bn_ref, w2_ref, kn_ref, o_ref, *, is_key, has_new, alibi_lanes):
    ns = a_ref.shape[2]
    row = lax.broadcasted_iota(jnp.int32, (ns, 1), 0)
    for g in range(NSA_KV):
        nxt = pltpu.roll(b_ref[0, g], ns - 1, 0)
        if has_new:
            nxt = jnp.where(row == ns - 1, bn_ref[0, g, 0:1, :], nxt)
        hid_pre = a_ref[0, g] + nxt
        hid = hid_pre * (1.0 / (1.0 + jnp.exp(-hid_pre)))
        out = jnp.dot(hid.astype(jnp.bfloat16), w2_ref[...], preferred_element_type=jnp.float32)
        if is_key:
            ms = jnp.sum(out * out, axis=-1, keepdims=True) * (1.0 / NSA_DK)
            out = out * lax.rsqrt(ms + EPS) * kn_ref[...]
            if alibi_lanes:
                out = out + _cmp_extra_lanes(ns, out.shape[1])
        o_ref[0, g] = out.astype(o_ref.dtype)


def nsa_compress_finish(part_a, part_b, part_b_new, w2, kn, *, is_key, alibi_lanes):
    batch, _, ns, _ = part_a.shape
    d = w2.shape[1]
    dp = NSA_DKP if is_key else d
    has_new = part_b_new is not None
    if not has_new:
        part_b_new = jnp.zeros((batch, NSA_KV, 8, CMP_HIDDEN), jnp.float32)
    w2p = jnp.pad(w2, ((0, 0), (0, dp - d))).astype(jnp.bfloat16)
    knp = jnp.pad(kn, (0, dp - d)).reshape(1, dp)
    blk = lambda n: pl.BlockSpec((1, NSA_KV, n, CMP_HIDDEN), lambda b: (b, 0, 0, 0))
    return pl.pallas_call(
        functools.partial(_compress_finish_kernel, is_key=is_key, has_new=has_new, alibi_lanes=alibi_lanes),
        grid=(batch,),
        in_specs=[blk(ns), blk(ns), blk(part_b_new.shape[2]),
                  pl.BlockSpec((CMP_HIDDEN, dp), lambda b: (0, 0)),
                  pl.BlockSpec((1, dp), lambda b: (0, 0))],
        out_specs=pl.BlockSpec((1, NSA_KV, ns, dp), lambda b: (b, 0, 0, 0)),
        out_shape=jax.ShapeDtypeStruct((batch, NSA_KV, ns, dp), jnp.bfloat16),
        compiler_params=_cparams("parallel"),
        name="nsa_compress_finish_k" if is_key else "nsa_compress_finish_v",
    )(part_a, part_b, part_b_new, w2p, knp)


def _split_dot(x, m_bf16):
    hi, lo = _split_bf16(x)
    f = functools.partial(jnp.dot, preferred_element_type=jnp.float32)
    return f(hi, m_bf16) + f(lo, m_bf16)


def _topn_mask_t(imp_t, t_row, nsb, topn):
    jb = lax.broadcasted_iota(jnp.int32, imp_t.shape, 0)
    cur = t_row // SLC_BLOCK
    forced = (jb == 0) | (jb == cur) | (jb == cur - 1)
    valid = (jb * SLC_BLOCK <= t_row) & (jb < nsb)
    impm = jnp.where(forced, jnp.inf, jnp.where(valid, imp_t, -jnp.inf))
    rank = jnp.zeros(imp_t.shape, jnp.float32)
    for k in range(nsb):
        row = impm[k:k + 1, :]
        beats = (row > impm) | ((row == impm) & (k < jb))
        rank = rank + jnp.where(beats, 1.0, 0.0)
    return (rank < topn) & (jb < nsb)


def _cmp_and_select(q, kc, vc, t0, tq, seq):
    rep = NSA_REP
    m_rows = rep * tq
    nc = seq // CMP_STRIDE - CMP_RATIO + 1
    nseg = kc.shape[0]
    nsb = seq // SLC_BLOCK
    t_col = t0 + lax.broadcasted_iota(jnp.int32, (rep, tq, 1), 1).reshape(m_rows, 1)
    t_row = t0 + lax.broadcasted_iota(jnp.int32, (1, tq), 1)
    n_row = lax.broadcasted_iota(jnp.int32, (1, nseg), 1)
    cvalid = (n_row * CMP_STRIDE + CMP_BLOCK - 1 <= t_col) & (n_row < nc)
    s = jnp.where(cvalid, _dot_nt(q, kc), _NEG)
    p = jnp.where(cvalid, jnp.exp(s - jnp.max(s, axis=1, keepdims=True)), 0.0)
    p = p / jnp.maximum(jnp.sum(p, axis=1, keepdims=True), 1e-30)
    o_c = jnp.dot(p.astype(jnp.bfloat16), vc, preferred_element_type=jnp.float32)
    psum = jnp.sum(p.reshape(rep, tq, nseg), axis=0)
    cj = lax.broadcasted_iota(jnp.int32, (_SEL_LANES, nseg), 0) * SLC_BLOCK
    ci = lax.broadcasted_iota(jnp.int32, (_SEL_LANES, nseg), 1) * CMP_STRIDE
    ov = jnp.maximum(jnp.minimum(ci + CMP_BLOCK, cj + SLC_BLOCK) - jnp.maximum(ci, cj), 0)
    c2s_t = (ov.astype(jnp.float32) * (1.0 / CMP_STRIDE)).astype(jnp.bfloat16)
    p_hi, p_lo = _split_bf16(psum)
    imp_t = _dot_nt(c2s_t, p_hi) + _dot_nt(c2s_t, p_lo)
    sel_t = _topn_mask_t(imp_t, t_row, nsb, min(SLC_TOPN, nsb))
    pen_t = jnp.where(sel_t, 0.0, -_MASK_BIG).astype(jnp.bfloat16)
    pi = lax.broadcasted_iota(jnp.int32, (_SEL_LANES, NSA_DKP), 0)
    pj = lax.broadcasted_iota(jnp.int32, (_SEL_LANES, NSA_DKP), 1)
    place = jnp.where(pj == pi + _SEL_LANE, 1.0, 0.0).astype(jnp.bfloat16)
    q_add = _dot_tn(pen_t, place).astype(jnp.bfloat16)
    q_slc = (q.reshape(rep, tq, NSA_DKP) + q_add[None]).reshape(m_rows, NSA_DKP)
    return o_c, q_slc


def _gate_combine(gl_ref, z_ref, o_ref, o_c, o_s, o_w, tq):
    gates = 1.0 / (1.0 + jnp.exp(-gl_ref[...]))
    for r in range(NSA_REP):
        rs = slice(r * tq, (r + 1) * tq)
        o = (gates[:, 3 * r:3 * r + 1] * o_c[rs] + gates[:, 3 * r + 1:3 * r + 2] * o_s[rs]
             + gates[:, 3 * r + 2:3 * r + 3] * o_w[rs])
        z = z_ref[:, r * NSA_DV:(r + 1) * NSA_DV]
        o_ref[:, r * NSA_DV:(r + 1) * NSA_DV] = (o * (z * (1.0 / (1.0 + jnp.exp(-z))))).astype(o_ref.dtype)


def _nsa_prompt_fast_kernel(q_ref, kc_ref, vc_ref, ks_ref, vs_ref, kw_ref, vw_ref, gl_ref, z_ref, o_ref,
                            *, tq, tk, seq):
    t0 = pl.program_id(2) * tq
    m_rows = NSA_REP * tq
    q = q_ref[0].reshape(m_rows, NSA_DKP)
    t_col = t0 + lax.broadcasted_iota(jnp.int32, (NSA_REP, tq, 1), 1).reshape(m_rows, 1)
    o_c, q_slc = _cmp_and_select(q, kc_ref[0, 0], vc_ref[0, 0], t0, tq, seq)
    kpos = lax.broadcasted_iota(jnp.int32, (1, tk), 1)
    kt_hi = (t0 + tq + tk - 1) // tk
    zero = jnp.zeros((m_rows, NSA_DVP), jnp.float32)

    def pv(p, v_ref, k0):
        return jnp.dot(p.astype(jnp.bfloat16), v_ref[0, pl.ds(k0, tk), :], preferred_element_type=jnp.float32)

    def slc_pair(j, acc):
        outs = []
        for h in range(2):
            kt = 2 * j + h
            k0 = pl.multiple_of(jnp.minimum(kt, kt_hi - 1) * tk, tk)
            s = _dot_nt(q_slc, ks_ref[0, pl.ds(k0, tk), :])
            outs.append((s, k0, ((k0 + kpos) <= t_col) & (kt < kt_hi)))
        ps = [jnp.where(ok, jnp.exp(s), 0.0) for s, _, ok in outs]
        return acc + pv(ps[0], vs_ref, outs[0][1]) + pv(ps[1], vs_ref, outs[1][1])

    acc = lax.fori_loop(0, (kt_hi + 1) // 2, slc_pair, zero)
    o_s = acc[:, 0:NSA_DV] / acc[:, NSA_DV:NSA_DV + 1]

    def win_tile(kt):
        k0 = pl.multiple_of(kt * tk, tk)
        dw = t_col - (k0 + kpos)
        return _dot_nt(q, kw_ref[0, pl.ds(k0, tk), :]), k0, (dw >= 0) & (dw < WINDOW)

    def win_body(kt, acc):
        s, k0, ok = win_tile(kt)
        return acc + pv(jnp.where(ok, jnp.exp(s), 0.0), vw_ref, k0)

    kt_lo = jnp.maximum(t0 - (WINDOW - 1), 0) // tk
    acc = lax.fori_loop(kt_lo, kt_hi - 2, win_body, zero)
    ta = win_tile(jnp.maximum(kt_hi - 2, 0))
    tb_ = win_tile(kt_hi - 1)
    pa = jnp.where(ta[2] & (kt_hi >= 2), jnp.exp(ta[0]), 0.0)
    pb = jnp.where(tb_[2], jnp.exp(tb_[0]), 0.0)
    acc = acc + pv(pa, vw_ref, ta[1]) + pv(pb, vw_ref, tb_[1])
    o_w = acc[:, 0:NSA_DV] / acc[:, NSA_DV:NSA_DV + 1]
    _gate_combine(gl_ref, z_ref, o_ref, o_c, o_s, o_w, tq)


def _flash_step(q, k, v, mask, carry):
    m, l, acc = carry
    s = _dot_nt(q, k)
    s = jnp.where(mask, s, _NEG)
    m_new = jnp.maximum(m, jnp.max(s, axis=1, keepdims=True))
    alpha = jnp.exp(m - m_new)
    p = jnp.exp(s - m_new)
    l = alpha * l + jnp.sum(p, axis=1, keepdims=True)
    acc = alpha * acc + jnp.dot(p.astype(jnp.bfloat16), v, preferred_element_type=jnp.float32)
    return m_new, l, acc


def _nsa_prompt_kernel(q_ref, kc_ref, vc_ref, ks_ref, vs_ref, kw_ref, vw_ref, gl_ref, z_ref, o_ref,
                       *, tq, tk, seq):
    t0 = pl.program_id(2) * tq
    m_rows = NSA_REP * tq
    q = q_ref[0].reshape(m_rows, NSA_DKP)
    t_col = t0 + lax.broadcasted_iota(jnp.int32, (NSA_REP, tq, 1), 1).reshape(m_rows, 1)
    o_c, q_slc = _cmp_and_select(q, kc_ref[0, 0], vc_ref[0, 0], t0, tq, seq)

    kpos = lax.broadcasted_iota(jnp.int32, (1, tk), 1)
    init = (jnp.full((m_rows, 1), _NEG, jnp.float32), jnp.zeros((m_rows, 1), jnp.float32),
            jnp.zeros((m_rows, NSA_DV), jnp.float32))

    def slc_body(kt, carry):
        k0 = pl.multiple_of(kt * tk, tk)
        mask = (k0 + kpos) <= t_col
        return _flash_step(q_slc, ks_ref[0, pl.ds(k0, tk), :], vs_ref[0, pl.ds(k0, tk), :], mask, carry)

    _, l_s, acc_s = lax.fori_loop(0, (t0 + tq + tk - 1) // tk, slc_body, init)

    def win_body(kt, carry):
        k0 = pl.multiple_of(kt * tk, tk)
        dw = t_col - (k0 + kpos)
        mask = (dw >= 0) & (dw < WINDOW)
        return _flash_step(q, kw_ref[0, pl.ds(k0, tk), :], vw_ref[0, pl.ds(k0, tk), :], mask, carry)

    kt_lo = jnp.maximum(t0 - (WINDOW - 1), 0) // tk
    _, l_w, acc_w = lax.fori_loop(kt_lo, (t0 + tq + tk - 1) // tk, win_body, init)
    _gate_combine(gl_ref, z_ref, o_ref, o_c, acc_s / l_s, acc_w / l_w, tq)


def nsa_prompt_attention(q_hm, kcmp, vcmp, ks_hm, vs_hm, kw_hm, vw_hm, gl, z, *, batch, tq, tk, fast):
    rows = z.shape[0]
    seq = rows // batch
    nq = seq // tq
    nseg = kcmp.shape[2]
    row_blk = lambda b, g, i: (b * nq + i, g)
    kv_blk = lambda b, g, i: (g, b, 0)
    dvb = NSA_DVP if fast else NSA_DV
    return pl.pallas_call(
        functools.partial(_nsa_prompt_fast_kernel if fast else _nsa_prompt_kernel, tq=tq, tk=tk, seq=seq),
        grid=(batch, NSA_KV, nq),
        in_specs=[pl.BlockSpec((1, NSA_REP, tq, NSA_DKP), lambda b, g, i: (g, 0, b * nq + i, 0)),
                  pl.BlockSpec((1, 1, nseg, NSA_DKP), lambda b, g, i: (b, g, 0, 0)),
                  pl.BlockSpec((1, 1, nseg, NSA_DV), lambda b, g, i: (b, g, 0, 0)),
                  pl.BlockSpec((1, seq, NSA_DKP), kv_blk),
                  pl.BlockSpec((1, seq, dvb), kv_blk),
                  pl.BlockSpec((1, seq, NSA_DKP), kv_blk),
                  pl.BlockSpec((1, seq, dvb), kv_blk),
                  pl.BlockSpec((tq, LANE), row_blk),
                  pl.BlockSpec((tq, NSA_REP * NSA_DV), row_blk)],
        out_specs=pl.BlockSpec((tq, NSA_REP * NSA_DV), row_blk),
        out_shape=jax.ShapeDtypeStruct((rows, NSA_O), jnp.bfloat16),
        compiler_params=_cparams("parallel", "parallel", "arbitrary"),
        name="nsa_prompt_attention_fast" if fast else "nsa_prompt_attention",
    )(q_hm, kcmp, vcmp, ks_hm, vs_hm, kw_hm, vw_hm, gl, z)


def _nsa_in_proj(x, norm_w, w_in):
    B, L, D = x.shape
    x2 = x.reshape(B * L, D)
    h = rmsnorm_bf16(x2, norm_w, tm=min(256, B * L))
    offs = np.cumsum((0,) + NSA_SPLITS)
    tm = _row_tile(B * L)

    wt = w_in.T

    def seg(i, tn, name):
        return matmul_wt(h, wt, tm=tm, tn=tn, n_off=int(offs[i]), n=int(offs[i + 1] - offs[i]), name=name)

    kv = matmul_wt(h, wt, tm=tm, tn=768, n_off=int(offs[1]), n=int(offs[7] - offs[1]), name="nsa_kv_proj")
    kc, vc, ks, vs, kw, vw = (kv[:, int(offs[i] - offs[1]):int(offs[i + 1] - offs[1])] for i in range(1, 7))
    wg = wt[offs[7]:offs[8]].reshape(NSA_KV, 3 * NSA_REP, D)
    wg = jnp.pad(wg, ((0, 0), (0, LANE - 3 * NSA_REP), (0, 0))).reshape(NSA_KV * LANE, D)
    gl = matmul_wt(h, wg, tm=tm, tn=NSA_KV * LANE, n_off=0, n=NSA_KV * LANE, name="nsa_gate_proj")
    z = matmul_wt(h, wt[offs[8]:], tm=tm, tn=_WT_TN, n_off=0, n=NSA_O, name="nsa_z_proj")
    q_of = functools.partial(nsa_q_proj_prep, h, wt, tm=tm)
    return x2, q_of, kc, vc, ks, vs, kw, vw, gl, z


def _nsa_prompt(x, norm_w, w_in, q_norm, kn_c, kn_s, kn_w, pe_k, w1_k, w2_k, pe_v, w1_v, w2_v, w_out):
    B, T, D = x.shape
    x2, q_of, kc_r, vc_r, ks_raw, vs_r, kw_raw, vw_r, gl, z = _nsa_in_proj(x, norm_w, w_in)
    tp = min(512, T)
    q_hm = q_of(q_norm, seq=T)
    ks_r, ks_hm = nsa_k_prep(ks_raw, kn_s, seq=T, tk=tp, onehot=True)
    kw_r, kw_hm = nsa_k_prep(kw_raw, kn_w, seq=T, tk=tp, onehot=False)
    vs_hm = nsa_v_prep(vs_r, tk=tp)
    vw_hm = nsa_v_prep(vw_r, tk=tp)
    npg = T // PAGE_ROWS
    table = (jnp.arange(B, dtype=jnp.int32)[:, None] * npg + jnp.arange(npg, dtype=jnp.int32)[None, :])
    seg3 = lambda a: a.reshape(B * T // CMP_STRIDE, CMP_STRIDE, a.shape[1])
    ka, kb = nsa_compress_partials(seg3(kc_r), table, pe_k, w1_k, pages=min(_CMP_PAGES, npg))
    va, vb = nsa_compress_partials(seg3(vc_r), table, pe_v, w1_v, pages=min(_CMP_PAGES, npg))
    kcmp = nsa_compress_finish(ka, kb, None, w2_k, kn_c, is_key=True, alibi_lanes=True)
    vcmp = nsa_compress_finish(va, vb, None, w2_v, jnp.zeros((NSA_DV,), jnp.float32), is_key=False, alibi_lanes=False)
    bound = jnp.max(jnp.abs(q_norm)) * jnp.maximum(jnp.max(jnp.abs(kn_s)), jnp.max(jnp.abs(kn_w))) * (NSA_DK ** 0.5)
    attend = lambda fast: functools.partial(nsa_prompt_attention, batch=B, tq=_NSA_TQ, tk=_NSA_TK, fast=fast)
    og = lax.cond(bound < _FAST_SCORE_BOUND, attend(True), attend(False),
                  q_hm, kcmp, vcmp, ks_hm, vs_hm, kw_hm, vw_hm, gl, z)
    y = matmul(og, w_out.astype(jnp.bfloat16), tm=_row_tile(B * T), tn=1024, residual=x2, name="nsa_out_proj")
    wl = min(WINDOW, T)
    r4 = lambda a, d: a.reshape(B, T, NSA_KV, d)
    return y.reshape(B, T, D), (r4(kc_r, NSA_DK), r4(vc_r, NSA_DV), r4(ks_r, NSA_DK), r4(vs_r, NSA_DV),
                                r4(kw_r, NSA_DK)[:, T - wl:], r4(vw_r, NSA_DV)[:, T - wl:])


_SMP_PAGES = 8
_SEL_PAD = 384


def _online_update(m_ref, l_ref, acc_ref, g, s, v):
    m_old = m_ref[g]
    m_new = jnp.maximum(m_old, jnp.max(s, axis=1, keepdims=True))
    alpha = jnp.exp(m_old - m_new)
    p = jnp.exp(s - m_new)
    l_ref[g] = alpha * l_ref[g] + jnp.sum(p, axis=1, keepdims=True)
    acc_ref[g] = alpha * acc_ref[g] + jnp.dot(p.astype(jnp.bfloat16), v.astype(jnp.bfloat16),
                                              preferred_element_type=jnp.float32)
    m_ref[g] = m_new


def _nsa_sample_kernel(tab_ref, *refs, past, steps, pages, nsb):
    kp_refs = refs[:pages]
    vp_refs = refs[pages:2 * pages]
    (q_ref, kc_ref, vc_ref, kn_ref, vn_ref, kwc_ref, vwc_ref, kwn_ref, vwn_ref, gl_ref, z_ref,
     o_ref, m_scr, l_scr, acc_scr, sel_scr, oc_scr) = refs[2 * pages:]
    i = pl.program_id(1)
    s_tok = o_ref.shape[0]
    rows = NSA_REP * s_tok
    ncmp = kc_ref.shape[2]
    r_col = lax.broadcasted_iota(jnp.int32, (NSA_REP, s_tok, 1), 0).reshape(rows, 1)
    t_col = past + lax.broadcasted_iota(jnp.int32, (NSA_REP, s_tok, 1), 1).reshape(rows, 1)
    t_tok = past + lax.broadcasted_iota(jnp.int32, (s_tok, 1), 0)

    def slope_col(g):
        return jnp.exp2((r_col + (g * NSA_REP + 1)).astype(jnp.float32) * (-8.0 / NSA_HEADS))

    def q_of(g):
        return q_ref[0, g * rows:(g + 1) * rows, :]

    def rep_rows(x):
        return jnp.concatenate([x] * NSA_REP, axis=0)

    @pl.when(i == 0)
    def _first():
        n_row = lax.broadcasted_iota(jnp.int32, (1, ncmp), 1)
        cvalid = (n_row * CMP_STRIDE + CMP_BLOCK - 1) <= t_col
        ci = lax.broadcasted_iota(jnp.int32, (ncmp, _SEL_PAD), 0) * CMP_STRIDE
        cj = lax.broadcasted_iota(jnp.int32, (ncmp, _SEL_PAD), 1) * SLC_BLOCK
        ov = jnp.maximum(jnp.minimum(ci + CMP_BLOCK, cj + SLC_BLOCK) - jnp.maximum(ci, cj), 0)
        c2s = (ov.astype(jnp.float32) * (1.0 / CMP_STRIDE)).astype(jnp.bfloat16)
        jb = lax.broadcasted_iota(jnp.int32, (s_tok, _SEL_PAD), 1)
        cur = t_tok // SLC_BLOCK
        forced = (jb == 0) | (jb == cur) | (jb == cur - 1)
        valid = (jb * SLC_BLOCK <= t_tok) & (jb < nsb)
        for g in range(NSA_KV):
            s = jnp.where(cvalid, _dot_nt(q_of(g), kc_ref[0, g]), _NEG)
            p = jnp.where(cvalid, jnp.exp(s - jnp.max(s, axis=1, keepdims=True)), 0.0)
            p = p / jnp.maximum(jnp.sum(p, axis=1, keepdims=True), 1e-30)
            oc_scr[g] = jnp.dot(p.astype(jnp.bfloat16), vc_ref[0, g], preferred_element_type=jnp.float32)
            imp = _split_dot(jnp.sum(p.reshape(NSA_REP, s_tok, ncmp), axis=0), c2s)
            work = jnp.where(forced, jnp.inf, jnp.where(valid, imp, -jnp.inf))
            sel = jnp.zeros((s_tok, _SEL_PAD), jnp.float32)
            for _ in range(min(SLC_TOPN, nsb)):
                mx = jnp.max(work, axis=1, keepdims=True)
                first = jnp.min(jnp.where(work == mx, jb, _SEL_PAD), axis=1, keepdims=True)
                hit = jb == first
                sel = jnp.where(hit, 1.0, sel)
                work = jnp.where(hit, -jnp.inf, work)
            sel_scr[g] = sel
            m_scr[g] = jnp.full((rows, 1), _NEG, jnp.float32)
            l_scr[g] = jnp.zeros((rows, 1), jnp.float32)
            acc_scr[g] = jnp.zeros((rows, NSA_DV), jnp.float32)

    def fold(score_of, v_of, nk, page0, causal):
        pos = page0 * PAGE_ROWS + lax.broadcasted_iota(jnp.int32, (1, nk), 1)
        bj = lax.broadcasted_iota(jnp.int32, (_SEL_PAD, nk), 0)
        bl = lax.broadcasted_iota(jnp.int32, (_SEL_PAD, nk), 1)
        expand = jnp.where(bj == (page0 * PAGE_ROWS + bl) // SLC_BLOCK, 1.0, 0.0).astype(jnp.bfloat16)
        for g in range(NSA_KV):
            selk = jnp.dot(sel_scr[g].astype(jnp.bfloat16), expand, preferred_element_type=jnp.float32)
            ok = rep_rows(selk) > 0.5
            if causal:
                ok = ok & (pos <= t_col)
            s = score_of(g, q_of(g)[:, 0:NSA_DK])
            s = s - slope_col(g) * (t_col - pos).astype(jnp.float32)
            _online_update(m_scr, l_scr, acc_scr, g, jnp.where(ok, s, _NEG), v_of(g))

    fold(lambda g, qg: _dot(qg, jnp.concatenate([r[0, g] for r in kp_refs], axis=1)),
         lambda g: jnp.concatenate([r[0, pl.ds(g, PAGE_ROWS, stride=NSA_KV), :] for r in vp_refs], axis=0),
         pages * PAGE_ROWS, i * pages, causal=False)

    @pl.when(i == steps - 1)
    def _last():
        fold(lambda g, qg: _dot_nt(qg, kn_ref[0, :, g * NSA_DK:(g + 1) * NSA_DK]),
             lambda g: vn_ref[0, :, g * NSA_DV:(g + 1) * NSA_DV], PAGE_ROWS, past // PAGE_ROWS, causal=True)
        nwc = kwc_ref.shape[3]
        posw = jnp.concatenate([past - nwc + lax.broadcasted_iota(jnp.int32, (1, nwc), 1),
                                past + lax.broadcasted_iota(jnp.int32, (1, PAGE_ROWS), 1)], axis=1)
        dw = t_col - posw
        wok = (dw >= 0) & (dw < WINDOW) & (posw >= 0)
        gates = 1.0 / (1.0 + jnp.exp(-gl_ref[...]))
        for g in range(NSA_KV):
            ksl = slice(g * NSA_DK, (g + 1) * NSA_DK)
            vsl = slice(g * NSA_DV, (g + 1) * NSA_DV)
            qg = q_of(g)[:, 0:NSA_DK]
            s = jnp.concatenate([_dot(qg, kwc_ref[0, g]), _dot_nt(qg, kwn_ref[0, :, ksl])], axis=1)
            s = jnp.where(wok, s - slope_col(g) * dw.astype(jnp.float32), _NEG)
            p = jnp.where(wok, jnp.exp(s - jnp.max(s, axis=1, keepdims=True)), 0.0)
            p = (p / jnp.maximum(jnp.sum(p, axis=1, keepdims=True), 1e-30)).astype(jnp.bfloat16)
            o_w = (jnp.dot(p[:, 0:nwc], vwc_ref[0, :, vsl].astype(jnp.bfloat16), preferred_element_type=jnp.float32)
                   + jnp.dot(p[:, nwc:], vwn_ref[0, :, vsl].astype(jnp.bfloat16), preferred_element_type=jnp.float32))
            o_s = acc_scr[g] / l_scr[g]
            o_c = oc_scr[g]
            for r in range(NSA_REP):
                rs = slice(r * s_tok, (r + 1) * s_tok)
                c0 = g * LANE + 3 * r
                o = (gates[:, c0:c0 + 1] * o_c[rs] + gates[:, c0 + 1:c0 + 2] * o_s[rs]
                     + gates[:, c0 + 2:c0 + 3] * o_w[rs])
                hs = slice((g * NSA_REP + r) * NSA_DV, (g * NSA_REP + r + 1) * NSA_DV)
                z = z_ref[:, hs]
                o_ref[:, hs] = o * (z * (1.0 / (1.0 + jnp.exp(-z))))


def nsa_sample_attention(q_rows, kcmp, vcmp, pool_k, pool_v, page_table, k_new, v_new, kw_cache, vw_cache,
                         kw_new, vw_new, gl, z, *, s_tok):
    batch, npg = page_table.shape
    pages = _SMP_PAGES
    steps = npg // pages
    past = npg * PAGE_ROWS
    nsb = -(-(past + s_tok) // SLC_BLOCK)
    rows = NSA_REP * s_tok
    ncmp = kcmp.shape[2]
    nwc = kw_cache.shape[3]
    kpage_spec = lambda k: pl.BlockSpec((1, NSA_KV, NSA_DK, PAGE_ROWS), lambda b, i, tab: (tab[b, i * pages + k], 0, 0, 0))
    vpage_spec = lambda k: pl.BlockSpec((1, PAGE_ROWS * NSA_KV, NSA_DV), lambda b, i, tab: (tab[b, i * pages + k], 0, 0))
    per_b = lambda shape: pl.BlockSpec((1,) + shape, lambda b, i, tab: (b,) + (0,) * len(shape))
    tok_blk = lambda w: pl.BlockSpec((s_tok, w), lambda b, i, tab: (b, 0))
    return pl.pallas_call(
        functools.partial(_nsa_sample_kernel, past=past, steps=steps, pages=pages, nsb=nsb),
        grid_spec=pltpu.PrefetchScalarGridSpec(
            num_scalar_prefetch=1,
            grid=(batch, steps),
            in_specs=[kpage_spec(k) for k in range(pages)] + [vpage_spec(k) for k in range(pages)]
            + [per_b((NSA_KV * rows, NSA_DKP)), per_b((NSA_KV, ncmp, NSA_DKP)), per_b((NSA_KV, ncmp, NSA_DV)),
               per_b((PAGE_ROWS, NSA_KW)), per_b((PAGE_ROWS, NSA_VW)),
               per_b((NSA_KV, NSA_DK, nwc)), per_b((nwc, NSA_VW)), per_b((PAGE_ROWS, NSA_KW)), per_b((PAGE_ROWS, NSA_VW)),
               tok_blk(NSA_KV * LANE), tok_blk(NSA_O)],
            out_specs=tok_blk(NSA_O),
            scratch_shapes=[pltpu.VMEM((NSA_KV, rows, 1), jnp.float32), pltpu.VMEM((NSA_KV, rows, 1), jnp.float32),
                            pltpu.VMEM((NSA_KV, rows, NSA_DV), jnp.float32),
                            pltpu.VMEM((NSA_KV, s_tok, _SEL_PAD), jnp.float32),
                            pltpu.VMEM((NSA_KV, rows, NSA_DV), jnp.float32)]),
        out_shape=jax.ShapeDtypeStruct((batch * s_tok, NSA_O), jnp.float32),
        compiler_params=_cparams("parallel", "arbitrary"),
        name="nsa_sample_attention",
    )(page_table, *([pool_k] * pages), *([pool_v] * pages), q_rows, kcmp, vcmp, k_new, v_new,
      kw_cache, vw_cache, kw_new, vw_new, gl, z)


def _nsa_sample(x, ck, cv, sk, sv, wk_buf, wv_buf, page_table, norm_w,
                w_in, q_norm, kn_c, kn_s, kn_w, pe_k, w1_k, w2_k, pe_v, w1_v, w2_v, w_out):
    B, S, D = x.shape
    x2, q_of, kc_r, vc_r, ks_raw, vs_r, kw_raw, vw_r, gl, z = _nsa_in_proj(x, norm_w, w_in)
    q_hm = q_of(q_norm, seq=S, pos0=page_table.shape[1] * PAGE_ROWS)
    q_rows = q_hm.reshape(NSA_KV, NSA_REP, B, S, NSA_DKP).transpose(2, 0, 1, 3, 4).reshape(B, NSA_KV * NSA_REP * S, NSA_DKP)
    ks_r, _ = nsa_k_prep(ks_raw, kn_s, seq=S, tk=B * S, onehot=False)
    kw_r, _ = nsa_k_prep(kw_raw, kn_w, seq=S, tk=B * S, onehot=False)
    as_page = lambda a: jnp.pad(a.reshape(B, S, -1), ((0, 0), (0, PAGE_ROWS - S), (0, 0)))
    seg3 = lambda a, w: a.reshape(-1, CMP_STRIDE, w)
    seg4 = lambda a: a.reshape(-1, CMP_STRIDE * NSA_KV, a.shape[3])
    ident = jnp.arange(B, dtype=jnp.int32)[:, None]
    rows_minor = lambda a: jnp.transpose(a, (0, 2, 3, 1))
    ka, kb = nsa_compress_partials(rows_minor(ck), page_table, pe_k, w1_k, pages=_CMP_PAGES, transposed=True)
    va, vb = nsa_compress_partials(seg4(cv), page_table, pe_v, w1_v, pages=_CMP_PAGES)
    _, kb_new = nsa_compress_partials(seg3(as_page(kc_r), NSA_KW), ident, pe_k, w1_k, pages=1)
    _, vb_new = nsa_compress_partials(seg3(as_page(vc_r), NSA_VW), ident, pe_v, w1_v, pages=1)
    kcmp = nsa_compress_finish(ka, kb, kb_new, w2_k, kn_c, is_key=True, alibi_lanes=True)
    vcmp = nsa_compress_finish(va, vb, vb_new, w2_v, jnp.zeros((NSA_DV,), jnp.float32), is_key=False, alibi_lanes=False)
    wl = wk_buf.shape[1]
    o = nsa_sample_attention(q_rows, kcmp, vcmp, rows_minor(sk), sv.reshape(sv.shape[0], PAGE_ROWS * NSA_KV, NSA_DV), page_table, as_page(ks_r), as_page(vs_r), rows_minor(wk_buf),
                             wv_buf.reshape(B, wl, NSA_VW), as_page(kw_r), as_page(vw_r), gl, z, s_tok=S)
    y = matmul(o.astype(jnp.bfloat16), w_out.astype(jnp.bfloat16), tm=B * S, tn=1024, residual=x2, name="nsa_out_proj")
    r4 = lambda a, d: a.reshape(B, S, NSA_KV, d)
    kw_all = jnp.concatenate([wk_buf, r4(kw_r, NSA_DK)], axis=1)
    vw_all = jnp.concatenate([wv_buf, r4(vw_r, NSA_DV)], axis=1)
    return y.reshape(B, S, D), (r4(kc_r, NSA_DK), r4(vc_r, NSA_DV), r4(ks_r, NSA_DK), r4(vs_r, NSA_DV),
                                kw_all[:, S:], vw_all[:, S:])


def kernel(x_prompt, x_sample, state_delta, state_conv, cache_cmp_k, cache_cmp_v, cache_slc_k, cache_slc_v, cache_win_k, cache_win_v, page_table, norm_dn, w_in_dn, conv_w_dn, a_log_dn, dt_bias_dn, out_norm_dn, w_out_dn, norm_nsa, w_in_nsa, q_norm_nsa, k_norm_cmp, k_norm_slc, k_norm_win, cmp_pe_k, cmp_w1_k, cmp_w2_k, cmp_pe_v, cmp_w1_v, cmp_w2_v, w_out_nsa):
    xp, xs = x_prompt, x_sample
    B = xp.shape[0]
    dw = (norm_dn[0], w_in_dn[0], conv_w_dn[0], a_log_dn[0], dt_bias_dn[0], out_norm_dn[0], w_out_dn[0])
    buf0 = jnp.zeros((B, DN_CONV - 1, conv_w_dn.shape[-1]), xp.dtype)
    st0 = jnp.zeros((B, DN_HEADS, DN_DK, DN_DV), jnp.float32)
    xp, p_dn = _deltanet_mixer(xp, buf0, st0, *dw)
    xs, s_dn = _deltanet_mixer(xs, state_conv[0], state_delta[0], *dw)
    nw = (norm_nsa[0], w_in_nsa[0], q_norm_nsa[0], k_norm_cmp[0], k_norm_slc[0], k_norm_win[0],
          cmp_pe_k[0], cmp_w1_k[0], cmp_w2_k[0], cmp_pe_v[0], cmp_w1_v[0], cmp_w2_v[0], w_out_nsa[0])
    xp, p_nsa = _nsa_prompt(xp, *nw)
    xs, s_nsa = _nsa_sample(xs, cache_cmp_k[0], cache_cmp_v[0], cache_slc_k[0], cache_slc_v[0],
                            cache_win_k[0], cache_win_v[0], page_table, *nw)
    return ((xp, xs, p_dn[0][None], p_dn[1][None]) + tuple(t[None] for t in p_nsa)
            + (s_dn[0][None], s_dn[1][None]) + tuple(t[None] for t in s_nsa))
```

```python
import functools

import jax
import jax.numpy as jnp
import numpy as np
from jax import lax
from jax.experimental import pallas as pl
from jax.experimental.pallas import tpu as pltpu

D_MODEL = 4096
EPS = 1e-6

DN_HEADS = 32
DN_DK = 128
DN_DV = 128
DN_CONV = 4
DN_CHUNK = 64

NSA_HEADS = 32
NSA_KV = 4
NSA_REP = NSA_HEADS // NSA_KV
NSA_DK = 192
NSA_DV = 128
CMP_BLOCK = 32
CMP_STRIDE = 16
CMP_RATIO = CMP_BLOCK // CMP_STRIDE
CMP_HIDDEN = 256
SLC_BLOCK = 64
SLC_TOPN = 16
WINDOW = 512
NSA_Q = NSA_HEADS * NSA_DK
NSA_KW = NSA_KV * NSA_DK
NSA_VW = NSA_KV * NSA_DV
NSA_O = NSA_HEADS * NSA_DV
NSA_SPLITS = (NSA_Q, NSA_KW, NSA_VW, NSA_KW, NSA_VW, NSA_KW, NSA_VW, 3 * NSA_HEADS, NSA_O)

VMEM_LIMIT_BYTES = 56 * 1024 * 1024
LANE = 128
_WT_TN = 512


def _cparams(*sem):
    return pltpu.CompilerParams(dimension_semantics=sem, vmem_limit_bytes=VMEM_LIMIT_BYTES)


def _rmsnorm_kernel(x_ref, w_ref, o_ref):
    x = x_ref[...]
    ms = jnp.mean(x * x, axis=-1, keepdims=True)
    o_ref[...] = (x * lax.rsqrt(ms + EPS) * w_ref[...]).astype(o_ref.dtype)


def rmsnorm_bf16(x, w, *, tm):
    m, d = x.shape
    return pl.pallas_call(
        _rmsnorm_kernel,
        grid=(m // tm,),
        in_specs=[pl.BlockSpec((tm, d), lambda i: (i, 0)), pl.BlockSpec((1, d), lambda i: (0, 0))],
        out_specs=pl.BlockSpec((tm, d), lambda i: (i, 0)),
        out_shape=jax.ShapeDtypeStruct((m, d), jnp.bfloat16),
        compiler_params=_cparams("parallel"),
        name="rmsnorm_bf16",
    )(x, w.reshape(1, d))


def _mm_kernel(a_ref, b_ref, o_ref):
    o_ref[...] = jnp.dot(a_ref[...], b_ref[...], preferred_element_type=jnp.float32).astype(o_ref.dtype)


def _mm_res_kernel(a_ref, b_ref, r_ref, o_ref):
    acc = jnp.dot(a_ref[...], b_ref[...], preferred_element_type=jnp.float32)
    o_ref[...] = r_ref[...] + acc


def matmul(a, b, *, tm, tn, residual=None, out_dtype=jnp.float32, name="matmul"):
    m, k = a.shape
    _, n = b.shape
    assert m % tm == 0 and n % tn == 0
    in_specs = [pl.BlockSpec((tm, k), lambda i, j: (i, 0)), pl.BlockSpec((k, tn), lambda i, j: (0, j))]
    args = [a, b]
    kern = _mm_kernel
    if residual is not None:
        in_specs.append(pl.BlockSpec((tm, tn), lambda i, j: (i, j)))
        args.append(residual)
        kern = _mm_res_kernel
    return pl.pallas_call(
        kern,
        grid=(m // tm, n // tn),
        in_specs=in_specs,
        out_specs=pl.BlockSpec((tm, tn), lambda i, j: (i, j)),
        out_shape=jax.ShapeDtypeStruct((m, n), out_dtype),
        compiler_params=_cparams("parallel", "parallel"),
        name=name,
    )(*args)


def _mm_wt_kernel(a_ref, w_ref, o_ref):
    o_ref[...] = lax.dot_general(a_ref[...], w_ref[...].astype(jnp.bfloat16), (((1,), (1,)), ((), ())),
                                 preferred_element_type=jnp.float32)


def matmul_wt(a, wt, *, tm, tn, n_off, n, name):
    m, k = a.shape
    assert m % tm == 0 and n % tn == 0 and n_off % tn == 0
    j0 = n_off // tn
    return pl.pallas_call(
        _mm_wt_kernel,
        grid=(m // tm, n // tn),
        in_specs=[pl.BlockSpec((tm, k), lambda i, j: (i, 0)), pl.BlockSpec((tn, k), lambda i, j: (j0 + j, 0))],
        out_specs=pl.BlockSpec((tm, tn), lambda i, j: (i, j)),
        out_shape=jax.ShapeDtypeStruct((m, n), jnp.float32),
        compiler_params=_cparams("parallel", "parallel"),
        name=name,
    )(a, wt)


def _row_tile(m):
    return 1024 if m % 1024 == 0 else m


def _dot(a, b):
    return jnp.dot(a.astype(jnp.bfloat16), b.astype(jnp.bfloat16), preferred_element_type=jnp.float32)


def _dot_nt(a, b):
    return lax.dot_general(a.astype(jnp.bfloat16), b.astype(jnp.bfloat16), (((1,), (1,)), ((), ())),
                           preferred_element_type=jnp.float32)


def _dot_tn(a, b):
    return lax.dot_general(a.astype(jnp.bfloat16), b.astype(jnp.bfloat16), (((0,), (0,)), ((), ())),
                           preferred_element_type=jnp.float32)


def _split_bf16(x):
    hi = x.astype(jnp.bfloat16)
    lo = (x - hi.astype(jnp.float32)).astype(jnp.bfloat16)
    return hi, lo


def _dn_gate_kernel(h_ref, wt_ref, alog_ref, dtb_ref, o_ref, *, heads):
    r = lax.dot_general(wt_ref[...], h_ref[...], (((1,), (1,)), ((), ())),
                        preferred_element_type=jnp.float32)
    b = r[:heads]
    a = r[heads:] + dtb_ref[...]
    softplus = jnp.maximum(a, 0.0) + jnp.log(1.0 + jnp.exp(-jnp.abs(a)))
    o_ref[0:heads, :] = 1.0 / (1.0 + jnp.exp(-b))
    o_ref[heads:2 * heads, :] = -jnp.exp(alog_ref[...]) * softplus


def dn_gates(h, w_ba_t, a_log, dt_bias, *, tm):
    m, d = h.shape
    heads = a_log.shape[0]
    return pl.pallas_call(
        functools.partial(_dn_gate_kernel, heads=heads),
        grid=(m // tm,),
        in_specs=[pl.BlockSpec((tm, d), lambda i: (i, 0)),
                  pl.BlockSpec((2 * heads, d), lambda i: (0, 0)),
                  pl.BlockSpec((heads, 1), lambda i: (0, 0)),
                  pl.BlockSpec((heads, 1), lambda i: (0, 0))],
        out_specs=pl.BlockSpec((2 * heads, tm), lambda i: (0, i)),
        out_shape=jax.ShapeDtypeStruct((2 * heads, m), jnp.float32),
        compiler_params=_cparams("parallel"),
        name="dn_gates",
    )(h, w_ba_t, a_log.reshape(heads, 1), dt_bias.reshape(heads, 1))


_DN_TAIL = 8
_DN_TB = 256
_DN_TMIN = 2 * DN_CHUNK
_DN_HB = 8


def _dn_chunk_prep(q, k, g_row, b_row):
    c = q.shape[0]
    ii = lax.broadcasted_iota(jnp.int32, (c, c), 0)
    jj = lax.broadcasted_iota(jnp.int32, (c, c), 1)
    lower, strict, eye = ii >= jj, ii > jj, ii == jj
    g_b = jnp.broadcast_to(g_row, (c, c))
    b_b = jnp.broadcast_to(b_row, (c, c))
    b_col = jnp.sum(jnp.where(eye, b_b, 0.0), axis=1, keepdims=True)
    g_col = jnp.sum(jnp.where(eye, g_b, 0.0), axis=1, keepdims=True)
    gc_col = jnp.sum(jnp.where(lower, g_b, 0.0), axis=1, keepdims=True)
    gc_row = jnp.sum(jnp.where(ii <= jj, g_col, 0.0), axis=0, keepdims=True)
    g_last = jnp.sum(g_row, axis=1, keepdims=True)
    decay = jnp.exp(jnp.where(lower, gc_col - gc_row, -jnp.inf))
    kk = _dot_nt(k, k)
    qk = _dot_nt(q, k)
    lmat = jnp.where(strict, kk * decay, 0.0) * b_col
    attn = qk * decay
    return dict(x=-lmat, p=eye.astype(jnp.float32) - lmat, attn=attn, b_row=b_row,
                e_row=jnp.exp(gc_row), e_col=jnp.exp(gc_col), c_col=jnp.exp(g_last - gc_col),
                e_last=jnp.exp(g_last))


def _dn_inverse_levels(items, c):
    m = 2
    while m < c:
        for it in items:
            it["x"] = _dot(it["x"], it["x"])
        for it in items:
            it["p"] = it["p"] + _dot(it["p"], it["x"])
        m *= 2


def _dn_kernel(q_ref, k_ref, v_ref, z_ref, beta_ref, g_ref, cw_ref, cs_ref, s0_ref, onw_ref,
               o_ref, s_out_ref, xpad, s_scr, *, tb, hb, live):
    n = pl.program_id(2)
    c = DN_CHUNK
    d = DN_DK
    nc = live

    @pl.when(n == 0)
    def _init():
        xpad[:, 0:_DN_TAIL, :] = cs_ref[0]
        s_scr[...] = s0_ref[0]

    def conv(i, ref):
        xpad[i, _DN_TAIL:_DN_TAIL + tb, :] = ref[...]
        acc = None
        for j in range(DN_CONV):
            off = _DN_TAIL - (DN_CONV - 1) + j
            term = xpad[i, off:off + tb, :] * cw_ref[i, j:j + 1, :]
            acc = term if acc is None else acc + term
        xpad[i, 0:_DN_TAIL, :] = xpad[i, tb:tb + _DN_TAIL, :]
        return acc * (1.0 / (1.0 + jnp.exp(-acc)))

    qc, kc, vc = conv(0, q_ref), conv(1, k_ref), conv(2, v_ref)
    items = []
    for hh in range(hb):
        sl = slice(hh * d, (hh + 1) * d)
        q = qc[:, sl]
        k = kc[:, sl]
        q = q * (lax.rsqrt(jnp.sum(q * q, axis=-1, keepdims=True) + EPS) * (d ** -0.5))
        k = k * lax.rsqrt(jnp.sum(k * k, axis=-1, keepdims=True) + EPS)
        g_all = g_ref[hh]
        b_all = beta_ref[hh]
        for ci in range(nc):
            rs = slice(ci * c, (ci + 1) * c)
            it = _dn_chunk_prep(q[rs], k[rs], g_all[:, rs], b_all[:, rs])
            it.update(q=q[rs], k=k[rs], v=vc[rs, sl])
            items.append(it)
    _dn_inverse_levels(items, c)
    for it in items:
        inv_b = it["p"] * it["b_row"]
        it["u"] = _dot(inv_b, it["v"])
        it["w"] = _dot(inv_b * it["e_row"], it["k"])
    states = [s_scr[hh] for hh in range(hb)]
    outs = [[None] * nc for _ in range(hb)]
    for ci in range(nc):
        for hh in range(hb):
            it = items[hh * nc + ci]
            s = states[hh]
            v_new = it["u"] - _dot(it["w"], s)
            outs[hh][ci] = it["e_col"] * _dot(it["q"], s) + _dot(it["attn"], v_new)
            states[hh] = s * it["e_last"] + _dot_tn(it["k"] * it["c_col"], v_new)
    for hh in range(hb):
        sl = slice(hh * d, (hh + 1) * d)
        s_scr[hh] = states[hh]
        o = jnp.concatenate(outs[hh], axis=0) if nc > 1 else outs[hh][0]
        o = o * lax.rsqrt(jnp.mean(o * o, axis=-1, keepdims=True) + EPS) * onw_ref[...]
        z = z_ref[0:nc * c, sl]
        o_ref[0:nc * c, sl] = (o * (z * (1.0 / (1.0 + jnp.exp(-z))))).astype(o_ref.dtype)
        if nc * c < tb:
            o_ref[nc * c:tb, sl] = jnp.zeros((tb - nc * c, d), o_ref.dtype)

    @pl.when(n == pl.num_programs(2) - 1)
    def _fin():
        s_out_ref[0] = s_scr[...]


def dn_delta(proj, gates, conv_w, conv_state, s0, out_norm, *, batch, tb, hb, live):
    rows, width = proj.shape
    d = DN_DK
    heads = width // (4 * d)
    t = rows // batch
    nt = t // tb
    hg = heads // hb
    wd = hb * d
    row_map = lambda off: (lambda b, h, n: (b * nt + n, off * hg + h))
    gate_map = lambda off: (lambda b, h, n: (off * hg + h, 0, b * nt + n))
    return pl.pallas_call(
        functools.partial(_dn_kernel, tb=tb, hb=hb, live=live),
        grid=(batch, hg, nt),
        in_specs=[pl.BlockSpec((tb, wd), row_map(0)),
                  pl.BlockSpec((tb, wd), row_map(1)),
                  pl.BlockSpec((tb, wd), row_map(2)),
                  pl.BlockSpec((tb, wd), row_map(3)),
                  pl.BlockSpec((hb, 1, tb), gate_map(0)),
                  pl.BlockSpec((hb, 1, tb), gate_map(1)),
                  pl.BlockSpec((3, DN_CONV, wd), lambda b, h, n: (0, 0, h)),
                  pl.BlockSpec((1, 3, _DN_TAIL, wd), lambda b, h, n: (b, 0, 0, h)),
                  pl.BlockSpec((1, hb, d, d), lambda b, h, n: (b, h, 0, 0)),
                  pl.BlockSpec((1, d), lambda b, h, n: (0, 0))],
        out_specs=[pl.BlockSpec((tb, wd), lambda b, h, n: (b * nt + n, h)),
                   pl.BlockSpec((1, hb, d, d), lambda b, h, n: (b, h, 0, 0))],
        out_shape=[jax.ShapeDtypeStruct((rows, heads * d), jnp.bfloat16),
                   jax.ShapeDtypeStruct(s0.shape, jnp.float32)],
        scratch_shapes=[pltpu.VMEM((3, tb + _DN_TAIL, wd), jnp.float32),
                        pltpu.VMEM((hb, d, d), jnp.float32)],
        compiler_params=_cparams("parallel", "parallel", "arbitrary"),
        name="dn_delta",
    )(proj, proj, proj, proj, gates, gates, conv_w, conv_state, s0, out_norm.reshape(1, d))


def _deltanet_mixer(x, conv_buf, s0, norm_w, w_in, conv_w, a_log, dt_bias, out_norm, w_out):
    B, L, D = x.shape
    H = a_log.shape[0]
    cdim = conv_w.shape[1]
    hd = cdim // 3
    x2 = x.reshape(B * L, D)
    h = rmsnorm_bf16(x2, norm_w, tm=min(256, B * L))
    wt = w_in.T
    w_ba_t = wt[cdim + hd:].astype(jnp.bfloat16)
    proj = matmul_wt(h, wt, tm=_row_tile(B * L), tn=_WT_TN, n_off=0, n=cdim + hd, name="dn_in_proj")
    gates = dn_gates(h, w_ba_t, a_log, dt_bias, tm=_row_tile(B * L))
    new_buf = proj.reshape(B, L, -1)[:, L - (DN_CONV - 1):, :cdim]
    T = -(-L // _DN_TMIN) * _DN_TMIN
    if T != L:
        proj = jnp.pad(proj.reshape(B, L, -1), ((0, 0), (0, T - L), (0, 0))).reshape(B * T, -1)
        gates = jnp.pad(gates.reshape(2 * H, B, L), ((0, 0), (0, 0), (0, T - L))).reshape(2 * H, B * T)
    cw = conv_w.reshape(DN_CONV, 3, hd).transpose(1, 0, 2)
    cs = conv_buf.reshape(B, DN_CONV - 1, 3, hd).transpose(0, 2, 1, 3)
    cs = jnp.pad(cs, ((0, 0), (0, 0), (_DN_TAIL - (DN_CONV - 1), 0), (0, 0)))
    tb = min(_DN_TB, T)
    live = tb // DN_CHUNK if T == L else -(-L // DN_CHUNK)
    assert T == L or T == tb
    o, s_new = dn_delta(proj, gates.reshape(2 * H, 1, B * T), cw, cs, s0.astype(jnp.float32), out_norm,
                        batch=B, tb=tb, hb=_DN_HB, live=live)
    if T != L:
        o = o.reshape(B, T, -1)[:, :L].reshape(B * L, -1)
    y = matmul(o, w_out.astype(jnp.bfloat16), tm=_row_tile(B * L), tn=1024, residual=x2, name="dn_out_proj")
    return y.reshape(B, L, D), (s_new, new_buf)


def _alibi_slopes():
    hh = jnp.arange(1, NSA_HEADS + 1, dtype=jnp.float32)
    return jnp.exp2(-8.0 * hh / NSA_HEADS).reshape(NSA_KV, NSA_REP)


PAGE_ROWS = 128
NSA_DKP = 256
_POS_LANE = NSA_DK
_SEL_LANE = NSA_DK + 4
_SEL_LANES = 32
_ROW_LANE = _SEL_LANE + _SEL_LANES
NSA_DVP = 256
_FAST_SCORE_BOUND = 40.0
_MASK_BIG = 2.0 ** 100
_NEG = -1e30
_NSA_TQ = 256
_NSA_TK = 256
_CMP_PAGES = 16


def _slope_table():
    s = _alibi_slopes()
    hi = s.astype(jnp.bfloat16).astype(jnp.float32)
    lo = (s - hi).astype(jnp.bfloat16).astype(jnp.float32)
    tab = jnp.stack([64.0 * hi, hi, 64.0 * lo, lo], axis=-1)
    return jnp.pad(tab, ((0, 0), (0, 0), (0, NSA_DKP - NSA_DK - 4)))


_QP_HEADS = 4


def _q_proj_prep_kernel(a_ref, w_ref, nw_ref, tab_ref, o_ref, *, seq, pos0):
    tm = a_ref.shape[0]
    npad = NSA_DKP - NSA_DK
    acc = lax.dot_general(a_ref[...], w_ref[...].astype(jnp.bfloat16), (((1,), (1,)), ((), ())),
                          preferred_element_type=jnp.float32)
    t = (pos0 + (pl.program_id(0) * tm) % seq + lax.broadcasted_iota(jnp.int32, (tm, npad), 0)).astype(jnp.float32)
    lane = lax.broadcasted_iota(jnp.int32, (tm, npad), 1)
    for r in range(_QP_HEADS):
        x = acc[:, r * NSA_DK:(r + 1) * NSA_DK]
        y = x * lax.rsqrt(jnp.mean(x * x, axis=-1, keepdims=True) + EPS) * (nw_ref[...] * (NSA_DK ** -0.5))
        o_ref[0, r, :, 0:NSA_DK] = y.astype(o_ref.dtype)
        tab = jnp.broadcast_to(tab_ref[0, r:r + 1, :], (tm, npad))
        slope = tab[:, 1:2] + tab[:, 3:4]
        extra = jnp.where(lane == _ROW_LANE - NSA_DK, -slope * t, tab)
        o_ref[0, r, :, NSA_DK:NSA_DKP] = extra.astype(o_ref.dtype)


def nsa_q_proj_prep(h, wt, q_norm, *, tm, seq, pos0=0):
    rows, k = h.shape
    per_g = NSA_REP // _QP_HEADS
    tn = _QP_HEADS * NSA_DK
    return pl.pallas_call(
        functools.partial(_q_proj_prep_kernel, seq=seq, pos0=pos0),
        grid=(rows // tm, NSA_HEADS // _QP_HEADS),
        in_specs=[pl.BlockSpec((tm, k), lambda i, j: (i, 0)),
                  pl.BlockSpec((tn, k), lambda i, j: (j, 0)),
                  pl.BlockSpec((1, NSA_DK), lambda i, j: (0, 0)),
                  pl.BlockSpec((1, _QP_HEADS, NSA_DKP - NSA_DK), lambda i, j: (j, 0, 0))],
        out_specs=pl.BlockSpec((1, _QP_HEADS, tm, NSA_DKP), lambda i, j: (j // per_g, j % per_g, i, 0)),
        out_shape=jax.ShapeDtypeStruct((NSA_KV, NSA_REP, rows, NSA_DKP), jnp.bfloat16),
        compiler_params=_cparams("parallel", "parallel"),
        name="nsa_q_proj_prep",
    )(h, wt, q_norm.reshape(1, NSA_DK), _slope_table().reshape(NSA_HEADS // _QP_HEADS, _QP_HEADS, NSA_DKP - NSA_DK))


def _k_prep_kernel(k_ref, w_ref, on_ref, oh_ref, *, seq, onehot):
    tk = k_ref.shape[0]
    pos = (pl.program_id(0) * tk) % seq + lax.broadcasted_iota(jnp.int32, (tk, NSA_DKP - NSA_DK), 0)
    lane = lax.broadcasted_iota(jnp.int32, (tk, NSA_DKP - NSA_DK), 1)
    blk, off = pos // SLC_BLOCK, pos % SLC_BLOCK
    extra = jnp.where((lane == 0) | (lane == 2), blk, jnp.where((lane == 1) | (lane == 3), off, 0))
    if onehot:
        extra = jnp.where((lane >= 4) & (lane < 4 + _SEL_LANES) & (lane - 4 == blk), 1, extra)
    extra = jnp.where(lane == _ROW_LANE - NSA_DK, 1, extra)
    extra = extra.astype(jnp.float32)
    for g in range(NSA_KV):
        x = k_ref[:, g * NSA_DK:(g + 1) * NSA_DK]
        x = x * lax.rsqrt(jnp.mean(x * x, axis=-1, keepdims=True) + EPS) * w_ref[...]
        on_ref[:, g * NSA_DK:(g + 1) * NSA_DK] = x
        oh_ref[g, :, 0:NSA_DK] = x.astype(oh_ref.dtype)
        oh_ref[g, :, NSA_DK:NSA_DKP] = extra.astype(oh_ref.dtype)


def nsa_k_prep(k_raw, k_norm, *, seq, tk, onehot):
    rows = k_raw.shape[0]
    return pl.pallas_call(
        functools.partial(_k_prep_kernel, seq=seq, onehot=onehot),
        grid=(rows // tk,),
        in_specs=[pl.BlockSpec((tk, NSA_KW), lambda i: (i, 0)), pl.BlockSpec((1, NSA_DK), lambda i: (0, 0))],
        out_specs=[pl.BlockSpec((tk, NSA_KW), lambda i: (i, 0)),
                   pl.BlockSpec((NSA_KV, tk, NSA_DKP), lambda i: (0, i, 0))],
        out_shape=[jax.ShapeDtypeStruct((rows, NSA_KW), jnp.float32),
                   jax.ShapeDtypeStruct((NSA_KV, rows, NSA_DKP), jnp.bfloat16)],
        compiler_params=_cparams("parallel"),
        name="nsa_k_prep",
    )(k_raw, k_norm.reshape(1, NSA_DK))


def _v_prep_kernel(v_ref, o_ref):
    tk = v_ref.shape[0]
    ones_col = jnp.where(lax.broadcasted_iota(jnp.int32, (tk, NSA_DVP - NSA_DV), 1) == 0, 1.0, 0.0)
    for g in range(NSA_KV):
        o_ref[g, :, 0:NSA_DV] = v_ref[:, g * NSA_DV:(g + 1) * NSA_DV].astype(o_ref.dtype)
        o_ref[g, :, NSA_DV:NSA_DVP] = ones_col.astype(o_ref.dtype)


def nsa_v_prep(v_raw, *, tk):
    rows = v_raw.shape[0]
    return pl.pallas_call(
        _v_prep_kernel,
        grid=(rows // tk,),
        in_specs=[pl.BlockSpec((tk, NSA_VW), lambda i: (i, 0))],
        out_specs=pl.BlockSpec((NSA_KV, tk, NSA_DVP), lambda i: (0, i, 0)),
        out_shape=jax.ShapeDtypeStruct((NSA_KV, rows, NSA_DVP), jnp.bfloat16),
        compiler_params=_cparams("parallel"),
        name="nsa_v_prep",
    )(v_raw)


def _compress_partials_kernel(tab_ref, *refs, d, pages, transposed):
    x_refs, (pe_ref, w1_ref, perm_ref, a_ref, b_ref) = refs[:pages], refs[pages:]
    seg = PAGE_ROWS // CMP_STRIDE
    acc_a = acc_b = None
    if transposed:
        rows_of = [[_dot_nt(perm_ref[...], r[0, g]) for r in x_refs] for g in range(NSA_KV)]
        take = lambda ri, l, g: rows_of[g][ri][l * seg:(l + 1) * seg, :]
    elif x_refs[0].shape[1] == CMP_STRIDE * NSA_KV:
        take = lambda ri, l, g: x_refs[ri][:, l * NSA_KV + g, :]
    else:
        take = lambda ri, l, g: x_refs[ri][:, l, g * d:(g + 1) * d]

    for l in range(CMP_STRIDE):
        xg = jnp.concatenate([take(ri, l, g) for g in range(NSA_KV) for ri in range(pages)], axis=0)
        ta = jnp.dot((xg + pe_ref[l:l + 1, :]).astype(jnp.bfloat16), w1_ref[l], preferred_element_type=jnp.float32)
        tb = jnp.dot((xg + pe_ref[CMP_STRIDE + l:CMP_STRIDE + l + 1, :]).astype(jnp.bfloat16),
                     w1_ref[CMP_STRIDE + l], preferred_element_type=jnp.float32)
        acc_a = ta if acc_a is None else acc_a + ta
        acc_b = tb if acc_b is None else acc_b + tb
    n = pages * seg
    for g in range(NSA_KV):
        a_ref[0, g] = acc_a[g * n:(g + 1) * n]
        b_ref[0, g] = acc_b[g * n:(g + 1) * n]


def nsa_compress_partials(pool3, table, pe, w1, *, pages, transposed=False):
    d = pe.shape[1]
    batch, npg = table.shape
    seg = PAGE_ROWS // CMP_STRIDE
    blk = ((1,) if transposed else (seg,)) + tuple(pool3.shape[1:])
    x_spec = lambda k: pl.BlockSpec(blk, lambda b, i, tab: (tab[b, i * pages + k],) + (0,) * (len(blk) - 1))
    out_spec = pl.BlockSpec((1, NSA_KV, pages * seg, CMP_HIDDEN), lambda b, i, tab: (b, 0, i, 0))
    out_sds = jax.ShapeDtypeStruct((batch, NSA_KV, npg * seg, CMP_HIDDEN), jnp.float32)
    i = jnp.arange(PAGE_ROWS)
    perm = (i[None, :] == (i % seg)[:, None] * CMP_STRIDE + (i // seg)[:, None]).astype(jnp.bfloat16)
    return pl.pallas_call(
        functools.partial(_compress_partials_kernel, d=d, pages=pages, transposed=transposed),
        grid_spec=pltpu.PrefetchScalarGridSpec(
            num_scalar_prefetch=1,
            grid=(batch, npg // pages),
            in_specs=[x_spec(k) for k in range(pages)]
            + [pl.BlockSpec((CMP_BLOCK, d), lambda b, i, tab: (0, 0)),
               pl.BlockSpec((CMP_BLOCK, d, CMP_HIDDEN), lambda b, i, tab: (0, 0, 0)),
               pl.BlockSpec((PAGE_ROWS, PAGE_ROWS), lambda b, i, tab: (0, 0))],
            out_specs=[out_spec, out_spec]),
        out_shape=[out_sds, out_sds],
        compiler_params=_cparams("parallel", "arbitrary"),
        name="nsa_compress_partials",
    )(table, *([pool3] * pages), pe, w1.astype(jnp.bfloat16), perm)


def _cmp_extra_lanes(nrows, width):
    n = lax.broadcasted_iota(jnp.int32, (nrows, width), 0)
    lane = lax.broadcasted_iota(jnp.int32, (nrows, width), 1)
    ec = n * CMP_STRIDE + CMP_BLOCK - 1
    blk, off = ec // SLC_BLOCK, ec % SLC_BLOCK
    pl_ = lane - _POS_LANE
    return jnp.where((pl_ == 0) | (pl_ == 2), blk, jnp.where((pl_ == 1) | (pl_ == 3), off, 0)).astype(jnp.float32)


def _compress_finish_kernel(a_ref, b_ref, bn_ref, w2_ref, kn_ref, o_ref, *, is_key, has_new, alibi_lanes):
    ns = a_ref.shape[2]
    row = lax.broadcasted_iota(jnp.int32, (ns, 1), 0)
    for g in range(NSA_KV):
        nxt = pltpu.roll(b_ref[0, g], ns - 1, 0)
        if has_new:
            nxt = jnp.where(row == ns - 1, bn_ref[0, g, 0:1, :], nxt)
        hid_pre = a_ref[0, g] + nxt
        hid = hid_pre * (1.0 / (1.0 + jnp.exp(-hid_pre)))
        out = jnp.dot(hid.astype(jnp.bfloat16), w2_ref[...], preferred_element_type=jnp.float32)
        if is_key:
            ms = jnp.sum(out * out, axis=-1, keepdims=True) * (1.0 / NSA_DK)
            out = out * lax.rsqrt(ms + EPS) * kn_ref[...]
            if alibi_lanes:
                out = out + _cmp_extra_lanes(ns, out.shape[1])
        o_ref[0, g] = out.astype(o_ref.dtype)


def nsa_compress_finish(part_a, part_b, part_b_new, w2, kn, *, is_key, alibi_lanes):
    batch, _, ns, _ = part_a.shape
    d = w2.shape[1]
    dp = NSA_DKP if is_key else d
    has_new = part_b_new is not None
    if not has_new:
        part_b_new = jnp.zeros((batch, NSA_KV, 8, CMP_HIDDEN), jnp.float32)
    w2p = jnp.pad(w2, ((0, 0), (0, dp - d))).astype(jnp.bfloat16)
    knp = jnp.pad(kn, (0, dp - d)).reshape(1, dp)
    blk = lambda n: pl.BlockSpec((1, NSA_KV, n, CMP_HIDDEN), lambda b: (b, 0, 0, 0))
    return pl.pallas_call(
        functools.partial(_compress_finish_kernel, is_key=is_key, has_new=has_new, alibi_lanes=alibi_lanes),
        grid=(batch,),
        in_specs=[blk(ns), blk(ns), blk(part_b_new.shape[2]),
                  pl.BlockSpec((CMP_HIDDEN, dp), lambda b: (0, 0)),
                  pl.BlockSpec((1, dp), lambda b: (0, 0))],
        out_specs=pl.BlockSpec((1, NSA_KV, ns, dp), lambda b: (b, 0, 0, 0)),
        out_shape=jax.ShapeDtypeStruct((batch, NSA_KV, ns, dp), jnp.bfloat16),
        compiler_params=_cparams("parallel"),
        name="nsa_compress_finish_k" if is_key else "nsa_compress_finish_v",
    )(part_a, part_b, part_b_new, w2p, knp)


def _split_dot(x, m_bf16):
    hi, lo = _split_bf16(x)
    f = functools.partial(jnp.dot, preferred_element_type=jnp.float32)
    return f(hi, m_bf16) + f(lo, m_bf16)


def _topn_mask_t(imp_t, t_row, nsb, topn):
    jb = lax.broadcasted_iota(jnp.int32, imp_t.shape, 0)
    cur = t_row // SLC_BLOCK
    forced = (jb == 0) | (jb == cur) | (jb == cur - 1)
    valid = (jb * SLC_BLOCK <= t_row) & (jb < nsb)
    impm = jnp.where(forced, jnp.inf, jnp.where(valid, imp_t, -jnp.inf))
    rank = jnp.zeros(imp_t.shape, jnp.float32)
    for k in range(nsb):
        row = impm[k:k + 1, :]
        beats = (row > impm) | ((row == impm) & (k < jb))
        rank = rank + jnp.where(beats, 1.0, 0.0)
    return (rank < topn) & (jb < nsb)


def _cmp_and_select(q, kc, vc, t0, tq, seq):
    rep = NSA_REP
    m_rows = rep * tq
    nc = seq // CMP_STRIDE - CMP_RATIO + 1
    nseg = kc.shape[0]
    nsb = seq // SLC_BLOCK
    t_col = t0 + lax.broadcasted_iota(jnp.int32, (rep, tq, 1), 1).reshape(m_rows, 1)
    t_row = t0 + lax.broadcasted_iota(jnp.int32, (1, tq), 1)
    n_row = lax.broadcasted_iota(jnp.int32, (1, nseg), 1)
    cvalid = (n_row * CMP_STRIDE + CMP_BLOCK - 1 <= t_col) & (n_row < nc)
    s = jnp.where(cvalid, _dot_nt(q, kc), _NEG)
    p = jnp.where(cvalid, jnp.exp(s - jnp.max(s, axis=1, keepdims=True)), 0.0)
    p = p / jnp.maximum(jnp.sum(p, axis=1, keepdims=True), 1e-30)
    o_c = jnp.dot(p.astype(jnp.bfloat16), vc, preferred_element_type=jnp.float32)
    psum = jnp.sum(p.reshape(rep, tq, nseg), axis=0)
    cj = lax.broadcasted_iota(jnp.int32, (_SEL_LANES, nseg), 0) * SLC_BLOCK
    ci = lax.broadcasted_iota(jnp.int32, (_SEL_LANES, nseg), 1) * CMP_STRIDE
    ov = jnp.maximum(jnp.minimum(ci + CMP_BLOCK, cj + SLC_BLOCK) - jnp.maximum(ci, cj), 0)
    c2s_t = (ov.astype(jnp.float32) * (1.0 / CMP_STRIDE)).astype(jnp.bfloat16)
    p_hi, p_lo = _split_bf16(psum)
    imp_t = _dot_nt(c2s_t, p_hi) + _dot_nt(c2s_t, p_lo)
    sel_t = _topn_mask_t(imp_t, t_row, nsb, min(SLC_TOPN, nsb))
    pen_t = jnp.where(sel_t, 0.0, -_MASK_BIG).astype(jnp.bfloat16)
    pi = lax.broadcasted_iota(jnp.int32, (_SEL_LANES, NSA_DKP), 0)
    pj = lax.broadcasted_iota(jnp.int32, (_SEL_LANES, NSA_DKP), 1)
    place = jnp.where(pj == pi + _SEL_LANE, 1.0, 0.0).astype(jnp.bfloat16)
    q_add = _dot_tn(pen_t, place).astype(jnp.bfloat16)
    q_slc = (q.reshape(rep, tq, NSA_DKP) + q_add[None]).reshape(m_rows, NSA_DKP)
    return o_c, q_slc


def _gate_combine(gl_ref, z_ref, o_ref, o_c, o_s, o_w, tq):
    gates = 1.0 / (1.0 + jnp.exp(-gl_ref[...]))
    for r in range(NSA_REP):
        rs = slice(r * tq, (r + 1) * tq)
        o = (gates[:, 3 * r:3 * r + 1] * o_c[rs] + gates[:, 3 * r + 1:3 * r + 2] * o_s[rs]
             + gates[:, 3 * r + 2:3 * r + 3] * o_w[rs])
        z = z_ref[:, r * NSA_DV:(r + 1) * NSA_DV]
        o_ref[:, r * NSA_DV:(r + 1) * NSA_DV] = (o * (z * (1.0 / (1.0 + jnp.exp(-z))))).astype(o_ref.dtype)


def _nsa_prompt_fast_kernel(q_ref, kc_ref, vc_ref, ks_ref, vs_ref, kw_ref, vw_ref, gl_ref, z_ref, o_ref,
                            *, tq, tk, seq):
    t0 = pl.program_id(2) * tq
    m_rows = NSA_REP * tq
    q = q_ref[0].reshape(m_rows, NSA_DKP)
    t_col = t0 + lax.broadcasted_iota(jnp.int32, (NSA_REP, tq, 1), 1).reshape(m_rows, 1)
    o_c, q_slc = _cmp_and_select(q, kc_ref[0, 0], vc_ref[0, 0], t0, tq, seq)
    kpos = lax.broadcasted_iota(jnp.int32, (1, tk), 1)
    kt_hi = (t0 + tq + tk - 1) // tk
    zero = jnp.zeros((m_rows, NSA_DVP), jnp.float32)

    def pv(p, v_ref, k0):
        return jnp.dot(p.astype(jnp.bfloat16), v_ref[0, pl.ds(k0, tk), :], preferred_element_type=jnp.float32)

    def slc_pair(j, acc):
        outs = []
        for h in range(2):
            kt = 2 * j + h
            k0 = pl.multiple_of(jnp.minimum(kt, kt_hi - 1) * tk, tk)
            s = _dot_nt(q_slc, ks_ref[0, pl.ds(k0, tk), :])
            outs.append((s, k0, ((k0 + kpos) <= t_col) & (kt < kt_hi)))
        ps = [jnp.where(ok, jnp.exp(s), 0.0) for s, _, ok in outs]
        return acc + pv(ps[0], vs_ref, outs[0][1]) + pv(ps[1], vs_ref, outs[1][1])

    acc = lax.fori_loop(0, (kt_hi + 1) // 2, slc_pair, zero)
    o_s = acc[:, 0:NSA_DV] / acc[:, NSA_DV:NSA_DV + 1]

    def win_tile(kt):
        k0 = pl.multiple_of(kt * tk, tk)
        dw = t_col - (k0 + kpos)
        return _dot_nt(q, kw_ref[0, pl.ds(k0, tk), :]), k0, (dw >= 0) & (dw < WINDOW)

    def win_body(kt, acc):
        s, k0, ok = win_tile(kt)
        return acc + pv(jnp.where(ok, jnp.exp(s), 0.0), vw_ref, k0)

    kt_lo = jnp.maximum(t0 - (WINDOW - 1), 0) // tk
    acc = lax.fori_loop(kt_lo, kt_hi - 2, win_body, zero)
    ta = win_tile(jnp.maximum(kt_hi - 2, 0))
    tb_ = win_tile(kt_hi - 1)
    pa = jnp.where(ta[2] & (kt_hi >= 2), jnp.exp(ta[0]), 0.0)
    pb = jnp.where(tb_[2], jnp.exp(tb_[0]), 0.0)
    acc = acc + pv(pa, vw_ref, ta[1]) + pv(pb, vw_ref, tb_[1])
    o_w = acc[:, 0:NSA_DV] / acc[:, NSA_DV:NSA_DV + 1]
    _gate_combine(gl_ref, z_ref, o_ref, o_c, o_s, o_w, tq)


def _flash_step(q, k, v, mask, carry):
    m, l, acc = carry
    s = _dot_nt(q, k)
    s = jnp.where(mask, s, _NEG)
    m_new = jnp.maximum(m, jnp.max(s, axis=1, keepdims=True))
    alpha = jnp.exp(m - m_new)
    p = jnp.exp(s - m_new)
    l = alpha * l + jnp.sum(p, axis=1, keepdims=True)
    acc = alpha * acc + jnp.dot(p.astype(jnp.bfloat16), v, preferred_element_type=jnp.float32)
    return m_new, l, acc


def _nsa_prompt_kernel(q_ref, kc_ref, vc_ref, ks_ref, vs_ref, kw_ref, vw_ref, gl_ref, z_ref, o_ref,
                       *, tq, tk, seq):
    t0 = pl.program_id(2) * tq
    m_rows = NSA_REP * tq
    q = q_ref[0].reshape(m_rows, NSA_DKP)
    t_col = t0 + lax.broadcasted_iota(jnp.int32, (NSA_REP, tq, 1), 1).reshape(m_rows, 1)
    o_c, q_slc = _cmp_and_select(q, kc_ref[0, 0], vc_ref[0, 0], t0, tq, seq)

    kpos = lax.broadcasted_iota(jnp.int32, (1, tk), 1)
    init = (jnp.full((m_rows, 1), _NEG, jnp.float32), jnp.zeros((m_rows, 1), jnp.float32),
            jnp.zeros((m_rows, NSA_DV), jnp.float32))

    def slc_body(kt, carry):
        k0 = pl.multiple_of(kt * tk, tk)
        mask = (k0 + kpos) <= t_col
        return _flash_step(q_slc, ks_ref[0, pl.ds(k0, tk), :], vs_ref[0, pl.ds(k0, tk), :], mask, carry)

    _, l_s, acc_s = lax.fori_loop(0, (t0 + tq + tk - 1) // tk, slc_body, init)

    def win_body(kt, carry):
        k0 = pl.multiple_of(kt * tk, tk)
        dw = t_col - (k0 + kpos)
        mask = (dw >= 0) & (dw < WINDOW)
        return _flash_step(q, kw_ref[0, pl.ds(k0, tk), :], vw_ref[0, pl.ds(k0, tk), :], mask, carry)

    kt_lo = jnp.maximum(t0 - (WINDOW - 1), 0) // tk
    _, l_w, acc_w = lax.fori_loop(kt_lo, (t0 + tq + tk - 1) // tk, win_body, init)
    _gate_combine(gl_ref, z_ref, o_ref, o_c, acc_s / l_s, acc_w / l_w, tq)


def nsa_prompt_attention(q_hm, kcmp, vcmp, ks_hm, vs_hm, kw_hm, vw_hm, gl, z, *, batch, tq, tk, fast):
    rows = z.shape[0]
    seq = rows // batch
    nq = seq // tq
    nseg = kcmp.shape[2]
    row_blk = lambda b, g, i: (b * nq + i, g)
    kv_blk = lambda b, g, i: (g, b, 0)
    dvb = NSA_DVP if fast else NSA_DV
    return pl.pallas_call(
        functools.partial(_nsa_prompt_fast_kernel if fast else _nsa_prompt_kernel, tq=tq, tk=tk, seq=seq),
        grid=(batch, NSA_KV, nq),
        in_specs=[pl.BlockSpec((1, NSA_REP, tq, NSA_DKP), lambda b, g, i: (g, 0, b * nq + i, 0)),
                  pl.BlockSpec((1, 1, nseg, NSA_DKP), lambda b, g, i: (b, g, 0, 0)),
                  pl.BlockSpec((1, 1, nseg, NSA_DV), lambda b, g, i: (b, g, 0, 0)),
                  pl.BlockSpec((1, seq, NSA_DKP), kv_blk),
                  pl.BlockSpec((1, seq, dvb), kv_blk),
                  pl.BlockSpec((1, seq, NSA_DKP), kv_blk),
                  pl.BlockSpec((1, seq, dvb), kv_blk),
                  pl.BlockSpec((tq, LANE), row_blk),
                  pl.BlockSpec((tq, NSA_REP * NSA_DV), row_blk)],
        out_specs=pl.BlockSpec((tq, NSA_REP * NSA_DV), row_blk),
        out_shape=jax.ShapeDtypeStruct((rows, NSA_O), jnp.bfloat16),
        compiler_params=_cparams("parallel", "parallel", "arbitrary"),
        name="nsa_prompt_attention_fast" if fast else "nsa_prompt_attention",
    )(q_hm, kcmp, vcmp, ks_hm, vs_hm, kw_hm, vw_hm, gl, z)


def _nsa_in_proj(x, norm_w, w_in):
    B, L, D = x.shape
    x2 = x.reshape(B * L, D)
    h = rmsnorm_bf16(x2, norm_w, tm=min(256, B * L))
    offs = np.cumsum((0,) + NSA_SPLITS)
    tm = _row_tile(B * L)

    wt = w_in.T

    def seg(i, tn, name):
        return matmul_wt(h, wt, tm=tm, tn=tn, n_off=int(offs[i]), n=int(offs[i + 1] - offs[i]), name=name)

    kv = matmul_wt(h, wt, tm=tm, tn=768, n_off=int(offs[1]), n=int(offs[7] - offs[1]), name="nsa_kv_proj")
    kc, vc, ks, vs, kw, vw = (kv[:, int(offs[i] - offs[1]):int(offs[i + 1] - offs[1])] for i in range(1, 7))
    wg = wt[offs[7]:offs[8]].reshape(NSA_KV, 3 * NSA_REP, D)
    wg = jnp.pad(wg, ((0, 0), (0, LANE - 3 * NSA_REP), (0, 0))).reshape(NSA_KV * LANE, D)
    gl = matmul_wt(h, wg, tm=tm, tn=NSA_KV * LANE, n_off=0, n=NSA_KV * LANE, name="nsa_gate_proj")
    z = matmul_wt(h, wt[offs[8]:], tm=tm, tn=_WT_TN, n_off=0, n=NSA_O, name="nsa_z_proj")
    q_of = functools.partial(nsa_q_proj_prep, h, wt, tm=tm)
    return x2, q_of, kc, vc, ks, vs, kw, vw, gl, z


def _nsa_prompt(x, norm_w, w_in, q_norm, kn_c, kn_s, kn_w, pe_k, w1_k, w2_k, pe_v, w1_v, w2_v, w_out):
    B, T, D = x.shape
    x2, q_of, kc_r, vc_r, ks_raw, vs_r, kw_raw, vw_r, gl, z = _nsa_in_proj(x, norm_w, w_in)
    tp = min(512, T)
    q_hm = q_of(q_norm, seq=T)
    ks_r, ks_hm = nsa_k_prep(ks_raw, kn_s, seq=T, tk=tp, onehot=True)
    kw_r, kw_hm = nsa_k_prep(kw_raw, kn_w, seq=T, tk=tp, onehot=False)
    vs_hm = nsa_v_prep(vs_r, tk=tp)
    vw_hm = nsa_v_prep(vw_r, tk=tp)
    npg = T // PAGE_ROWS
    table = (jnp.arange(B, dtype=jnp.int32)[:, None] * npg + jnp.arange(npg, dtype=jnp.int32)[None, :])
    seg3 = lambda a: a.reshape(B * T // CMP_STRIDE, CMP_STRIDE, a.shape[1])
    ka, kb = nsa_compress_partials(seg3(kc_r), table, pe_k, w1_k, pages=min(_CMP_PAGES, npg))
    va, vb = nsa_compress_partials(seg3(vc_r), table, pe_v, w1_v, pages=min(_CMP_PAGES, npg))
    kcmp = nsa_compress_finish(ka, kb, None, w2_k, kn_c, is_key=True, alibi_lanes=True)
    vcmp = nsa_compress_finish(va, vb, None, w2_v, jnp.zeros((NSA_DV,), jnp.float32), is_key=False, alibi_lanes=False)
    bound = jnp.max(jnp.abs(q_norm)) * jnp.maximum(jnp.max(jnp.abs(kn_s)), jnp.max(jnp.abs(kn_w))) * (NSA_DK ** 0.5)
    attend = lambda fast: functools.partial(nsa_prompt_attention, batch=B, tq=_NSA_TQ, tk=_NSA_TK, fast=fast)
    og = lax.cond(bound < _FAST_SCORE_BOUND, attend(True), attend(False),
                  q_hm, kcmp, vcmp, ks_hm, vs_hm, kw_hm, vw_hm, gl, z)
    y = matmul(og, w_out.astype(jnp.bfloat16), tm=_row_tile(B * T), tn=1024, residual=x2, name="nsa_out_proj")
    wl = min(WINDOW, T)
    r4 = lambda a, d: a.reshape(B, T, NSA_KV, d)
    return y.reshape(B, T, D), (r4(kc_r, NSA_DK), r4(vc_r, NSA_DV), r4(ks_r, NSA_DK), r4(vs_r, NSA_DV),
                                r4(kw_r, NSA_DK)[:, T - wl:], r4(vw_r, NSA_DV)[:, T - wl:])


_SMP_PAGES = 8
_SEL_PAD = 384


def _online_update(m_ref, l_ref, acc_ref, g, s, v):
    m_old = m_ref[g]
    m_new = jnp.maximum(m_old, jnp.max(s, axis=1, keepdims=True))
    alpha = jnp.exp(m_old - m_new)
    p = jnp.exp(s - m_new)
    l_ref[g] = alpha * l_ref[g] + jnp.sum(p, axis=1, keepdims=True)
    acc_ref[g] = alpha * acc_ref[g] + jnp.dot(p.astype(jnp.bfloat16), v.astype(jnp.bfloat16),
                                              preferred_element_type=jnp.float32)
    m_ref[g] = m_new


def _nsa_sample_kernel(tab_ref, *refs, past, steps, pages, nsb):
    kp_refs = refs[:pages]
    vp_refs = refs[pages:2 * pages]
    (q_ref, kc_ref, vc_ref, kn_ref, vn_ref, kwc_ref, vwc_ref, kwn_ref, vwn_ref, gl_ref, z_ref,
     o_ref, m_scr, l_scr, acc_scr, sel_scr, oc_scr) = refs[2 * pages:]
    i = pl.program_id(1)
    s_tok = o_ref.shape[0]
    rows = NSA_REP * s_tok
    ncmp = kc_ref.shape[2]
    r_col = lax.broadcasted_iota(jnp.int32, (NSA_REP, s_tok, 1), 0).reshape(rows, 1)
    t_col = past + lax.broadcasted_iota(jnp.int32, (NSA_REP, s_tok, 1), 1).reshape(rows, 1)
    t_tok = past + lax.broadcasted_iota(jnp.int32, (s_tok, 1), 0)

    def slope_col(g):
        return jnp.exp2((r_col + (g * NSA_REP + 1)).astype(jnp.float32) * (-8.0 / NSA_HEADS))

    def q_of(g):
        return q_ref[0, g * rows:(g + 1) * rows, :]

    def rep_rows(x):
        return jnp.concatenate([x] * NSA_REP, axis=0)

    @pl.when(i == 0)
    def _first():
        n_row = lax.broadcasted_iota(jnp.int32, (1, ncmp), 1)
        cvalid = (n_row * CMP_STRIDE + CMP_BLOCK - 1) <= t_col
        ci = lax.broadcasted_iota(jnp.int32, (ncmp, _SEL_PAD), 0) * CMP_STRIDE
        cj = lax.broadcasted_iota(jnp.int32, (ncmp, _SEL_PAD), 1) * SLC_BLOCK
        ov = jnp.maximum(jnp.minimum(ci + CMP_BLOCK, cj + SLC_BLOCK) - jnp.maximum(ci, cj), 0)
        c2s = (ov.astype(jnp.float32) * (1.0 / CMP_STRIDE)).astype(jnp.bfloat16)
        jb = lax.broadcasted_iota(jnp.int32, (s_tok, _SEL_PAD), 1)
        cur = t_tok // SLC_BLOCK
        forced = (jb == 0) | (jb == cur) | (jb == cur - 1)
        valid = (jb * SLC_BLOCK <= t_tok) & (jb < nsb)
        for g in range(NSA_KV):
            s = jnp.where(cvalid, _dot_nt(q_of(g), kc_ref[0, g]), _NEG)
            p = jnp.where(cvalid, jnp.exp(s - jnp.max(s, axis=1, keepdims=True)), 0.0)
            p = p / jnp.maximum(jnp.sum(p, axis=1, keepdims=True), 1e-30)
            oc_scr[g] = jnp.dot(p.astype(jnp.bfloat16), vc_ref[0, g], preferred_element_type=jnp.float32)
            imp = _split_dot(jnp.sum(p.reshape(NSA_REP, s_tok, ncmp), axis=0), c2s)
            work = jnp.where(forced, jnp.inf, jnp.where(valid, imp, -jnp.inf))
            sel = jnp.zeros((s_tok, _SEL_PAD), jnp.float32)
            for _ in range(min(SLC_TOPN, nsb)):
                mx = jnp.max(work, axis=1, keepdims=True)
                first = jnp.min(jnp.where(work == mx, jb, _SEL_PAD), axis=1, keepdims=True)
                hit = jb == first
                sel = jnp.where(hit, 1.0, sel)
                work = jnp.where(hit, -jnp.inf, work)
            sel_scr[g] = sel
            m_scr[g] = jnp.full((rows, 1), _NEG, jnp.float32)
            l_scr[g] = jnp.zeros((rows, 1), jnp.float32)
            acc_scr[g] = jnp.zeros((rows, NSA_DV), jnp.float32)

    def fold(score_of, v_of, nk, page0, causal):
        pos = page0 * PAGE_ROWS + lax.broadcasted_iota(jnp.int32, (1, nk), 1)
        bj = lax.broadcasted_iota(jnp.int32, (_SEL_PAD, nk), 0)
        bl = lax.broadcasted_iota(jnp.int32, (_SEL_PAD, nk), 1)
        expand = jnp.where(bj == (page0 * PAGE_ROWS + bl) // SLC_BLOCK, 1.0, 0.0).astype(jnp.bfloat16)
        for g in range(NSA_KV):
            selk = jnp.dot(sel_scr[g].astype(jnp.bfloat16), expand, preferred_element_type=jnp.float32)
            ok = rep_rows(selk) > 0.5
            if causal:
                ok = ok & (pos <= t_col)
            s = score_of(g, q_of(g)[:, 0:NSA_DK])
            s = s - slope_col(g) * (t_col - pos).astype(jnp.float32)
            _online_update(m_scr, l_scr, acc_scr, g, jnp.where(ok, s, _NEG), v_of(g))

    fold(lambda g, qg: _dot(qg, jnp.concatenate([r[0, g] for r in kp_refs], axis=1)),
         lambda g: jnp.concatenate([r[0, pl.ds(g, PAGE_ROWS, stride=NSA_KV), :] for r in vp_refs], axis=0),
         pages * PAGE_ROWS, i * pages, causal=False)

    @pl.when(i == steps - 1)
    def _last():
        fold(lambda g, qg: _dot_nt(qg, kn_ref[0, :, g * NSA_DK:(g + 1) * NSA_DK]),
             lambda g: vn_ref[0, :, g * NSA_DV:(g + 1) * NSA_DV], PAGE_ROWS, past // PAGE_ROWS, causal=True)
        nwc = kwc_ref.shape[3]
        posw = jnp.concatenate([past - nwc + lax.broadcasted_iota(jnp.int32, (1, nwc), 1),
                                past + lax.broadcasted_iota(jnp.int32, (1, PAGE_ROWS), 1)], axis=1)
        dw = t_col - posw
        wok = (dw >= 0) & (dw < WINDOW) & (posw >= 0)
        gates = 1.0 / (1.0 + jnp.exp(-gl_ref[...]))
        for g in range(NSA_KV):
            ksl = slice(g * NSA_DK, (g + 1) * NSA_DK)
            vsl = slice(g * NSA_DV, (g + 1) * NSA_DV)
            qg = q_of(g)[:, 0:NSA_DK]
            s = jnp.concatenate([_dot(qg, kwc_ref[0, g]), _dot_nt(qg, kwn_ref[0, :, ksl])], axis=1)
            s = jnp.where(wok, s - slope_col(g) * dw.astype(jnp.float32), _NEG)
            p = jnp.where(wok, jnp.exp(s - jnp.max(s, axis=1, keepdims=True)), 0.0)
            p = (p / jnp.maximum(jnp.sum(p, axis=1, keepdims=True), 1e-30)).astype(jnp.bfloat16)
            o_w = (jnp.dot(p[:, 0:nwc], vwc_ref[0, :, vsl].astype(jnp.bfloat16), preferred_element_type=jnp.float32)
                   + jnp.dot(p[:, nwc:], vwn_ref[0, :, vsl].astype(jnp.bfloat16), preferred_element_type=jnp.float32))
            o_s = acc_scr[g] / l_scr[g]
            o_c = oc_scr[g]
            for r in range(NSA_REP):
                rs = slice(r * s_tok, (r + 1) * s_tok)
                c0 = g * LANE + 3 * r
                o = (gates[:, c0:c0 + 1] * o_c[rs] + gates[:, c0 + 1:c0 + 2] * o_s[rs]
                     + gates[:, c0 + 2:c0 + 3] * o_w[rs])
                hs = slice((g * NSA_REP + r) * NSA_DV, (g * NSA_REP + r + 1) * NSA_DV)
                z = z_ref[:, hs]
                o_ref[:, hs] = o * (z * (1.0 / (1.0 + jnp.exp(-z))))


def nsa_sample_attention(q_rows, kcmp, vcmp, pool_k, pool_v, page_table, k_new, v_new, kw_cache, vw_cache,
                         kw_new, vw_new, gl, z, *, s_tok):
    batch, npg = page_table.shape
    pages = _SMP_PAGES
    steps = npg // pages
    past = npg * PAGE_ROWS
    nsb = -(-(past + s_tok) // SLC_BLOCK)
    rows = NSA_REP * s_tok
    ncmp = kcmp.shape[2]
    nwc = kw_cache.shape[3]
    kpage_spec = lambda k: pl.BlockSpec((1, NSA_KV, NSA_DK, PAGE_ROWS), lambda b, i, tab: (tab[b, i * pages + k], 0, 0, 0))
    vpage_spec = lambda k: pl.BlockSpec((1, PAGE_ROWS * NSA_KV, NSA_DV), lambda b, i, tab: (tab[b, i * pages + k], 0, 0))
    per_b = lambda shape: pl.BlockSpec((1,) + shape, lambda b, i, tab: (b,) + (0,) * len(shape))
    tok_blk = lambda w: pl.BlockSpec((s_tok, w), lambda b, i, tab: (b, 0))
    return pl.pallas_call(
        functools.partial(_nsa_sample_kernel, past=past, steps=steps, pages=pages, nsb=nsb),
        grid_spec=pltpu.PrefetchScalarGridSpec(
            num_scalar_prefetch=1,
            grid=(batch, steps),
            in_specs=[kpage_spec(k) for k in range(pages)] + [vpage_spec(k) for k in range(pages)]
            + [per_b((NSA_KV * rows, NSA_DKP)), per_b((NSA_KV, ncmp, NSA_DKP)), per_b((NSA_KV, ncmp, NSA_DV)),
               per_b((PAGE_ROWS, NSA_KW)), per_b((PAGE_ROWS, NSA_VW)),
               per_b((NSA_KV, NSA_DK, nwc)), per_b((nwc, NSA_VW)), per_b((PAGE_ROWS, NSA_KW)), per_b((PAGE_ROWS, NSA_VW)),
               tok_blk(NSA_KV * LANE), tok_blk(NSA_O)],
            out_specs=tok_blk(NSA_O),
            scratch_shapes=[pltpu.VMEM((NSA_KV, rows, 1), jnp.float32), pltpu.VMEM((NSA_KV, rows, 1), jnp.float32),
                            pltpu.VMEM((NSA_KV, rows, NSA_DV), jnp.float32),
                            pltpu.VMEM((NSA_KV, s_tok, _SEL_PAD), jnp.float32),
                            pltpu.VMEM((NSA_KV, rows, NSA_DV), jnp.float32)]),
        out_shape=jax.ShapeDtypeStruct((batch * s_tok, NSA_O), jnp.float32),
        compiler_params=_cparams("parallel", "arbitrary"),
        name="nsa_sample_attention",
    )(page_table, *([pool_k] * pages), *([pool_v] * pages), q_rows, kcmp, vcmp, k_new, v_new,
      kw_cache, vw_cache, kw_new, vw_new, gl, z)


def _nsa_sample(x, ck, cv, sk, sv, wk_buf, wv_buf, page_table, norm_w,
                w_in, q_norm, kn_c, kn_s, kn_w, pe_k, w1_k, w2_k, pe_v, w1_v, w2_v, w_out):
    B, S, D = x.shape
    x2, q_of, kc_r, vc_r, ks_raw, vs_r, kw_raw, vw_r, gl, z = _nsa_in_proj(x, norm_w, w_in)
    q_hm = q_of(q_norm, seq=S, pos0=page_table.shape[1] * PAGE_ROWS)
    q_rows = q_hm.reshape(NSA_KV, NSA_REP, B, S, NSA_DKP).transpose(2, 0, 1, 3, 4).reshape(B, NSA_KV * NSA_REP * S, NSA_DKP)
    ks_r, _ = nsa_k_prep(ks_raw, kn_s, seq=S, tk=B * S, onehot=False)
    kw_r, _ = nsa_k_prep(kw_raw, kn_w, seq=S, tk=B * S, onehot=False)
    as_page = lambda a: jnp.pad(a.reshape(B, S, -1), ((0, 0), (0, PAGE_ROWS - S), (0, 0)))
    seg3 = lambda a, w: a.reshape(-1, CMP_STRIDE, w)
    seg4 = lambda a: a.reshape(-1, CMP_STRIDE * NSA_KV, a.shape[3])
    ident = jnp.arange(B, dtype=jnp.int32)[:, None]
    rows_minor = lambda a: jnp.transpose(a, (0, 2, 3, 1))
    ka, kb = nsa_compress_partials(rows_minor(ck), page_table, pe_k, w1_k, pages=_CMP_PAGES, transposed=True)
    va, vb = nsa_compress_partials(seg4(cv), page_table, pe_v, w1_v, pages=_CMP_PAGES)
    _, kb_new = nsa_compress_partials(seg3(as_page(kc_r), NSA_KW), ident, pe_k, w1_k, pages=1)
    _, vb_new = nsa_compress_partials(seg3(as_page(vc_r), NSA_VW), ident, pe_v, w1_v, pages=1)
    kcmp = nsa_compress_finish(ka, kb, kb_new, w2_k, kn_c, is_key=True, alibi_lanes=True)
    vcmp = nsa_compress_finish(va, vb, vb_new, w2_v, jnp.zeros((NSA_DV,), jnp.float32), is_key=False, alibi_lanes=False)
    wl = wk_buf.shape[1]
    o = nsa_sample_attention(q_rows, kcmp, vcmp, rows_minor(sk), sv.reshape(sv.shape[0], PAGE_ROWS * NSA_KV, NSA_DV), page_table, as_page(ks_r), as_page(vs_r), rows_minor(wk_buf),
                             wv_buf.reshape(B, wl, NSA_VW), as_page(kw_r), as_page(vw_r), gl, z, s_tok=S)
    y = matmul(o.astype(jnp.bfloat16), w_out.astype(jnp.bfloat16), tm=B * S, tn=1024, residual=x2, name="nsa_out_proj")
    r4 = lambda a, d: a.reshape(B, S, NSA_KV, d)
    kw_all = jnp.concatenate([wk_buf, r4(kw_r, NSA_DK)], axis=1)
    vw_all = jnp.concatenate([wv_buf, r4(vw_r, NSA_DV)], axis=1)
    return y.reshape(B, S, D), (r4(kc_r, NSA_DK), r4(vc_r, NSA_DV), r4(ks_r, NSA_DK), r4(vs_r, NSA_DV),
                                kw_all[:, S:], vw_all[:, S:])


def kernel(x_prompt, x_sample, state_delta, state_conv, cache_cmp_k, cache_cmp_v, cache_slc_k, cache_slc_v, cache_win_k, cache_win_v, page_table, norm_dn, w_in_dn, conv_w_dn, a_log_dn, dt_bias_dn, out_norm_dn, w_out_dn, norm_nsa, w_in_nsa, q_norm_nsa, k_norm_cmp, k_norm_slc, k_norm_win, cmp_pe_k, cmp_w1_k, cmp_w2_k, cmp_pe_v, cmp_w1_v, cmp_w2_v, w_out_nsa):
    xp, xs = x_prompt, x_sample
    B = xp.shape[0]
    dw = (norm_dn[0], w_in_dn[0], conv_w_dn[0], a_log_dn[0], dt_bias_dn[0], out_norm_dn[0], w_out_dn[0])
    buf0 = jnp.zeros((B, DN_CONV - 1, conv_w_dn.shape[-1]), xp.dtype)
    st0 = jnp.zeros((B, DN_HEADS, DN_DK, DN_DV), jnp.float32)
    xp, p_dn = _deltanet_mixer(xp, buf0, st0, *dw)
    xs, s_dn = _deltanet_mixer(xs, state_conv[0], state_delta[0], *dw)
    nw = (norm_nsa[0], w_in_nsa[0], q_norm_nsa[0], k_norm_cmp[0], k_norm_slc[0], k_norm_win[0],
          cmp_pe_k[0], cmp_w1_k[0], cmp_w2_k[0], cmp_pe_v[0], cmp_w1_v[0], cmp_w2_v[0], w_out_nsa[0])
    xp, p_nsa = _nsa_prompt(xp, *nw)
    xs, s_nsa = _nsa_sample(xs, cache_cmp_k[0], cache_cmp_v[0], cache_slc_k[0], cache_slc_v[0],
                            cache_win_k[0], cache_win_v[0], page_table, *nw)
    return ((xp, xs, p_dn[0][None], p_dn[1][None]) + tuple(t[None] for t in p_nsa)
            + (s_dn[0][None], s_dn[1][None]) + tuple(t[None] for t in s_nsa))
```

```python
import functools

import jax
import jax.numpy as jnp
import numpy as np
from jax import lax
from jax.experimental import pallas as pl
from jax.experimental.pallas import tpu as pltpu

D_MODEL = 4096
EPS = 1e-6

DN_HEADS = 32
DN_DK = 128
DN_DV = 128
DN_CONV = 4
DN_CHUNK = 64

NSA_HEADS = 32
NSA_KV = 4
NSA_REP = NSA_HEADS // NSA_KV
NSA_DK = 192
NSA_DV = 128
CMP_BLOCK = 32
CMP_STRIDE = 16
CMP_RATIO = CMP_BLOCK // CMP_STRIDE
CMP_HIDDEN = 256
SLC_BLOCK = 64
SLC_TOPN = 16
WINDOW = 512
NSA_Q = NSA_HEADS * NSA_DK
NSA_KW = NSA_KV * NSA_DK
NSA_VW = NSA_KV * NSA_DV
NSA_O = NSA_HEADS * NSA_DV
NSA_SPLITS = (NSA_Q, NSA_KW, NSA_VW, NSA_KW, NSA_VW, NSA_KW, NSA_VW, 3 * NSA_HEADS, NSA_O)

VMEM_LIMIT_BYTES = 56 * 1024 * 1024
LANE = 128
_WT_TN = 512


def _cparams(*sem):
    return pltpu.CompilerParams(dimension_semantics=sem, vmem_limit_bytes=VMEM_LIMIT_BYTES)


def _rmsnorm_kernel(x_ref, w_ref, o_ref):
    x = x_ref[...]
    ms = jnp.mean(x * x, axis=-1, keepdims=True)
    o_ref[...] = (x * lax.rsqrt(ms + EPS) * w_ref[...]).astype(o_ref.dtype)


def rmsnorm_bf16(x, w, *, tm):
    m, d = x.shape
    return pl.pallas_call(
        _rmsnorm_kernel,
        grid=(m // tm,),
        in_specs=[pl.BlockSpec((tm, d), lambda i: (i, 0)), pl.BlockSpec((1, d), lambda i: (0, 0))],
        out_specs=pl.BlockSpec((tm, d), lambda i: (i, 0)),
        out_shape=jax.ShapeDtypeStruct((m, d), jnp.bfloat16),
        compiler_params=_cparams("parallel"),
        name="rmsnorm_bf16",
    )(x, w.reshape(1, d))


def _mm_kernel(a_ref, b_ref, o_ref):
    o_ref[...] = jnp.dot(a_ref[...], b_ref[...], preferred_element_type=jnp.float32).astype(o_ref.dtype)


def _mm_res_kernel(a_ref, b_ref, r_ref, o_ref):
    acc = jnp.dot(a_ref[...], b_ref[...], preferred_element_type=jnp.float32)
    o_ref[...] = r_ref[...] + acc


def matmul(a, b, *, tm, tn, residual=None, out_dtype=jnp.float32, name="matmul"):
    m, k = a.shape
    _, n = b.shape
    assert m % tm == 0 and n % tn == 0
    in_specs = [pl.BlockSpec((tm, k), lambda i, j: (i, 0)), pl.BlockSpec((k, tn), lambda i, j: (0, j))]
    args = [a, b]
    kern = _mm_kernel
    if residual is not None:
        in_specs.append(pl.BlockSpec((tm, tn), lambda i, j: (i, j)))
        args.append(residual)
        kern = _mm_res_kernel
    return pl.pallas_call(
        kern,
        grid=(m // tm, n // tn),
        in_specs=in_specs,
        out_specs=pl.BlockSpec((tm, tn), lambda i, j: (i, j)),
        out_shape=jax.ShapeDtypeStruct((m, n), out_dtype),
        compiler_params=_cparams("parallel", "parallel"),
        name=name,
    )(*args)


def _mm_wt_kernel(a_ref, w_ref, o_ref):
    o_ref[...] = lax.dot_general(a_ref[...], w_ref[...].astype(jnp.bfloat16), (((1,), (1,)), ((), ())),
                                 preferred_element_type=jnp.float32)


def matmul_wt(a, wt, *, tm, tn, n_off, n, name):
    m, k = a.shape
    assert m % tm == 0 and n % tn == 0 and n_off % tn == 0
    j0 = n_off // tn
    return pl.pallas_call(
        _mm_wt_kernel,
        grid=(m // tm, n // tn),
        in_specs=[pl.BlockSpec((tm, k), lambda i, j: (i, 0)), pl.BlockSpec((tn, k), lambda i, j: (j0 + j, 0))],
        out_specs=pl.BlockSpec((tm, tn), lambda i, j: (i, j)),
        out_shape=jax.ShapeDtypeStruct((m, n), jnp.float32),
        compiler_params=_cparams("parallel", "parallel"),
        name=name,
    )(a, wt)


def _row_tile(m):
    return 1024 if m % 1024 == 0 else m


def _dot(a, b):
    return jnp.dot(a.astype(jnp.bfloat16), b.astype(jnp.bfloat16), preferred_element_type=jnp.float32)


def _dot_nt(a, b):
    return lax.dot_general(a.astype(jnp.bfloat16), b.astype(jnp.bfloat16), (((1,), (1,)), ((), ())),
                           preferred_element_type=jnp.float32)


def _dot_tn(a, b):
    return lax.dot_general(a.astype(jnp.bfloat16), b.astype(jnp.bfloat16), (((0,), (0,)), ((), ())),
                           preferred_element_type=jnp.float32)


def _split_bf16(x):
    hi = x.astype(jnp.bfloat16)
    lo = (x - hi.astype(jnp.float32)).astype(jnp.bfloat16)
    return hi, lo


def _dn_gate_kernel(h_ref, wt_ref, alog_ref, dtb_ref, o_ref, *, heads):
    r = lax.dot_general(wt_ref[...], h_ref[...], (((1,), (1,)), ((), ())),
                        preferred_element_type=jnp.float32)
    b = r[:heads]
    a = r[heads:] + dtb_ref[...]
    softplus = jnp.maximum(a, 0.0) + jnp.log(1.0 + jnp.exp(-jnp.abs(a)))
    o_ref[0:heads, :] = 1.0 / (1.0 + jnp.exp(-b))
    o_ref[heads:2 * heads, :] = -jnp.exp(alog_ref[...]) * softplus


def dn_gates(h, w_ba_t, a_log, dt_bias, *, tm):
    m, d = h.shape
    heads = a_log.shape[0]
    return pl.pallas_call(
        functools.partial(_dn_gate_kernel, heads=heads),
        grid=(m // tm,),
        in_specs=[pl.BlockSpec((tm, d), lambda i: (i, 0)),
                  pl.BlockSpec((2 * heads, d), lambda i: (0, 0)),
                  pl.BlockSpec((heads, 1), lambda i: (0, 0)),
                  pl.BlockSpec((heads, 1), lambda i: (0, 0))],
        out_specs=pl.BlockSpec((2 * heads, tm), lambda i: (0, i)),
        out_shape=jax.ShapeDtypeStruct((2 * heads, m), jnp.float32),
        compiler_params=_cparams("parallel"),
        name="dn_gates",
    )(h, w_ba_t, a_log.reshape(heads, 1), dt_bias.reshape(heads, 1))


_DN_TAIL = 8
_DN_TB = 256
_DN_TMIN = 2 * DN_CHUNK
_DN_HB = 8


def _dn_chunk_prep(q, k, g_row, b_row):
    c = q.shape[0]
    ii = lax.broadcasted_iota(jnp.int32, (c, c), 0)
    jj = lax.broadcasted_iota(jnp.int32, (c, c), 1)
    lower, strict, eye = ii >= jj, ii > jj, ii == jj
    g_b = jnp.broadcast_to(g_row, (c, c))
    b_b = jnp.broadcast_to(b_row, (c, c))
    b_col = jnp.sum(jnp.where(eye, b_b, 0.0), axis=1, keepdims=True)
    g_col = jnp.sum(jnp.where(eye, g_b, 0.0), axis=1, keepdims=True)
    gc_col = jnp.sum(jnp.where(lower, g_b, 0.0), axis=1, keepdims=True)
    gc_row = jnp.sum(jnp.where(ii <= jj, g_col, 0.0), axis=0, keepdims=True)
    g_last = jnp.sum(g_row, axis=1, keepdims=True)
    decay = jnp.exp(jnp.where(lower, gc_col - gc_row, -jnp.inf))
    kk = _dot_nt(k, k)
    qk = _dot_nt(q, k)
    lmat = jnp.where(strict, kk * decay, 0.0) * b_col
    attn = qk * decay
    return dict(x=-lmat, p=eye.astype(jnp.float32) - lmat, attn=attn, b_row=b_row,
                e_row=jnp.exp(gc_row), e_col=jnp.exp(gc_col), c_col=jnp.exp(g_last - gc_col),
                e_last=jnp.exp(g_last))


def _dn_inverse_levels(items, c):
    m = 2
    while m < c:
        for it in items:
            it["x"] = _dot(it["x"], it["x"])
        for it in items:
            it["p"] = it["p"] + _dot(it["p"], it["x"])
        m *= 2


def _dn_kernel(q_ref, k_ref, v_ref, z_ref, beta_ref, g_ref, cw_ref, cs_ref, s0_ref, onw_ref,
               o_ref, s_out_ref, xpad, s_scr, *, tb, hb, live):
    n = pl.program_id(2)
    c = DN_CHUNK
    d = DN_DK
    nc = live

    @pl.when(n == 0)
    def _init():
        xpad[:, 0:_DN_TAIL, :] = cs_ref[0]
        s_scr[...] = s0_ref[0]

    def conv(i, ref):
        xpad[i, _DN_TAIL:_DN_TAIL + tb, :] = ref[...]
        acc = None
        for j in range(DN_CONV):
            off = _DN_TAIL - (DN_CONV - 1) + j
            term = xpad[i, off:off + tb, :] * cw_ref[i, j:j + 1, :]
            acc = term if acc is None else acc + term
        xpad[i, 0:_DN_TAIL, :] = xpad[i, tb:tb + _DN_TAIL, :]
        return acc * (1.0 / (1.0 + jnp.exp(-acc)))

    qc, kc, vc = conv(0, q_ref), conv(1, k_ref), conv(2, v_ref)
    items = []
    for hh in range(hb):
        sl = slice(hh * d, (hh + 1) * d)
        q = qc[:, sl]
        k = kc[:, sl]
        q = q * (lax.rsqrt(jnp.sum(q * q, axis=-1, keepdims=True) + EPS) * (d ** -0.5))
        k = k * lax.rsqrt(jnp.sum(k * k, axis=-1, keepdims=True) + EPS)
        g_all = g_ref[hh]
        b_all = beta_ref[hh]
        for ci in range(nc):
            rs = slice(ci * c, (ci + 1) * c)
            it = _dn_chunk_prep(q[rs], k[rs], g_all[:, rs], b_all[:, rs])
            it.update(q=q[rs], k=k[rs], v=vc[rs, sl])
            items.append(it)
    _dn_inverse_levels(items, c)
    for it in items:
        inv_b = it["p"] * it["b_row"]
        it["u"] = _dot(inv_b, it["v"])
        it["w"] = _dot(inv_b * it["e_row"], it["k"])
    states = [s_scr[hh] for hh in range(hb)]
    outs = [[None] * nc for _ in range(hb)]
    for ci in range(nc):
        for hh in range(hb):
            it = items[hh * nc + ci]
            s = states[hh]
            v_new = it["u"] - _dot(it["w"], s)
            outs[hh][ci] = it["e_col"] * _dot(it["q"], s) + _dot(it["attn"], v_new)
            states[hh] = s * it["e_last"] + _dot_tn(it["k"] * it["c_col"], v_new)
    for hh in range(hb):
        sl = slice(hh * d, (hh + 1) * d)
        s_scr[hh] = states[hh]
        o = jnp.concatenate(outs[hh], axis=0) if nc > 1 else outs[hh][0]
        o = o * lax.rsqrt(jnp.mean(o * o, axis=-1, keepdims=True) + EPS) * onw_ref[...]
        z = z_ref[0:nc * c, sl]
        o_ref[0:nc * c, sl] = (o * (z * (1.0 / (1.0 + jnp.exp(-z))))).astype(o_ref.dtype)
        if nc * c < tb:
            o_ref[nc * c:tb, sl] = jnp.zeros((tb - nc * c, d), o_ref.dtype)

    @pl.when(n == pl.num_programs(2) - 1)
    def _fin():
        s_out_ref[0] = s_scr[...]


def dn_delta(proj, gates, conv_w, conv_state, s0, out_norm, *, batch, tb, hb, live):
    rows, width = proj.shape
    d = DN_DK
    heads = width // (4 * d)
    t = rows // batch
    nt = t // tb
    hg = heads // hb
    wd = hb * d
    row_map = lambda off: (lambda b, h, n: (b * nt + n, off * hg + h))
    gate_map = lambda off: (lambda b, h, n: (off * hg + h, 0, b * nt + n))
    return pl.pallas_call(
        functools.partial(_dn_kernel, tb=tb, hb=hb, live=live),
        grid=(batch, hg, nt),
        in_specs=[pl.BlockSpec((tb, wd), row_map(0)),
                  pl.BlockSpec((tb, wd), row_map(1)),
                  pl.BlockSpec((tb, wd), row_map(2)),
                  pl.BlockSpec((tb, wd), row_map(3)),
                  pl.BlockSpec((hb, 1, tb), gate_map(0)),
                  pl.BlockSpec((hb, 1, tb), gate_map(1)),
                  pl.BlockSpec((3, DN_CONV, wd), lambda b, h, n: (0, 0, h)),
                  pl.BlockSpec((1, 3, _DN_TAIL, wd), lambda b, h, n: (b, 0, 0, h)),
                  pl.BlockSpec((1, hb, d, d), lambda b, h, n: (b, h, 0, 0)),
                  pl.BlockSpec((1, d), lambda b, h, n: (0, 0))],
        out_specs=[pl.BlockSpec((tb, wd), lambda b, h, n: (b * nt + n, h)),
                   pl.BlockSpec((1, hb, d, d), lambda b, h, n: (b, h, 0, 0))],
        out_shape=[jax.ShapeDtypeStruct((rows, heads * d), jnp.bfloat16),
                   jax.ShapeDtypeStruct(s0.shape, jnp.float32)],
        scratch_shapes=[pltpu.VMEM((3, tb + _DN_TAIL, wd), jnp.float32),
                        pltpu.VMEM((hb, d, d), jnp.float32)],
        compiler_params=_cparams("parallel", "parallel", "arbitrary"),
        name="dn_delta",
    )(proj, proj, proj, proj, gates, gates, conv_w, conv_state, s0, out_norm.reshape(1, d))


def _deltanet_mixer(x, conv_buf, s0, norm_w, w_in, conv_w, a_log, dt_bias, out_norm, w_out):
    B, L, D = x.shape
    H = a_log.shape[0]
    cdim = conv_w.shape[1]
    hd = cdim // 3
    x2 = x.reshape(B * L, D)
    h = rmsnorm_bf16(x2, norm_w, tm=min(256, B * L))
    wt = w_in.T
    w_ba_t = wt[cdim + hd:].astype(jnp.bfloat16)
    proj = matmul_wt(h, wt, tm=_row_tile(B * L), tn=_WT_TN, n_off=0, n=cdim + hd, name="dn_in_proj")
    gates = dn_gates(h, w_ba_t, a_log, dt_bias, tm=_row_tile(B * L))
    new_buf = proj.reshape(B, L, -1)[:, L - (DN_CONV - 1):, :cdim]
    T = -(-L // _DN_TMIN) * _DN_TMIN
    if T != L:
        proj = jnp.pad(proj.reshape(B, L, -1), ((0, 0), (0, T - L), (0, 0))).reshape(B * T, -1)
        gates = jnp.pad(gates.reshape(2 * H, B, L), ((0, 0), (0, 0), (0, T - L))).reshape(2 * H, B * T)
    cw = conv_w.reshape(DN_CONV, 3, hd).transpose(1, 0, 2)
    cs = conv_buf.reshape(B, DN_CONV - 1, 3, hd).transpose(0, 2, 1, 3)
    cs = jnp.pad(cs, ((0, 0), (0, 0), (_DN_TAIL - (DN_CONV - 1), 0), (0, 0)))
    tb = min(_DN_TB, T)
    live = tb // DN_CHUNK if T == L else -(-L // DN_CHUNK)
    assert T == L or T == tb
    o, s_new = dn_delta(proj, gates.reshape(2 * H, 1, B * T), cw, cs, s0.astype(jnp.float32), out_norm,
                        batch=B, tb=tb, hb=_DN_HB, live=live)
    if T != L:
        o = o.reshape(B, T, -1)[:, :L].reshape(B * L, -1)
    y = matmul(o, w_out.astype(jnp.bfloat16), tm=_row_tile(B * L), tn=1024, residual=x2, name="dn_out_proj")
    return y.reshape(B, L, D), (s_new, new_buf)


def _alibi_slopes():
    hh = jnp.arange(1, NSA_HEADS + 1, dtype=jnp.float32)
    return jnp.exp2(-8.0 * hh / NSA_HEADS).reshape(NSA_KV, NSA_REP)


PAGE_ROWS = 128
NSA_DKP = 256
_POS_LANE = NSA_DK
_SEL_LANE = NSA_DK + 4
_SEL_LANES = 32
_ROW_LANE = _SEL_LANE + _SEL_LANES
NSA_DVP = 256
_FAST_SCORE_BOUND = 40.0
_MASK_BIG = 2.0 ** 100
_NEG = -1e30
_NSA_TQ = 256
_NSA_TK = 256
_CMP_PAGES = 16


def _slope_table():
    s = _alibi_slopes()
    hi = s.astype(jnp.bfloat16).astype(jnp.float32)
    lo = (s - hi).astype(jnp.bfloat16).astype(jnp.float32)
    tab = jnp.stack([64.0 * hi, hi, 64.0 * lo, lo], axis=-1)
    return jnp.pad(tab, ((0, 0), (0, 0), (0, NSA_DKP - NSA_DK - 4)))


_QP_HEADS = 4


def _q_proj_prep_kernel(a_ref, w_ref, nw_ref, tab_ref, o_ref, *, seq, pos0):
    tm = a_ref.shape[0]
    npad = NSA_DKP - NSA_DK
    acc = lax.dot_general(a_ref[...], w_ref[...].astype(jnp.bfloat16), (((1,), (1,)), ((), ())),
                          preferred_element_type=jnp.float32)
    t = (pos0 + (pl.program_id(0) * tm) % seq + lax.broadcasted_iota(jnp.int32, (tm, npad), 0)).astype(jnp.float32)
    lane = lax.broadcasted_iota(jnp.int32, (tm, npad), 1)
    for r in range(_QP_HEADS):
        x = acc[:, r * NSA_DK:(r + 1) * NSA_DK]
        y = x * lax.rsqrt(jnp.mean(x * x, axis=-1, keepdims=True) + EPS) * (nw_ref[...] * (NSA_DK ** -0.5))
        o_ref[0, r, :, 0:NSA_DK] = y.astype(o_ref.dtype)
        tab = jnp.broadcast_to(tab_ref[0, r:r + 1, :], (tm, npad))
        slope = tab[:, 1:2] + tab[:, 3:4]
        extra = jnp.where(lane == _ROW_LANE - NSA_DK, -slope * t, tab)
        o_ref[0, r, :, NSA_DK:NSA_DKP] = extra.astype(o_ref.dtype)


def nsa_q_proj_prep(h, wt, q_norm, *, tm, seq, pos0=0):
    rows, k = h.shape
    per_g = NSA_REP // _QP_HEADS
    tn = _QP_HEADS * NSA_DK
    return pl.pallas_call(
        functools.partial(_q_proj_prep_kernel, seq=seq, pos0=pos0),
        grid=(rows // tm, NSA_HEADS // _QP_HEADS),
        in_specs=[pl.BlockSpec((tm, k), lambda i, j: (i, 0)),
                  pl.BlockSpec((tn, k), lambda i, j: (j, 0)),
                  pl.BlockSpec((1, NSA_DK), lambda i, j: (0, 0)),
                  pl.BlockSpec((1, _QP_HEADS, NSA_DKP - NSA_DK), lambda i, j: (j, 0, 0))],
        out_specs=pl.BlockSpec((1, _QP_HEADS, tm, NSA_DKP), lambda i, j: (j // per_g, j % per_g, i, 0)),
        out_shape=jax.ShapeDtypeStruct((NSA_KV, NSA_REP, rows, NSA_DKP), jnp.bfloat16),
        compiler_params=_cparams("parallel", "parallel"),
        name="nsa_q_proj_prep",
    )(h, wt, q_norm.reshape(1, NSA_DK), _slope_table().reshape(NSA_HEADS // _QP_HEADS, _QP_HEADS, NSA_DKP - NSA_DK))


def _k_prep_kernel(k_ref, w_ref, on_ref, oh_ref, *, seq, onehot):
    tk = k_ref.shape[0]
    pos = (pl.program_id(0) * tk) % seq + lax.broadcasted_iota(jnp.int32, (tk, NSA_DKP - NSA_DK), 0)
    lane = lax.broadcasted_iota(jnp.int32, (tk, NSA_DKP - NSA_DK), 1)
    blk, off = pos // SLC_BLOCK, pos % SLC_BLOCK
    extra = jnp.where((lane == 0) | (lane == 2), blk, jnp.where((lane == 1) | (lane == 3), off, 0))
    if onehot:
        extra = jnp.where((lane >= 4) & (lane < 4 + _SEL_LANES) & (lane - 4 == blk), 1, extra)
    extra = jnp.where(lane == _ROW_LANE - NSA_DK, 1, extra)
    extra = extra.astype(jnp.float32)
    for g in range(NSA_KV):
        x = k_ref[:, g * NSA_DK:(g + 1) * NSA_DK]
        x = x * lax.rsqrt(jnp.mean(x * x, axis=-1, keepdims=True) + EPS) * w_ref[...]
        on_ref[:, g * NSA_DK:(g + 1) * NSA_DK] = x
        oh_ref[g, :, 0:NSA_DK] = x.astype(oh_ref.dtype)
        oh_ref[g, :, NSA_DK:NSA_DKP] = extra.astype(oh_ref.dtype)


def nsa_k_prep(k_raw, k_norm, *, seq, tk, onehot):
    rows = k_raw.shape[0]
    return pl.pallas_call(
        functools.partial(_k_prep_kernel, seq=seq, onehot=onehot),
        grid=(rows // tk,),
        in_specs=[pl.BlockSpec((tk, NSA_KW), lambda i: (i, 0)), pl.BlockSpec((1, NSA_DK), lambda i: (0, 0))],
        out_specs=[pl.BlockSpec((tk, NSA_KW), lambda i: (i, 0)),
                   pl.BlockSpec((NSA_KV, tk, NSA_DKP), lambda i: (0, i, 0))],
        out_shape=[jax.ShapeDtypeStruct((rows, NSA_KW), jnp.float32),
                   jax.ShapeDtypeStruct((NSA_KV, rows, NSA_DKP), jnp.bfloat16)],
        compiler_params=_cparams("parallel"),
        name="nsa_k_prep",
    )(k_raw, k_norm.reshape(1, NSA_DK))


def _v_prep_kernel(v_ref, o_ref):
    tk = v_ref.shape[0]
    ones_col = jnp.where(lax.broadcasted_iota(jnp.int32, (tk, NSA_DVP - NSA_DV), 1) == 0, 1.0, 0.0)
    for g in range(NSA_KV):
        o_ref[g, :, 0:NSA_DV] = v_ref[:, g * NSA_DV:(g + 1) * NSA_DV].astype(o_ref.dtype)
        o_ref[g, :, NSA_DV:NSA_DVP] = ones_col.astype(o_ref.dtype)


def nsa_v_prep(v_raw, *, tk):
    rows = v_raw.shape[0]
    return pl.pallas_call(
        _v_prep_kernel,
        grid=(rows // tk,),
        in_specs=[pl.BlockSpec((tk, NSA_VW), lambda i: (i, 0))],
        out_specs=pl.BlockSpec((NSA_KV, tk, NSA_DVP), lambda i: (0, i, 0)),
        out_shape=jax.ShapeDtypeStruct((NSA_KV, rows, NSA_DVP), jnp.bfloat16),
        compiler_params=_cparams("parallel"),
        name="nsa_v_prep",
    )(v_raw)


def _compress_partials_kernel(tab_ref, *refs, d, pages, transposed):
    x_refs, (pe_ref, w1_ref, perm_ref, a_ref, b_ref) = refs[:pages], refs[pages:]
    seg = PAGE_ROWS // CMP_STRIDE
    acc_a = acc_b = None
    if transposed:
        rows_of = [[_dot_nt(perm_ref[...], r[0, g]) for r in x_refs] for g in range(NSA_KV)]
        take = lambda ri, l, g: rows_of[g][ri][l * seg:(l + 1) * seg, :]
    elif x_refs[0].shape[1] == CMP_STRIDE * NSA_KV:
        take = lambda ri, l, g: x_refs[ri][:, l * NSA_KV + g, :]
    else:
        take = lambda ri, l, g: x_refs[ri][:, l, g * d:(g + 1) * d]

    for l in range(CMP_STRIDE):
        xg = jnp.concatenate([take(ri, l, g) for g in range(NSA_KV) for ri in range(pages)], axis=0)
        ta = jnp.dot((xg + pe_ref[l:l + 1, :]).astype(jnp.bfloat16), w1_ref[l], preferred_element_type=jnp.float32)
        tb = jnp.dot((xg + pe_ref[CMP_STRIDE + l:CMP_STRIDE + l + 1, :]).astype(jnp.bfloat16),
                     w1_ref[CMP_STRIDE + l], preferred_element_type=jnp.float32)
        acc_a = ta if acc_a is None else acc_a + ta
        acc_b = tb if acc_b is None else acc_b + tb
    n = pages * seg
    for g in range(NSA_KV):
        a_ref[0, g] = acc_a[g * n:(g + 1) * n]
        b_ref[0, g] = acc_b[g * n:(g + 1) * n]


def nsa_compress_partials(pool3, table, pe, w1, *, pages, transposed=False):
    d = pe.shape[1]
    batch, npg = table.shape
    seg = PAGE_ROWS // CMP_STRIDE
    blk = ((1,) if transposed else (seg,)) + tuple(pool3.shape[1:])
    x_spec = lambda k: pl.BlockSpec(blk, lambda b, i, tab: (tab[b, i * pages + k],) + (0,) * (len(blk) - 1))
    out_spec = pl.BlockSpec((1, NSA_KV, pages * seg, CMP_HIDDEN), lambda b, i, tab: (b, 0, i, 0))
    out_sds = jax.ShapeDtypeStruct((batch, NSA_KV, npg * seg, CMP_HIDDEN), jnp.float32)
    i = jnp.arange(PAGE_ROWS)
    perm = (i[None, :] == (i % seg)[:, None] * CMP_STRIDE + (i // seg)[:, None]).astype(jnp.bfloat16)
    return pl.pallas_call(
        functools.partial(_compress_partials_kernel, d=d, pages=pages, transposed=transposed),
        grid_spec=pltpu.PrefetchScalarGridSpec(
            num_scalar_prefetch=1,
            grid=(batch, npg // pages),
            in_specs=[x_spec(k) for k in range(pages)]
            + [pl.BlockSpec((CMP_BLOCK, d), lambda b, i, tab: (0, 0)),
               pl.BlockSpec((CMP_BLOCK, d, CMP_HIDDEN), lambda b, i, tab: (0, 0, 0)),
               pl.BlockSpec((PAGE_ROWS, PAGE_ROWS), lambda b, i, tab: (0, 0))],
            out_specs=[out_spec, out_spec]),
        out_shape=[out_sds, out_sds],
        compiler_params=_cparams("parallel", "arbitrary"),
        name="nsa_compress_partials",
    )(table, *([pool3] * pages), pe, w1.astype(jnp.bfloat16), perm)


def _cmp_extra_lanes(nrows, width):
    n = lax.broadcasted_iota(jnp.int32, (nrows, width), 0)
    lane = lax.broadcasted_iota(jnp.int32, (nrows, width), 1)
    ec = n * CMP_STRIDE + CMP_BLOCK - 1
    blk, off = ec // SLC_BLOCK, ec % SLC_BLOCK
    pl_ = lane - _POS_LANE
    return jnp.where((pl_ == 0) | (pl_ == 2), blk, jnp.where((pl_ == 1) | (pl_ == 3), off, 0)).astype(jnp.float32)


def _compress_finish_kernel(a_ref, b_ref, bn_ref, w2_ref, kn_ref, o_ref, *, is_key, has_new, alibi_lanes):
    ns = a_ref.shape[2]
    row = lax.broadcasted_iota(jnp.int32, (ns, 1), 0)
    for g in range(NSA_KV):
        nxt = pltpu.roll(b_ref[0, g], ns - 1, 0)
        if has_new:
            nxt = jnp.where(row == ns - 1, bn_ref[0, g, 0:1, :], nxt)
        hid_pre = a_ref[0, g] + nxt
        hid = hid_pre * (1.0 / (1.0 + jnp.exp(-hid_pre)))
        out = jnp.dot(hid.astype(jnp.bfloat16), w2_ref[...], preferred_element_type=jnp.float32)
        if is_key:
            ms = jnp.sum(out * out, axis=-1, keepdims=True) * (1.0 / NSA_DK)
            out = out * lax.rsqrt(ms + EPS) * kn_ref[...]
            if alibi_lanes:
                out = out + _cmp_extra_lanes(ns, out.shape[1])
        o_ref[0, g] = out.astype(o_ref.dtype)


def nsa_compress_finish(part_a, part_b, part_b_new, w2, kn, *, is_key, alibi_lanes):
    batch, _, ns, _ = part_a.shape
    d = w2.shape[1]
    dp = NSA_DKP if is_key else d
    has_new = part_b_new is not None
    if not has_new:
        part_b_new = jnp.zeros((batch, NSA_KV, 8, CMP_HIDDEN), jnp.float32)
    w2p = jnp.pad(w2, ((0, 0), (0, dp - d))).astype(jnp.bfloat16)
    knp = jnp.pad(kn, (0, dp - d)).reshape(1, dp)
    blk = lambda n: pl.BlockSpec((1, NSA_KV, n, CMP_HIDDEN), lambda b: (b, 0, 0, 0))
    return pl.pallas_call(
        functools.partial(_compress_finish_kernel, is_key=is_key, has_new=has_new, alibi_lanes=alibi_lanes),
        grid=(batch,),
        in_specs=[blk(ns), blk(ns), blk(part_b_new.shape[2]),
                  pl.BlockSpec((CMP_HIDDEN, dp), lambda b: (0, 0)),
                  pl.BlockSpec((1, dp), lambda b: (0, 0))],
        out_specs=pl.BlockSpec((1, NSA_KV, ns, dp), lambda b: (b, 0, 0, 0)),
        out_shape=jax.ShapeDtypeStruct((batch, NSA_KV, ns, dp), jnp.bfloat16),
        compiler_params=_cparams("parallel"),
        name="nsa_compress_finish_k" if is_key else "nsa_compress_finish_v",
    )(part_a, part_b, part_b_new, w2p, knp)


def _split_dot(x, m_bf16):
    hi, lo = _split_bf16(x)
    f = functools.partial(jnp.dot, preferred_element_type=jnp.float32)
    return f(hi, m_bf16) + f(lo, m_bf16)


def _topn_mask_t(imp_t, t_row, nsb, topn):
    jb = lax.broadcasted_iota(jnp.int32, imp_t.shape, 0)
    cur = t_row // SLC_BLOCK
    forced = (jb == 0) | (jb == cur) | (jb == cur - 1)
    valid = (jb * SLC_BLOCK <= t_row) & (jb < nsb)
    impm = jnp.where(forced, jnp.inf, jnp.where(valid, imp_t, -jnp.inf))
    rank = jnp.zeros(imp_t.shape, jnp.float32)
    for k in range(nsb):
        row = impm[k:k + 1, :]
        beats = (row > impm) | ((row == impm) & (k < jb))
        rank = rank + jnp.where(beats, 1.0, 0.0)
    return (rank < topn) & (jb < nsb)


def _cmp_and_select(q, kc, vc, t0, tq, seq):
    rep = NSA_REP
    m_rows = rep * tq
    nc = seq // CMP_STRIDE - CMP_RATIO + 1
    nseg = kc.shape[0]
    nsb = seq // SLC_BLOCK
    t_col = t0 + lax.broadcasted_iota(jnp.int32, (rep, tq, 1), 1).reshape(m_rows, 1)
    t_row = t0 + lax.broadcasted_iota(jnp.int32, (1, tq), 1)
    n_row = lax.broadcasted_iota(jnp.int32, (1, nseg), 1)
    cvalid = (n_row * CMP_STRIDE + CMP_BLOCK - 1 <= t_col) & (n_row < nc)
    s = jnp.where(cvalid, _dot_nt(q, kc), _NEG)
    p = jnp.where(cvalid, jnp.exp(s - jnp.max(s, axis=1, keepdims=True)), 0.0)
    p = p / jnp.maximum(jnp.sum(p, axis=1, keepdims=True), 1e-30)
    o_c = jnp.dot(p.astype(jnp.bfloat16), vc, preferred_element_type=jnp.float32)
    psum = jnp.sum(p.reshape(rep, tq, nseg), axis=0)
    cj = lax.broadcasted_iota(jnp.int32, (_SEL_LANES, nseg), 0) * SLC_BLOCK
    ci = lax.broadcasted_iota(jnp.int32, (_SEL_LANES, nseg), 1) * CMP_STRIDE
    ov = jnp.maximum(jnp.minimum(ci + CMP_BLOCK, cj + SLC_BLOCK) - jnp.maximum(ci, cj), 0)
    c2s_t = (ov.astype(jnp.float32) * (1.0 / CMP_STRIDE)).astype(jnp.bfloat16)
    p_hi, p_lo = _split_bf16(psum)
    imp_t = _dot_nt(c2s_t, p_hi) + _dot_nt(c2s_t, p_lo)
    sel_t = _topn_mask_t(imp_t, t_row, nsb, min(SLC_TOPN, nsb))
    pen_t = jnp.where(sel_t, 0.0, -_MASK_BIG).astype(jnp.bfloat16)
    pi = lax.broadcasted_iota(jnp.int32, (_SEL_LANES, NSA_DKP), 0)
    pj = lax.broadcasted_iota(jnp.int32, (_SEL_LANES, NSA_DKP), 1)
    place = jnp.where(pj == pi + _SEL_LANE, 1.0, 0.0).astype(jnp.bfloat16)
    q_add = _dot_tn(pen_t, place).astype(jnp.bfloat16)
    q_slc = (q.reshape(rep, tq, NSA_DKP) + q_add[None]).reshape(m_rows, NSA_DKP)
    return o_c, q_slc


def _gate_combine(gl_ref, z_ref, o_ref, o_c, o_s, o_w, tq):
    gates = 1.0 / (1.0 + jnp.exp(-gl_ref[...]))
    for r in range(NSA_REP):
        rs = slice(r * tq, (r + 1) * tq)
        o = (gates[:, 3 * r:3 * r + 1] * o_c[rs] + gates[:, 3 * r + 1:3 * r + 2] * o_s[rs]
             + gates[:, 3 * r + 2:3 * r + 3] * o_w[rs])
        z = z_ref[:, r * NSA_DV:(r + 1) * NSA_DV]
        o_ref[:, r * NSA_DV:(r + 1) * NSA_DV] = (o * (z * (1.0 / (1.0 + jnp.exp(-z))))).astype(o_ref.dtype)


def _nsa_prompt_fast_kernel(q_ref, kc_ref, vc_ref, ks_ref, vs_ref, kw_ref, vw_ref, gl_ref, z_ref, o_ref,
                            *, tq, tk, seq):
    t0 = pl.program_id(2) * tq
    m_rows = NSA_REP * tq
    q = q_ref[0].reshape(m_rows, NSA_DKP)
    t_col = t0 + lax.broadcasted_iota(jnp.int32, (NSA_REP, tq, 1), 1).reshape(m_rows, 1)
    o_c, q_slc = _cmp_and_select(q, kc_ref[0, 0], vc_ref[0, 0], t0, tq, seq)
    kpos = lax.broadcasted_iota(jnp.int32, (1, tk), 1)
    kt_hi = (t0 + tq + tk - 1) // tk
    zero = jnp.zeros((m_rows, NSA_DVP), jnp.float32)

    def pv(p, v_ref, k0):
        return jnp.dot(p.astype(jnp.bfloat16), v_ref[0, pl.ds(k0, tk), :], preferred_element_type=jnp.float32)

    def slc_pair(j, acc):
        outs = []
        for h in range(2):
            kt = 2 * j + h
            k0 = pl.multiple_of(jnp.minimum(kt, kt_hi - 1) * tk, tk)
            s = _dot_nt(q_slc, ks_ref[0, pl.ds(k0, tk), :])
            outs.append((s, k0, ((k0 + kpos) <= t_col) & (kt < kt_hi)))
        ps = [jnp.where(ok, jnp.exp(s), 0.0) for s, _, ok in outs]
        return acc + pv(ps[0], vs_ref, outs[0][1]) + pv(ps[1], vs_ref, outs[1][1])

    acc = lax.fori_loop(0, (kt_hi + 1) // 2, slc_pair, zero)
    o_s = acc[:, 0:NSA_DV] / acc[:, NSA_DV:NSA_DV + 1]

    def win_tile(kt):
        k0 = pl.multiple_of(kt * tk, tk)
        dw = t_col - (k0 + kpos)
        return _dot_nt(q, kw_ref[0, pl.ds(k0, tk), :]), k0, (dw >= 0) & (dw < WINDOW)

    def win_body(kt, acc):
        s, k0, ok = win_tile(kt)
        return acc + pv(jnp.where(ok, jnp.exp(s), 0.0), vw_ref, k0)

    kt_lo = jnp.maximum(t0 - (WINDOW - 1), 0) // tk
    acc = lax.fori_loop(kt_lo, kt_hi - 2, win_body, zero)
    ta = win_tile(jnp.maximum(kt_hi - 2, 0))
    tb_ = win_tile(kt_hi - 1)
    pa = jnp.where(ta[2] & (kt_hi >= 2), jnp.exp(ta[0]), 0.0)
    pb = jnp.where(tb_[2], jnp.exp(tb_[0]), 0.0)
    acc = acc + pv(pa, vw_ref, ta[1]) + pv(pb, vw_ref, tb_[1])
    o_w = acc[:, 0:NSA_DV] / acc[:, NSA_DV:NSA_DV + 1]
    _gate_combine(gl_ref, z_ref, o_ref, o_c, o_s, o_w, tq)


def _flash_step(q, k, v, mask, carry):
    m, l, acc = carry
    s = _dot_nt(q, k)
    s = jnp.where(mask, s, _NEG)
    m_new = jnp.maximum(m, jnp.max(s, axis=1, keepdims=True))
    alpha = jnp.exp(m - m_new)
    p = jnp.exp(s - m_new)
    l = alpha * l + jnp.sum(p, axis=1, keepdims=True)
    acc = alpha * acc + jnp.dot(p.astype(jnp.bfloat16), v, preferred_element_type=jnp.float32)
    return m_new, l, acc


def _nsa_prompt_kernel(q_ref, kc_ref, vc_ref, ks_ref, vs_ref, kw_ref, vw_ref, gl_ref, z_ref, o_ref,
                       *, tq, tk, seq):
    t0 = pl.program_id(2) * tq
    m_rows = NSA_REP * tq
    q = q_ref[0].reshape(m_rows, NSA_DKP)
    t_col = t0 + lax.broadcasted_iota(jnp.int32, (NSA_REP, tq, 1), 1).reshape(m_rows, 1)
    o_c, q_slc = _cmp_and_select(q, kc_ref[0, 0], vc_ref[0, 0], t0, tq, seq)

    kpos = lax.broadcasted_iota(jnp.int32, (1, tk), 1)
    init = (jnp.full((m_rows, 1), _NEG, jnp.float32), jnp.zeros((m_rows, 1), jnp.float32),
            jnp.zeros((m_rows, NSA_DV), jnp.float32))

    def slc_body(kt, carry):
        k0 = pl.multiple_of(kt * tk, tk)
        mask = (k0 + kpos) <= t_col
        return _flash_step(q_slc, ks_ref[0, pl.ds(k0, tk), :], vs_ref[0, pl.ds(k0, tk), :], mask, carry)

    _, l_s, acc_s = lax.fori_loop(0, (t0 + tq + tk - 1) // tk, slc_body, init)

    def win_body(kt, carry):
        k0 = pl.multiple_of(kt * tk, tk)
        dw = t_col - (k0 + kpos)
        mask = (dw >= 0) & (dw < WINDOW)
        return _flash_step(q, kw_ref[0, pl.ds(k0, tk), :], vw_ref[0, pl.ds(k0, tk), :], mask, carry)

    kt_lo = jnp.maximum(t0 - (WINDOW - 1), 0) // tk
    _, l_w, acc_w = lax.fori_loop(kt_lo, (t0 + tq + tk - 1) // tk, win_body, init)
    _gate_combine(gl_ref, z_ref, o_ref, o_c, acc_s / l_s, acc_w / l_w, tq)


def nsa_prompt_attention(q_hm, kcmp, vcmp, ks_hm, vs_hm, kw_hm, vw_hm, gl, z, *, batch, tq, tk, fast):
    rows = z.shape[0]
    seq = rows // batch
    nq = seq // tq
    nseg = kcmp.shape[2]
    row_blk = lambda b, g, i: (b * nq + i, g)
    kv_blk = lambda b, g, i: (g, b, 0)
    dvb = NSA_DVP if fast else NSA_DV
    return pl.pallas_call(
        functools.partial(_nsa_prompt_fast_kernel if fast else _nsa_prompt_kernel, tq=tq, tk=tk, seq=seq),
        grid=(batch, NSA_KV, nq),
        in_specs=[pl.BlockSpec((1, NSA_REP, tq, NSA_DKP), lambda b, g, i: (g, 0, b * nq + i, 0)),
                  pl.BlockSpec((1, 1, nseg, NSA_DKP), lambda b, g, i: (b, g, 0, 0)),
                  pl.BlockSpec((1, 1, nseg, NSA_DV), lambda b, g, i: (b, g, 0, 0)),
                  pl.BlockSpec((1, seq, NSA_DKP), kv_blk),
                  pl.BlockSpec((1, seq, dvb), kv_blk),
                  pl.BlockSpec((1, seq, NSA_DKP), kv_blk),
                  pl.BlockSpec((1, seq, dvb), kv_blk),
                  pl.BlockSpec((tq, LANE), row_blk),
                  pl.BlockSpec((tq, NSA_REP * NSA_DV), row_blk)],
        out_specs=pl.BlockSpec((tq, NSA_REP * NSA_DV), row_blk),
        out_shape=jax.ShapeDtypeStruct((rows, NSA_O), jnp.bfloat16),
        compiler_params=_cparams("parallel", "parallel", "arbitrary"),
        name="nsa_prompt_attention_fast" if fast else "nsa_prompt_attention",
    )(q_hm, kcmp, vcmp, ks_hm, vs_hm, kw_hm, vw_hm, gl, z)


def _nsa_in_proj(x, norm_w, w_in):
    B, L, D = x.shape
    x2 = x.reshape(B * L, D)
    h = rmsnorm_bf16(x2, norm_w, tm=min(256, B * L))
    offs = np.cumsum((0,) + NSA_SPLITS)
    tm = _row_tile(B * L)

    wt = w_in.T

    def seg(i, tn, name):
        return matmul_wt(h, wt, tm=tm, tn=tn, n_off=int(offs[i]), n=int(offs[i + 1] - offs[i]), name=name)

    kv = matmul_wt(h, wt, tm=tm, tn=768, n_off=int(offs[1]), n=int(offs[7] - offs[1]), name="nsa_kv_proj")
    kc, vc, ks, vs, kw, vw = (kv[:, int(offs[i] - offs[1]):int(offs[i + 1] - offs[1])] for i in range(1, 7))
    wg = wt[offs[7]:offs[8]].reshape(NSA_KV, 3 * NSA_REP, D)
    wg = jnp.pad(wg, ((0, 0), (0, LANE - 3 * NSA_REP), (0, 0))).reshape(NSA_KV * LANE, D)
    gl = matmul_wt(h, wg, tm=tm, tn=NSA_KV * LANE, n_off=0, n=NSA_KV * LANE, name="nsa_gate_proj")
    z = matmul_wt(h, wt[offs[8]:], tm=tm, tn=_WT_TN, n_off=0, n=NSA_O, name="nsa_z_proj")
    q_of = functools.partial(nsa_q_proj_prep, h, wt, tm=tm)
    return x2, q_of, kc, vc, ks, vs, kw, vw, gl, z


def _nsa_prompt(x, norm_w, w_in, q_norm, kn_c, kn_s, kn_w, pe_k, w1_k, w2_k, pe_v, w1_v, w2_v, w_out):
    B, T, D = x.shape
    x2, q_of, kc_r, vc_r, ks_raw, vs_r, kw_raw, vw_r, gl, z = _nsa_in_proj(x, norm_w, w_in)
    tp = min(512, T)
    q_hm = q_of(q_norm, seq=T)
    ks_r, ks_hm = nsa_k_prep(ks_raw, kn_s, seq=T, tk=tp, onehot=True)
    kw_r, kw_hm = nsa_k_prep(kw_raw, kn_w, seq=T, tk=tp, onehot=False)
    vs_hm = nsa_v_prep(vs_r, tk=tp)
    vw_hm = nsa_v_prep(vw_r, tk=tp)
    npg = T // PAGE_ROWS
    table = (jnp.arange(B, dtype=jnp.int32)[:, None] * npg + jnp.arange(npg, dtype=jnp.int32)[None, :])
    seg3 = lambda a: a.reshape(B * T // CMP_STRIDE, CMP_STRIDE, a.shape[1])
    ka, kb = nsa_compress_partials(seg3(kc_r), table, pe_k, w1_k, pages=min(_CMP_PAGES, npg))
    va, vb = nsa_compress_partials(seg3(vc_r), table, pe_v, w1_v, pages=min(_CMP_PAGES, npg))
    kcmp = nsa_compress_finish(ka, kb, None, w2_k, kn_c, is_key=True, alibi_lanes=True)
    vcmp = nsa_compress_finish(va, vb, None, w2_v, jnp.zeros((NSA_DV,), jnp.float32), is_key=False, alibi_lanes=False)
    bound = jnp.max(jnp.abs(q_norm)) * jnp.maximum(jnp.max(jnp.abs(kn_s)), jnp.max(jnp.abs(kn_w))) * (NSA_DK ** 0.5)
    attend = lambda fast: functools.partial(nsa_prompt_attention, batch=B, tq=_NSA_TQ, tk=_NSA_TK, fast=fast)
    og = lax.cond(bound < _FAST_SCORE_BOUND, attend(True), attend(False),
                  q_hm, kcmp, vcmp, ks_hm, vs_hm, kw_hm, vw_hm, gl, z)
    y = matmul(og, w_out.astype(jnp.bfloat16), tm=_row_tile(B * T), tn=1024, residual=x2, name="nsa_out_proj")
    wl = min(WINDOW, T)
    r4 = lambda a, d: a.reshape(B, T, NSA_KV, d)
    return y.reshape(B, T, D), (r4(kc_r, NSA_DK), r4(vc_r, NSA_DV), r4(ks_r, NSA_DK), r4(vs_r, NSA_DV),
                                r4(kw_r, NSA_DK)[:, T - wl:], r4(vw_r, NSA_DV)[:, T - wl:])


_SMP_PAGES = 16
_SEL_PAD = 384


def _online_update(m_ref, l_ref, acc_ref, g, s, v):
    m_old = m_ref[g]
    m_new = jnp.maximum(m_old, jnp.max(s, axis=1, keepdims=True))
    alpha = jnp.exp(m_old - m_new)
    p = jnp.exp(s - m_new)
    l_ref[g] = alpha * l_ref[g] + jnp.sum(p, axis=1, keepdims=True)
    acc_ref[g] = alpha * acc_ref[g] + jnp.dot(p.astype(jnp.bfloat16), v.astype(jnp.bfloat16),
                                              preferred_element_type=jnp.float32)
    m_ref[g] = m_new


def _nsa_sample_kernel(tab_ref, *refs, past, steps, pages, nsb):
    kp_refs = refs[:pages]
    vp_refs = refs[pages:2 * pages]
    (q_ref, kc_ref, vc_ref, kn_ref, vn_ref, kwc_ref, vwc_ref, kwn_ref, vwn_ref, gl_ref, z_ref,
     o_ref, m_scr, l_scr, acc_scr, sel_scr, oc_scr) = refs[2 * pages:]
    i = pl.program_id(1)
    s_tok = o_ref.shape[0]
    rows = NSA_REP * s_tok
    ncmp = kc_ref.shape[2]
    r_col = lax.broadcasted_iota(jnp.int32, (NSA_REP, s_tok, 1), 0).reshape(rows, 1)
    t_col = past + lax.broadcasted_iota(jnp.int32, (NSA_REP, s_tok, 1), 1).reshape(rows, 1)
    t_tok = past + lax.broadcasted_iota(jnp.int32, (s_tok, 1), 0)

    def slope_col(g):
        return jnp.exp2((r_col + (g * NSA_REP + 1)).astype(jnp.float32) * (-8.0 / NSA_HEADS))

    def q_of(g):
        return q_ref[0, g * rows:(g + 1) * rows, :]

    def rep_rows(x):
        return jnp.concatenate([x] * NSA_REP, axis=0)

    @pl.when(i == 0)
    def _first():
        n_row = lax.broadcasted_iota(jnp.int32, (1, ncmp), 1)
        cvalid = (n_row * CMP_STRIDE + CMP_BLOCK - 1) <= t_col
        ci = lax.broadcasted_iota(jnp.int32, (ncmp, _SEL_PAD), 0) * CMP_STRIDE
        cj = lax.broadcasted_iota(jnp.int32, (ncmp, _SEL_PAD), 1) * SLC_BLOCK
        ov = jnp.maximum(jnp.minimum(ci + CMP_BLOCK, cj + SLC_BLOCK) - jnp.maximum(ci, cj), 0)
        c2s = (ov.astype(jnp.float32) * (1.0 / CMP_STRIDE)).astype(jnp.bfloat16)
        jb = lax.broadcasted_iota(jnp.int32, (s_tok, _SEL_PAD), 1)
        cur = t_tok // SLC_BLOCK
        forced = (jb == 0) | (jb == cur) | (jb == cur - 1)
        valid = (jb * SLC_BLOCK <= t_tok) & (jb < nsb)
        for g in range(NSA_KV):
            s = jnp.where(cvalid, _dot_nt(q_of(g), kc_ref[0, g]), _NEG)
            p = jnp.where(cvalid, jnp.exp(s - jnp.max(s, axis=1, keepdims=True)), 0.0)
            p = p / jnp.maximum(jnp.sum(p, axis=1, keepdims=True), 1e-30)
            oc_scr[g] = jnp.dot(p.astype(jnp.bfloat16), vc_ref[0, g], preferred_element_type=jnp.float32)
            imp = _split_dot(jnp.sum(p.reshape(NSA_REP, s_tok, ncmp), axis=0), c2s)
            work = jnp.where(forced, jnp.inf, jnp.where(valid, imp, -jnp.inf))
            sel = jnp.zeros((s_tok, _SEL_PAD), jnp.float32)
            for _ in range(min(SLC_TOPN, nsb)):
                mx = jnp.max(work, axis=1, keepdims=True)
                first = jnp.min(jnp.where(work == mx, jb, _SEL_PAD), axis=1, keepdims=True)
                hit = jb == first
                sel = jnp.where(hit, 1.0, sel)
                work = jnp.where(hit, -jnp.inf, work)
            sel_scr[g] = sel
            m_scr[g] = jnp.full((rows, 1), _NEG, jnp.float32)
            l_scr[g] = jnp.zeros((rows, 1), jnp.float32)
            acc_scr[g] = jnp.zeros((rows, NSA_DV), jnp.float32)

    def fold(score_of, v_of, nk, page0, causal):
        pos = page0 * PAGE_ROWS + lax.broadcasted_iota(jnp.int32, (1, nk), 1)
        bj = lax.broadcasted_iota(jnp.int32, (_SEL_PAD, nk), 0)
        bl = lax.broadcasted_iota(jnp.int32, (_SEL_PAD, nk), 1)
        expand = jnp.where(bj == (page0 * PAGE_ROWS + bl) // SLC_BLOCK, 1.0, 0.0).astype(jnp.bfloat16)
        for g in range(NSA_KV):
            selk = jnp.dot(sel_scr[g].astype(jnp.bfloat16), expand, preferred_element_type=jnp.float32)
            ok = rep_rows(selk) > 0.5
            if causal:
                ok = ok & (pos <= t_col)
            s = score_of(g, q_of(g)[:, 0:NSA_DK])
            s = s - slope_col(g) * (t_col - pos).astype(jnp.float32)
            _online_update(m_scr, l_scr, acc_scr, g, jnp.where(ok, s, _NEG), v_of(g))

    fold(lambda g, qg: _dot(qg, jnp.concatenate([r[0, g] for r in kp_refs], axis=1)),
         lambda g: jnp.concatenate([r[0, pl.ds(g, PAGE_ROWS, stride=NSA_KV), :] for r in vp_refs], axis=0),
         pages * PAGE_ROWS, i * pages, causal=False)

    @pl.when(i == steps - 1)
    def _last():
        fold(lambda g, qg: _dot_nt(qg, kn_ref[0, :, g * NSA_DK:(g + 1) * NSA_DK]),
             lambda g: vn_ref[0, :, g * NSA_DV:(g + 1) * NSA_DV], PAGE_ROWS, past // PAGE_ROWS, causal=True)
        nwc = kwc_ref.shape[3]
        posw = jnp.concatenate([past - nwc + lax.broadcasted_iota(jnp.int32, (1, nwc), 1),
                                past + lax.broadcasted_iota(jnp.int32, (1, PAGE_ROWS), 1)], axis=1)
        dw = t_col - posw
        wok = (dw >= 0) & (dw < WINDOW) & (posw >= 0)
        gates = 1.0 / (1.0 + jnp.exp(-gl_ref[...]))
        for g in range(NSA_KV):
            ksl = slice(g * NSA_DK, (g + 1) * NSA_DK)
            vsl = slice(g * NSA_DV, (g + 1) * NSA_DV)
            qg = q_of(g)[:, 0:NSA_DK]
            s = jnp.concatenate([_dot(qg, kwc_ref[0, g]), _dot_nt(qg, kwn_ref[0, :, ksl])], axis=1)
            s = jnp.where(wok, s - slope_col(g) * dw.astype(jnp.float32), _NEG)
            p = jnp.where(wok, jnp.exp(s - jnp.max(s, axis=1, keepdims=True)), 0.0)
            p = (p / jnp.maximum(jnp.sum(p, axis=1, keepdims=True), 1e-30)).astype(jnp.bfloat16)
            o_w = (jnp.dot(p[:, 0:nwc], vwc_ref[0, :, vsl].astype(jnp.bfloat16), preferred_element_type=jnp.float32)
                   + jnp.dot(p[:, nwc:], vwn_ref[0, :, vsl].astype(jnp.bfloat16), preferred_element_type=jnp.float32))
            o_s = acc_scr[g] / l_scr[g]
            o_c = oc_scr[g]
            for r in range(NSA_REP):
                rs = slice(r * s_tok, (r + 1) * s_tok)
                c0 = g * LANE + 3 * r
                o = (gates[:, c0:c0 + 1] * o_c[rs] + gates[:, c0 + 1:c0 + 2] * o_s[rs]
                     + gates[:, c0 + 2:c0 + 3] * o_w[rs])
                hs = slice((g * NSA_REP + r) * NSA_DV, (g * NSA_REP + r + 1) * NSA_DV)
                z = z_ref[:, hs]
                o_ref[:, hs] = o * (z * (1.0 / (1.0 + jnp.exp(-z))))


def nsa_sample_attention(q_rows, kcmp, vcmp, pool_k, pool_v, page_table, k_new, v_new, kw_cache, vw_cache,
                         kw_new, vw_new, gl, z, *, s_tok):
    batch, npg = page_table.shape
    pages = _SMP_PAGES
    steps = npg // pages
    past = npg * PAGE_ROWS
    nsb = -(-(past + s_tok) // SLC_BLOCK)
    rows = NSA_REP * s_tok
    ncmp = kcmp.shape[2]
    nwc = kw_cache.shape[3]
    kpage_spec = lambda k: pl.BlockSpec((1, NSA_KV, NSA_DK, PAGE_ROWS), lambda b, i, tab: (tab[b, i * pages + k], 0, 0, 0))
    vpage_spec = lambda k: pl.BlockSpec((1, PAGE_ROWS * NSA_KV, NSA_DV), lambda b, i, tab: (tab[b, i * pages + k], 0, 0))
    per_b = lambda shape: pl.BlockSpec((1,) + shape, lambda b, i, tab: (b,) + (0,) * len(shape))
    tok_blk = lambda w: pl.BlockSpec((s_tok, w), lambda b, i, tab: (b, 0))
    return pl.pallas_call(
        functools.partial(_nsa_sample_kernel, past=past, steps=steps, pages=pages, nsb=nsb),
        grid_spec=pltpu.PrefetchScalarGridSpec(
            num_scalar_prefetch=1,
            grid=(batch, steps),
            in_specs=[kpage_spec(k) for k in range(pages)] + [vpage_spec(k) for k in range(pages)]
            + [per_b((NSA_KV * rows, NSA_DKP)), per_b((NSA_KV, ncmp, NSA_DKP)), per_b((NSA_KV, ncmp, NSA_DV)),
               per_b((PAGE_ROWS, NSA_KW)), per_b((PAGE_ROWS, NSA_VW)),
               per_b((NSA_KV, NSA_DK, nwc)), per_b((nwc, NSA_VW)), per_b((PAGE_ROWS, NSA_KW)), per_b((PAGE_ROWS, NSA_VW)),
               tok_blk(NSA_KV * LANE), tok_blk(NSA_O)],
            out_specs=tok_blk(NSA_O),
            scratch_shapes=[pltpu.VMEM((NSA_KV, rows, 1), jnp.float32), pltpu.VMEM((NSA_KV, rows, 1), jnp.float32),
                            pltpu.VMEM((NSA_KV, rows, NSA_DV), jnp.float32),
                            pltpu.VMEM((NSA_KV, s_tok, _SEL_PAD), jnp.float32),
                            pltpu.VMEM((NSA_KV, rows, NSA_DV), jnp.float32)]),
        out_shape=jax.ShapeDtypeStruct((batch * s_tok, NSA_O), jnp.float32),
        compiler_params=_cparams("parallel", "arbitrary"),
        name="nsa_sample_attention",
    )(page_table, *([pool_k] * pages), *([pool_v] * pages), q_rows, kcmp, vcmp, k_new, v_new,
      kw_cache, vw_cache, kw_new, vw_new, gl, z)


def _nsa_sample(x, ck, cv, sk, sv, wk_buf, wv_buf, page_table, norm_w,
                w_in, q_norm, kn_c, kn_s, kn_w, pe_k, w1_k, w2_k, pe_v, w1_v, w2_v, w_out):
    B, S, D = x.shape
    x2, q_of, kc_r, vc_r, ks_raw, vs_r, kw_raw, vw_r, gl, z = _nsa_in_proj(x, norm_w, w_in)
    q_hm = q_of(q_norm, seq=S, pos0=page_table.shape[1] * PAGE_ROWS)
    q_rows = q_hm.reshape(NSA_KV, NSA_REP, B, S, NSA_DKP).transpose(2, 0, 1, 3, 4).reshape(B, NSA_KV * NSA_REP * S, NSA_DKP)
    ks_r, _ = nsa_k_prep(ks_raw, kn_s, seq=S, tk=B * S, onehot=False)
    kw_r, _ = nsa_k_prep(kw_raw, kn_w, seq=S, tk=B * S, onehot=False)
    as_page = lambda a: jnp.pad(a.reshape(B, S, -1), ((0, 0), (0, PAGE_ROWS - S), (0, 0)))
    seg3 = lambda a, w: a.reshape(-1, CMP_STRIDE, w)
    seg4 = lambda a: a.reshape(-1, CMP_STRIDE * NSA_KV, a.shape[3])
    ident = jnp.arange(B, dtype=jnp.int32)[:, None]
    rows_minor = lambda a: jnp.transpose(a, (0, 2, 3, 1))
    ka, kb = nsa_compress_partials(rows_minor(ck), page_table, pe_k, w1_k, pages=_CMP_PAGES, transposed=True)
    va, vb = nsa_compress_partials(seg4(cv), page_table, pe_v, w1_v, pages=_CMP_PAGES)
    _, kb_new = nsa_compress_partials(seg3(as_page(kc_r), NSA_KW), ident, pe_k, w1_k, pages=1)
    _, vb_new = nsa_compress_partials(seg3(as_page(vc_r), NSA_VW), ident, pe_v, w1_v, pages=1)
    kcmp = nsa_compress_finish(ka, kb, kb_new, w2_k, kn_c, is_key=True, alibi_lanes=True)
    vcmp = nsa_compress_finish(va, vb, vb_new, w2_v, jnp.zeros((NSA_DV,), jnp.float32), is_key=False, alibi_lanes=False)
    wl = wk_buf.shape[1]
    o = nsa_sample_attention(q_rows, kcmp, vcmp, rows_minor(sk), sv.reshape(sv.shape[0], PAGE_ROWS * NSA_KV, NSA_DV), page_table, as_page(ks_r), as_page(vs_r), rows_minor(wk_buf),
                             wv_buf.reshape(B, wl, NSA_VW), as_page(kw_r), as_page(vw_r), gl, z, s_tok=S)
    y = matmul(o.astype(jnp.bfloat16), w_out.astype(jnp.bfloat16), tm=B * S, tn=1024, residual=x2, name="nsa_out_proj")
    r4 = lambda a, d: a.reshape(B, S, NSA_KV, d)
    kw_all = jnp.concatenate([wk_buf, r4(kw_r, NSA_DK)], axis=1)
    vw_all = jnp.concatenate([wv_buf, r4(vw_r, NSA_DV)], axis=1)
    return y.reshape(B, S, D), (r4(kc_r, NSA_DK), r4(vc_r, NSA_DV), r4(ks_r, NSA_DK), r4(vs_r, NSA_DV),
                                kw_all[:, S:], vw_all[:, S:])


def kernel(x_prompt, x_sample, state_delta, state_conv, cache_cmp_k, cache_cmp_v, cache_slc_k, cache_slc_v, cache_win_k, cache_win_v, page_table, norm_dn, w_in_dn, conv_w_dn, a_log_dn, dt_bias_dn, out_norm_dn, w_out_dn, norm_nsa, w_in_nsa, q_norm_nsa, k_norm_cmp, k_norm_slc, k_norm_win, cmp_pe_k, cmp_w1_k, cmp_w2_k, cmp_pe_v, cmp_w1_v, cmp_w2_v, w_out_nsa):
    xp, xs = x_prompt, x_sample
    B = xp.shape[0]
    dw = (norm_dn[0], w_in_dn[0], conv_w_dn[0], a_log_dn[0], dt_bias_dn[0], out_norm_dn[0], w_out_dn[0])
    buf0 = jnp.zeros((B, DN_CONV - 1, conv_w_dn.shape[-1]), xp.dtype)
    st0 = jnp.zeros((B, DN_HEADS, DN_DK, DN_DV), jnp.float32)
    xp, p_dn = _deltanet_mixer(xp, buf0, st0, *dw)
    xs, s_dn = _deltanet_mixer(xs, state_conv[0], state_delta[0], *dw)
    nw = (norm_nsa[0], w_in_nsa[0], q_norm_nsa[0], k_norm_cmp[0], k_norm_slc[0], k_norm_win[0],
          cmp_pe_k[0], cmp_w1_k[0], cmp_w2_k[0], cmp_pe_v[0], cmp_w1_v[0], cmp_w2_v[0], w_out_nsa[0])
    xp, p_nsa = _nsa_prompt(xp, *nw)
    xs, s_nsa = _nsa_sample(xs, cache_cmp_k[0], cache_cmp_v[0], cache_slc_k[0], cache_slc_v[0],
                            cache_win_k[0], cache_win_v[0], page_table, *nw)
    return ((xp, xs, p_dn[0][None], p_dn[1][None]) + tuple(t[None] for t in p_nsa)
            + (s_dn[0][None], s_dn[1][None]) + tuple(t[None] for t in s_nsa))
```

```python
import functools

import jax
import jax.numpy as jnp
import numpy as np
from jax import lax
from jax.experimental import pallas as pl
from jax.experimental.pallas import tpu as pltpu

D_MODEL = 4096
EPS = 1e-6

DN_HEADS = 32
DN_DK = 128
DN_DV = 128
DN_CONV = 4
DN_CHUNK = 64

NSA_HEADS = 32
NSA_KV = 4
NSA_REP = NSA_HEADS // NSA_KV
NSA_DK = 192
NSA_DV = 128
CMP_BLOCK = 32
CMP_STRIDE = 16
CMP_RATIO = CMP_BLOCK // CMP_STRIDE
CMP_HIDDEN = 256
SLC_BLOCK = 64
SLC_TOPN = 16
WINDOW = 512
NSA_Q = NSA_HEADS * NSA_DK
NSA_KW = NSA_KV * NSA_DK
NSA_VW = NSA_KV * NSA_DV
NSA_O = NSA_HEADS * NSA_DV
NSA_SPLITS = (NSA_Q, NSA_KW, NSA_VW, NSA_KW, NSA_VW, NSA_KW, NSA_VW, 3 * NSA_HEADS, NSA_O)

VMEM_LIMIT_BYTES = 56 * 1024 * 1024
LANE = 128
_WT_TN = 1024


def _cparams(*sem):
    return pltpu.CompilerParams(dimension_semantics=sem, vmem_limit_bytes=VMEM_LIMIT_BYTES)


def _rmsnorm_kernel(x_ref, w_ref, o_ref):
    x = x_ref[...]
    ms = jnp.mean(x * x, axis=-1, keepdims=True)
    o_ref[...] = (x * lax.rsqrt(ms + EPS) * w_ref[...]).astype(o_ref.dtype)


def rmsnorm_bf16(x, w, *, tm):
    m, d = x.shape
    return pl.pallas_call(
        _rmsnorm_kernel,
        grid=(m // tm,),
        in_specs=[pl.BlockSpec((tm, d), lambda i: (i, 0)), pl.BlockSpec((1, d), lambda i: (0, 0))],
        out_specs=pl.BlockSpec((tm, d), lambda i: (i, 0)),
        out_shape=jax.ShapeDtypeStruct((m, d), jnp.bfloat16),
        compiler_params=_cparams("parallel"),
        name="rmsnorm_bf16",
    )(x, w.reshape(1, d))


def _mm_kernel(a_ref, b_ref, o_ref):
    o_ref[...] = jnp.dot(a_ref[...], b_ref[...], preferred_element_type=jnp.float32).astype(o_ref.dtype)


def _mm_res_kernel(a_ref, b_ref, r_ref, o_ref):
    acc = jnp.dot(a_ref[...], b_ref[...], preferred_element_type=jnp.float32)
    o_ref[...] = r_ref[...] + acc


def matmul(a, b, *, tm, tn, residual=None, out_dtype=jnp.float32, name="matmul"):
    m, k = a.shape
    _, n = b.shape
    assert m % tm == 0 and n % tn == 0
    in_specs = [pl.BlockSpec((tm, k), lambda i, j: (i, 0)), pl.BlockSpec((k, tn), lambda i, j: (0, j))]
    args = [a, b]
    kern = _mm_kernel
    if residual is not None:
        in_specs.append(pl.BlockSpec((tm, tn), lambda i, j: (i, j)))
        args.append(residual)
        kern = _mm_res_kernel
    return pl.pallas_call(
        kern,
        grid=(m // tm, n // tn),
        in_specs=in_specs,
        out_specs=pl.BlockSpec((tm, tn), lambda i, j: (i, j)),
        out_shape=jax.ShapeDtypeStruct((m, n), out_dtype),
        compiler_params=_cparams("parallel", "parallel"),
        name=name,
    )(*args)


def _mm_wt_kernel(a_ref, w_ref, o_ref):
    o_ref[...] = lax.dot_general(a_ref[...], w_ref[...].astype(jnp.bfloat16), (((1,), (1,)), ((), ())),
                                 preferred_element_type=jnp.float32)


def matmul_wt(a, wt, *, tm, tn, n_off, n, name):
    m, k = a.shape
    assert m % tm == 0 and n % tn == 0 and n_off % tn == 0
    j0 = n_off // tn
    a_mode = dict(pipeline_mode=pl.Buffered(1))
    return pl.pallas_call(
        _mm_wt_kernel,
        grid=(m // tm, n // tn),
        in_specs=[pl.BlockSpec((tm, k), lambda i, j: (i, 0), **a_mode),
                  pl.BlockSpec((tn, k), lambda i, j: (j0 + j, 0))],
        out_specs=pl.BlockSpec((tm, tn), lambda i, j: (i, j)),
        out_shape=jax.ShapeDtypeStruct((m, n), jnp.float32),
        compiler_params=_cparams("parallel", "parallel"),
        name=name,
    )(a, wt)


def _row_tile(m):
    return 1024 if m % 1024 == 0 else m


def _dot(a, b):
    return jnp.dot(a.astype(jnp.bfloat16), b.astype(jnp.bfloat16), preferred_element_type=jnp.float32)


def _dot_nt(a, b):
    return lax.dot_general(a.astype(jnp.bfloat16), b.astype(jnp.bfloat16), (((1,), (1,)), ((), ())),
                           preferred_element_type=jnp.float32)


def _dot_tn(a, b):
    return lax.dot_general(a.astype(jnp.bfloat16), b.astype(jnp.bfloat16), (((0,), (0,)), ((), ())),
                           preferred_element_type=jnp.float32)


def _split_bf16(x):
    hi = x.astype(jnp.bfloat16)
    lo = (x - hi.astype(jnp.float32)).astype(jnp.bfloat16)
    return hi, lo


def _dn_gate_kernel(h_ref, wt_ref, alog_ref, dtb_ref, o_ref, *, heads):
    r = lax.dot_general(wt_ref[...], h_ref[...], (((1,), (1,)), ((), ())),
                        preferred_element_type=jnp.float32)
    b = r[:heads]
    a = r[heads:] + dtb_ref[...]
    softplus = jnp.maximum(a, 0.0) + jnp.log(1.0 + jnp.exp(-jnp.abs(a)))
    o_ref[0:heads, :] = 1.0 / (1.0 + jnp.exp(-b))
    o_ref[heads:2 * heads, :] = -jnp.exp(alog_ref[...]) * softplus


def dn_gates(h, w_ba_t, a_log, dt_bias, *, tm):
    m, d = h.shape
    heads = a_log.shape[0]
    return pl.pallas_call(
        functools.partial(_dn_gate_kernel, heads=heads),
        grid=(m // tm,),
        in_specs=[pl.BlockSpec((tm, d), lambda i: (i, 0)),
                  pl.BlockSpec((2 * heads, d), lambda i: (0, 0)),
                  pl.BlockSpec((heads, 1), lambda i: (0, 0)),
                  pl.BlockSpec((heads, 1), lambda i: (0, 0))],
        out_specs=pl.BlockSpec((2 * heads, tm), lambda i: (0, i)),
        out_shape=jax.ShapeDtypeStruct((2 * heads, m), jnp.float32),
        compiler_params=_cparams("parallel"),
        name="dn_gates",
    )(h, w_ba_t, a_log.reshape(heads, 1), dt_bias.reshape(heads, 1))


_DN_TAIL = 8
_DN_TB = 256
_DN_TMIN = 2 * DN_CHUNK
_DN_HB = 8


def _dn_chunk_prep(q, k, g_row, b_row):
    c = q.shape[0]
    ii = lax.broadcasted_iota(jnp.int32, (c, c), 0)
    jj = lax.broadcasted_iota(jnp.int32, (c, c), 1)
    lower, strict, eye = ii >= jj, ii > jj, ii == jj
    g_b = jnp.broadcast_to(g_row, (c, c))
    b_b = jnp.broadcast_to(b_row, (c, c))
    b_col = jnp.sum(jnp.where(eye, b_b, 0.0), axis=1, keepdims=True)
    g_col = jnp.sum(jnp.where(eye, g_b, 0.0), axis=1, keepdims=True)
    gc_col = jnp.sum(jnp.where(lower, g_b, 0.0), axis=1, keepdims=True)
    gc_row = jnp.sum(jnp.where(ii <= jj, g_col, 0.0), axis=0, keepdims=True)
    g_last = jnp.sum(g_row, axis=1, keepdims=True)
    decay = jnp.exp(jnp.where(lower, gc_col - gc_row, -jnp.inf))
    kk = _dot_nt(k, k)
    qk = _dot_nt(q, k)
    lmat = jnp.where(strict, kk * decay, 0.0) * b_col
    attn = qk * decay
    return dict(x=-lmat, p=eye.astype(jnp.float32) - lmat, attn=attn, b_row=b_row,
                e_row=jnp.exp(gc_row), e_col=jnp.exp(gc_col), c_col=jnp.exp(g_last - gc_col),
                e_last=jnp.exp(g_last))


def _dn_inverse_levels(items, c):
    m = 2
    while m < c:
        for it in items:
            it["x"] = _dot(it["x"], it["x"])
        for it in items:
            it["p"] = it["p"] + _dot(it["p"], it["x"])
        m *= 2


def _dn_kernel(q_ref, k_ref, v_ref, z_ref, beta_ref, g_ref, cw_ref, cs_ref, s0_ref, onw_ref,
               o_ref, s_out_ref, xpad, s_scr, *, tb, hb, live):
    n = pl.program_id(2)
    c = DN_CHUNK
    d = DN_DK
    nc = live

    @pl.when(n == 0)
    def _init():
        xpad[:, 0:_DN_TAIL, :] = cs_ref[0]
        s_scr[...] = s0_ref[0]

    def conv(i, ref):
        xpad[i, _DN_TAIL:_DN_TAIL + tb, :] = ref[...]
        acc = None
        for j in range(DN_CONV):
            off = _DN_TAIL - (DN_CONV - 1) + j
            term = xpad[i, off:off + tb, :] * cw_ref[i, j:j + 1, :]
            acc = term if acc is None else acc + term
        xpad[i, 0:_DN_TAIL, :] = xpad[i, tb:tb + _DN_TAIL, :]
        return acc * (1.0 / (1.0 + jnp.exp(-acc)))

    qc, kc, vc = conv(0, q_ref), conv(1, k_ref), conv(2, v_ref)
    items = []
    for hh in range(hb):
        sl = slice(hh * d, (hh + 1) * d)
        q = qc[:, sl]
        k = kc[:, sl]
        q = q * (lax.rsqrt(jnp.sum(q * q, axis=-1, keepdims=True) + EPS) * (d ** -0.5))
        k = k * lax.rsqrt(jnp.sum(k * k, axis=-1, keepdims=True) + EPS)
        g_all = g_ref[hh]
        b_all = beta_ref[hh]
        for ci in range(nc):
            rs = slice(ci * c, (ci + 1) * c)
            it = _dn_chunk_prep(q[rs], k[rs], g_all[:, rs], b_all[:, rs])
            it.update(q=q[rs], k=k[rs], v=vc[rs, sl])
            items.append(it)
    _dn_inverse_levels(items, c)
    for it in items:
        inv_b = it["p"] * it["b_row"]
        it["u"] = _dot(inv_b, it["v"])
        it["w"] = _dot(inv_b * it["e_row"], it["k"])
    states = [s_scr[hh] for hh in range(hb)]
    outs = [[None] * nc for _ in range(hb)]
    for ci in range(nc):
        for hh in range(hb):
            it = items[hh * nc + ci]
            s = states[hh]
            v_new = it["u"] - _dot(it["w"], s)
            outs[hh][ci] = it["e_col"] * _dot(it["q"], s) + _dot(it["attn"], v_new)
            states[hh] = s * it["e_last"] + _dot_tn(it["k"] * it["c_col"], v_new)
    for hh in range(hb):
        sl = slice(hh * d, (hh + 1) * d)
        s_scr[hh] = states[hh]
        o = jnp.concatenate(outs[hh], axis=0) if nc > 1 else outs[hh][0]
        o = o * lax.rsqrt(jnp.mean(o * o, axis=-1, keepdims=True) + EPS) * onw_ref[...]
        z = z_ref[0:nc * c, sl]
        o_ref[0:nc * c, sl] = (o * (z * (1.0 / (1.0 + jnp.exp(-z))))).astype(o_ref.dtype)
        if nc * c < tb:
            o_ref[nc * c:tb, sl] = jnp.zeros((tb - nc * c, d), o_ref.dtype)

    @pl.when(n == pl.num_programs(2) - 1)
    def _fin():
        s_out_ref[0] = s_scr[...]


def dn_delta(proj, gates, conv_w, conv_state, s0, out_norm, *, batch, tb, hb, live):
    rows, width = proj.shape
    d = DN_DK
    heads = width // (4 * d)
    t = rows // batch
    nt = t // tb
    hg = heads // hb
    wd = hb * d
    row_map = lambda off: (lambda b, h, n: (b * nt + n, off * hg + h))
    gate_map = lambda off: (lambda b, h, n: (off * hg + h, 0, b * nt + n))
    return pl.pallas_call(
        functools.partial(_dn_kernel, tb=tb, hb=hb, live=live),
        grid=(batch, hg, nt),
        in_specs=[pl.BlockSpec((tb, wd), row_map(0)),
                  pl.BlockSpec((tb, wd), row_map(1)),
                  pl.BlockSpec((tb, wd), row_map(2)),
                  pl.BlockSpec((tb, wd), row_map(3)),
                  pl.BlockSpec((hb, 1, tb), gate_map(0)),
                  pl.BlockSpec((hb, 1, tb), gate_map(1)),
                  pl.BlockSpec((3, DN_CONV, wd), lambda b, h, n: (0, 0, h)),
                  pl.BlockSpec((1, 3, _DN_TAIL, wd), lambda b, h, n: (b, 0, 0, h)),
                  pl.BlockSpec((1, hb, d, d), lambda b, h, n: (b, h, 0, 0)),
                  pl.BlockSpec((1, d), lambda b, h, n: (0, 0))],
        out_specs=[pl.BlockSpec((tb, wd), lambda b, h, n: (b * nt + n, h)),
                   pl.BlockSpec((1, hb, d, d), lambda b, h, n: (b, h, 0, 0))],
        out_shape=[jax.ShapeDtypeStruct((rows, heads * d), jnp.bfloat16),
                   jax.ShapeDtypeStruct(s0.shape, jnp.float32)],
        scratch_shapes=[pltpu.VMEM((3, tb + _DN_TAIL, wd), jnp.float32),
                        pltpu.VMEM((hb, d, d), jnp.float32)],
        compiler_params=_cparams("parallel", "parallel", "arbitrary"),
        name="dn_delta",
    )(proj, proj, proj, proj, gates, gates, conv_w, conv_state, s0, out_norm.reshape(1, d))


def _deltanet_mixer(x, conv_buf, s0, norm_w, w_in, conv_w, a_log, dt_bias, out_norm, w_out):
    B, L, D = x.shape
    H = a_log.shape[0]
    cdim = conv_w.shape[1]
    hd = cdim // 3
    x2 = x.reshape(B * L, D)
    h = rmsnorm_bf16(x2, norm_w, tm=min(256, B * L))
    wt = w_in.T
    w_ba_t = wt[cdim + hd:].astype(jnp.bfloat16)
    proj = matmul_wt(h, wt, tm=_row_tile(B * L), tn=_WT_TN, n_off=0, n=cdim + hd, name="dn_in_proj")
    gates = dn_gates(h, w_ba_t, a_log, dt_bias, tm=_row_tile(B * L))
    new_buf = proj.reshape(B, L, -1)[:, L - (DN_CONV - 1):, :cdim]
    T = -(-L // _DN_TMIN) * _DN_TMIN
    if T != L:
        proj = jnp.pad(proj.reshape(B, L, -1), ((0, 0), (0, T - L), (0, 0))).reshape(B * T, -1)
        gates = jnp.pad(gates.reshape(2 * H, B, L), ((0, 0), (0, 0), (0, T - L))).reshape(2 * H, B * T)
    cw = conv_w.reshape(DN_CONV, 3, hd).transpose(1, 0, 2)
    cs = conv_buf.reshape(B, DN_CONV - 1, 3, hd).transpose(0, 2, 1, 3)
    cs = jnp.pad(cs, ((0, 0), (0, 0), (_DN_TAIL - (DN_CONV - 1), 0), (0, 0)))
    tb = min(_DN_TB, T)
    live = tb // DN_CHUNK if T == L else -(-L // DN_CHUNK)
    assert T == L or T == tb
    o, s_new = dn_delta(proj, gates.reshape(2 * H, 1, B * T), cw, cs, s0.astype(jnp.float32), out_norm,
                        batch=B, tb=tb, hb=_DN_HB, live=live)
    if T != L:
        o = o.reshape(B, T, -1)[:, :L].reshape(B * L, -1)
    y = matmul(o, w_out.astype(jnp.bfloat16), tm=_row_tile(B * L), tn=1024, residual=x2, name="dn_out_proj")
    return y.reshape(B, L, D), (s_new, new_buf)


def _alibi_slopes():
    hh = jnp.arange(1, NSA_HEADS + 1, dtype=jnp.float32)
    return jnp.exp2(-8.0 * hh / NSA_HEADS).reshape(NSA_KV, NSA_REP)


PAGE_ROWS = 128
NSA_DKP = 256
_POS_LANE = NSA_DK
_SEL_LANE = NSA_DK + 4
_SEL_LANES = 32
_ROW_LANE = _SEL_LANE + _SEL_LANES
NSA_DVP = 256
_FAST_SCORE_BOUND = 40.0
_MASK_BIG = 2.0 ** 100
_NEG = -1e30
_NSA_TQ = 256
_NSA_TK = 256
_CMP_PAGES = 16


def _slope_table():
    s = _alibi_slopes()
    hi = s.astype(jnp.bfloat16).astype(jnp.float32)
    lo = (s - hi).astype(jnp.bfloat16).astype(jnp.float32)
    tab = jnp.stack([64.0 * hi, hi, 64.0 * lo, lo], axis=-1)
    return jnp.pad(tab, ((0, 0), (0, 0), (0, NSA_DKP - NSA_DK - 4)))


_QP_HEADS = 4


def _q_proj_prep_kernel(a_ref, w_ref, nw_ref, tab_ref, o_ref, *, seq, pos0):
    tm = a_ref.shape[0]
    npad = NSA_DKP - NSA_DK
    acc = lax.dot_general(a_ref[...], w_ref[...].astype(jnp.bfloat16), (((1,), (1,)), ((), ())),
                          preferred_element_type=jnp.float32)
    t = (pos0 + (pl.program_id(0) * tm) % seq + lax.broadcasted_iota(jnp.int32, (tm, npad), 0)).astype(jnp.float32)
    lane = lax.broadcasted_iota(jnp.int32, (tm, npad), 1)
    for r in range(_QP_HEADS):
        x = acc[:, r * NSA_DK:(r + 1) * NSA_DK]
        y = x * lax.rsqrt(jnp.mean(x * x, axis=-1, keepdims=True) + EPS) * (nw_ref[...] * (NSA_DK ** -0.5))
        o_ref[0, r, :, 0:NSA_DK] = y.astype(o_ref.dtype)
        tab = jnp.broadcast_to(tab_ref[0, r:r + 1, :], (tm, npad))
        slope = tab[:, 1:2] + tab[:, 3:4]
        extra = jnp.where(lane == _ROW_LANE - NSA_DK, -slope * t, tab)
        o_ref[0, r, :, NSA_DK:NSA_DKP] = extra.astype(o_ref.dtype)


def nsa_q_proj_prep(h, wt, q_norm, *, tm, seq, pos0=0):
    rows, k = h.shape
    per_g = NSA_REP // _QP_HEADS
    tn = _QP_HEADS * NSA_DK
    return pl.pallas_call(
        functools.partial(_q_proj_prep_kernel, seq=seq, pos0=pos0),
        grid=(rows // tm, NSA_HEADS // _QP_HEADS),
        in_specs=[pl.BlockSpec((tm, k), lambda i, j: (i, 0)),
                  pl.BlockSpec((tn, k), lambda i, j: (j, 0)),
                  pl.BlockSpec((1, NSA_DK), lambda i, j: (0, 0)),
                  pl.BlockSpec((1, _QP_HEADS, NSA_DKP - NSA_DK), lambda i, j: (j, 0, 0))],
        out_specs=pl.BlockSpec((1, _QP_HEADS, tm, NSA_DKP), lambda i, j: (j // per_g, j % per_g, i, 0)),
        out_shape=jax.ShapeDtypeStruct((NSA_KV, NSA_REP, rows, NSA_DKP), jnp.bfloat16),
        compiler_params=_cparams("parallel", "parallel"),
        name="nsa_q_proj_prep",
    )(h, wt, q_norm.reshape(1, NSA_DK), _slope_table().reshape(NSA_HEADS // _QP_HEADS, _QP_HEADS, NSA_DKP - NSA_DK))


def _k_prep_kernel(k_ref, w_ref, on_ref, oh_ref, *, seq, onehot):
    tk = k_ref.shape[0]
    pos = (pl.program_id(0) * tk) % seq + lax.broadcasted_iota(jnp.int32, (tk, NSA_DKP - NSA_DK), 0)
    lane = lax.broadcasted_iota(jnp.int32, (tk, NSA_DKP - NSA_DK), 1)
    blk, off = pos // SLC_BLOCK, pos % SLC_BLOCK
    extra = jnp.where((lane == 0) | (lane == 2), blk, jnp.where((lane == 1) | (lane == 3), off, 0))
    if onehot:
        extra = jnp.where((lane >= 4) & (lane < 4 + _SEL_LANES) & (lane - 4 == blk), 1, extra)
    extra = jnp.where(lane == _ROW_LANE - NSA_DK, 1, extra)
    extra = extra.astype(jnp.float32)
    for g in range(NSA_KV):
        x = k_ref[:, g * NSA_DK:(g + 1) * NSA_DK]
        x = x * lax.rsqrt(jnp.mean(x * x, axis=-1, keepdims=True) + EPS) * w_ref[...]
        on_ref[:, g * NSA_DK:(g + 1) * NSA_DK] = x
        oh_ref[g, :, 0:NSA_DK] = x.astype(oh_ref.dtype)
        oh_ref[g, :, NSA_DK:NSA_DKP] = extra.astype(oh_ref.dtype)


def nsa_k_prep(k_raw, k_norm, *, seq, tk, onehot):
    rows = k_raw.shape[0]
    return pl.pallas_call(
        functools.partial(_k_prep_kernel, seq=seq, onehot=onehot),
        grid=(rows // tk,),
        in_specs=[pl.BlockSpec((tk, NSA_KW), lambda i: (i, 0)), pl.BlockSpec((1, NSA_DK), lambda i: (0, 0))],
        out_specs=[pl.BlockSpec((tk, NSA_KW), lambda i: (i, 0)),
                   pl.BlockSpec((NSA_KV, tk, NSA_DKP), lambda i: (0, i, 0))],
        out_shape=[jax.ShapeDtypeStruct((rows, NSA_KW), jnp.float32),
                   jax.ShapeDtypeStruct((NSA_KV, rows, NSA_DKP), jnp.bfloat16)],
        compiler_params=_cparams("parallel"),
        name="nsa_k_prep",
    )(k_raw, k_norm.reshape(1, NSA_DK))


def _v_prep_kernel(v_ref, o_ref):
    tk = v_ref.shape[0]
    ones_col = jnp.where(lax.broadcasted_iota(jnp.int32, (tk, NSA_DVP - NSA_DV), 1) == 0, 1.0, 0.0)
    for g in range(NSA_KV):
        o_ref[g, :, 0:NSA_DV] = v_ref[:, g * NSA_DV:(g + 1) * NSA_DV].astype(o_ref.dtype)
        o_ref[g, :, NSA_DV:NSA_DVP] = ones_col.astype(o_ref.dtype)


def nsa_v_prep(v_raw, *, tk):
    rows = v_raw.shape[0]
    return pl.pallas_call(
        _v_prep_kernel,
        grid=(rows // tk,),
        in_specs=[pl.BlockSpec((tk, NSA_VW), lambda i: (i, 0))],
        out_specs=pl.BlockSpec((NSA_KV, tk, NSA_DVP), lambda i: (0, i, 0)),
        out_shape=jax.ShapeDtypeStruct((NSA_KV, rows, NSA_DVP), jnp.bfloat16),
        compiler_params=_cparams("parallel"),
        name="nsa_v_prep",
    )(v_raw)


def _compress_partials_kernel(tab_ref, *refs, d, pages, transposed):
    x_refs, (pe_ref, w1_ref, perm_ref, a_ref, b_ref) = refs[:pages], refs[pages:]
    seg = PAGE_ROWS // CMP_STRIDE
    acc_a = acc_b = None
    if transposed:
        rows_of = [[_dot_nt(perm_ref[...], r[0, g]) for r in x_refs] for g in range(NSA_KV)]
        take = lambda ri, l, g: rows_of[g][ri][l * seg:(l + 1) * seg, :]
    elif x_refs[0].shape[1] == CMP_STRIDE * NSA_KV:
        take = lambda ri, l, g: x_refs[ri][:, l * NSA_KV + g, :]
    else:
        take = lambda ri, l, g: x_refs[ri][:, l, g * d:(g + 1) * d]

    for l in range(CMP_STRIDE):
        xg = jnp.concatenate([take(ri, l, g) for g in range(NSA_KV) for ri in range(pages)], axis=0)
        ta = jnp.dot((xg + pe_ref[l:l + 1, :]).astype(jnp.bfloat16), w1_ref[l], preferred_element_type=jnp.float32)
        tb = jnp.dot((xg + pe_ref[CMP_STRIDE + l:CMP_STRIDE + l + 1, :]).astype(jnp.bfloat16),
                     w1_ref[CMP_STRIDE + l], preferred_element_type=jnp.float32)
        acc_a = ta if acc_a is None else acc_a + ta
        acc_b = tb if acc_b is None else acc_b + tb
    n = pages * seg
    for g in range(NSA_KV):
        a_ref[0, g] = acc_a[g * n:(g + 1) * n]
        b_ref[0, g] = acc_b[g * n:(g + 1) * n]


def nsa_compress_partials(pool3, table, pe, w1, *, pages, transposed=False):
    d = pe.shape[1]
    batch, npg = table.shape
    seg = PAGE_ROWS // CMP_STRIDE
    blk = ((1,) if transposed else (seg,)) + tuple(pool3.shape[1:])
    x_spec = lambda k: pl.BlockSpec(blk, lambda b, i, tab: (tab[b, i * pages + k],) + (0,) * (len(blk) - 1))
    out_spec = pl.BlockSpec((1, NSA_KV, pages * seg, CMP_HIDDEN), lambda b, i, tab: (b, 0, i, 0))
    out_sds = jax.ShapeDtypeStruct((batch, NSA_KV, npg * seg, CMP_HIDDEN), jnp.float32)
    i = jnp.arange(PAGE_ROWS)
    perm = (i[None, :] == (i % seg)[:, None] * CMP_STRIDE + (i // seg)[:, None]).astype(jnp.bfloat16)
    return pl.pallas_call(
        functools.partial(_compress_partials_kernel, d=d, pages=pages, transposed=transposed),
        grid_spec=pltpu.PrefetchScalarGridSpec(
            num_scalar_prefetch=1,
            grid=(batch, npg // pages),
            in_specs=[x_spec(k) for k in range(pages)]
            + [pl.BlockSpec((CMP_BLOCK, d), lambda b, i, tab: (0, 0)),
               pl.BlockSpec((CMP_BLOCK, d, CMP_HIDDEN), lambda b, i, tab: (0, 0, 0)),
               pl.BlockSpec((PAGE_ROWS, PAGE_ROWS), lambda b, i, tab: (0, 0))],
            out_specs=[out_spec, out_spec]),
        out_shape=[out_sds, out_sds],
        compiler_params=_cparams("parallel", "arbitrary"),
        name="nsa_compress_partials",
    )(table, *([pool3] * pages), pe, w1.astype(jnp.bfloat16), perm)


def _cmp_extra_lanes(nrows, width):
    n = lax.broadcasted_iota(jnp.int32, (nrows, width), 0)
    lane = lax.broadcasted_iota(jnp.int32, (nrows, width), 1)
    ec = n * CMP_STRIDE + CMP_BLOCK - 1
    blk, off = ec // SLC_BLOCK, ec % SLC_BLOCK
    pl_ = lane - _POS_LANE
    return jnp.where((pl_ == 0) | (pl_ == 2), blk, jnp.where((pl_ == 1) | (pl_ == 3), off, 0)).astype(jnp.float32)


def _compress_finish_kernel(a_ref, b_ref, bn_ref, w2_ref, kn_ref, o_ref, *, is_key, has_new, alibi_lanes):
    ns = a_ref.shape[2]
    row = lax.broadcasted_iota(jnp.int32, (ns, 1), 0)
    for g in range(NSA_KV):
        nxt = pltpu.roll(b_ref[0, g], ns - 1, 0)
        if has_new:
            nxt = jnp.where(row == ns - 1, bn_ref[0, g, 0:1, :], nxt)
        hid_pre = a_ref[0, g] + nxt
        hid = hid_pre * (1.0 / (1.0 + jnp.exp(-hid_pre)))
        out = jnp.dot(hid.astype(jnp.bfloat16), w2_ref[...], preferred_element_type=jnp.float32)
        if is_key:
            ms = jnp.sum(out * out, axis=-1, keepdims=True) * (1.0 / NSA_DK)
            out = out * lax.rsqrt(ms + EPS) * kn_ref[...]
            if alibi_lanes:
                out = out + _cmp_extra_lanes(ns, out.shape[1])
        o_ref[0, g] = out.astype(o_ref.dtype)


def nsa_compress_finish(part_a, part_b, part_b_new, w2, kn, *, is_key, alibi_lanes):
    batch, _, ns, _ = part_a.shape
    d = w2.shape[1]
    dp = NSA_DKP if is_key else d
    has_new = part_b_new is not None
    if not has_new:
        part_b_new = jnp.zeros((batch, NSA_KV, 8, CMP_HIDDEN), jnp.float32)
    w2p = jnp.pad(w2, ((0, 0), (0, dp - d))).astype(jnp.bfloat16)
    knp = jnp.pad(kn, (0, dp - d)).reshape(1, dp)
    blk = lambda n: pl.BlockSpec((1, NSA_KV, n, CMP_HIDDEN), lambda b: (b, 0, 0, 0))
    return pl.pallas_call(
        functools.partial(_compress_finish_kernel, is_key=is_key, has_new=has_new, alibi_lanes=alibi_lanes),
        grid=(batch,),
        in_specs=[blk(ns), blk(ns), blk(part_b_new.shape[2]),
                  pl.BlockSpec((CMP_HIDDEN, dp), lambda b: (0, 0)),
                  pl.BlockSpec((1, dp), lambda b: (0, 0))],
        out_specs=pl.BlockSpec((1, NSA_KV, ns, dp), lambda b: (b, 0, 0, 0)),
        out_shape=jax.ShapeDtypeStruct((batch, NSA_KV, ns, dp), jnp.bfloat16),
        compiler_params=_cparams("parallel"),
        name="nsa_compress_finish_k" if is_key else "nsa_compress_finish_v",
    )(part_a, part_b, part_b_new, w2p, knp)


def _split_dot(x, m_bf16):
    hi, lo = _split_bf16(x)
    f = functools.partial(jnp.dot, preferred_element_type=jnp.float32)
    return f(hi, m_bf16) + f(lo, m_bf16)


def _topn_mask_t(imp_t, t_row, nsb, topn):
    jb = lax.broadcasted_iota(jnp.int32, imp_t.shape, 0)
    cur = t_row // SLC_BLOCK
    forced = (jb == 0) | (jb == cur) | (jb == cur - 1)
    valid = (jb * SLC_BLOCK <= t_row) & (jb < nsb)
    impm = jnp.where(forced, jnp.inf, jnp.where(valid, imp_t, -jnp.inf))
    rank = jnp.zeros(imp_t.shape, jnp.float32)
    for k in range(nsb):
        row = impm[k:k + 1, :]
        beats = (row > impm) | ((row == impm) & (k < jb))
        rank = rank + jnp.where(beats, 1.0, 0.0)
    return (rank < topn) & (jb < nsb)


def _cmp_and_select(q, kc, vc, t0, tq, seq):
    rep = NSA_REP
    m_rows = rep * tq
    nc = seq // CMP_STRIDE - CMP_RATIO + 1
    nseg = kc.shape[0]
    nsb = seq // SLC_BLOCK
    t_col = t0 + lax.broadcasted_iota(jnp.int32, (rep, tq, 1), 1).reshape(m_rows, 1)
    t_row = t0 + lax.broadcasted_iota(jnp.int32, (1, tq), 1)
    n_row = lax.broadcasted_iota(jnp.int32, (1, nseg), 1)
    cvalid = (n_row * CMP_STRIDE + CMP_BLOCK - 1 <= t_col) & (n_row < nc)
    s = jnp.where(cvalid, _dot_nt(q, kc), _NEG)
    p = jnp.where(cvalid, jnp.exp(s - jnp.max(s, axis=1, keepdims=True)), 0.0)
    p = p / jnp.maximum(jnp.sum(p, axis=1, keepdims=True), 1e-30)
    o_c = jnp.dot(p.astype(jnp.bfloat16), vc, preferred_element_type=jnp.float32)
    psum = jnp.sum(p.reshape(rep, tq, nseg), axis=0)
    cj = lax.broadcasted_iota(jnp.int32, (_SEL_LANES, nseg), 0) * SLC_BLOCK
    ci = lax.broadcasted_iota(jnp.int32, (_SEL_LANES, nseg), 1) * CMP_STRIDE
    ov = jnp.maximum(jnp.minimum(ci + CMP_BLOCK, cj + SLC_BLOCK) - jnp.maximum(ci, cj), 0)
    c2s_t = (ov.astype(jnp.float32) * (1.0 / CMP_STRIDE)).astype(jnp.bfloat16)
    p_hi, p_lo = _split_bf16(psum)
    imp_t = _dot_nt(c2s_t, p_hi) + _dot_nt(c2s_t, p_lo)
    sel_t = _topn_mask_t(imp_t, t_row, nsb, min(SLC_TOPN, nsb))
    pen_t = jnp.where(sel_t, 0.0, -_MASK_BIG).astype(jnp.bfloat16)
    pi = lax.broadcasted_iota(jnp.int32, (_SEL_LANES, NSA_DKP), 0)
    pj = lax.broadcasted_iota(jnp.int32, (_SEL_LANES, NSA_DKP), 1)
    place = jnp.where(pj == pi + _SEL_LANE, 1.0, 0.0).astype(jnp.bfloat16)
    q_add = _dot_tn(pen_t, place).astype(jnp.bfloat16)
    q_slc = (q.reshape(rep, tq, NSA_DKP) + q_add[None]).reshape(m_rows, NSA_DKP)
    return o_c, q_slc


def _gate_combine(gl_ref, z_ref, o_ref, o_c, o_s, o_w, tq):
    gates = 1.0 / (1.0 + jnp.exp(-gl_ref[...]))
    for r in range(NSA_REP):
        rs = slice(r * tq, (r + 1) * tq)
        o = (gates[:, 3 * r:3 * r + 1] * o_c[rs] + gates[:, 3 * r + 1:3 * r + 2] * o_s[rs]
             + gates[:, 3 * r + 2:3 * r + 3] * o_w[rs])
        z = z_ref[:, r * NSA_DV:(r + 1) * NSA_DV]
        o_ref[:, r * NSA_DV:(r + 1) * NSA_DV] = (o * (z * (1.0 / (1.0 + jnp.exp(-z))))).astype(o_ref.dtype)


def _nsa_prompt_fast_kernel(q_ref, kc_ref, vc_ref, ks_ref, vs_ref, kw_ref, vw_ref, gl_ref, z_ref, o_ref,
                            *, tq, tk, seq):
    t0 = pl.program_id(2) * tq
    m_rows = NSA_REP * tq
    q = q_ref[0].reshape(m_rows, NSA_DKP)
    t_col = t0 + lax.broadcasted_iota(jnp.int32, (NSA_REP, tq, 1), 1).reshape(m_rows, 1)
    o_c, q_slc = _cmp_and_select(q, kc_ref[0, 0], vc_ref[0, 0], t0, tq, seq)
    kpos = lax.broadcasted_iota(jnp.int32, (1, tk), 1)
    kt_hi = (t0 + tq + tk - 1) // tk
    zero = jnp.zeros((m_rows, NSA_DVP), jnp.float32)

    def pv(p, v_ref, k0):
        return jnp.dot(p.astype(jnp.bfloat16), v_ref[0, pl.ds(k0, tk), :], preferred_element_type=jnp.float32)

    def slc_pair(j, acc):
        outs = []
        for h in range(2):
            kt = 2 * j + h
            k0 = pl.multiple_of(jnp.minimum(kt, kt_hi - 1) * tk, tk)
            s = _dot_nt(q_slc, ks_ref[0, pl.ds(k0, tk), :])
            outs.append((s, k0, ((k0 + kpos) <= t_col) & (kt < kt_hi)))
        ps = [jnp.where(ok, jnp.exp(s), 0.0) for s, _, ok in outs]
        return acc + pv(ps[0], vs_ref, outs[0][1]) + pv(ps[1], vs_ref, outs[1][1])

    acc = lax.fori_loop(0, (kt_hi + 1) // 2, slc_pair, zero)
    o_s = acc[:, 0:NSA_DV] / acc[:, NSA_DV:NSA_DV + 1]

    def win_tile(kt):
        k0 = pl.multiple_of(kt * tk, tk)
        dw = t_col - (k0 + kpos)
        return _dot_nt(q, kw_ref[0, pl.ds(k0, tk), :]), k0, (dw >= 0) & (dw < WINDOW)

    def win_body(kt, acc):
        s, k0, ok = win_tile(kt)
        return acc + pv(jnp.where(ok, jnp.exp(s), 0.0), vw_ref, k0)

    kt_lo = jnp.maximum(t0 - (WINDOW - 1), 0) // tk
    acc = lax.fori_loop(kt_lo, kt_hi - 2, win_body, zero)
    ta = win_tile(jnp.maximum(kt_hi - 2, 0))
    tb_ = win_tile(kt_hi - 1)
    pa = jnp.where(ta[2] & (kt_hi >= 2), jnp.exp(ta[0]), 0.0)
    pb = jnp.where(tb_[2], jnp.exp(tb_[0]), 0.0)
    acc = acc + pv(pa, vw_ref, ta[1]) + pv(pb, vw_ref, tb_[1])
    o_w = acc[:, 0:NSA_DV] / acc[:, NSA_DV:NSA_DV + 1]
    _gate_combine(gl_ref, z_ref, o_ref, o_c, o_s, o_w, tq)


def _flash_step(q, k, v, mask, carry):
    m, l, acc = carry
    s = _dot_nt(q, k)
    s = jnp.where(mask, s, _NEG)
    m_new = jnp.maximum(m, jnp.max(s, axis=1, keepdims=True))
    alpha = jnp.exp(m - m_new)
    p = jnp.exp(s - m_new)
    l = alpha * l + jnp.sum(p, axis=1, keepdims=True)
    acc = alpha * acc + jnp.dot(p.astype(jnp.bfloat16), v, preferred_element_type=jnp.float32)
    return m_new, l, acc


def _nsa_prompt_kernel(q_ref, kc_ref, vc_ref, ks_ref, vs_ref, kw_ref, vw_ref, gl_ref, z_ref, o_ref,
                       *, tq, tk, seq):
    t0 = pl.program_id(2) * tq
    m_rows = NSA_REP * tq
    q = q_ref[0].reshape(m_rows, NSA_DKP)
    t_col = t0 + lax.broadcasted_iota(jnp.int32, (NSA_REP, tq, 1), 1).reshape(m_rows, 1)
    o_c, q_slc = _cmp_and_select(q, kc_ref[0, 0], vc_ref[0, 0], t0, tq, seq)

    kpos = lax.broadcasted_iota(jnp.int32, (1, tk), 1)
    init = (jnp.full((m_rows, 1), _NEG, jnp.float32), jnp.zeros((m_rows, 1), jnp.float32),
            jnp.zeros((m_rows, NSA_DV), jnp.float32))

    def slc_body(kt, carry):
        k0 = pl.multiple_of(kt * tk, tk)
        mask = (k0 + kpos) <= t_col
        return _flash_step(q_slc, ks_ref[0, pl.ds(k0, tk), :], vs_ref[0, pl.ds(k0, tk), :], mask, carry)

    _, l_s, acc_s = lax.fori_loop(0, (t0 + tq + tk - 1) // tk, slc_body, init)

    def win_body(kt, carry):
        k0 = pl.multiple_of(kt * tk, tk)
        dw = t_col - (k0 + kpos)
        mask = (dw >= 0) & (dw < WINDOW)
        return _flash_step(q, kw_ref[0, pl.ds(k0, tk), :], vw_ref[0, pl.ds(k0, tk), :], mask, carry)

    kt_lo = jnp.maximum(t0 - (WINDOW - 1), 0) // tk
    _, l_w, acc_w = lax.fori_loop(kt_lo, (t0 + tq + tk - 1) // tk, win_body, init)
    _gate_combine(gl_ref, z_ref, o_ref, o_c, acc_s / l_s, acc_w / l_w, tq)


def nsa_prompt_attention(q_hm, kcmp, vcmp, ks_hm, vs_hm, kw_hm, vw_hm, gl, z, *, batch, tq, tk, fast):
    rows = z.shape[0]
    seq = rows // batch
    nq = seq // tq
    nseg = kcmp.shape[2]
    row_blk = lambda b, g, i: (b * nq + i, g)
    kv_blk = lambda b, g, i: (g, b, 0)
    dvb = NSA_DVP if fast else NSA_DV
    return pl.pallas_call(
        functools.partial(_nsa_prompt_fast_kernel if fast else _nsa_prompt_kernel, tq=tq, tk=tk, seq=seq),
        grid=(batch, NSA_KV, nq),
        in_specs=[pl.BlockSpec((1, NSA_REP, tq, NSA_DKP), lambda b, g, i: (g, 0, b * nq + i, 0)),
                  pl.BlockSpec((1, 1, nseg, NSA_DKP), lambda b, g, i: (b, g, 0, 0)),
                  pl.BlockSpec((1, 1, nseg, NSA_DV), lambda b, g, i: (b, g, 0, 0)),
                  pl.BlockSpec((1, seq, NSA_DKP), kv_blk),
                  pl.BlockSpec((1, seq, dvb), kv_blk),
                  pl.BlockSpec((1, seq, NSA_DKP), kv_blk),
                  pl.BlockSpec((1, seq, dvb), kv_blk),
                  pl.BlockSpec((tq, LANE), row_blk),
                  pl.BlockSpec((tq, NSA_REP * NSA_DV), row_blk)],
        out_specs=pl.BlockSpec((tq, NSA_REP * NSA_DV), row_blk),
        out_shape=jax.ShapeDtypeStruct((rows, NSA_O), jnp.bfloat16),
        compiler_params=_cparams("parallel", "parallel", "arbitrary"),
        name="nsa_prompt_attention_fast" if fast else "nsa_prompt_attention",
    )(q_hm, kcmp, vcmp, ks_hm, vs_hm, kw_hm, vw_hm, gl, z)


def _nsa_in_proj(x, norm_w, w_in):
    B, L, D = x.shape
    x2 = x.reshape(B * L, D)
    h = rmsnorm_bf16(x2, norm_w, tm=min(256, B * L))
    offs = np.cumsum((0,) + NSA_SPLITS)
    tm = _row_tile(B * L)

    wt = w_in.T

    def seg(i, tn, name):
        return matmul_wt(h, wt, tm=tm, tn=tn, n_off=int(offs[i]), n=int(offs[i + 1] - offs[i]), name=name)

    kv = matmul_wt(h, wt, tm=tm, tn=768, n_off=int(offs[1]), n=int(offs[7] - offs[1]), name="nsa_kv_proj")
    kc, vc, ks, vs, kw, vw = (kv[:, int(offs[i] - offs[1]):int(offs[i + 1] - offs[1])] for i in range(1, 7))
    wg = wt[offs[7]:offs[8]].reshape(NSA_KV, 3 * NSA_REP, D)
    wg = jnp.pad(wg, ((0, 0), (0, LANE - 3 * NSA_REP), (0, 0))).reshape(NSA_KV * LANE, D)
    gl = matmul_wt(h, wg, tm=tm, tn=NSA_KV * LANE, n_off=0, n=NSA_KV * LANE, name="nsa_gate_proj")
    z = matmul_wt(h, wt[offs[8]:], tm=tm, tn=_WT_TN, n_off=0, n=NSA_O, name="nsa_z_proj")
    q_of = functools.partial(nsa_q_proj_prep, h, wt, tm=tm)
    return x2, q_of, kc, vc, ks, vs, kw, vw, gl, z


def _nsa_prompt(x, norm_w, w_in, q_norm, kn_c, kn_s, kn_w, pe_k, w1_k, w2_k, pe_v, w1_v, w2_v, w_out):
    B, T, D = x.shape
    x2, q_of, kc_r, vc_r, ks_raw, vs_r, kw_raw, vw_r, gl, z = _nsa_in_proj(x, norm_w, w_in)
    tp = min(512, T)
    q_hm = q_of(q_norm, seq=T)
    ks_r, ks_hm = nsa_k_prep(ks_raw, kn_s, seq=T, tk=tp, onehot=True)
    kw_r, kw_hm = nsa_k_prep(kw_raw, kn_w, seq=T, tk=tp, onehot=False)
    vs_hm = nsa_v_prep(vs_r, tk=tp)
    vw_hm = nsa_v_prep(vw_r, tk=tp)
    npg = T // PAGE_ROWS
    table = (jnp.arange(B, dtype=jnp.int32)[:, None] * npg + jnp.arange(npg, dtype=jnp.int32)[None, :])
    seg3 = lambda a: a.reshape(B * T // CMP_STRIDE, CMP_STRIDE, a.shape[1])
    ka, kb = nsa_compress_partials(seg3(kc_r), table, pe_k, w1_k, pages=min(_CMP_PAGES, npg))
    va, vb = nsa_compress_partials(seg3(vc_r), table, pe_v, w1_v, pages=min(_CMP_PAGES, npg))
    kcmp = nsa_compress_finish(ka, kb, None, w2_k, kn_c, is_key=True, alibi_lanes=True)
    vcmp = nsa_compress_finish(va, vb, None, w2_v, jnp.zeros((NSA_DV,), jnp.float32), is_key=False, alibi_lanes=False)
    bound = jnp.max(jnp.abs(q_norm)) * jnp.maximum(jnp.max(jnp.abs(kn_s)), jnp.max(jnp.abs(kn_w))) * (NSA_DK ** 0.5)
    attend = lambda fast: functools.partial(nsa_prompt_attention, batch=B, tq=_NSA_TQ, tk=_NSA_TK, fast=fast)
    og = lax.cond(bound < _FAST_SCORE_BOUND, attend(True), attend(False),
                  q_hm, kcmp, vcmp, ks_hm, vs_hm, kw_hm, vw_hm, gl, z)
    y = matmul(og, w_out.astype(jnp.bfloat16), tm=_row_tile(B * T), tn=1024, residual=x2, name="nsa_out_proj")
    wl = min(WINDOW, T)
    r4 = lambda a, d: a.reshape(B, T, NSA_KV, d)
    return y.reshape(B, T, D), (r4(kc_r, NSA_DK), r4(vc_r, NSA_DV), r4(ks_r, NSA_DK), r4(vs_r, NSA_DV),
                                r4(kw_r, NSA_DK)[:, T - wl:], r4(vw_r, NSA_DV)[:, T - wl:])


_SMP_PAGES = 16
_SEL_PAD = 384


def _online_update(m_ref, l_ref, acc_ref, g, s, v):
    m_old = m_ref[g]
    m_new = jnp.maximum(m_old, jnp.max(s, axis=1, keepdims=True))
    alpha = jnp.exp(m_old - m_new)
    p = jnp.exp(s - m_new)
    l_ref[g] = alpha * l_ref[g] + jnp.sum(p, axis=1, keepdims=True)
    acc_ref[g] = alpha * acc_ref[g] + jnp.dot(p.astype(jnp.bfloat16), v.astype(jnp.bfloat16),
                                              preferred_element_type=jnp.float32)
    m_ref[g] = m_new


def _nsa_sample_kernel(tab_ref, *refs, past, steps, pages, nsb):
    kp_refs = refs[:pages]
    vp_refs = refs[pages:2 * pages]
    (q_ref, kc_ref, vc_ref, kn_ref, vn_ref, kwc_ref, vwc_ref, kwn_ref, vwn_ref, gl_ref, z_ref,
     o_ref, m_scr, l_scr, acc_scr, sel_scr, oc_scr) = refs[2 * pages:]
    i = pl.program_id(1)
    s_tok = o_ref.shape[0]
    rows = NSA_REP * s_tok
    ncmp = kc_ref.shape[2]
    r_col = lax.broadcasted_iota(jnp.int32, (NSA_REP, s_tok, 1), 0).reshape(rows, 1)
    t_col = past + lax.broadcasted_iota(jnp.int32, (NSA_REP, s_tok, 1), 1).reshape(rows, 1)
    t_tok = past + lax.broadcasted_iota(jnp.int32, (s_tok, 1), 0)

    def slope_col(g):
        return jnp.exp2((r_col + (g * NSA_REP + 1)).astype(jnp.float32) * (-8.0 / NSA_HEADS))

    def q_of(g):
        return q_ref[0, g * rows:(g + 1) * rows, :]

    def rep_rows(x):
        return jnp.concatenate([x] * NSA_REP, axis=0)

    @pl.when(i == 0)
    def _first():
        n_row = lax.broadcasted_iota(jnp.int32, (1, ncmp), 1)
        cvalid = (n_row * CMP_STRIDE + CMP_BLOCK - 1) <= t_col
        ci = lax.broadcasted_iota(jnp.int32, (ncmp, _SEL_PAD), 0) * CMP_STRIDE
        cj = lax.broadcasted_iota(jnp.int32, (ncmp, _SEL_PAD), 1) * SLC_BLOCK
        ov = jnp.maximum(jnp.minimum(ci + CMP_BLOCK, cj + SLC_BLOCK) - jnp.maximum(ci, cj), 0)
        c2s = (ov.astype(jnp.float32) * (1.0 / CMP_STRIDE)).astype(jnp.bfloat16)
        jb = lax.broadcasted_iota(jnp.int32, (s_tok, _SEL_PAD), 1)
        cur = t_tok // SLC_BLOCK
        forced = (jb == 0) | (jb == cur) | (jb == cur - 1)
        valid = (jb * SLC_BLOCK <= t_tok) & (jb < nsb)
        for g in range(NSA_KV):
            s = jnp.where(cvalid, _dot_nt(q_of(g), kc_ref[0, g]), _NEG)
            p = jnp.where(cvalid, jnp.exp(s - jnp.max(s, axis=1, keepdims=True)), 0.0)
            p = p / jnp.maximum(jnp.sum(p, axis=1, keepdims=True), 1e-30)
            oc_scr[g] = jnp.dot(p.astype(jnp.bfloat16), vc_ref[0, g], preferred_element_type=jnp.float32)
            imp = _split_dot(jnp.sum(p.reshape(NSA_REP, s_tok, ncmp), axis=0), c2s)
            work = jnp.where(forced, jnp.inf, jnp.where(valid, imp, -jnp.inf))
            sel = jnp.zeros((s_tok, _SEL_PAD), jnp.float32)
            for _ in range(min(SLC_TOPN, nsb)):
                mx = jnp.max(work, axis=1, keepdims=True)
                first = jnp.min(jnp.where(work == mx, jb, _SEL_PAD), axis=1, keepdims=True)
                hit = jb == first
                sel = jnp.where(hit, 1.0, sel)
                work = jnp.where(hit, -jnp.inf, work)
            sel_scr[g] = sel
            m_scr[g] = jnp.full((rows, 1), _NEG, jnp.float32)
            l_scr[g] = jnp.zeros((rows, 1), jnp.float32)
            acc_scr[g] = jnp.zeros((rows, NSA_DV), jnp.float32)

    def fold(score_of, v_of, nk, page0, causal):
        pos = page0 * PAGE_ROWS + lax.broadcasted_iota(jnp.int32, (1, nk), 1)
        bj = lax.broadcasted_iota(jnp.int32, (_SEL_PAD, nk), 0)
        bl = lax.broadcasted_iota(jnp.int32, (_SEL_PAD, nk), 1)
        expand = jnp.where(bj == (page0 * PAGE_ROWS + bl) // SLC_BLOCK, 1.0, 0.0).astype(jnp.bfloat16)
        for g in range(NSA_KV):
            selk = jnp.dot(sel_scr[g].astype(jnp.bfloat16), expand, preferred_element_type=jnp.float32)
            ok = rep_rows(selk) > 0.5
            if causal:
                ok = ok & (pos <= t_col)
            s = score_of(g, q_of(g)[:, 0:NSA_DK])
            s = s - slope_col(g) * (t_col - pos).astype(jnp.float32)
            _online_update(m_scr, l_scr, acc_scr, g, jnp.where(ok, s, _NEG), v_of(g))

    fold(lambda g, qg: _dot(qg, jnp.concatenate([r[0, g] for r in kp_refs], axis=1)),
         lambda g: jnp.concatenate([r[0, pl.ds(g, PAGE_ROWS, stride=NSA_KV), :] for r in vp_refs], axis=0),
         pages * PAGE_ROWS, i * pages, causal=False)

    @pl.when(i == steps - 1)
    def _last():
        fold(lambda g, qg: _dot_nt(qg, kn_ref[0, :, g * NSA_DK:(g + 1) * NSA_DK]),
             lambda g: vn_ref[0, :, g * NSA_DV:(g + 1) * NSA_DV], PAGE_ROWS, past // PAGE_ROWS, causal=True)
        nwc = kwc_ref.shape[3]
        posw = jnp.concatenate([past - nwc + lax.broadcasted_iota(jnp.int32, (1, nwc), 1),
                                past + lax.broadcasted_iota(jnp.int32, (1, PAGE_ROWS), 1)], axis=1)
        dw = t_col - posw
        wok = (dw >= 0) & (dw < WINDOW) & (posw >= 0)
        gates = 1.0 / (1.0 + jnp.exp(-gl_ref[...]))
        for g in range(NSA_KV):
            ksl = slice(g * NSA_DK, (g + 1) * NSA_DK)
            vsl = slice(g * NSA_DV, (g + 1) * NSA_DV)
            qg = q_of(g)[:, 0:NSA_DK]
            s = jnp.concatenate([_dot(qg, kwc_ref[0, g]), _dot_nt(qg, kwn_ref[0, :, ksl])], axis=1)
            s = jnp.where(wok, s - slope_col(g) * dw.astype(jnp.float32), _NEG)
            p = jnp.where(wok, jnp.exp(s - jnp.max(s, axis=1, keepdims=True)), 0.0)
            p = (p / jnp.maximum(jnp.sum(p, axis=1, keepdims=True), 1e-30)).astype(jnp.bfloat16)
            o_w = (jnp.dot(p[:, 0:nwc], vwc_ref[0, :, vsl].astype(jnp.bfloat16), preferred_element_type=jnp.float32)
                   + jnp.dot(p[:, nwc:], vwn_ref[0, :, vsl].astype(jnp.bfloat16), preferred_element_type=jnp.float32))
            o_s = acc_scr[g] / l_scr[g]
            o_c = oc_scr[g]
            for r in range(NSA_REP):
                rs = slice(r * s_tok, (r + 1) * s_tok)
                c0 = g * LANE + 3 * r
                o = (gates[:, c0:c0 + 1] * o_c[rs] + gates[:, c0 + 1:c0 + 2] * o_s[rs]
                     + gates[:, c0 + 2:c0 + 3] * o_w[rs])
                hs = slice((g * NSA_REP + r) * NSA_DV, (g * NSA_REP + r + 1) * NSA_DV)
                z = z_ref[:, hs]
                o_ref[:, hs] = o * (z * (1.0 / (1.0 + jnp.exp(-z))))


def nsa_sample_attention(q_rows, kcmp, vcmp, pool_k, pool_v, page_table, k_new, v_new, kw_cache, vw_cache,
                         kw_new, vw_new, gl, z, *, s_tok):
    batch, npg = page_table.shape
    pages = _SMP_PAGES
    steps = npg // pages
    past = npg * PAGE_ROWS
    nsb = -(-(past + s_tok) // SLC_BLOCK)
    rows = NSA_REP * s_tok
    ncmp = kcmp.shape[2]
    nwc = kw_cache.shape[3]
    kpage_spec = lambda k: pl.BlockSpec((1, NSA_KV, NSA_DK, PAGE_ROWS), lambda b, i, tab: (tab[b, i * pages + k], 0, 0, 0))
    vpage_spec = lambda k: pl.BlockSpec((1, PAGE_ROWS * NSA_KV, NSA_DV), lambda b, i, tab: (tab[b, i * pages + k], 0, 0))
    per_b = lambda shape: pl.BlockSpec((1,) + shape, lambda b, i, tab: (b,) + (0,) * len(shape))
    tok_blk = lambda w: pl.BlockSpec((s_tok, w), lambda b, i, tab: (b, 0))
    return pl.pallas_call(
        functools.partial(_nsa_sample_kernel, past=past, steps=steps, pages=pages, nsb=nsb),
        grid_spec=pltpu.PrefetchScalarGridSpec(
            num_scalar_prefetch=1,
            grid=(batch, steps),
            in_specs=[kpage_spec(k) for k in range(pages)] + [vpage_spec(k) for k in range(pages)]
            + [per_b((NSA_KV * rows, NSA_DKP)), per_b((NSA_KV, ncmp, NSA_DKP)), per_b((NSA_KV, ncmp, NSA_DV)),
               per_b((PAGE_ROWS, NSA_KW)), per_b((PAGE_ROWS, NSA_VW)),
               per_b((NSA_KV, NSA_DK, nwc)), per_b((nwc, NSA_VW)), per_b((PAGE_ROWS, NSA_KW)), per_b((PAGE_ROWS, NSA_VW)),
               tok_blk(NSA_KV * LANE), tok_blk(NSA_O)],
            out_specs=tok_blk(NSA_O),
            scratch_shapes=[pltpu.VMEM((NSA_KV, rows, 1), jnp.float32), pltpu.VMEM((NSA_KV, rows, 1), jnp.float32),
                            pltpu.VMEM((NSA_KV, rows, NSA_DV), jnp.float32),
                            pltpu.VMEM((NSA_KV, s_tok, _SEL_PAD), jnp.float32),
                            pltpu.VMEM((NSA_KV, rows, NSA_DV), jnp.float32)]),
        out_shape=jax.ShapeDtypeStruct((batch * s_tok, NSA_O), jnp.float32),
        compiler_params=_cparams("parallel", "arbitrary"),
        name="nsa_sample_attention",
    )(page_table, *([pool_k] * pages), *([pool_v] * pages), q_rows, kcmp, vcmp, k_new, v_new,
      kw_cache, vw_cache, kw_new, vw_new, gl, z)


def _nsa_sample(x, ck, cv, sk, sv, wk_buf, wv_buf, page_table, norm_w,
                w_in, q_norm, kn_c, kn_s, kn_w, pe_k, w1_k, w2_k, pe_v, w1_v, w2_v, w_out):
    B, S, D = x.shape
    x2, q_of, kc_r, vc_r, ks_raw, vs_r, kw_raw, vw_r, gl, z = _nsa_in_proj(x, norm_w, w_in)
    q_hm = q_of(q_norm, seq=S, pos0=page_table.shape[1] * PAGE_ROWS)
    q_rows = q_hm.reshape(NSA_KV, NSA_REP, B, S, NSA_DKP).transpose(2, 0, 1, 3, 4).reshape(B, NSA_KV * NSA_REP * S, NSA_DKP)
    ks_r, _ = nsa_k_prep(ks_raw, kn_s, seq=S, tk=B * S, onehot=False)
    kw_r, _ = nsa_k_prep(kw_raw, kn_w, seq=S, tk=B * S, onehot=False)
    as_page = lambda a: jnp.pad(a.reshape(B, S, -1), ((0, 0), (0, PAGE_ROWS - S), (0, 0)))
    seg3 = lambda a, w: a.reshape(-1, CMP_STRIDE, w)
    seg4 = lambda a: a.reshape(-1, CMP_STRIDE * NSA_KV, a.shape[3])
    ident = jnp.arange(B, dtype=jnp.int32)[:, None]
    rows_minor = lambda a: jnp.transpose(a, (0, 2, 3, 1))
    ka, kb = nsa_compress_partials(rows_minor(ck), page_table, pe_k, w1_k, pages=_CMP_PAGES, transposed=True)
    va, vb = nsa_compress_partials(seg4(cv), page_table, pe_v, w1_v, pages=_CMP_PAGES)
    _, kb_new = nsa_compress_partials(seg3(as_page(kc_r), NSA_KW), ident, pe_k, w1_k, pages=1)
    _, vb_new = nsa_compress_partials(seg3(as_page(vc_r), NSA_VW), ident, pe_v, w1_v, pages=1)
    kcmp = nsa_compress_finish(ka, kb, kb_new, w2_k, kn_c, is_key=True, alibi_lanes=True)
    vcmp = nsa_compress_finish(va, vb, vb_new, w2_v, jnp.zeros((NSA_DV,), jnp.float32), is_key=False, alibi_lanes=False)
    wl = wk_buf.shape[1]
    o = nsa_sample_attention(q_rows, kcmp, vcmp, rows_minor(sk), sv.reshape(sv.shape[0], PAGE_ROWS * NSA_KV, NSA_DV), page_table, as_page(ks_r), as_page(vs_r), rows_minor(wk_buf),
                             wv_buf.reshape(B, wl, NSA_VW), as_page(kw_r), as_page(vw_r), gl, z, s_tok=S)
    y = matmul(o.astype(jnp.bfloat16), w_out.astype(jnp.bfloat16), tm=B * S, tn=1024, residual=x2, name="nsa_out_proj")
    r4 = lambda a, d: a.reshape(B, S, NSA_KV, d)
    kw_all = jnp.concatenate([wk_buf, r4(kw_r, NSA_DK)], axis=1)
    vw_all = jnp.concatenate([wv_buf, r4(vw_r, NSA_DV)], axis=1)
    return y.reshape(B, S, D), (r4(kc_r, NSA_DK), r4(vc_r, NSA_DV), r4(ks_r, NSA_DK), r4(vs_r, NSA_DV),
                                kw_all[:, S:], vw_all[:, S:])


def kernel(x_prompt, x_sample, state_delta, state_conv, cache_cmp_k, cache_cmp_v, cache_slc_k, cache_slc_v, cache_win_k, cache_win_v, page_table, norm_dn, w_in_dn, conv_w_dn, a_log_dn, dt_bias_dn, out_norm_dn, w_out_dn, norm_nsa, w_in_nsa, q_norm_nsa, k_norm_cmp, k_norm_slc, k_norm_win, cmp_pe_k, cmp_w1_k, cmp_w2_k, cmp_pe_v, cmp_w1_v, cmp_w2_v, w_out_nsa):
    xp, xs = x_prompt, x_sample
    B = xp.shape[0]
    dw = (norm_dn[0], w_in_dn[0], conv_w_dn[0], a_log_dn[0], dt_bias_dn[0], out_norm_dn[0], w_out_dn[0])
    buf0 = jnp.zeros((B, DN_CONV - 1, conv_w_dn.shape[-1]), xp.dtype)
    st0 = jnp.zeros((B, DN_HEADS, DN_DK, DN_DV), jnp.float32)
    xp, p_dn = _deltanet_mixer(xp, buf0, st0, *dw)
    xs, s_dn = _deltanet_mixer(xs, state_conv[0], state_delta[0], *dw)
    nw = (norm_nsa[0], w_in_nsa[0], q_norm_nsa[0], k_norm_cmp[0], k_norm_slc[0], k_norm_win[0],
          cmp_pe_k[0], cmp_w1_k[0], cmp_w2_k[0], cmp_pe_v[0], cmp_w1_v[0], cmp_w2_v[0], w_out_nsa[0])
    xp, p_nsa = _nsa_prompt(xp, *nw)
    xs, s_nsa = _nsa_sample(xs, cache_cmp_k[0], cache_cmp_v[0], cache_slc_k[0], cache_slc_v[0],
                            cache_win_k[0], cache_win_v[0], page_table, *nw)
    return ((xp, xs, p_dn[0][None], p_dn[1][None]) + tuple(t[None] for t in p_nsa)
            + (s_dn[0][None], s_dn[1][None]) + tuple(t[None] for t in s_nsa))
```
